```python
import numpy as np
import jax
import jax.numpy as jnp
from jax import lax

D_MODEL = 4096
BATCH = 4
SEQ = 2048
DEPTH = 1
DEC_BATCH = 32
DEC_SEQ = 1
PAST_LEN = 8192
PAGE_SIZE = 128

GDN_HEADS = 16
GDN_DK = 128
GDN_DV = 128
GDN_QK_W = GDN_HEADS * GDN_DK
GDN_V_W = GDN_HEADS * GDN_DV
CONV_W = 4
CONV_CH = 2 * GDN_QK_W + GDN_V_W
GDN_CHUNK = 64

ATTN_GROUPS = ((128, 1), (512, 4), (2048, 16))
N_GROUPS = len(ATTN_GROUPS)
GROUP_HEADS = 4
N_ATTN_HEADS = N_GROUPS * GROUP_HEADS
HEAD_DIM = 128
ATTN_W = N_ATTN_HEADS * HEAD_DIM
ATTN_OUT_W = GROUP_HEADS * HEAD_DIM
Q_BLOCK = 128
N_BUCKETS = 32
MAX_DISTANCE = 2048

PEER_HEADS = 8
N_KEYS = 128
N_EXPERTS = N_KEYS * N_KEYS
PEER_TOPK = 16
PEER_QDIM = 256
PEER_SUB = PEER_QDIM // 2
PEER_BLOCK = 128

SPLITS = (CONV_CH, GDN_V_W, GDN_HEADS, GDN_HEADS, ATTN_W, ATTN_W, ATTN_W, D_MODEL, D_MODEL)
N_IN = sum(SPLITS)
EPS = 1e-6
NEG_INF = -1e30

kernel_name = 'hybrid_gdn_dilated_peer_step'

F32 = jnp.float32


def rms_norm(x, gain):
    xf = x.astype(F32)
    y = xf * lax.rsqrt(jnp.mean(xf * xf, axis=-1, keepdims=True) + EPS)
    return y * gain.astype(F32)


def l2_norm(x):
    return x * lax.rsqrt(jnp.sum(x * x, axis=-1, keepdims=True) + EPS)


def t5_causal_bucket(dist):
    max_exact = N_BUCKETS // 2
    d = np.maximum(dist, max_exact).astype(np.float32)
    large = max_exact + (np.log(d / max_exact) / np.log(MAX_DISTANCE / max_exact)
                         * (N_BUCKETS - max_exact)).astype(np.int32)
    large = np.minimum(large, N_BUCKETS - 1)
    return np.where(dist < max_exact, dist, large).astype(np.int32)


def chunk_gated_delta_rule(q, k, v, g, beta, state):
    B, T, H, _ = q.shape
    DV = v.shape[-1]
    C = min(GDN_CHUNK, T)
    pad = (-T) % C
    N = (T + pad) // C

    def chunks(t):
        t = jnp.pad(t, [(0, 0), (0, pad)] + [(0, 0)] * (t.ndim - 2))
        t = t.reshape((B, N, C) + t.shape[2:])
        return jnp.moveaxis(t, 3, 1)

    q, k, v, g, beta = chunks(q), chunks(k), chunks(v), chunks(g), chunks(beta)
    gc = jnp.cumsum(g, axis=-1)
    kb = k * beta[..., None]
    vb = v * beta[..., None]
    tri = np.tril(np.ones((C, C), dtype=bool))
    strict = np.tril(np.ones((C, C), dtype=bool), -1)
    diff = gc[..., :, None] - gc[..., None, :]
    decay = jnp.where(tri, jnp.exp(jnp.where(tri, diff, 0.0)), 0.0)
    lmat = jnp.where(strict, jnp.einsum('bhnid,bhnjd->bhnij', kb, k) * decay, 0.0)
    a_mat = lmat + jnp.eye(C, dtype=F32)
    rhs = jnp.concatenate([vb, kb * jnp.exp(gc)[..., None]], axis=-1)
    sol = lax.linalg.triangular_solve(a_mat, rhs, left_side=True, lower=True, unit_diagonal=True)
    u, w = sol[..., :DV], sol[..., DV:]
    intra = jnp.where(tri, jnp.einsum('bhnid,bhnjd->bhnij', q, k) * decay, 0.0)
    xs = tuple(jnp.moveaxis(t, 2, 0) for t in (q, k, u, w, gc, intra))

    def step(S, inp):
        qi, ki, ui, wi, gi, ai = inp
        v_new = ui - jnp.einsum('bhck,bhkv->bhcv', wi, S)
        o = (jnp.einsum('bhck,bhkv->bhcv', qi * jnp.exp(gi)[..., None], S)
             + jnp.einsum('bhcj,bhjv->bhcv', ai, v_new))
        g_last = gi[..., -1:]
        S = (S * jnp.exp(g_last)[..., None]
             + jnp.einsum('bhck,bhcv->bhkv', ki * jnp.exp(g_last - gi)[..., None], v_new))
        return S, o

    state, o = lax.scan(step, state, xs)
    o = o.transpose(1, 0, 3, 2, 4).reshape(B, N * C, H, DV)[:, :T]
    return o, state


def gdn_branch(qkv, z, a, b, conv_state, ssm_state, conv_w, a_log, dt_bias, gdn_norm):
    B, T, _ = qkv.shape
    xc = jnp.concatenate([conv_state.astype(qkv.dtype), qkv], axis=1)
    new_conv = xc[:, -(CONV_W - 1):]
    y = lax.conv_general_dilated(xc, conv_w.astype(qkv.dtype)[:, None, :], (1,), 'VALID',
                                 dimension_numbers=('NWC', 'WIO', 'NWC'),
                                 feature_group_count=CONV_CH)
    y = jax.nn.silu(y.astype(F32))
    q, k, v = jnp.split(y, [GDN_QK_W, 2 * GDN_QK_W], axis=-1)
    q = l2_norm(q.reshape(B, T, GDN_HEADS, GDN_DK)) * (GDN_DK ** -0.5)
    k = l2_norm(k.reshape(B, T, GDN_HEADS, GDN_DK))
    v = v.reshape(B, T, GDN_HEADS, GDN_DV)
    beta = jax.nn.sigmoid(b.astype(F32))
    g = -jnp.exp(a_log.astype(F32)) * jax.nn.softplus(a.astype(F32) + dt_bias.astype(F32))
    o, new_state = chunk_gated_delta_rule(q, k, v, g, beta, ssm_state.astype(F32))
    o = rms_norm(o, gdn_norm) * jax.nn.silu(z.astype(F32).reshape(B, T, GDN_HEADS, GDN_DV))
    return o.reshape(B, T, GDN_V_W), new_conv, new_state


def dilated_group_attention(q, k, v, q_pos, bias, dilation, n_keys):
    offsets = jnp.arange(n_keys, dtype=jnp.int32) * dilation
    idx = q_pos[:, None] - offsets[None, :]
    valid = idx >= 0
    idx = jnp.maximum(idx, 0)
    kg = jnp.take(k, idx, axis=1, mode='clip').astype(F32)
    vg = jnp.take(v, idx, axis=1, mode='clip').astype(F32)
    s = jnp.einsum('bqhd,bqjhd->bhqj', q, kg) + bias[None, :, None, :]
    s = jnp.where(valid[None, None], s, NEG_INF)
    m = jnp.max(s, axis=-1, keepdims=True)
    p = jnp.exp(s - m)
    l = jnp.sum(p, axis=-1, keepdims=True)
    o = jnp.einsum('bhqj,bqjhd->bqhd', p / l, vg)
    log_den = (m + jnp.log(l))[..., 0]
    return o, log_den


def dilated_mixture(q, ks, vs, q_pos, rel_bias):
    outs, dens = [], []
    for gi, (window, dilation) in enumerate(ATTN_GROUPS):
        n_keys = window // dilation + 1
        buckets = t5_causal_bucket(np.arange(n_keys) * dilation)
        bias = rel_bias[buckets][:, gi * GROUP_HEADS:(gi + 1) * GROUP_HEADS].T.astype(F32)
        o, ld = dilated_group_attention(q[:, :, gi], ks[gi], vs[gi], q_pos[gi], bias, dilation, n_keys)
        outs.append(o)
        dens.append(ld)
    alpha = jax.nn.softmax(jnp.stack(dens, 0), axis=0)
    alpha = alpha.transpose(0, 1, 3, 2)[..., None]
    return jnp.sum(jnp.stack(outs, 0) * alpha, axis=0)


def attention_branch(q, k, v, kv_past, rel_bias):
    B, T = q.shape[:2]
    ks = [jnp.concatenate([kv_past[gi][:, :, 0].astype(k.dtype), k[:, :, gi]], axis=1)
          for gi in range(N_GROUPS)]
    vs = [jnp.concatenate([kv_past[gi][:, :, 1].astype(v.dtype), v[:, :, gi]], axis=1)
          for gi in range(N_GROUPS)]
    starts = [kv_past[gi].shape[1] for gi in range(N_GROUPS)]
    if T % Q_BLOCK == 0:
        nb = T // Q_BLOCK
        qb = q.reshape(B, nb, Q_BLOCK, N_GROUPS, GROUP_HEADS, HEAD_DIM).swapaxes(0, 1)
        rel = jnp.arange(T, dtype=jnp.int32).reshape(nb, Q_BLOCK)

        def body(args):
            qq, rr = args
            return dilated_mixture(qq, ks, vs, [s0 + rr for s0 in starts], rel_bias)

        o = lax.map(body, (qb, rel)).swapaxes(0, 1).reshape(B, T, GROUP_HEADS, HEAD_DIM)
    else:
        rr = jnp.arange(T, dtype=jnp.int32)
        o = dilated_mixture(q, ks, vs, [s0 + rr for s0 in starts], rel_bias)
    return o.reshape(B, T, ATTN_OUT_W)


def peer_block(x, w_query, subkeys, u_tab, v_tab):
    T = x.shape[0]
    qh = (x @ w_query).astype(F32).reshape(T, PEER_HEADS, 2, PEER_SUB)
    s = jnp.einsum('thps,pns->thpn', qh, subkeys.astype(F32))
    s_top, i_top = lax.top_k(s, PEER_TOPK)
    cand = (s_top[:, :, 0, :, None] + s_top[:, :, 1, None, :]).reshape(T, PEER_HEADS, PEER_TOPK * PEER_TOPK)
    cand_idx = (i_top[:, :, 0, :, None] * N_KEYS + i_top[:, :, 1, None, :]).reshape(T, PEER_HEADS, PEER_TOPK * PEER_TOPK)
    best, pos = lax.top_k(cand, PEER_TOPK)
    experts = jnp.take_along_axis(cand_idx, pos, axis=-1).reshape(T, PEER_HEADS * PEER_TOPK)
    gates = jax.nn.softmax(best, axis=-1).reshape(T, PEER_HEADS * PEER_TOPK)
    u = jnp.take(u_tab, experts, axis=0, mode='clip')
    act = jax.nn.gelu(jnp.einsum('td,ted->te', x, u).astype(F32), approximate=False)
    v = jnp.take(v_tab, experts, axis=0, mode='clip')
    return jnp.einsum('te,ted->td', (gates * act).astype(x.dtype), v)


def peer_ffn(x, w_query, subkeys, u_tab, v_tab):
    T = x.shape[0]
    if T % PEER_BLOCK == 0:
        xb = x.reshape(T // PEER_BLOCK, PEER_BLOCK, D_MODEL)
        out = lax.map(lambda xx: peer_block(xx, w_query, subkeys, u_tab, v_tab), xb)
        return out.reshape(T, D_MODEL)
    return peer_block(x, w_query, subkeys, u_tab, v_tab)


def trunk_layer(x, conv_state, ssm_state, kv_past, rel_bias, norm_mix, w_in, conv_w, a_log, dt_bias,
                gdn_norm, q_norm, k_norm, w_branch_a, w_branch_b, w_out, norm_ffn,
                peer_w_query, peer_subkeys, peer_u, peer_v):
    B, T, _ = x.shape
    dt = x.dtype
    h = rms_norm(x, norm_mix).astype(dt)
    proj = h @ w_in
    split_points = [int(s) for s in np.cumsum(SPLITS)[:-1]]
    qkv, z, a, b, aq, ak, av, gate_a, gate_b = jnp.split(proj, split_points, axis=-1)
    ya, new_conv, new_ssm = gdn_branch(qkv, z, a, b, conv_state, ssm_state, conv_w, a_log, dt_bias, gdn_norm)
    hs = (B, T, N_GROUPS, GROUP_HEADS, HEAD_DIM)
    aq = rms_norm(aq.reshape(hs), q_norm) * (HEAD_DIM ** -0.5)
    ak = rms_norm(ak.reshape(hs), k_norm).astype(dt)
    av = av.reshape(hs)
    yb = attention_branch(aq, ak, av, kv_past, rel_bias)
    ya_p = (ya.astype(dt) @ w_branch_a).astype(F32)
    yb_p = (yb.astype(dt) @ w_branch_b).astype(F32)
    merged = jax.nn.sigmoid(gate_a.astype(F32)) * ya_p + jax.nn.sigmoid(gate_b.astype(F32)) * yb_p
    x = x + (merged.astype(dt) @ w_out).astype(dt)
    hf = rms_norm(x, norm_ffn).astype(dt).reshape(B * T, D_MODEL)
    x = x + peer_ffn(hf, peer_w_query, peer_subkeys, peer_u, peer_v).reshape(B, T, D_MODEL).astype(dt)
    return x, new_conv, new_ssm, ak, av


def setup_inputs(seed: int = 0) -> dict:
    key = jax.random.key(seed)
    ks = jax.random.split(key, 24)

    def nrm(k, shape, scale):
        return jax.random.normal(k, shape, F32) * scale

    kv = [nrm(ks[4 + gi], (DEPTH, DEC_BATCH, min(w, PAST_LEN), 2, GROUP_HEADS, HEAD_DIM), 1.0)
          for gi, (w, _) in enumerate(ATTN_GROUPS)]
    return {
        'x_prompt': nrm(ks[0], (BATCH, SEQ, D_MODEL), 1.0),
        'x_sample': nrm(ks[1], (DEC_BATCH, DEC_SEQ, D_MODEL), 1.0),
        'state_ssm': nrm(ks[2], (DEPTH, DEC_BATCH, GDN_HEADS, GDN_DK, GDN_DV), 0.1),
        'state_conv': nrm(ks[3], (DEPTH, DEC_BATCH, CONV_W - 1, CONV_CH), 1.0),
        'cache_kv_g0': kv[0],
        'cache_kv_g1': kv[1],
        'cache_kv_g2': kv[2],
        'rel_bias': nrm(ks[7], (N_BUCKETS, N_ATTN_HEADS), 0.1),
        'norm_mix': 1.0 + nrm(ks[8], (DEPTH, D_MODEL), 0.02),
        'w_in': nrm(ks[9], (DEPTH, D_MODEL, N_IN), D_MODEL ** -0.5),
        'conv_w': nrm(ks[10], (DEPTH, CONV_W, CONV_CH), CONV_W ** -0.5),
        'a_log': jnp.log(jax.random.uniform(ks[11], (DEPTH, GDN_HEADS), F32, 1.0, 16.0)),
        'dt_bias': nrm(ks[12], (DEPTH, GDN_HEADS), 0.1),
        'gdn_norm': 1.0 + nrm(ks[13], (DEPTH, GDN_DV), 0.02),
        'q_norm': 1.0 + nrm(ks[14], (DEPTH, HEAD_DIM), 0.02),
        'k_norm': 1.0 + nrm(ks[15], (DEPTH, HEAD_DIM), 0.02),
        'w_branch_a': nrm(ks[16], (DEPTH, GDN_V_W, D_MODEL), GDN_V_W ** -0.5),
        'w_branch_b': nrm(ks[17], (DEPTH, ATTN_OUT_W, D_MODEL), ATTN_OUT_W ** -0.5),
        'w_out': nrm(ks[18], (DEPTH, D_MODEL, D_MODEL), D_MODEL ** -0.5),
        'norm_ffn': 1.0 + nrm(ks[19], (DEPTH, D_MODEL), 0.02),
        'peer_w_query': nrm(ks[20], (DEPTH, D_MODEL, PEER_HEADS * PEER_QDIM), D_MODEL ** -0.5),
        'peer_subkeys': nrm(ks[21], (DEPTH, 2, N_KEYS, PEER_SUB), PEER_SUB ** -0.5),
        'peer_u': nrm(ks[22], (DEPTH, N_EXPERTS, D_MODEL), D_MODEL ** -0.5),
        'peer_v': nrm(ks[23], (DEPTH, N_EXPERTS, D_MODEL), PEER_HEADS ** -0.5),
    }


def reference(x_prompt, x_sample, state_ssm, state_conv, cache_kv_g0, cache_kv_g1, cache_kv_g2,
              rel_bias, norm_mix, w_in, conv_w, a_log, dt_bias, gdn_norm, q_norm, k_norm,
              w_branch_a, w_branch_b, w_out, norm_ffn, peer_w_query, peer_subkeys, peer_u, peer_v):
    B, T = x_prompt.shape[0], x_prompt.shape[1]
    caches = (cache_kv_g0, cache_kv_g1, cache_kv_g2)
    hp, hs = x_prompt, x_sample
    p_ssm, p_conv, s_ssm, s_conv = [], [], [], []
    p_kv = [[] for _ in ATTN_GROUPS]
    s_kv = [[] for _ in ATTN_GROUPS]
    for layer in range(DEPTH):
        lw = (norm_mix[layer], w_in[layer], conv_w[layer], a_log[layer], dt_bias[layer], gdn_norm[layer],
              q_norm[layer], k_norm[layer], w_branch_a[layer], w_branch_b[layer], w_out[layer],
              norm_ffn[layer], peer_w_query[layer], peer_subkeys[layer], peer_u[layer], peer_v[layer])
        empty_kv = tuple(jnp.zeros((B, 0, 2, GROUP_HEADS, HEAD_DIM), x_prompt.dtype) for _ in ATTN_GROUPS)
        hp, conv_p, ssm_p, kp, vp = trunk_layer(
            hp, jnp.zeros((B, CONV_W - 1, CONV_CH), x_prompt.dtype),
            jnp.zeros((B, GDN_HEADS, GDN_DK, GDN_DV), F32), empty_kv, rel_bias, *lw)
        hs, conv_s, ssm_s, kn, vn = trunk_layer(
            hs, state_conv[layer], state_ssm[layer], tuple(c[layer] for c in caches), rel_bias, *lw)
        p_ssm.append(ssm_p)
        p_conv.append(conv_p)
        s_ssm.append(ssm_s)
        s_conv.append(conv_s)
        for gi, (window, _) in enumerate(ATTN_GROUPS):
            keep = min(window, T)
            p_kv[gi].append(jnp.stack([kp[:, T - keep:, gi], vp[:, T - keep:, gi]], axis=2))
            s_kv[gi].append(jnp.stack([kn[:, :, gi], vn[:, :, gi]], axis=2))
    return (hp, hs,
            jnp.stack(p_ssm, 0), jnp.stack(p_conv, 0),
            jnp.stack(p_kv[0], 0), jnp.stack(p_kv[1], 0), jnp.stack(p_kv[2], 0),
            jnp.stack(s_ssm, 0), jnp.stack(s_conv, 0),
            jnp.stack(s_kv[0], 0), jnp.stack(s_kv[1], 0), jnp.stack(s_kv[2], 0))
```

```python
import functools

import numpy as np
import jax
import jax.numpy as jnp
from jax import lax
from jax.experimental import pallas as pl
from jax.experimental.pallas import tpu as pltpu

F32 = jnp.float32
BF16 = jnp.bfloat16
HIGHEST = lax.Precision.HIGHEST

D_MODEL = 4096
GDN_HEADS = 16
GDN_DK = 128
GDN_DV = 128
GDN_QK_W = GDN_HEADS * GDN_DK
GDN_V_W = GDN_HEADS * GDN_DV
CONV_W = 4
CONV_CH = 2 * GDN_QK_W + GDN_V_W
GDN_CHUNK = 64
ATTN_GROUPS = ((128, 1), (512, 4), (2048, 16))
N_GROUPS = 3
GROUP_HEADS = 4
HEAD_DIM = 128
GROUP_W = GROUP_HEADS * HEAD_DIM
ATTN_W = N_GROUPS * GROUP_W
Q_BLOCK = 128
N_BUCKETS = 32
MAX_DISTANCE = 2048
PEER_HEADS = 8
N_KEYS = 128
PEER_TOPK = 16
EPS = 1e-6
NEG_INF = -1e30

OFF_QKV = 0
OFF_Z = CONV_CH
OFF_AQ = OFF_Z + GDN_V_W
OFF_AK = OFF_AQ + ATTN_W
OFF_AV = OFF_AK + ATTN_W
OFF_GA = OFF_AV + ATTN_W
OFF_GB = OFF_GA + D_MODEL
OFF_AB = OFF_GB + D_MODEL
N_PROJ = 21504

GDN_HB = 4
GDN_HG = GDN_HEADS // GDN_HB
VMEM_LIMIT = 56 * 1024 * 1024


def _cparams(sem):
    return pltpu.CompilerParams(dimension_semantics=sem, vmem_limit_bytes=VMEM_LIMIT)


def _sigmoid(x):
    return 1.0 / (1.0 + jnp.exp(-x))


def _silu(x):
    return x * _sigmoid(x)


def _softplus(x):
    return jnp.maximum(x, 0.0) + jnp.log(1.0 + jnp.exp(-jnp.abs(x)))


def _bdot(a, b):
    return jnp.dot(a.astype(BF16), b.astype(BF16), preferred_element_type=F32)


def _bdot_nt(a, b):
    return lax.dot_general(a.astype(BF16), b.astype(BF16), (((1,), (1,)), ((), ())),
                           preferred_element_type=F32)


def _bdot_tn(a, b):
    return lax.dot_general(a.astype(BF16), b.astype(BF16), (((0,), (0,)), ((), ())),
                           preferred_element_type=F32)


def _hdot(a, b):
    return jnp.dot(a, b, precision=HIGHEST, preferred_element_type=F32)


def _rmsnorm_body(x_ref, g_ref, o_ref):
    x = x_ref[...]
    ms = jnp.mean(x * x, axis=-1, keepdims=True)
    o_ref[...] = (x * lax.rsqrt(ms + EPS) * g_ref[...]).astype(o_ref.dtype)


def rmsnorm_bf16(x, gain):
    M, D = x.shape
    tm = min(M, 256)
    return pl.pallas_call(
        _rmsnorm_body,
        grid=(M // tm,),
        in_specs=[pl.BlockSpec((tm, D), lambda i: (i, 0)),
                  pl.BlockSpec((1, D), lambda i: (0, 0))],
        out_specs=pl.BlockSpec((tm, D), lambda i: (i, 0)),
        out_shape=jax.ShapeDtypeStruct((M, D), BF16),
        compiler_params=_cparams(("parallel",)),
        name="rmsnorm",
    )(x, gain.reshape(1, D))


def _mm_body(a_ref, b_ref, o_ref):
    o_ref[...] = jnp.dot(a_ref[...], b_ref[...], preferred_element_type=F32)


def _mm_res_body(a_ref, b_ref, r_ref, o_ref):
    o_ref[...] = r_ref[...] + jnp.dot(a_ref[...], b_ref[...], preferred_element_type=F32)


def matmul(a, b, residual=None, tm=1024, tn=512, name="matmul"):
    M, K = a.shape
    N = b.shape[1]
    tm = min(tm, M)
    tn = min(tn, N)
    in_specs = [pl.BlockSpec((tm, K), lambda i, j: (i, 0)),
                pl.BlockSpec((K, tn), lambda i, j: (0, j))]
    args = [a, b]
    body = _mm_body
    if residual is not None:
        in_specs.append(pl.BlockSpec((tm, tn), lambda i, j: (i, j)))
        args.append(residual)
        body = _mm_res_body
    return pl.pallas_call(
        body,
        grid=(M // tm, N // tn),
        in_specs=in_specs,
        out_specs=pl.BlockSpec((tm, tn), lambda i, j: (i, j)),
        out_shape=jax.ShapeDtypeStruct((M, N), F32),
        compiler_params=_cparams(("parallel", "parallel")),
        name=name,
    )(*args)


def _gdn_chunk_body(q_ref, k_ref, v_ref, z_ref, ab_ref, cwq_ref, cwk_ref, cwv_ref,
                    alog_ref, dtb_ref, gn_ref, o_ref, s_out_ref,
                    xq_s, xk_s, xv_s, state_s):
    n = pl.program_id(2)
    C = GDN_CHUNK
    HW = GDN_HB * GDN_DK

    @pl.when(n == 0)
    def _():
        xq_s[0:8, :] = jnp.zeros((8, HW), F32)
        xk_s[0:8, :] = jnp.zeros((8, HW), F32)
        xv_s[0:8, :] = jnp.zeros((8, HW), F32)
        state_s[...] = jnp.zeros(state_s.shape, F32)

    def conv(x_ref, xs, cw_ref):
        xs[8:8 + C, :] = x_ref[0]
        y = cw_ref[CONV_W - 1:CONV_W, :] * xs[8:8 + C, :]
        for w in range(CONV_W - 1):
            y = y + cw_ref[w:w + 1, :] * xs[5 + w:5 + w + C, :]
        xs[0:8, :] = xs[C:C + 8, :]
        return _silu(y)

    q_all = conv(q_ref, xq_s, cwq_ref)
    k_all = conv(k_ref, xk_s, cwk_ref)
    v_all = conv(v_ref, xv_s, cwv_ref)

    ab = ab_ref[0, 0]
    g_all = -jnp.exp(alog_ref[0]) * _softplus(ab + dtb_ref[0])
    beta_all = _sigmoid(ab)
    ri = lax.broadcasted_iota(jnp.int32, (C, C), 0)
    ci = lax.broadcasted_iota(jnp.int32, (C, C), 1)
    tri = ri >= ci
    strict = ri > ci
    gc_all = _hdot(tri.astype(F32), g_all)
    gc_t = jnp.concatenate([gc_all, jnp.zeros((128 - C, 128), F32)], axis=0).T

    for j in range(GDN_HB):
        sl = slice(j * GDN_DK, (j + 1) * GDN_DK)
        qh = q_all[:, sl]
        kh = k_all[:, sl]
        vh = v_all[:, sl]
        qh = qh * lax.rsqrt(jnp.sum(qh * qh, axis=-1, keepdims=True) + EPS) * (GDN_DK ** -0.5)
        kh = kh * lax.rsqrt(jnp.sum(kh * kh, axis=-1, keepdims=True) + EPS)
        beta = beta_all[:, GDN_HB + j:GDN_HB + j + 1]
        gcol = gc_all[:, j:j + 1]
        grow = gc_t[j:j + 1, :C]
        kb = kh * beta
        vb = vh * beta
        decay = jnp.where(tri, jnp.exp(jnp.where(tri, gcol - grow, 0.0)), 0.0)
        lmat = jnp.where(strict, _bdot_nt(kb, kh) * decay, 0.0)
        pw = -lmat
        acc = pw
        for _ in range(5):
            pw = _hdot(pw, pw)
            acc = acc + pw + _hdot(acc, pw)
        rhs = jnp.concatenate([vb, kb * jnp.exp(gcol)], axis=-1)
        sol = rhs + _hdot(acc, rhs)
        u = sol[:, :GDN_DV]
        w = sol[:, GDN_DV:]
        intra = jnp.where(tri, _bdot_nt(qh, kh) * decay, 0.0)
        S = state_s[j]
        v_new = u - _bdot(w, S)
        o = _bdot(qh * jnp.exp(gcol), S) + _bdot(intra, v_new)
        g_last = gcol[C - 1:C, :]
        S_new = S * jnp.exp(g_last) + _bdot_tn(kh * jnp.exp(g_last - gcol), v_new)
        state_s[j] = S_new
        ms = jnp.mean(o * o, axis=-1, keepdims=True)
        on = o * lax.rsqrt(ms + EPS) * gn_ref[...]
        o_ref[0, :, sl] = (on * _silu(z_ref[0, :, sl])).astype(o_ref.dtype)

    @pl.when(n == pl.num_programs(2) - 1)
    def _():
        s_out_ref[0] = state_s[...]


def gdn_prompt(proj3, ab_g, conv_w, alog_g, dtb_g, gdn_norm):
    B, T, _ = proj3.shape
    C = GDN_CHUNK
    N = T // C
    HW = GDN_HB * GDN_DK
    nq = GDN_QK_W // HW

    def col(base):
        return lambda b, hg, n: (b, n, base + hg)

    def cwcol(base):
        return lambda b, hg, n: (0, base + hg)

    return pl.pallas_call(
        _gdn_chunk_body,
        grid=(B, GDN_HG, N),
        in_specs=[
            pl.BlockSpec((1, C, HW), col(0)),
            pl.BlockSpec((1, C, HW), col(nq)),
            pl.BlockSpec((1, C, HW), col(2 * nq)),
            pl.BlockSpec((1, C, HW), col(OFF_Z // HW)),
            pl.BlockSpec((1, 1, C, 128), lambda b, hg, n: (b, hg, n, 0)),
            pl.BlockSpec((CONV_W, HW), cwcol(0)),
            pl.BlockSpec((CONV_W, HW), cwcol(nq)),
            pl.BlockSpec((CONV_W, HW), cwcol(2 * nq)),
            pl.BlockSpec((1, 1, 128), lambda b, hg, n: (hg, 0, 0)),
            pl.BlockSpec((1, 1, 128), lambda b, hg, n: (hg, 0, 0)),
            pl.BlockSpec((1, GDN_DV), lambda b, hg, n: (0, 0)),
        ],
        out_specs=[
            pl.BlockSpec((1, C, HW), lambda b, hg, n: (b, n, hg)),
            pl.BlockSpec((1, GDN_HB, GDN_DK, GDN_DV), lambda b, hg, n: (b, hg, 0, 0)),
        ],
        out_shape=[
            jax.ShapeDtypeStruct((B, T, GDN_V_W), BF16),
            jax.ShapeDtypeStruct((B, GDN_HEADS, GDN_DK, GDN_DV), F32),
        ],
        scratch_shapes=[
            pltpu.VMEM((C + 8, HW), F32),
            pltpu.VMEM((C + 8, HW), F32),
            pltpu.VMEM((C + 8, HW), F32),
            pltpu.VMEM((GDN_HB, GDN_DK, GDN_DV), F32),
        ],
        compiler_params=_cparams(("parallel", "parallel", "arbitrary")),
        name="gdn_prompt",
    )(proj3, proj3, proj3, proj3, ab_g, conv_w, conv_w, conv_w, alog_g, dtb_g,
      gdn_norm.reshape(1, GDN_DV))


def _gdn_step_body(x_ref, z_ref, ab_ref, cs_ref, s_ref, cw_ref, alog_ref, dtb_ref, gn_ref,
                   o_ref, cs_out_ref, s_out_ref):
    x = x_ref[0]
    y = cw_ref[CONV_W - 1] * x
    for w in range(CONV_W - 1):
        y = y + cw_ref[w] * cs_ref[0, w]
    y = _silu(y)
    cs_out_ref[0, 0] = cs_ref[0, 1]
    cs_out_ref[0, 1] = cs_ref[0, 2]
    cs_out_ref[0, 2] = x
    r_ab = (OFF_AB // 128) % 8
    ab = ab_ref[0, r_ab:r_ab + 1, :]
    g_all = -jnp.exp(alog_ref[...]) * _softplus(ab + dtb_ref[...])
    beta_all = _sigmoid(ab)
    eye = (lax.broadcasted_iota(jnp.int32, (GDN_DK, GDN_DK), 0)
           == lax.broadcasted_iota(jnp.int32, (GDN_DK, GDN_DK), 1))

    def to_col(row):
        return jnp.sum(jnp.where(eye, jnp.broadcast_to(row, (GDN_DK, GDN_DK)), 0.0),
                       axis=1, keepdims=True)

    for h in range(GDN_HEADS):
        q = y[h:h + 1, :]
        k = y[GDN_HEADS + h:GDN_HEADS + h + 1, :]
        v = y[2 * GDN_HEADS + h:2 * GDN_HEADS + h + 1, :]
        q = q * lax.rsqrt(jnp.sum(q * q, axis=-1, keepdims=True) + EPS) * (GDN_DK ** -0.5)
        k = k * lax.rsqrt(jnp.sum(k * k, axis=-1, keepdims=True) + EPS)
        g = g_all[:, h:h + 1]
        beta = beta_all[:, GDN_HEADS + h:GDN_HEADS + h + 1]
        eg = jnp.exp(g)
        S = s_ref[0, h]
        kcol = to_col(k)
        qcol = to_col(q)
        k_s = jnp.sum(S * kcol, axis=0, keepdims=True)
        q_s = jnp.sum(S * qcol, axis=0, keepdims=True)
        v_new = v * beta - (beta * eg) * k_s
        o = eg * q_s + jnp.sum(q * k, axis=-1, keepdims=True) * v_new
        s_out_ref[0, h] = S * eg + kcol * v_new
        ms = jnp.mean(o * o, axis=-1, keepdims=True)
        on = o * lax.rsqrt(ms + EPS) * gn_ref[...]
        o_ref[0, h:h + 1, :] = on * _silu(z_ref[0, h:h + 1, :])


def gdn_step(proj_s, conv_state, ssm_state, conv_w, a_log, dt_bias, gdn_norm):
    Bs = proj_s.shape[0]
    nrow = N_PROJ // 128
    p3 = proj_s.reshape(Bs, nrow, 128)
    cs4 = conv_state.reshape(Bs, CONV_W - 1, CONV_CH // 128, 128)
    cw3 = conv_w.reshape(CONV_W, CONV_CH // 128, 128)
    pad = jnp.zeros((128 - GDN_HEADS,), F32)
    alog = jnp.concatenate([a_log, pad]).reshape(1, 128)
    dtb = jnp.concatenate([dt_bias, pad]).reshape(1, 128)
    nr = CONV_CH // 128
    o, cs_new, s_new = pl.pallas_call(
        _gdn_step_body,
        grid=(Bs,),
        in_specs=[
            pl.BlockSpec((1, nr, 128), lambda b: (b, 0, 0)),
            pl.BlockSpec((1, GDN_HEADS, 128), lambda b: (b, OFF_Z // 128 // GDN_HEADS, 0)),
            pl.BlockSpec((1, 8, 128), lambda b: (b, OFF_AB // 128 // 8, 0)),
            pl.BlockSpec((1, CONV_W - 1, nr, 128), lambda b: (b, 0, 0, 0)),
            pl.BlockSpec((1, GDN_HEADS, GDN_DK, GDN_DV), lambda b: (b, 0, 0, 0)),
            pl.BlockSpec((CONV_W, nr, 128), lambda b: (0, 0, 0)),
            pl.BlockSpec((1, 128), lambda b: (0, 0)),
            pl.BlockSpec((1, 128), lambda b: (0, 0)),
            pl.BlockSpec((1, GDN_DV), lambda b: (0, 0)),
        ],
        out_specs=[
            pl.BlockSpec((1, GDN_HEADS, GDN_DV), lambda b: (b, 0, 0)),
            pl.BlockSpec((1, CONV_W - 1, nr, 128), lambda b: (b, 0, 0, 0)),
            pl.BlockSpec((1, GDN_HEADS, GDN_DK, GDN_DV), lambda b: (b, 0, 0, 0)),
        ],
        out_shape=[
            jax.ShapeDtypeStruct((Bs, GDN_HEADS, GDN_DV), F32),
            jax.ShapeDtypeStruct((Bs, CONV_W - 1, nr, 128), F32),
            jax.ShapeDtypeStruct((Bs, GDN_HEADS, GDN_DK, GDN_DV), F32),
        ],
        compiler_params=_cparams(("parallel",)),
        name="gdn_step",
    )(p3, p3, p3, cs4, ssm_state, cw3, alog, dtb, gdn_norm.reshape(1, GDN_DV))
    return (o.reshape(Bs, GDN_V_W), cs_new.reshape(Bs, CONV_W - 1, CONV_CH), s_new)


def _qknorm_body(q_ref, k_ref, v_ref, qg_ref, kg_ref, qn_ref, kv_ref):
    for h in range(GROUP_HEADS):
        sl = slice(h * HEAD_DIM, (h + 1) * HEAD_DIM)
        q = q_ref[:, sl]
        qn = q * lax.rsqrt(jnp.mean(q * q, axis=-1, keepdims=True) + EPS) * qg_ref[...]
        qn_ref[:, sl] = qn * (HEAD_DIM ** -0.5)
        k = k_ref[:, sl]
        kv_ref[:, sl] = k * lax.rsqrt(jnp.mean(k * k, axis=-1, keepdims=True) + EPS) * kg_ref[...]
    kv_ref[:, GROUP_W:] = v_ref[...]


def qk_norm(proj, q_norm, k_norm):
    M = proj.shape[0]
    tm = min(M, 512)
    blk = GROUP_W
    return pl.pallas_call(
        _qknorm_body,
        grid=(M // tm, N_GROUPS),
        in_specs=[
            pl.BlockSpec((tm, blk), lambda i, g: (i, OFF_AQ // blk + g)),
            pl.BlockSpec((tm, blk), lambda i, g: (i, OFF_AK // blk + g)),
            pl.BlockSpec((tm, blk), lambda i, g: (i, OFF_AV // blk + g)),
            pl.BlockSpec((1, HEAD_DIM), lambda i, g: (0, 0)),
            pl.BlockSpec((1, HEAD_DIM), lambda i, g: (0, 0)),
        ],
        out_specs=[
            pl.BlockSpec((tm, blk), lambda i, g: (i, g)),
            pl.BlockSpec((tm, 2 * blk), lambda i, g: (i, g)),
        ],
        out_shape=[jax.ShapeDtypeStruct((M, ATTN_W), F32),
                   jax.ShapeDtypeStruct((M, N_GROUPS * 2 * GROUP_W), F32)],
        compiler_params=_cparams(("parallel", "parallel")),
        name="qk_norm",
    )(proj, proj, proj, q_norm.reshape(1, HEAD_DIM), k_norm.reshape(1, HEAD_DIM))


def _t5_bucket(dist):
    max_exact = N_BUCKETS // 2
    d = np.maximum(dist, max_exact).astype(np.float32)
    large = max_exact + (np.log(d / max_exact) / np.log(MAX_DISTANCE / max_exact)
                         * (N_BUCKETS - max_exact)).astype(np.int32)
    large = np.minimum(large, N_BUCKETS - 1)
    return np.where(dist < max_exact, dist, large).astype(np.int32)


def _group_bias(rel_bias, gi):
    window, dilation = ATTN_GROUPS[gi]
    n_keys = window // dilation + 1
    buckets = _t5_bucket(np.arange(n_keys) * dilation)
    return rel_bias[buckets][:, gi * GROUP_HEADS:(gi + 1) * GROUP_HEADS].T.astype(F32)


def _attn_prompt_body(has_prev, q_ref, kc_ref, vc_ref, kp_ref, vp_ref, bias_ref, o_ref, ld_ref):
    i = pl.program_id(2)
    QB = Q_BLOCK
    lane = lax.broadcasted_iota(jnp.int32, (QB, 128), 1)
    ld_all = jnp.zeros((QB, 128), F32)
    if has_prev:
        colid = lax.broadcasted_iota(jnp.int32, (QB, 2 * QB), 1)
        no_prev = jnp.logical_and(colid < QB, i == 0)
    for h in range(GROUP_HEADS):
        sl = slice(h * HEAD_DIM, (h + 1) * HEAD_DIM)
        q = q_ref[0, :, sl].astype(BF16)
        if has_prev:
            kcat = jnp.concatenate([kp_ref[0, :, sl], kc_ref[0, :, sl]], axis=0).astype(BF16)
            vcat = jnp.concatenate([vp_ref[0, :, sl], vc_ref[0, :, sl]], axis=0).astype(BF16)
            s = lax.dot_general(q, kcat, (((1,), (1,)), ((), ())), preferred_element_type=F32)
            s = s + bias_ref[h]
            s = jnp.where(no_prev, NEG_INF, s)
        else:
            kcat = kc_ref[0, :, sl].astype(BF16)
            vcat = vc_ref[0, :, sl].astype(BF16)
            s = lax.dot_general(q, kcat, (((1,), (1,)), ((), ())), preferred_element_type=F32)
            s = s + bias_ref[h, :, QB:]
        m = jnp.max(s, axis=-1, keepdims=True)
        p = jnp.exp(s - m)
        l = jnp.sum(p, axis=-1, keepdims=True)
        o = jnp.dot(p.astype(BF16), vcat, preferred_element_type=F32)
        o_ref[0, :, sl] = o / l
        ld = m + jnp.log(l)
        ld_all = jnp.where(lane == h, ld, ld_all)
    ld_ref[0] = ld_all


def attn_prompt_group(qn3, kv3, bias_mat, gi):
    B, T, _ = qn3.shape
    window, d = ATTN_GROUPS[gi]
    Ts = T // d
    nq = Ts // Q_BLOCK
    has_prev = nq > 1
    qv = qn3.reshape(B, Ts, d * ATTN_W)
    kvv = kv3.reshape(B, Ts, d * N_GROUPS * 2 * GROUP_W)
    blk = (1, Q_BLOCK, GROUP_W)
    nkv = 2 * N_GROUPS
    in_specs = [
        pl.BlockSpec(blk, lambda b, r, i: (b, i, r * N_GROUPS + gi)),
        pl.BlockSpec(blk, lambda b, r, i: (b, i, nkv * r + 2 * gi)),
        pl.BlockSpec(blk, lambda b, r, i: (b, i, nkv * r + 2 * gi + 1)),
        pl.BlockSpec(blk, lambda b, r, i: (b, jnp.maximum(i - 1, 0), nkv * r + 2 * gi)),
        pl.BlockSpec(blk, lambda b, r, i: (b, jnp.maximum(i - 1, 0), nkv * r + 2 * gi + 1)),
        pl.BlockSpec((GROUP_HEADS, Q_BLOCK, 2 * Q_BLOCK), lambda b, r, i: (0, 0, 0)),
    ]
    o, ld = pl.pallas_call(
        functools.partial(_attn_prompt_body, has_prev),
        grid=(B, d, nq),
        in_specs=in_specs,
        out_specs=[
            pl.BlockSpec(blk, lambda b, r, i: (b, i, r)),
            pl.BlockSpec((1, Q_BLOCK, 128), lambda b, r, i: (b, i, r)),
        ],
        out_shape=[
            jax.ShapeDtypeStruct((B, Ts, d * GROUP_W), F32),
            jax.ShapeDtypeStruct((B, Ts, d * 128), F32),
        ],
        compiler_params=_cparams(("parallel", "parallel", "arbitrary")),
        name=f"attn_prompt_g{gi}",
    )(qv, kvv, kvv, kvv, kvv, bias_mat)
    return o.reshape(B * T, GROUP_W), ld.reshape(B * T, 128)


def _prompt_bias_matrix(rel_bias, gi):
    bias = _group_bias(rel_bias, gi)
    r = np.arange(Q_BLOCK)[:, None]
    c = np.arange(2 * Q_BLOCK)[None, :]
    dist = Q_BLOCK + r - c
    valid = (dist >= 0) & (dist <= Q_BLOCK)
    idx = np.clip(dist, 0, Q_BLOCK)
    return jnp.where(valid[None], bias[:, idx], NEG_INF)


def _attn_merge_body(o0_ref, o1_ref, o2_ref, l0_ref, l1_ref, l2_ref, y_ref):
    for h in range(GROUP_HEADS):
        sl = slice(h * HEAD_DIM, (h + 1) * HEAD_DIM)
        l0 = l0_ref[:, h:h + 1]
        l1 = l1_ref[:, h:h + 1]
        l2 = l2_ref[:, h:h + 1]
        m = jnp.maximum(jnp.maximum(l0, l1), l2)
        e0 = jnp.exp(l0 - m)
        e1 = jnp.exp(l1 - m)
        e2 = jnp.exp(l2 - m)
        den = e0 + e1 + e2
        y = (o0_ref[:, sl] * (e0 / den) + o1_ref[:, sl] * (e1 / den) + o2_ref[:, sl] * (e2 / den))
        y_ref[:, sl] = y.astype(y_ref.dtype)


def attn_merge(os_, lds):
    M = os_[0].shape[0]
    tm = min(M, 512)
    ospec = pl.BlockSpec((tm, GROUP_W), lambda i: (i, 0))
    lspec = pl.BlockSpec((tm, 128), lambda i: (i, 0))
    return pl.pallas_call(
        _attn_merge_body,
        grid=(M // tm,),
        in_specs=[ospec, ospec, ospec, lspec, lspec, lspec],
        out_specs=ospec,
        out_shape=jax.ShapeDtypeStruct((M, GROUP_W), BF16),
        compiler_params=_cparams(("parallel",)),
        name="attn_merge",
    )(*os_, *lds)


def _attn_decode_body(q_ref, n0_ref, n1_ref, n2_ref, c0_ref, c1_ref, c2_ref, bc_ref, bn_ref, y_ref):
    new_refs = (n0_ref, n1_ref, n2_ref)
    cache_refs = (c0_ref, c1_ref, c2_ref)
    for h in range(GROUP_HEADS):
        outs, lds = [], []
        for gi in range(N_GROUPS):
            sl = slice(h * HEAD_DIM, (h + 1) * HEAD_DIM)
            vsl = slice(GROUP_W + h * HEAD_DIM, GROUP_W + (h + 1) * HEAD_DIM)
            q = q_ref[0, gi * GROUP_HEADS + h:gi * GROUP_HEADS + h + 1, :]
            kc = cache_refs[gi][0, :, sl]
            vc = cache_refs[gi][0, :, vsl]
            kn = new_refs[gi][0, h:h + 1, :]
            vn = new_refs[gi][0, GROUP_HEADS + h:GROUP_HEADS + h + 1, :]
            s_c = jnp.sum(kc * q, axis=-1, keepdims=True) + bc_ref[gi, :, h * HEAD_DIM:h * HEAD_DIM + 1]
            s_n = jnp.sum(kn * q, axis=-1, keepdims=True) + bn_ref[gi, :, h * HEAD_DIM:h * HEAD_DIM + 1]
            m = jnp.maximum(jnp.max(s_c, axis=0, keepdims=True), s_n)
            p_c = jnp.exp(s_c - m)
            p_n = jnp.exp(s_n - m)
            l = jnp.sum(p_c, axis=0, keepdims=True) + p_n
            o = (jnp.sum(p_c * vc, axis=0, keepdims=True) + p_n * vn) / l
            outs.append(o)
            lds.append(m + jnp.log(l))
        m = jnp.maximum(jnp.maximum(lds[0], lds[1]), lds[2])
        es = [jnp.exp(x - m) for x in lds]
        den = es[0] + es[1] + es[2]
        y_ref[0, h:h + 1, :] = (outs[0] * (es[0] / den) + outs[1] * (es[1] / den)
                                + outs[2] * (es[2] / den))


def attn_decode(qn_s, kv_new, caches, rel_bias):
    Bs = qn_s.shape[0]
    q3 = qn_s.reshape(Bs, N_GROUPS * GROUP_HEADS, HEAD_DIM)
    new3 = [x.reshape(Bs, 2 * GROUP_HEADS, HEAD_DIM) for x in kv_new]
    cviews, bcs, bns = [], [], []
    for gi, (window, d) in enumerate(ATTN_GROUPS):
        nk = window // d
        assert caches[gi].shape[1] == window and nk == Q_BLOCK
        cviews.append(caches[gi].reshape(Bs, nk, d * 2 * GROUP_W))
        bias = _group_bias(rel_bias, gi)
        bc = bias[:, nk - np.arange(nk)]
        bcs.append(jnp.repeat(bc.T, HEAD_DIM, axis=1))
        bns.append(jnp.repeat(bias[:, 0:1].T, HEAD_DIM, axis=1))
    bc_all = jnp.stack(bcs, 0)
    bn_all = jnp.stack(bns, 0)
    nspec = pl.BlockSpec((1, 2 * GROUP_HEADS, HEAD_DIM), lambda b: (b, 0, 0))
    cspec = pl.BlockSpec((1, Q_BLOCK, 2 * GROUP_W), lambda b: (b, 0, 0))
    y = pl.pallas_call(
        _attn_decode_body,
        grid=(Bs,),
        in_specs=[
            pl.BlockSpec((1, N_GROUPS * GROUP_HEADS, HEAD_DIM), lambda b: (b, 0, 0)),
            nspec, nspec, nspec, cspec, cspec, cspec,
            pl.BlockSpec((N_GROUPS, Q_BLOCK, GROUP_W), lambda b: (0, 0, 0)),
            pl.BlockSpec((N_GROUPS, 1, GROUP_W), lambda b: (0, 0, 0)),
        ],
        out_specs=pl.BlockSpec((1, GROUP_HEADS, HEAD_DIM), lambda b: (b, 0, 0)),
        out_shape=jax.ShapeDtypeStruct((Bs, GROUP_HEADS, HEAD_DIM), F32),
        compiler_params=_cparams(("parallel",)),
        name="attn_decode",
    )(q3, *new3, *cviews, bc_all, bn_all)
    return y.reshape(Bs, GROUP_W)


def _merge_body(ya_ref, yb_ref, wa_ref, wb_ref, ga_ref, gb_ref, o_ref):
    pa = jnp.dot(ya_ref[...], wa_ref[...], preferred_element_type=F32)
    pb = jnp.dot(yb_ref[...], wb_ref[...], preferred_element_type=F32)
    o_ref[...] = (_sigmoid(ga_ref[...]) * pa + _sigmoid(gb_ref[...]) * pb).astype(o_ref.dtype)


def branch_merge(ya, yb, wa, wb, proj):
    M = ya.shape[0]
    tm = min(M, 1024)
    tn = 512
    return pl.pallas_call(
        _merge_body,
        grid=(M // tm, D_MODEL // tn),
        in_specs=[
            pl.BlockSpec((tm, GDN_V_W), lambda i, j: (i, 0)),
            pl.BlockSpec((tm, GROUP_W), lambda i, j: (i, 0)),
            pl.BlockSpec((GDN_V_W, tn), lambda i, j: (0, j)),
            pl.BlockSpec((GROUP_W, tn), lambda i, j: (0, j)),
            pl.BlockSpec((tm, tn), lambda i, j: (i, OFF_GA // tn + j)),
            pl.BlockSpec((tm, tn), lambda i, j: (i, OFF_GB // tn + j)),
        ],
        out_specs=pl.BlockSpec((tm, tn), lambda i, j: (i, j)),
        out_shape=jax.ShapeDtypeStruct((M, D_MODEL), BF16),
        compiler_params=_cparams(("parallel", "parallel")),
        name="branch_merge",
    )(ya, yb, wa, wb, proj, proj)


def _extract_top(x, count):
    R = x.shape[0]
    rid = lax.broadcasted_iota(jnp.int32, x.shape, 0).astype(F32)
    vid = lax.broadcasted_iota(jnp.int32, (count, x.shape[1]), 0)
    vals = jnp.zeros((count, x.shape[1]), F32)
    for a in range(count):
        m = jnp.max(x, axis=0, keepdims=True)
        first = jnp.min(jnp.where(x == m, rid, float(R)), axis=0, keepdims=True)
        x = jnp.where(rid == first, -jnp.inf, x)
        vals = jnp.where(vid == a, m, vals)
    return vals, x == -jnp.inf


def _peer_topk_body(q_ref, sk_ref, s1_ref, s2_ref, e1_ref, e2_ref, tau_ref, cand_s):
    K = PEER_TOPK

    def head(h, carry):
        off = pl.multiple_of(h * (2 * N_KEYS), 2 * N_KEYS)
        q1 = q_ref[:, pl.ds(off, N_KEYS)]
        q2 = q_ref[:, pl.ds(pl.multiple_of(off + N_KEYS, N_KEYS), N_KEYS)]
        s1 = lax.dot_general(sk_ref[0], q1, (((1,), (1,)), ((), ())), precision=HIGHEST,
                             preferred_element_type=F32)
        s2 = lax.dot_general(sk_ref[1], q2, (((1,), (1,)), ((), ())), precision=HIGHEST,
                             preferred_element_type=F32)
        v1, sel1 = _extract_top(s1, K)
        v2, sel2 = _extract_top(s2, K)
        for a in range(K):
            cand_s[a * K:(a + 1) * K, :] = v2 + v1[a:a + 1, :]
        best, _ = _extract_top(cand_s[...], K)
        cmax = best[0:1, :]
        z = jnp.sum(jnp.exp(best - cmax), axis=0, keepdims=True)
        s1_ref[h] = s1
        s2_ref[h] = s2
        e1_ref[h] = jnp.where(sel1, jnp.exp(s1 - v1[0:1, :]), 0.0)
        e2_ref[h] = jnp.where(sel2, jnp.exp(s2 - v2[0:1, :]), 0.0) / z
        tau_ref[h] = best[K - 1:K, :]
        return carry

    lax.fori_loop(0, PEER_HEADS, head, 0)


def peer_topk(query, subkeys):
    M = query.shape[0]
    tm = min(M, 256)
    big = jax.ShapeDtypeStruct((PEER_HEADS, N_KEYS, M), F32)
    bspec = pl.BlockSpec((PEER_HEADS, N_KEYS, tm), lambda i: (0, 0, i))
    return pl.pallas_call(
        _peer_topk_body,
        grid=(M // tm,),
        in_specs=[pl.BlockSpec((tm, PEER_HEADS * 2 * N_KEYS), lambda i: (i, 0)),
                  pl.BlockSpec((2, N_KEYS, N_KEYS), lambda i: (0, 0, 0))],
        out_specs=[bspec, bspec, bspec, bspec,
                   pl.BlockSpec((PEER_HEADS, 1, tm), lambda i: (0, 0, i))],
        out_shape=[big, big, big, big, jax.ShapeDtypeStruct((PEER_HEADS, 1, M), F32)],
        scratch_shapes=[pltpu.VMEM((PEER_TOPK * PEER_TOPK, tm), F32)],
        compiler_params=_cparams(("parallel",)),
        name="peer_topk",
    )(query, subkeys)


def _gelu_exact(x):
    return 0.5 * x * (1.0 + lax.erf(x * (2.0 ** -0.5)))


def _peer_dense_body(tn, hf_ref, u_ref, v_ref, s1_ref, s2_ref, e1_ref, e2_ref, tau_ref, o_ref, g_s):
    j = pl.program_id(1)
    tm = hf_ref.shape[0]

    @pl.when(j == 0)
    def _():
        o_ref[...] = jnp.zeros(o_ref.shape, F32)

    for kk in range(tn // N_KEYS):
        i1 = j * (tn // N_KEYS) + kk
        grp = pl.multiple_of((i1 // 8) * 8, 8)
        rsel = lax.broadcasted_iota(jnp.int32, (8, 128), 0) == (i1 % 8)
        for tc in range(tm // 128):
            ts = slice(tc * 128, (tc + 1) * 128)
            acc = jnp.zeros((N_KEYS, 128), F32)
            for h in range(PEER_HEADS):
                s1row = jnp.sum(jnp.where(rsel, s1_ref[h, pl.ds(grp, 8), ts], 0.0),
                                axis=0, keepdims=True)
                e1row = jnp.sum(jnp.where(rsel, e1_ref[h, pl.ds(grp, 8), ts], 0.0),
                                axis=0, keepdims=True)
                c = s2_ref[h, :, ts] + s1row
                acc = acc + jnp.where(c >= tau_ref[h, :, ts], e2_ref[h, :, ts] * e1row, 0.0)
            g_s[ts, kk * N_KEYS:(kk + 1) * N_KEYS] = acc.T
    hmat = lax.dot_general(hf_ref[...], u_ref[...], (((1,), (1,)), ((), ())),
                           preferred_element_type=F32)
    w = (g_s[...] * _gelu_exact(hmat)).astype(BF16)
    o_ref[...] += jnp.dot(w, v_ref[...], preferred_element_type=F32)


def peer_dense(hf, u_tab, v_tab, s1, s2, e1, e2, tau, tm=512, tn=256):
    M = hf.shape[0]
    tm = min(M, tm)
    E = u_tab.shape[0]
    sspec = pl.BlockSpec((PEER_HEADS, N_KEYS, tm), lambda i, j: (0, 0, i))
    return pl.pallas_call(
        functools.partial(_peer_dense_body, tn),
        grid=(M // tm, E // tn),
        in_specs=[
            pl.BlockSpec((tm, D_MODEL), lambda i, j: (i, 0)),
            pl.BlockSpec((tn, D_MODEL), lambda i, j: (j, 0)),
            pl.BlockSpec((tn, D_MODEL), lambda i, j: (j, 0)),
            sspec, sspec, sspec, sspec,
            pl.BlockSpec((PEER_HEADS, 1, tm), lambda i, j: (0, 0, i)),
        ],
        out_specs=pl.BlockSpec((tm, D_MODEL), lambda i, j: (i, 0)),
        out_shape=jax.ShapeDtypeStruct((M, D_MODEL), F32),
        scratch_shapes=[pltpu.VMEM((tm, tn), F32)],
        compiler_params=_cparams(("parallel", "arbitrary")),
        name="peer_dense",
    )(hf, u_tab, v_tab, s1, s2, e1, e2, tau)


def _add_body(a_ref, b_ref, o_ref):
    o_ref[...] = a_ref[...] + b_ref[...]


def add(a, b):
    M, D = a.shape
    tm = min(M, 256)
    spec = pl.BlockSpec((tm, D), lambda i: (i, 0))
    return pl.pallas_call(
        _add_body, grid=(M // tm,), in_specs=[spec, spec], out_specs=spec,
        out_shape=jax.ShapeDtypeStruct((M, D), F32),
        compiler_params=_cparams(("parallel",)), name="residual_add",
    )(a, b)


def peer_ffn(x1, norm_ffn, wq_bf, subkeys, u_bf, v_bf):
    M = x1.shape[0]
    hf = rmsnorm_bf16(x1, norm_ffn)
    Mp = max(M, 128)
    if Mp != M:
        hf = jnp.concatenate([hf, jnp.zeros((Mp - M, D_MODEL), BF16)], axis=0)
    query = matmul(hf, wq_bf, name="peer_query")
    s1, s2, e1, e2, tau = peer_topk(query, subkeys)
    peer = peer_dense(hf, u_bf, v_bf, s1, s2, e1, e2, tau)
    return add(x1, peer[:M])


def _prep_w_in(w_in):
    pad = jnp.zeros((D_MODEL, N_PROJ - OFF_AB - 2 * GDN_HEADS), w_in.dtype)
    ab0 = OFF_AQ
    return jnp.concatenate([w_in[:, :ab0], w_in[:, ab0 + 2 * GDN_HEADS:],
                            w_in[:, ab0:ab0 + 2 * GDN_HEADS], pad], axis=1).astype(BF16)


def _gdn_group_params(a_log, dt_bias):
    pad = jnp.zeros((GDN_HG, 128 - GDN_HB), F32)
    alog = jnp.concatenate([a_log.reshape(GDN_HG, GDN_HB), pad], axis=1).reshape(GDN_HG, 1, 128)
    dtb = jnp.concatenate([dt_bias.reshape(GDN_HG, GDN_HB), pad], axis=1).reshape(GDN_HG, 1, 128)
    return alog, dtb


def _ab_grouped(proj, B, T):
    ab = proj[:, OFF_AB:OFF_AB + 2 * GDN_HEADS].reshape(B, T, 2, GDN_HG, GDN_HB)
    ab = ab.transpose(0, 3, 1, 2, 4).reshape(B, GDN_HG, T, 2 * GDN_HB)
    return jnp.concatenate([ab, jnp.zeros((B, GDN_HG, T, 128 - 2 * GDN_HB), F32)], axis=-1)


def kernel(x_prompt, x_sample, state_ssm, state_conv, cache_kv_g0, cache_kv_g1, cache_kv_g2, rel_bias, norm_mix, w_in, conv_w, a_log, dt_bias, gdn_norm, q_norm, k_norm, w_branch_a, w_branch_b, w_out, norm_ffn, peer_w_query, peer_subkeys, peer_u, peer_v):
    B, T, D = x_prompt.shape
    Bs = x_sample.shape[0]
    assert state_ssm.shape[0] == 1 and x_sample.shape[1] == 1
    caches = (cache_kv_g0[0], cache_kv_g1[0], cache_kv_g2[0])
    w_cat = _prep_w_in(w_in[0])
    wa_bf = w_branch_a[0].astype(BF16)
    wb_bf = w_branch_b[0].astype(BF16)
    wo_bf = w_out[0].astype(BF16)
    wq_bf = peer_w_query[0].astype(BF16)
    u_bf = peer_u[0].astype(BF16)
    v_bf = peer_v[0].astype(BF16)
    cw = conv_w[0]

    xp = x_prompt.reshape(B * T, D)
    xs = x_sample.reshape(Bs, D)

    proj = matmul(rmsnorm_bf16(xp, norm_mix[0]), w_cat, name="in_proj")
    alog_g, dtb_g = _gdn_group_params(a_log[0], dt_bias[0])
    ya, ssm_p = gdn_prompt(proj.reshape(B, T, N_PROJ), _ab_grouped(proj, B, T), cw, alog_g, dtb_g,
                           gdn_norm[0])
    conv_p = proj.reshape(B, T, N_PROJ)[:, T - (CONV_W - 1):, :CONV_CH]
    qn, kv = qk_norm(proj, q_norm[0], k_norm[0])
    kv3 = kv.reshape(B, T, N_GROUPS * 2 * GROUP_W)
    kvs = [kv3[:, :, gi * 2 * GROUP_W:(gi + 1) * 2 * GROUP_W] for gi in range(N_GROUPS)]
    os_, lds = [], []
    for gi in range(N_GROUPS):
        o, ld = attn_prompt_group(qn.reshape(B, T, ATTN_W), kv3,
                                  _prompt_bias_matrix(rel_bias, gi), gi)
        os_.append(o)
        lds.append(ld)
    yb = attn_merge(os_, lds)
    merged = branch_merge(ya.reshape(B * T, GDN_V_W), yb, wa_bf, wb_bf, proj)
    x1 = matmul(merged, wo_bf, residual=xp, name="out_proj")
    y_prompt = peer_ffn(x1, norm_ffn[0], wq_bf, peer_subkeys[0], u_bf, v_bf).reshape(B, T, D)
    p_kv = []
    for gi, (window, _) in enumerate(ATTN_GROUPS):
        keep = min(window, T)
        kvg = kvs[gi][:, T - keep:].reshape(B, keep, 2, GROUP_HEADS, HEAD_DIM)
        p_kv.append(kvg[None])

    proj_s = matmul(rmsnorm_bf16(xs, norm_mix[0]), w_cat, name="in_proj_s")
    ya_s, conv_s, ssm_s = gdn_step(proj_s, state_conv[0], state_ssm[0], cw, a_log[0], dt_bias[0],
                                   gdn_norm[0])
    qn_s, kv_s = qk_norm(proj_s, q_norm[0], k_norm[0])
    kvs_s = [kv_s[:, gi * 2 * GROUP_W:(gi + 1) * 2 * GROUP_W] for gi in range(N_GROUPS)]
    yb_s = attn_decode(qn_s, kvs_s, caches, rel_bias)
    merged_s = branch_merge(ya_s.astype(BF16), yb_s.astype(BF16), wa_bf, wb_bf, proj_s)
    x1_s = matmul(merged_s, wo_bf, residual=xs, name="out_proj_s")
    y_sample = peer_ffn(x1_s, norm_ffn[0], wq_bf, peer_subkeys[0], u_bf, v_bf).reshape(Bs, 1, D)
    s_kv = [kvs_s[gi].reshape(1, Bs, 1, 2, GROUP_HEADS, HEAD_DIM) for gi in range(N_GROUPS)]

    return (y_prompt, y_sample,
            ssm_p[None], conv_p[None], p_kv[0], p_kv[1], p_kv[2],
            ssm_s[None], conv_s[None], s_kv[0], s_kv[1], s_kv[2])
```

```python
import functools

import numpy as np
import jax
import jax.numpy as jnp
from jax import lax
from jax.experimental import pallas as pl
from jax.experimental.pallas import tpu as pltpu

F32 = jnp.float32
BF16 = jnp.bfloat16
HIGHEST = lax.Precision.HIGHEST

D_MODEL = 4096
GDN_HEADS = 16
GDN_DK = 128
GDN_DV = 128
GDN_QK_W = GDN_HEADS * GDN_DK
GDN_V_W = GDN_HEADS * GDN_DV
CONV_W = 4
CONV_CH = 2 * GDN_QK_W + GDN_V_W
GDN_CHUNK = 64
ATTN_GROUPS = ((128, 1), (512, 4), (2048, 16))
N_GROUPS = 3
GROUP_HEADS = 4
HEAD_DIM = 128
GROUP_W = GROUP_HEADS * HEAD_DIM
ATTN_W = N_GROUPS * GROUP_W
Q_BLOCK = 128
N_BUCKETS = 32
MAX_DISTANCE = 2048
PEER_HEADS = 8
N_KEYS = 128
PEER_TOPK = 16
EPS = 1e-6
NEG_INF = -1e30

OFF_QKV = 0
OFF_Z = CONV_CH
OFF_AQ = OFF_Z + GDN_V_W
OFF_AK = OFF_AQ + ATTN_W
OFF_AV = OFF_AK + ATTN_W
OFF_GA = OFF_AV + ATTN_W
OFF_GB = OFF_GA + D_MODEL
OFF_AB = OFF_GB + D_MODEL
N_PROJ = 21504

GDN_HB = 8
GDN_HG = GDN_HEADS // GDN_HB
VMEM_LIMIT = 56 * 1024 * 1024


def _cparams(sem):
    return pltpu.CompilerParams(dimension_semantics=sem, vmem_limit_bytes=VMEM_LIMIT)


def _sigmoid(x):
    return 1.0 / (1.0 + jnp.exp(-x))


def _silu(x):
    return x * _sigmoid(x)


def _softplus(x):
    return jnp.maximum(x, 0.0) + jnp.log(1.0 + jnp.exp(-jnp.abs(x)))


def _bdot(a, b):
    return jnp.dot(a.astype(BF16), b.astype(BF16), preferred_element_type=F32)


def _bdot_nt(a, b):
    return lax.dot_general(a.astype(BF16), b.astype(BF16), (((1,), (1,)), ((), ())),
                           preferred_element_type=F32)


def _bdot_tn(a, b):
    return lax.dot_general(a.astype(BF16), b.astype(BF16), (((0,), (0,)), ((), ())),
                           preferred_element_type=F32)


def _hdot(a, b):
    return jnp.dot(a, b, precision=HIGHEST, preferred_element_type=F32)


def _split_bf16(a):
    hi = a.astype(BF16)
    lo = (a - hi.astype(F32)).astype(BF16)
    return hi, lo


def _dot3(a_parts, b_parts):
    a_hi, a_lo = a_parts
    b_hi, b_lo = b_parts
    return (jnp.dot(a_hi, b_hi, preferred_element_type=F32)
            + jnp.dot(a_lo, b_hi, preferred_element_type=F32)
            + jnp.dot(a_hi, b_lo, preferred_element_type=F32))


def _rmsnorm_body(x_ref, g_ref, o_ref):
    x = x_ref[...]
    ms = jnp.mean(x * x, axis=-1, keepdims=True)
    o_ref[...] = (x * lax.rsqrt(ms + EPS) * g_ref[...]).astype(o_ref.dtype)


def rmsnorm_bf16(x, gain):
    M, D = x.shape
    tm = min(M, 256)
    return pl.pallas_call(
        _rmsnorm_body,
        grid=(M // tm,),
        in_specs=[pl.BlockSpec((tm, D), lambda i: (i, 0)),
                  pl.BlockSpec((1, D), lambda i: (0, 0))],
        out_specs=pl.BlockSpec((tm, D), lambda i: (i, 0)),
        out_shape=jax.ShapeDtypeStruct((M, D), BF16),
        compiler_params=_cparams(("parallel",)),
        name="rmsnorm",
    )(x, gain.reshape(1, D))


def _mm_body(a_ref, b_ref, o_ref):
    o_ref[...] = jnp.dot(a_ref[...], b_ref[...], preferred_element_type=F32)


def _mm_res_body(a_ref, b_ref, r_ref, o_ref):
    o_ref[...] = r_ref[...] + jnp.dot(a_ref[...], b_ref[...], preferred_element_type=F32)


def matmul(a, b, residual=None, tm=1024, tn=512, name="matmul"):
    M, K = a.shape
    N = b.shape[1]
    tm = min(tm, M)
    tn = min(tn, N)
    in_specs = [pl.BlockSpec((tm, K), lambda i, j: (i, 0)),
                pl.BlockSpec((K, tn), lambda i, j: (0, j))]
    args = [a, b]
    body = _mm_body
    if residual is not None:
        in_specs.append(pl.BlockSpec((tm, tn), lambda i, j: (i, j)))
        args.append(residual)
        body = _mm_res_body
    return pl.pallas_call(
        body,
        grid=(M // tm, N // tn),
        in_specs=in_specs,
        out_specs=pl.BlockSpec((tm, tn), lambda i, j: (i, j)),
        out_shape=jax.ShapeDtypeStruct((M, N), F32),
        compiler_params=_cparams(("parallel", "parallel")),
        name=name,
    )(*args)


def _gdn_chunk_body(q_ref, k_ref, v_ref, z_ref, ab_ref, cwq_ref, cwk_ref, cwv_ref,
                    alog_ref, dtb_ref, gn_ref, o_ref, s_out_ref,
                    xq_s, xk_s, xv_s, state_s):
    n = pl.program_id(2)
    C = GDN_CHUNK
    HW = GDN_HB * GDN_DK

    @pl.when(n == 0)
    def _():
        xq_s[0:8, :] = jnp.zeros((8, HW), F32)
        xk_s[0:8, :] = jnp.zeros((8, HW), F32)
        xv_s[0:8, :] = jnp.zeros((8, HW), F32)
        state_s[...] = jnp.zeros(state_s.shape, F32)

    def conv(x_ref, xs, cw_ref):
        xs[8:8 + C, :] = x_ref[0]
        y = cw_ref[CONV_W - 1:CONV_W, :] * xs[8:8 + C, :]
        for w in range(CONV_W - 1):
            y = y + cw_ref[w:w + 1, :] * xs[5 + w:5 + w + C, :]
        xs[0:8, :] = xs[C:C + 8, :]
        return _silu(y)

    q_all = conv(q_ref, xq_s, cwq_ref)
    k_all = conv(k_ref, xk_s, cwk_ref)
    v_all = conv(v_ref, xv_s, cwv_ref)

    ab = ab_ref[0, 0]
    g_all = -jnp.exp(alog_ref[0]) * _softplus(ab + dtb_ref[0])
    beta_all = _sigmoid(ab)
    ri = lax.broadcasted_iota(jnp.int32, (C, C), 0)
    ci = lax.broadcasted_iota(jnp.int32, (C, C), 1)
    tri = ri >= ci
    strict = ri > ci
    gc_all = _hdot(tri.astype(F32), g_all)
    gc_t = jnp.concatenate([gc_all, jnp.zeros((128 - C, 128), F32)], axis=0).T

    H = range(GDN_HB)
    sls = [slice(j * GDN_DK, (j + 1) * GDN_DK) for j in H]
    qh, kh, kb, gcol, decay, rhs = [], [], [], [], [], []
    for j in H:
        q = q_all[:, sls[j]]
        k = k_all[:, sls[j]]
        q = q * lax.rsqrt(jnp.sum(q * q, axis=-1, keepdims=True) + EPS) * (GDN_DK ** -0.5)
        k = k * lax.rsqrt(jnp.sum(k * k, axis=-1, keepdims=True) + EPS)
        beta = beta_all[:, GDN_HB + j:GDN_HB + j + 1]
        gc = gc_all[:, j:j + 1]
        grow = gc_t[j:j + 1, :C]
        qh.append(q)
        kh.append(k)
        kb.append(k * beta)
        gcol.append(gc)
        decay.append(jnp.where(tri, jnp.exp(jnp.where(tri, gc - grow, 0.0)), 0.0))
        rhs.append(jnp.concatenate([v_all[:, sls[j]] * beta, kb[j] * jnp.exp(gc)], axis=-1))
    kq = [_bdot_nt(jnp.concatenate([kb[j], qh[j]], axis=0), kh[j]) for j in H]
    intra = [jnp.where(tri, kq[j][C:] * decay[j], 0.0) for j in H]
    acc = [-jnp.where(strict, kq[j][:C] * decay[j], 0.0) for j in H]
    parts = [_split_bf16(acc[j]) for j in H]
    pw = [_dot3(parts[j], parts[j]) for j in H]
    for r in range(5):
        pw_parts = [_split_bf16(pw[j]) for j in H]
        acc_parts = [_split_bf16(acc[j]) for j in H]
        acc = [acc[j] + pw[j] + _dot3(pw_parts[j], acc_parts[j]) for j in H]
        if r < 4:
            pw = [_dot3(pw_parts[j], pw_parts[j]) for j in H]
    sol = [rhs[j] + _dot3(_split_bf16(acc[j]), _split_bf16(rhs[j])) for j in H]
    S = [state_s[j] for j in H]
    wq_s = [_bdot(jnp.concatenate([sol[j][:, GDN_DV:], qh[j] * jnp.exp(gcol[j])], axis=0), S[j])
            for j in H]
    v_new = [sol[j][:, :GDN_DV] - wq_s[j][:C] for j in H]
    o = [wq_s[j][C:] + _bdot(intra[j], v_new[j]) for j in H]
    for j in H:
        g_last = gcol[j][C - 1:C, :]
        state_s[j] = (S[j] * jnp.exp(g_last)
                      + _bdot_tn(kh[j] * jnp.exp(g_last - gcol[j]), v_new[j]))
    for j in H:
        ms = jnp.mean(o[j] * o[j], axis=-1, keepdims=True)
        on = o[j] * lax.rsqrt(ms + EPS) * gn_ref[...]
        o_ref[0, :, sls[j]] = (on * _silu(z_ref[0, :, sls[j]])).astype(o_ref.dtype)

    @pl.when(n == pl.num_programs(2) - 1)
    def _():
        s_out_ref[0] = state_s[...]


def gdn_prompt(proj3, ab_g, conv_w, alog_g, dtb_g, gdn_norm):
    B, T, _ = proj3.shape
    C = GDN_CHUNK
    N = T // C
    HW = GDN_HB * GDN_DK
    nq = GDN_QK_W // HW

    def col(base):
        return lambda b, hg, n: (b, n, base + hg)

    def cwcol(base):
        return lambda b, hg, n: (0, base + hg)

    return pl.pallas_call(
        _gdn_chunk_body,
        grid=(B, GDN_HG, N),
        in_specs=[
            pl.BlockSpec((1, C, HW), col(0)),
            pl.BlockSpec((1, C, HW), col(nq)),
            pl.BlockSpec((1, C, HW), col(2 * nq)),
            pl.BlockSpec((1, C, HW), col(OFF_Z // HW)),
            pl.BlockSpec((1, 1, C, 128), lambda b, hg, n: (b, hg, n, 0)),
            pl.BlockSpec((CONV_W, HW), cwcol(0)),
            pl.BlockSpec((CONV_W, HW), cwcol(nq)),
            pl.BlockSpec((CONV_W, HW), cwcol(2 * nq)),
            pl.BlockSpec((1, 1, 128), lambda b, hg, n: (hg, 0, 0)),
            pl.BlockSpec((1, 1, 128), lambda b, hg, n: (hg, 0, 0)),
            pl.BlockSpec((1, GDN_DV), lambda b, hg, n: (0, 0)),
        ],
        out_specs=[
            pl.BlockSpec((1, C, HW), lambda b, hg, n: (b, n, hg)),
            pl.BlockSpec((1, GDN_HB, GDN_DK, GDN_DV), lambda b, hg, n: (b, hg, 0, 0)),
        ],
        out_shape=[
            jax.ShapeDtypeStruct((B, T, GDN_V_W), BF16),
            jax.ShapeDtypeStruct((B, GDN_HEADS, GDN_DK, GDN_DV), F32),
        ],
        scratch_shapes=[
            pltpu.VMEM((C + 8, HW), F32),
            pltpu.VMEM((C + 8, HW), F32),
            pltpu.VMEM((C + 8, HW), F32),
            pltpu.VMEM((GDN_HB, GDN_DK, GDN_DV), F32),
        ],
        compiler_params=_cparams(("parallel", "parallel", "arbitrary")),
        name="gdn_prompt",
    )(proj3, proj3, proj3, proj3, ab_g, conv_w, conv_w, conv_w, alog_g, dtb_g,
      gdn_norm.reshape(1, GDN_DV))


def _gdn_step_body(x_ref, z_ref, ab_ref, cs_ref, s_ref, cw_ref, alog_ref, dtb_ref, gn_ref,
                   o_ref, cs_out_ref, s_out_ref):
    x = x_ref[0]
    y = cw_ref[CONV_W - 1] * x
    for w in range(CONV_W - 1):
        y = y + cw_ref[w] * cs_ref[0, w]
    y = _silu(y)
    cs_out_ref[0, 0] = cs_ref[0, 1]
    cs_out_ref[0, 1] = cs_ref[0, 2]
    cs_out_ref[0, 2] = x
    r_ab = (OFF_AB // 128) % 8
    ab = ab_ref[0, r_ab:r_ab + 1, :]
    g_all = -jnp.exp(alog_ref[...]) * _softplus(ab + dtb_ref[...])
    beta_all = _sigmoid(ab)
    eye = (lax.broadcasted_iota(jnp.int32, (GDN_DK, GDN_DK), 0)
           == lax.broadcasted_iota(jnp.int32, (GDN_DK, GDN_DK), 1))

    def to_col(row):
        return jnp.sum(jnp.where(eye, jnp.broadcast_to(row, (GDN_DK, GDN_DK)), 0.0),
                       axis=1, keepdims=True)

    for h in range(GDN_HEADS):
        q = y[h:h + 1, :]
        k = y[GDN_HEADS + h:GDN_HEADS + h + 1, :]
        v = y[2 * GDN_HEADS + h:2 * GDN_HEADS + h + 1, :]
        q = q * lax.rsqrt(jnp.sum(q * q, axis=-1, keepdims=True) + EPS) * (GDN_DK ** -0.5)
        k = k * lax.rsqrt(jnp.sum(k * k, axis=-1, keepdims=True) + EPS)
        g = g_all[:, h:h + 1]
        beta = beta_all[:, GDN_HEADS + h:GDN_HEADS + h + 1]
        eg = jnp.exp(g)
        S = s_ref[0, h]
        kcol = to_col(k)
        qcol = to_col(q)
        k_s = jnp.sum(S * kcol, axis=0, keepdims=True)
        q_s = jnp.sum(S * qcol, axis=0, keepdims=True)
        v_new = v * beta - (beta * eg) * k_s
        o = eg * q_s + jnp.sum(q * k, axis=-1, keepdims=True) * v_new
        s_out_ref[0, h] = S * eg + kcol * v_new
        ms = jnp.mean(o * o, axis=-1, keepdims=True)
        on = o * lax.rsqrt(ms + EPS) * gn_ref[...]
        o_ref[0, h:h + 1, :] = on * _silu(z_ref[0, h:h + 1, :])


def gdn_step(proj_s, conv_state, ssm_state, conv_w, a_log, dt_bias, gdn_norm):
    Bs = proj_s.shape[0]
    nrow = N_PROJ // 128
    p3 = proj_s.reshape(Bs, nrow, 128)
    cs4 = conv_state.reshape(Bs, CONV_W - 1, CONV_CH // 128, 128)
    cw3 = conv_w.reshape(CONV_W, CONV_CH // 128, 128)
    pad = jnp.zeros((128 - GDN_HEADS,), F32)
    alog = jnp.concatenate([a_log, pad]).reshape(1, 128)
    dtb = jnp.concatenate([dt_bias, pad]).reshape(1, 128)
    nr = CONV_CH // 128
    o, cs_new, s_new = pl.pallas_call(
        _gdn_step_body,
        grid=(Bs,),
        in_specs=[
            pl.BlockSpec((1, nr, 128), lambda b: (b, 0, 0)),
            pl.BlockSpec((1, GDN_HEADS, 128), lambda b: (b, OFF_Z // 128 // GDN_HEADS, 0)),
            pl.BlockSpec((1, 8, 128), lambda b: (b, OFF_AB // 128 // 8, 0)),
            pl.BlockSpec((1, CONV_W - 1, nr, 128), lambda b: (b, 0, 0, 0)),
            pl.BlockSpec((1, GDN_HEADS, GDN_DK, GDN_DV), lambda b: (b, 0, 0, 0)),
            pl.BlockSpec((CONV_W, nr, 128), lambda b: (0, 0, 0)),
            pl.BlockSpec((1, 128), lambda b: (0, 0)),
            pl.BlockSpec((1, 128), lambda b: (0, 0)),
            pl.BlockSpec((1, GDN_DV), lambda b: (0, 0)),
        ],
        out_specs=[
            pl.BlockSpec((1, GDN_HEADS, GDN_DV), lambda b: (b, 0, 0)),
            pl.BlockSpec((1, CONV_W - 1, nr, 128), lambda b: (b, 0, 0, 0)),
            pl.BlockSpec((1, GDN_HEADS, GDN_DK, GDN_DV), lambda b: (b, 0, 0, 0)),
        ],
        out_shape=[
            jax.ShapeDtypeStruct((Bs, GDN_HEADS, GDN_DV), F32),
            jax.ShapeDtypeStruct((Bs, CONV_W - 1, nr, 128), F32),
            jax.ShapeDtypeStruct((Bs, GDN_HEADS, GDN_DK, GDN_DV), F32),
        ],
        compiler_params=_cparams(("parallel",)),
        name="gdn_step",
    )(p3, p3, p3, cs4, ssm_state, cw3, alog, dtb, gdn_norm.reshape(1, GDN_DV))
    return (o.reshape(Bs, GDN_V_W), cs_new.reshape(Bs, CONV_W - 1, CONV_CH), s_new)


def _qknorm_body(q_ref, k_ref, v_ref, qg_ref, kg_ref, qn_ref, kv_ref):
    for h in range(GROUP_HEADS):
        sl = slice(h * HEAD_DIM, (h + 1) * HEAD_DIM)
        q = q_ref[:, sl]
        qn = q * lax.rsqrt(jnp.mean(q * q, axis=-1, keepdims=True) + EPS) * qg_ref[...]
        qn_ref[:, sl] = qn * (HEAD_DIM ** -0.5)
        k = k_ref[:, sl]
        kv_ref[:, sl] = k * lax.rsqrt(jnp.mean(k * k, axis=-1, keepdims=True) + EPS) * kg_ref[...]
    kv_ref[:, GROUP_W:] = v_ref[...]


def qk_norm(proj, q_norm, k_norm):
    M = proj.shape[0]
    tm = min(M, 512)
    blk = GROUP_W
    return pl.pallas_call(
        _qknorm_body,
        grid=(M // tm, N_GROUPS),
        in_specs=[
            pl.BlockSpec((tm, blk), lambda i, g: (i, OFF_AQ // blk + g)),
            pl.BlockSpec((tm, blk), lambda i, g: (i, OFF_AK // blk + g)),
            pl.BlockSpec((tm, blk), lambda i, g: (i, OFF_AV // blk + g)),
            pl.BlockSpec((1, HEAD_DIM), lambda i, g: (0, 0)),
            pl.BlockSpec((1, HEAD_DIM), lambda i, g: (0, 0)),
        ],
        out_specs=[
            pl.BlockSpec((tm, blk), lambda i, g: (i, g)),
            pl.BlockSpec((tm, 2 * blk), lambda i, g: (i, g)),
        ],
        out_shape=[jax.ShapeDtypeStruct((M, ATTN_W), F32),
                   jax.ShapeDtypeStruct((M, N_GROUPS * 2 * GROUP_W), F32)],
        compiler_params=_cparams(("parallel", "parallel")),
        name="qk_norm",
    )(proj, proj, proj, q_norm.reshape(1, HEAD_DIM), k_norm.reshape(1, HEAD_DIM))


def _t5_bucket(dist):
    max_exact = N_BUCKETS // 2
    d = np.maximum(dist, max_exact).astype(np.float32)
    large = max_exact + (np.log(d / max_exact) / np.log(MAX_DISTANCE / max_exact)
                         * (N_BUCKETS - max_exact)).astype(np.int32)
    large = np.minimum(large, N_BUCKETS - 1)
    return np.where(dist < max_exact, dist, large).astype(np.int32)


def _group_bias(rel_bias, gi):
    window, dilation = ATTN_GROUPS[gi]
    n_keys = window // dilation + 1
    buckets = _t5_bucket(np.arange(n_keys) * dilation)
    return rel_bias[buckets][:, gi * GROUP_HEADS:(gi + 1) * GROUP_HEADS].T.astype(F32)


def _attn_prompt_body(has_prev, q_ref, kc_ref, vc_ref, kp_ref, vp_ref, bias_ref, o_ref, ld_ref):
    i = pl.program_id(2)
    QB = Q_BLOCK
    lane = lax.broadcasted_iota(jnp.int32, (QB, 128), 1)
    ld_all = jnp.zeros((QB, 128), F32)
    if has_prev:
        colid = lax.broadcasted_iota(jnp.int32, (QB, 2 * QB), 1)
        no_prev = jnp.logical_and(colid < QB, i == 0)
    for h in range(GROUP_HEADS):
        sl = slice(h * HEAD_DIM, (h + 1) * HEAD_DIM)
        q = q_ref[0, :, sl].astype(BF16)
        if has_prev:
            kcat = jnp.concatenate([kp_ref[0, :, sl], kc_ref[0, :, sl]], axis=0).astype(BF16)
            vcat = jnp.concatenate([vp_ref[0, :, sl], vc_ref[0, :, sl]], axis=0).astype(BF16)
            s = lax.dot_general(q, kcat, (((1,), (1,)), ((), ())), preferred_element_type=F32)
            s = s + bias_ref[h]
            s = jnp.where(no_prev, NEG_INF, s)
        else:
            kcat = kc_ref[0, :, sl].astype(BF16)
            vcat = vc_ref[0, :, sl].astype(BF16)
            s = lax.dot_general(q, kcat, (((1,), (1,)), ((), ())), preferred_element_type=F32)
            s = s + bias_ref[h, :, QB:]
        m = jnp.max(s, axis=-1, keepdims=True)
        p = jnp.exp(s - m)
        l = jnp.sum(p, axis=-1, keepdims=True)
        o = jnp.dot(p.astype(BF16), vcat, preferred_element_type=F32)
        o_ref[0, :, sl] = o / l
        ld = m + jnp.log(l)
        ld_all = jnp.where(lane == h, ld, ld_all)
    ld_ref[0] = ld_all


def attn_prompt_group(qn3, kv3, bias_mat, gi):
    B, T, _ = qn3.shape
    window, d = ATTN_GROUPS[gi]
    Ts = T // d
    nq = Ts // Q_BLOCK
    has_prev = nq > 1
    qv = qn3.reshape(B, Ts, d * ATTN_W)
    kvv = kv3.reshape(B, Ts, d * N_GROUPS * 2 * GROUP_W)
    blk = (1, Q_BLOCK, GROUP_W)
    nkv = 2 * N_GROUPS
    in_specs = [
        pl.BlockSpec(blk, lambda b, r, i: (b, i, r * N_GROUPS + gi)),
        pl.BlockSpec(blk, lambda b, r, i: (b, i, nkv * r + 2 * gi)),
        pl.BlockSpec(blk, lambda b, r, i: (b, i, nkv * r + 2 * gi + 1)),
        pl.BlockSpec(blk, lambda b, r, i: (b, jnp.maximum(i - 1, 0), nkv * r + 2 * gi)),
        pl.BlockSpec(blk, lambda b, r, i: (b, jnp.maximum(i - 1, 0), nkv * r + 2 * gi + 1)),
        pl.BlockSpec((GROUP_HEADS, Q_BLOCK, 2 * Q_BLOCK), lambda b, r, i: (0, 0, 0)),
    ]
    o, ld = pl.pallas_call(
        functools.partial(_attn_prompt_body, has_prev),
        grid=(B, d, nq),
        in_specs=in_specs,
        out_specs=[
            pl.BlockSpec(blk, lambda b, r, i: (b, i, r)),
            pl.BlockSpec((1, Q_BLOCK, 128), lambda b, r, i: (b, i, r)),
        ],
        out_shape=[
            jax.ShapeDtypeStruct((B, Ts, d * GROUP_W), F32),
            jax.ShapeDtypeStruct((B, Ts, d * 128), F32),
        ],
        compiler_params=_cparams(("parallel", "parallel", "arbitrary")),
        name=f"attn_prompt_g{gi}",
    )(qv, kvv, kvv, kvv, kvv, bias_mat)
    return o.reshape(B * T, GROUP_W), ld.reshape(B * T, 128)


def _prompt_bias_matrix(rel_bias, gi):
    QB = Q_BLOCK
    bias = _group_bias(rel_bias, gi)
    row0 = jnp.concatenate([bias[:, ::-1], jnp.full((GROUP_HEADS, QB), NEG_INF, F32)], axis=1)
    flat = jnp.tile(row0, (1, QB))[:, :QB * 2 * QB]
    return flat.reshape(GROUP_HEADS, QB, 2 * QB)


def _attn_merge_body(o0_ref, o1_ref, o2_ref, l0_ref, l1_ref, l2_ref, y_ref):
    for h in range(GROUP_HEADS):
        sl = slice(h * HEAD_DIM, (h + 1) * HEAD_DIM)
        l0 = l0_ref[:, h:h + 1]
        l1 = l1_ref[:, h:h + 1]
        l2 = l2_ref[:, h:h + 1]
        m = jnp.maximum(jnp.maximum(l0, l1), l2)
        e0 = jnp.exp(l0 - m)
        e1 = jnp.exp(l1 - m)
        e2 = jnp.exp(l2 - m)
        den = e0 + e1 + e2
        y = (o0_ref[:, sl] * (e0 / den) + o1_ref[:, sl] * (e1 / den) + o2_ref[:, sl] * (e2 / den))
        y_ref[:, sl] = y.astype(y_ref.dtype)


def attn_merge(os_, lds):
    M = os_[0].shape[0]
    tm = min(M, 512)
    ospec = pl.BlockSpec((tm, GROUP_W), lambda i: (i, 0))
    lspec = pl.BlockSpec((tm, 128), lambda i: (i, 0))
    return pl.pallas_call(
        _attn_merge_body,
        grid=(M // tm,),
        in_specs=[ospec, ospec, ospec, lspec, lspec, lspec],
        out_specs=ospec,
        out_shape=jax.ShapeDtypeStruct((M, GROUP_W), BF16),
        compiler_params=_cparams(("parallel",)),
        name="attn_merge",
    )(*os_, *lds)


def _attn_decode_body(q_ref, n0_ref, n1_ref, n2_ref, c0_ref, c1_ref, c2_ref, bc_ref, bn_ref, y_ref):
    new_refs = (n0_ref, n1_ref, n2_ref)
    cache_refs = (c0_ref, c1_ref, c2_ref)
    outs, lds = [], []
    for gi in range(N_GROUPS):
        q = q_ref[0, gi * GROUP_HEADS:(gi + 1) * GROUP_HEADS, :]
        kc = cache_refs[gi][:, 0]
        vc = cache_refs[gi][:, 1]
        kn = new_refs[gi][0, 0:GROUP_HEADS, :]
        vn = new_refs[gi][0, GROUP_HEADS:2 * GROUP_HEADS, :]
        s_c = jnp.sum(kc * q[None], axis=-1, keepdims=True) + bc_ref[gi, :, :, 0:1]
        s_n = jnp.sum(kn * q, axis=-1, keepdims=True) + bn_ref[gi, :, 0:1]
        m = jnp.maximum(jnp.max(s_c, axis=0), s_n)
        p_c = jnp.exp(s_c - m[None])
        p_n = jnp.exp(s_n - m)
        l = jnp.sum(p_c, axis=0) + p_n
        outs.append((jnp.sum(p_c * vc, axis=0) + p_n * vn) / l)
        lds.append(m + jnp.log(l))
    m = jnp.maximum(jnp.maximum(lds[0], lds[1]), lds[2])
    es = [jnp.exp(x - m) for x in lds]
    den = es[0] + es[1] + es[2]
    y_ref[0] = outs[0] * (es[0] / den) + outs[1] * (es[1] / den) + outs[2] * (es[2] / den)


def attn_decode(qn_s, kv_new, caches, rel_bias):
    Bs = qn_s.shape[0]
    q3 = qn_s.reshape(Bs, N_GROUPS * GROUP_HEADS, HEAD_DIM)
    new3 = [x.reshape(Bs, 2 * GROUP_HEADS, HEAD_DIM) for x in kv_new]
    cviews, bcs, bns = [], [], []
    for gi, (window, d) in enumerate(ATTN_GROUPS):
        nk = window // d
        assert caches[gi].shape[1] == window and nk == Q_BLOCK
        cviews.append(caches[gi].reshape(Bs, nk, d, 2, GROUP_HEADS, HEAD_DIM))
        bias = _group_bias(rel_bias, gi)
        bc = bias[:, nk - np.arange(nk)]
        bcs.append(jnp.broadcast_to(bc.T[:, :, None], (nk, GROUP_HEADS, HEAD_DIM)))
        bns.append(jnp.broadcast_to(bias[:, 0:1], (GROUP_HEADS, HEAD_DIM)))
    bc_all = jnp.stack(bcs, 0)
    bn_all = jnp.stack(bns, 0)
    nspec = pl.BlockSpec((1, 2 * GROUP_HEADS, HEAD_DIM), lambda b: (b, 0, 0))
    cspec = pl.BlockSpec((None, Q_BLOCK, None, 2, GROUP_HEADS, HEAD_DIM),
                         lambda b: (b, 0, 0, 0, 0, 0))
    y = pl.pallas_call(
        _attn_decode_body,
        grid=(Bs,),
        in_specs=[
            pl.BlockSpec((1, N_GROUPS * GROUP_HEADS, HEAD_DIM), lambda b: (b, 0, 0)),
            nspec, nspec, nspec, cspec, cspec, cspec,
            pl.BlockSpec((N_GROUPS, Q_BLOCK, GROUP_HEADS, HEAD_DIM), lambda b: (0, 0, 0, 0)),
            pl.BlockSpec((N_GROUPS, GROUP_HEADS, HEAD_DIM), lambda b: (0, 0, 0)),
        ],
        out_specs=pl.BlockSpec((1, GROUP_HEADS, HEAD_DIM), lambda b: (b, 0, 0)),
        out_shape=jax.ShapeDtypeStruct((Bs, GROUP_HEADS, HEAD_DIM), F32),
        compiler_params=_cparams(("parallel",)),
        name="attn_decode",
    )(q3, *new3, *cviews, bc_all, bn_all)
    return y.reshape(Bs, GROUP_W)


def _merge_body(ya_ref, yb_ref, wa_ref, wb_ref, ga_ref, gb_ref, o_ref):
    pa = jnp.dot(ya_ref[...], wa_ref[...], preferred_element_type=F32)
    pb = jnp.dot(yb_ref[...], wb_ref[...], preferred_element_type=F32)
    o_ref[...] = (_sigmoid(ga_ref[...]) * pa + _sigmoid(gb_ref[...]) * pb).astype(o_ref.dtype)


def branch_merge(ya, yb, wa, wb, proj):
    M = ya.shape[0]
    tm = min(M, 1024)
    tn = 512
    return pl.pallas_call(
        _merge_body,
        grid=(M // tm, D_MODEL // tn),
        in_specs=[
            pl.BlockSpec((tm, GDN_V_W), lambda i, j: (i, 0)),
            pl.BlockSpec((tm, GROUP_W), lambda i, j: (i, 0)),
            pl.BlockSpec((GDN_V_W, tn), lambda i, j: (0, j)),
            pl.BlockSpec((GROUP_W, tn), lambda i, j: (0, j)),
            pl.BlockSpec((tm, tn), lambda i, j: (i, OFF_GA // tn + j)),
            pl.BlockSpec((tm, tn), lambda i, j: (i, OFF_GB // tn + j)),
        ],
        out_specs=pl.BlockSpec((tm, tn), lambda i, j: (i, j)),
        out_shape=jax.ShapeDtypeStruct((M, D_MODEL), BF16),
        compiler_params=_cparams(("parallel", "parallel")),
        name="branch_merge",
    )(ya, yb, wa, wb, proj, proj)


def _extract_top(x, count):
    R = x.shape[0]
    rid = lax.broadcasted_iota(jnp.int32, x.shape, 0).astype(F32)
    vid = lax.broadcasted_iota(jnp.int32, (count, x.shape[1]), 0)
    vals = jnp.zeros((count, x.shape[1]), F32)
    for a in range(count):
        m = jnp.max(x, axis=0, keepdims=True)
        first = jnp.min(jnp.where(x == m, rid, float(R)), axis=0, keepdims=True)
        x = jnp.where(rid == first, -jnp.inf, x)
        vals = jnp.where(vid == a, m, vals)
    return vals, x == -jnp.inf


def _peer_topk_body(q_ref, sk_ref, s1_ref, s2_ref, e1_ref, e2_ref, tau_ref, cand_s):
    K = PEER_TOPK

    def head(h, carry):
        off = pl.multiple_of(h * (2 * N_KEYS), 2 * N_KEYS)
        q1 = q_ref[:, pl.ds(off, N_KEYS)]
        q2 = q_ref[:, pl.ds(pl.multiple_of(off + N_KEYS, N_KEYS), N_KEYS)]
        s1 = lax.dot_general(sk_ref[0], q1, (((1,), (1,)), ((), ())), precision=HIGHEST,
                             preferred_element_type=F32)
        s2 = lax.dot_general(sk_ref[1], q2, (((1,), (1,)), ((), ())), precision=HIGHEST,
                             preferred_element_type=F32)
        v1, sel1 = _extract_top(s1, K)
        v2, sel2 = _extract_top(s2, K)
        for a in range(K):
            cand_s[a * K:(a + 1) * K, :] = v2 + v1[a:a + 1, :]
        best, _ = _extract_top(cand_s[...], K)
        cmax = best[0:1, :]
        z = jnp.sum(jnp.exp(best - cmax), axis=0, keepdims=True)
        s1_ref[h] = s1
        s2_ref[h] = s2
        e1_ref[h] = jnp.where(sel1, jnp.exp(s1 - v1[0:1, :]), 0.0)
        e2_ref[h] = jnp.where(sel2, jnp.exp(s2 - v2[0:1, :]), 0.0) / z
        tau_ref[h] = best[K - 1:K, :]
        return carry

    lax.fori_loop(0, PEER_HEADS, head, 0)


def peer_topk(query, subkeys):
    M = query.shape[0]
    tm = min(M, 256)
    big = jax.ShapeDtypeStruct((PEER_HEADS, N_KEYS, M), F32)
    bspec = pl.BlockSpec((PEER_HEADS, N_KEYS, tm), lambda i: (0, 0, i))
    return pl.pallas_call(
        _peer_topk_body,
        grid=(M // tm,),
        in_specs=[pl.BlockSpec((tm, PEER_HEADS * 2 * N_KEYS), lambda i: (i, 0)),
                  pl.BlockSpec((2, N_KEYS, N_KEYS), lambda i: (0, 0, 0))],
        out_specs=[bspec, bspec, bspec, bspec,
                   pl.BlockSpec((PEER_HEADS, 1, tm), lambda i: (0, 0, i))],
        out_shape=[big, big, big, big, jax.ShapeDtypeStruct((PEER_HEADS, 1, M), F32)],
        scratch_shapes=[pltpu.VMEM((PEER_TOPK * PEER_TOPK, tm), F32)],
        compiler_params=_cparams(("parallel",)),
        name="peer_topk",
    )(query, subkeys)


def _gelu_exact(x):
    return 0.5 * x * (1.0 + lax.erf(x * (2.0 ** -0.5)))


def _peer_gate_body(tn, hf_ref, u_ref, s1_ref, s2_ref, e1_ref, e2_ref, tau_ref, w_ref, g_s):
    j = pl.program_id(1)
    tm = hf_ref.shape[0]

    for kk in range(tn // N_KEYS):
        i1 = j * (tn // N_KEYS) + kk
        grp = pl.multiple_of((i1 // 8) * 8, 8)
        rsel = lax.broadcasted_iota(jnp.int32, (8, 128), 0) == (i1 % 8)
        for tc in range(tm // 128):
            ts = slice(tc * 128, (tc + 1) * 128)
            acc = jnp.zeros((N_KEYS, 128), F32)
            for h in range(PEER_HEADS):
                s1row = jnp.sum(jnp.where(rsel, s1_ref[h, pl.ds(grp, 8), ts], 0.0),
                                axis=0, keepdims=True)
                e1row = jnp.sum(jnp.where(rsel, e1_ref[h, pl.ds(grp, 8), ts], 0.0),
                                axis=0, keepdims=True)
                c = s2_ref[h, :, ts] + s1row
                acc = acc + jnp.where(c >= tau_ref[h, :, ts], e2_ref[h, :, ts] * e1row, 0.0)
            g_s[ts, kk * N_KEYS:(kk + 1) * N_KEYS] = acc.T
    hmat = lax.dot_general(hf_ref[...], u_ref[...], (((1,), (1,)), ((), ())),
                           preferred_element_type=F32)
    w_ref[...] = (g_s[...] * _gelu_exact(hmat)).astype(w_ref.dtype)


def peer_gate_weights(hf, u_tab, s1, s2, e1, e2, tau, tm=512, tn=512):
    M = hf.shape[0]
    tm = min(M, tm)
    E = u_tab.shape[0]
    sspec = pl.BlockSpec((PEER_HEADS, N_KEYS, tm), lambda i, j: (0, 0, i))
    return pl.pallas_call(
        functools.partial(_peer_gate_body, tn),
        grid=(M // tm, E // tn),
        in_specs=[
            pl.BlockSpec((tm, D_MODEL), lambda i, j: (i, 0)),
            pl.BlockSpec((tn, D_MODEL), lambda i, j: (j, 0)),
            sspec, sspec, sspec, sspec,
            pl.BlockSpec((PEER_HEADS, 1, tm), lambda i, j: (0, 0, i)),
        ],
        out_specs=pl.BlockSpec((tm, tn), lambda i, j: (i, j)),
        out_shape=jax.ShapeDtypeStruct((M, E), BF16),
        scratch_shapes=[pltpu.VMEM((tm, tn), F32)],
        compiler_params=_cparams(("parallel", "arbitrary")),
        name="peer_gate",
    )(hf, u_tab, s1, s2, e1, e2, tau)


def _mm_kres_body(a_ref, b_ref, r_ref, o_ref):
    k = pl.program_id(2)

    @pl.when(k == 0)
    def _():
        o_ref[...] = r_ref[...]

    o_ref[...] += jnp.dot(a_ref[...], b_ref[...], preferred_element_type=F32)


def matmul_kres(a, b, residual, tm=1024, tn=1024, tk=2048, name="matmul_k"):
    M, K = a.shape
    N = b.shape[1]
    tm = min(tm, M)
    return pl.pallas_call(
        _mm_kres_body,
        grid=(M // tm, N // tn, K // tk),
        in_specs=[pl.BlockSpec((tm, tk), lambda i, j, k: (i, k)),
                  pl.BlockSpec((tk, tn), lambda i, j, k: (k, j)),
                  pl.BlockSpec((tm, tn), lambda i, j, k: (i, j))],
        out_specs=pl.BlockSpec((tm, tn), lambda i, j, k: (i, j)),
        out_shape=jax.ShapeDtypeStruct((M, N), F32),
        compiler_params=_cparams(("parallel", "parallel", "arbitrary")),
        name=name,
    )(a, b, residual)


def peer_ffn(x1, norm_ffn, wq_bf, subkeys, u_bf, v_bf):
    M = x1.shape[0]
    Mp = max(M, 128)
    if Mp != M:
        x1 = jnp.concatenate([x1, jnp.zeros((Mp - M, D_MODEL), F32)], axis=0)
    hf = rmsnorm_bf16(x1, norm_ffn)
    query = matmul(hf, wq_bf, name="peer_query")
    s1, s2, e1, e2, tau = peer_topk(query, subkeys)
    w = peer_gate_weights(hf, u_bf, s1, s2, e1, e2, tau)
    return matmul_kres(w, v_bf, x1, name="peer_out")[:M]


W_BLK = 512
N_AB = 2 * GDN_HEADS
J_AB = OFF_AB // W_BLK


def _prep_w_in_body(a_ref, b_ref, o_ref):
    j = pl.program_id(1)
    tr = a_ref.shape[0]

    @pl.when(j < OFF_AQ // W_BLK)
    def _():
        o_ref[...] = a_ref[...].astype(o_ref.dtype)

    @pl.when(jnp.logical_and(j >= OFF_AQ // W_BLK, j < J_AB))
    def _():
        cat = jnp.concatenate([a_ref[...], b_ref[...]], axis=1)
        o_ref[...] = pltpu.roll(cat, W_BLK + 128 - N_AB, axis=1)[:, :W_BLK].astype(o_ref.dtype)

    @pl.when(j == J_AB)
    def _():
        lane = lax.broadcasted_iota(jnp.int32, (tr, W_BLK), 1)
        ext = jnp.concatenate([b_ref[...], jnp.zeros((tr, W_BLK - 128), F32)], axis=1)
        o_ref[...] = jnp.where(lane < N_AB, ext, 0.0).astype(o_ref.dtype)


def _prep_w_in(w_in):
    D = w_in.shape[0]
    tr = 2048
    sub = W_BLK // 128
    j0 = OFF_AQ // W_BLK
    return pl.pallas_call(
        _prep_w_in_body,
        grid=(D // tr, N_PROJ // W_BLK),
        in_specs=[
            pl.BlockSpec((tr, W_BLK), lambda i, j: (i, jnp.where(j == J_AB, j0, j))),
            pl.BlockSpec((tr, 128), lambda i, j: (i, jnp.where(j == J_AB, j0 * sub,
                                                               jnp.where(j >= j0, (j + 1) * sub, 0)))),
        ],
        out_specs=pl.BlockSpec((tr, W_BLK), lambda i, j: (i, j)),
        out_shape=jax.ShapeDtypeStruct((D, N_PROJ), BF16),
        compiler_params=_cparams(("parallel", "parallel")),
        name="prep_w_in",
    )(w_in, w_in)


def _gdn_group_params(a_log, dt_bias):
    pad = jnp.zeros((GDN_HG, 128 - GDN_HB), F32)
    alog = jnp.concatenate([a_log.reshape(GDN_HG, GDN_HB), pad], axis=1).reshape(GDN_HG, 1, 128)
    dtb = jnp.concatenate([dt_bias.reshape(GDN_HG, GDN_HB), pad], axis=1).reshape(GDN_HG, 1, 128)
    return alog, dtb


def _ab_grouped(proj, B, T):
    ab = proj[:, OFF_AB:OFF_AB + 2 * GDN_HEADS].reshape(B, T, 2, GDN_HG, GDN_HB)
    ab = ab.transpose(0, 3, 1, 2, 4).reshape(B, GDN_HG, T, 2 * GDN_HB)
    return jnp.concatenate([ab, jnp.zeros((B, GDN_HG, T, 128 - 2 * GDN_HB), F32)], axis=-1)


def kernel(x_prompt, x_sample, state_ssm, state_conv, cache_kv_g0, cache_kv_g1, cache_kv_g2, rel_bias, norm_mix, w_in, conv_w, a_log, dt_bias, gdn_norm, q_norm, k_norm, w_branch_a, w_branch_b, w_out, norm_ffn, peer_w_query, peer_subkeys, peer_u, peer_v):
    B, T, D = x_prompt.shape
    Bs = x_sample.shape[0]
    assert state_ssm.shape[0] == 1 and x_sample.shape[1] == 1
    caches = (cache_kv_g0[0], cache_kv_g1[0], cache_kv_g2[0])
    w_cat = _prep_w_in(w_in[0])
    wa_bf = w_branch_a[0].astype(BF16)
    wb_bf = w_branch_b[0].astype(BF16)
    wo_bf = w_out[0].astype(BF16)
    wq_bf = peer_w_query[0].astype(BF16)
    u_bf = peer_u[0].astype(BF16)
    v_bf = peer_v[0].astype(BF16)
    cw = conv_w[0]

    xp = x_prompt.reshape(B * T, D)
    xs = x_sample.reshape(Bs, D)

    proj = matmul(rmsnorm_bf16(xp, norm_mix[0]), w_cat, name="in_proj")
    alog_g, dtb_g = _gdn_group_params(a_log[0], dt_bias[0])
    ya, ssm_p = gdn_prompt(proj.reshape(B, T, N_PROJ), _ab_grouped(proj, B, T), cw, alog_g, dtb_g,
                           gdn_norm[0])
    conv_p = proj.reshape(B, T, N_PROJ)[:, T - (CONV_W - 1):, :CONV_CH]
    qn, kv = qk_norm(proj, q_norm[0], k_norm[0])
    kv3 = kv.reshape(B, T, N_GROUPS * 2 * GROUP_W)
    kvs = [kv3[:, :, gi * 2 * GROUP_W:(gi + 1) * 2 * GROUP_W] for gi in range(N_GROUPS)]
    os_, lds = [], []
    for gi in range(N_GROUPS):
        o, ld = attn_prompt_group(qn.reshape(B, T, ATTN_W), kv3,
                                  _prompt_bias_matrix(rel_bias, gi), gi)
        os_.append(o)
        lds.append(ld)
    yb = attn_merge(os_, lds)
    merged = branch_merge(ya.reshape(B * T, GDN_V_W), yb, wa_bf, wb_bf, proj)
    x1 = matmul(merged, wo_bf, residual=xp, name="out_proj")
    y_prompt = peer_ffn(x1, norm_ffn[0], wq_bf, peer_subkeys[0], u_bf, v_bf).reshape(B, T, D)
    p_kv = []
    for gi, (window, _) in enumerate(ATTN_GROUPS):
        keep = min(window, T)
        kvg = kvs[gi][:, T - keep:].reshape(B, keep, 2, GROUP_HEADS, HEAD_DIM)
        p_kv.append(kvg[None])

    proj_s = matmul(rmsnorm_bf16(xs, norm_mix[0]), w_cat, name="in_proj_s")
    ya_s, conv_s, ssm_s = gdn_step(proj_s, state_conv[0], state_ssm[0], cw, a_log[0], dt_bias[0],
                                   gdn_norm[0])
    qn_s, kv_s = qk_norm(proj_s, q_norm[0], k_norm[0])
    kvs_s = [kv_s[:, gi * 2 * GROUP_W:(gi + 1) * 2 * GROUP_W] for gi in range(N_GROUPS)]
    yb_s = attn_decode(qn_s, kvs_s, caches, rel_bias)
    merged_s = branch_merge(ya_s.astype(BF16), yb_s.astype(BF16), wa_bf, wb_bf, proj_s)
    x1_s = matmul(merged_s, wo_bf, residual=xs, name="out_proj_s")
    y_sample = peer_ffn(x1_s, norm_ffn[0], wq_bf, peer_subkeys[0], u_bf, v_bf).reshape(Bs, 1, D)
    s_kv = [kvs_s[gi].reshape(1, Bs, 1, 2, GROUP_HEADS, HEAD_DIM) for gi in range(N_GROUPS)]

    return (y_prompt, y_sample,
            ssm_p[None], conv_p[None], p_kv[0], p_kv[1], p_kv[2],
            ssm_s[None], conv_s[None], s_kv[0], s_kv[1], s_kv[2])
```

```python
import functools

import numpy as np
import jax
import jax.numpy as jnp
from jax import lax
from jax.experimental import pallas as pl
from jax.experimental.pallas import tpu as pltpu

F32 = jnp.float32
BF16 = jnp.bfloat16
HIGHEST = lax.Precision.HIGHEST

D_MODEL = 4096
GDN_HEADS = 16
GDN_DK = 128
GDN_DV = 128
GDN_QK_W = GDN_HEADS * GDN_DK
GDN_V_W = GDN_HEADS * GDN_DV
CONV_W = 4
CONV_CH = 2 * GDN_QK_W + GDN_V_W
GDN_CHUNK = 64
ATTN_GROUPS = ((128, 1), (512, 4), (2048, 16))
N_GROUPS = 3
GROUP_HEADS = 4
HEAD_DIM = 128
GROUP_W = GROUP_HEADS * HEAD_DIM
ATTN_W = N_GROUPS * GROUP_W
Q_BLOCK = 128
N_BUCKETS = 32
MAX_DISTANCE = 2048
PEER_HEADS = 8
N_KEYS = 128
PEER_TOPK = 16
EPS = 1e-6
NEG_INF = -1e30

OFF_QKV = 0
OFF_Z = CONV_CH
OFF_AQ = OFF_Z + GDN_V_W
OFF_AK = OFF_AQ + ATTN_W
OFF_AV = OFF_AK + ATTN_W
OFF_GA = OFF_AV + ATTN_W
OFF_GB = OFF_GA + D_MODEL
OFF_AB = OFF_GB + D_MODEL
N_PROJ = 21504

GDN_HB = 8
GDN_HG = GDN_HEADS // GDN_HB
VMEM_LIMIT = 56 * 1024 * 1024


def _cparams(sem):
    return pltpu.CompilerParams(dimension_semantics=sem, vmem_limit_bytes=VMEM_LIMIT)


def _sigmoid(x):
    return 1.0 / (1.0 + jnp.exp(-x))


def _silu(x):
    return x * _sigmoid(x)


def _softplus(x):
    return jnp.maximum(x, 0.0) + jnp.log(1.0 + jnp.exp(-jnp.abs(x)))


def _bdot(a, b):
    return jnp.dot(a.astype(BF16), b.astype(BF16), preferred_element_type=F32)


def _bdot_nt(a, b):
    return lax.dot_general(a.astype(BF16), b.astype(BF16), (((1,), (1,)), ((), ())),
                           preferred_element_type=F32)


def _bdot_tn(a, b):
    return lax.dot_general(a.astype(BF16), b.astype(BF16), (((0,), (0,)), ((), ())),
                           preferred_element_type=F32)


def _hdot(a, b):
    return jnp.dot(a, b, precision=HIGHEST, preferred_element_type=F32)


def _split_bf16(a):
    hi = a.astype(BF16)
    lo = (a - hi.astype(F32)).astype(BF16)
    return hi, lo


def _dot3(a_parts, b_parts):
    a_hi, a_lo = a_parts
    b_hi, b_lo = b_parts
    return (jnp.dot(a_hi, b_hi, preferred_element_type=F32)
            + jnp.dot(a_lo, b_hi, preferred_element_type=F32)
            + jnp.dot(a_hi, b_lo, preferred_element_type=F32))


def _rmsnorm_body(x_ref, g_ref, o_ref):
    x = x_ref[...]
    ms = jnp.mean(x * x, axis=-1, keepdims=True)
    o_ref[...] = (x * lax.rsqrt(ms + EPS) * g_ref[...]).astype(o_ref.dtype)


def rmsnorm_bf16(x, gain):
    M, D = x.shape
    tm = min(M, 256)
    return pl.pallas_call(
        _rmsnorm_body,
        grid=(M // tm,),
        in_specs=[pl.BlockSpec((tm, D), lambda i: (i, 0)),
                  pl.BlockSpec((1, D), lambda i: (0, 0))],
        out_specs=pl.BlockSpec((tm, D), lambda i: (i, 0)),
        out_shape=jax.ShapeDtypeStruct((M, D), BF16),
        compiler_params=_cparams(("parallel",)),
        name="rmsnorm",
    )(x, gain.reshape(1, D))


def _mm_body(a_ref, b_ref, o_ref):
    o_ref[...] = jnp.dot(a_ref[...], b_ref[...], preferred_element_type=F32)


def _mm_res_body(a_ref, b_ref, r_ref, o_ref):
    o_ref[...] = r_ref[...] + jnp.dot(a_ref[...], b_ref[...], preferred_element_type=F32)


def matmul(a, b, residual=None, tm=1024, tn=512, name="matmul"):
    M, K = a.shape
    N = b.shape[1]
    tm = min(tm, M)
    tn = min(tn, N)
    in_specs = [pl.BlockSpec((tm, K), lambda i, j: (i, 0)),
                pl.BlockSpec((K, tn), lambda i, j: (0, j))]
    args = [a, b]
    body = _mm_body
    if residual is not None:
        in_specs.append(pl.BlockSpec((tm, tn), lambda i, j: (i, j)))
        args.append(residual)
        body = _mm_res_body
    return pl.pallas_call(
        body,
        grid=(M // tm, N // tn),
        in_specs=in_specs,
        out_specs=pl.BlockSpec((tm, tn), lambda i, j: (i, j)),
        out_shape=jax.ShapeDtypeStruct((M, N), F32),
        compiler_params=_cparams(("parallel", "parallel")),
        name=name,
    )(*args)


def _mm_nt_body(a_ref, bt_ref, o_ref):
    o_ref[...] = lax.dot_general(a_ref[...], bt_ref[...], (((1,), (1,)), ((), ())),
                                 preferred_element_type=F32)


def matmul_nt(a, bt, tm=1024, tn=512, name="matmul_nt"):
    M, K = a.shape
    N = bt.shape[0]
    tm = min(tm, M)
    return pl.pallas_call(
        _mm_nt_body,
        grid=(M // tm, N // tn),
        in_specs=[pl.BlockSpec((tm, K), lambda i, j: (i, 0)),
                  pl.BlockSpec((tn, K), lambda i, j: (j, 0))],
        out_specs=pl.BlockSpec((tm, tn), lambda i, j: (i, j)),
        out_shape=jax.ShapeDtypeStruct((M, N), F32),
        compiler_params=_cparams(("parallel", "parallel")),
        name=name,
    )(a, bt)


def _gdn_chunk_body(q_ref, k_ref, v_ref, z_ref, ab_ref, cwq_ref, cwk_ref, cwv_ref,
                    alog_ref, dtb_ref, gn_ref, o_ref, s_out_ref,
                    xq_s, xk_s, xv_s, state_s):
    n = pl.program_id(2)
    C = GDN_CHUNK
    HW = GDN_HB * GDN_DK

    @pl.when(n == 0)
    def _():
        xq_s[0:8, :] = jnp.zeros((8, HW), F32)
        xk_s[0:8, :] = jnp.zeros((8, HW), F32)
        xv_s[0:8, :] = jnp.zeros((8, HW), F32)
        state_s[...] = jnp.zeros(state_s.shape, F32)

    def conv(x_ref, xs, cw_ref):
        xs[8:8 + C, :] = x_ref[0]
        y = cw_ref[CONV_W - 1:CONV_W, :] * xs[8:8 + C, :]
        for w in range(CONV_W - 1):
            y = y + cw_ref[w:w + 1, :] * xs[5 + w:5 + w + C, :]
        xs[0:8, :] = xs[C:C + 8, :]
        return _silu(y)

    q_all = conv(q_ref, xq_s, cwq_ref)
    k_all = conv(k_ref, xk_s, cwk_ref)
    v_all = conv(v_ref, xv_s, cwv_ref)

    ab = ab_ref[0, 0]
    g_all = -jnp.exp(alog_ref[0]) * _softplus(ab + dtb_ref[0])
    beta_all = _sigmoid(ab)
    ri = lax.broadcasted_iota(jnp.int32, (C, C), 0)
    ci = lax.broadcasted_iota(jnp.int32, (C, C), 1)
    tri = ri >= ci
    strict = ri > ci
    gc_all = _hdot(tri.astype(F32), g_all)
    gc_t = jnp.concatenate([gc_all, jnp.zeros((128 - C, 128), F32)], axis=0).T

    H = range(GDN_HB)
    sls = [slice(j * GDN_DK, (j + 1) * GDN_DK) for j in H]
    qh, kh, kb, gcol, decay, rhs = [], [], [], [], [], []
    for j in H:
        q = q_all[:, sls[j]]
        k = k_all[:, sls[j]]
        q = q * lax.rsqrt(jnp.sum(q * q, axis=-1, keepdims=True) + EPS) * (GDN_DK ** -0.5)
        k = k * lax.rsqrt(jnp.sum(k * k, axis=-1, keepdims=True) + EPS)
        beta = beta_all[:, GDN_HB + j:GDN_HB + j + 1]
        gc = gc_all[:, j:j + 1]
        grow = gc_t[j:j + 1, :C]
        qh.append(q)
        kh.append(k)
        kb.append(k * beta)
        gcol.append(gc)
        decay.append(jnp.where(tri, jnp.exp(jnp.where(tri, gc - grow, 0.0)), 0.0))
        rhs.append(jnp.concatenate([v_all[:, sls[j]] * beta, kb[j] * jnp.exp(gc)], axis=-1))
    kq = [_bdot_nt(jnp.concatenate([kb[j], qh[j]], axis=0), kh[j]) for j in H]
    intra = [jnp.where(tri, kq[j][C:] * decay[j], 0.0) for j in H]
    acc = [-jnp.where(strict, kq[j][:C] * decay[j], 0.0) for j in H]
    parts = [_split_bf16(acc[j]) for j in H]
    pw = [_dot3(parts[j], parts[j]) for j in H]
    for r in range(5):
        pw_parts = [_split_bf16(pw[j]) for j in H]
        acc_parts = [_split_bf16(acc[j]) for j in H]
        acc = [acc[j] + pw[j] + _dot3(pw_parts[j], acc_parts[j]) for j in H]
        if r < 4:
            pw = [_dot3(pw_parts[j], pw_parts[j]) for j in H]
    sol = [rhs[j] + _dot3(_split_bf16(acc[j]), _split_bf16(rhs[j])) for j in H]
    S = [state_s[j] for j in H]
    wq_s = [_bdot(jnp.concatenate([sol[j][:, GDN_DV:], qh[j] * jnp.exp(gcol[j])], axis=0), S[j])
            for j in H]
    v_new = [sol[j][:, :GDN_DV] - wq_s[j][:C] for j in H]
    o = [wq_s[j][C:] + _bdot(intra[j], v_new[j]) for j in H]
    for j in H:
        g_last = gcol[j][C - 1:C, :]
        state_s[j] = (S[j] * jnp.exp(g_last)
                      + _bdot_tn(kh[j] * jnp.exp(g_last - gcol[j]), v_new[j]))
    for j in H:
        ms = jnp.mean(o[j] * o[j], axis=-1, keepdims=True)
        on = o[j] * lax.rsqrt(ms + EPS) * gn_ref[...]
        o_ref[0, :, sls[j]] = (on * _silu(z_ref[0, :, sls[j]])).astype(o_ref.dtype)

    @pl.when(n == pl.num_programs(2) - 1)
    def _():
        s_out_ref[0] = state_s[...]


def gdn_prompt(proj3, ab_g, conv_w, alog_g, dtb_g, gdn_norm):
    B, T, _ = proj3.shape
    C = GDN_CHUNK
    N = T // C
    HW = GDN_HB * GDN_DK
    nq = GDN_QK_W // HW

    def col(base):
        return lambda b, hg, n: (b, n, base + hg)

    def cwcol(base):
        return lambda b, hg, n: (0, base + hg)

    return pl.pallas_call(
        _gdn_chunk_body,
        grid=(B, GDN_HG, N),
        in_specs=[
            pl.BlockSpec((1, C, HW), col(0)),
            pl.BlockSpec((1, C, HW), col(nq)),
            pl.BlockSpec((1, C, HW), col(2 * nq)),
            pl.BlockSpec((1, C, HW), col(OFF_Z // HW)),
            pl.BlockSpec((1, 1, C, 128), lambda b, hg, n: (b, hg, n, 0)),
            pl.BlockSpec((CONV_W, HW), cwcol(0)),
            pl.BlockSpec((CONV_W, HW), cwcol(nq)),
            pl.BlockSpec((CONV_W, HW), cwcol(2 * nq)),
            pl.BlockSpec((1, 1, 128), lambda b, hg, n: (hg, 0, 0)),
            pl.BlockSpec((1, 1, 128), lambda b, hg, n: (hg, 0, 0)),
            pl.BlockSpec((1, GDN_DV), lambda b, hg, n: (0, 0)),
        ],
        out_specs=[
            pl.BlockSpec((1, C, HW), lambda b, hg, n: (b, n, hg)),
            pl.BlockSpec((1, GDN_HB, GDN_DK, GDN_DV), lambda b, hg, n: (b, hg, 0, 0)),
        ],
        out_shape=[
            jax.ShapeDtypeStruct((B, T, GDN_V_W), BF16),
            jax.ShapeDtypeStruct((B, GDN_HEADS, GDN_DK, GDN_DV), F32),
        ],
        scratch_shapes=[
            pltpu.VMEM((C + 8, HW), F32),
            pltpu.VMEM((C + 8, HW), F32),
            pltpu.VMEM((C + 8, HW), F32),
            pltpu.VMEM((GDN_HB, GDN_DK, GDN_DV), F32),
        ],
        compiler_params=_cparams(("parallel", "parallel", "arbitrary")),
        name="gdn_prompt",
    )(proj3, proj3, proj3, proj3, ab_g, conv_w, conv_w, conv_w, alog_g, dtb_g,
      gdn_norm.reshape(1, GDN_DV))


def _gdn_step_body(x_ref, z_ref, ab_ref, cs_ref, s_ref, cw_ref, alog_ref, dtb_ref, gn_ref,
                   o_ref, cs_out_ref, s_out_ref):
    x = x_ref[0]
    y = cw_ref[CONV_W - 1] * x
    for w in range(CONV_W - 1):
        y = y + cw_ref[w] * cs_ref[0, w]
    y = _silu(y)
    cs_out_ref[0, 0] = cs_ref[0, 1]
    cs_out_ref[0, 1] = cs_ref[0, 2]
    cs_out_ref[0, 2] = x
    r_ab = (OFF_AB // 128) % 8
    ab = ab_ref[0, r_ab:r_ab + 1, :]
    g_all = -jnp.exp(alog_ref[...]) * _softplus(ab + dtb_ref[...])
    beta_all = _sigmoid(ab)
    eye = (lax.broadcasted_iota(jnp.int32, (GDN_DK, GDN_DK), 0)
           == lax.broadcasted_iota(jnp.int32, (GDN_DK, GDN_DK), 1))

    def to_col(row):
        return jnp.sum(jnp.where(eye, jnp.broadcast_to(row, (GDN_DK, GDN_DK)), 0.0),
                       axis=1, keepdims=True)

    for h in range(GDN_HEADS):
        q = y[h:h + 1, :]
        k = y[GDN_HEADS + h:GDN_HEADS + h + 1, :]
        v = y[2 * GDN_HEADS + h:2 * GDN_HEADS + h + 1, :]
        q = q * lax.rsqrt(jnp.sum(q * q, axis=-1, keepdims=True) + EPS) * (GDN_DK ** -0.5)
        k = k * lax.rsqrt(jnp.sum(k * k, axis=-1, keepdims=True) + EPS)
        g = g_all[:, h:h + 1]
        beta = beta_all[:, GDN_HEADS + h:GDN_HEADS + h + 1]
        eg = jnp.exp(g)
        S = s_ref[0, h]
        kcol = to_col(k)
        qcol = to_col(q)
        k_s = jnp.sum(S * kcol, axis=0, keepdims=True)
        q_s = jnp.sum(S * qcol, axis=0, keepdims=True)
        v_new = v * beta - (beta * eg) * k_s
        o = eg * q_s + jnp.sum(q * k, axis=-1, keepdims=True) * v_new
        s_out_ref[0, h] = S * eg + kcol * v_new
        ms = jnp.mean(o * o, axis=-1, keepdims=True)
        on = o * lax.rsqrt(ms + EPS) * gn_ref[...]
        o_ref[0, h:h + 1, :] = on * _silu(z_ref[0, h:h + 1, :])


def gdn_step(proj_s, conv_state, ssm_state, conv_w, a_log, dt_bias, gdn_norm):
    Bs = proj_s.shape[0]
    nrow = N_PROJ // 128
    p3 = proj_s.reshape(Bs, nrow, 128)
    cs4 = conv_state.reshape(Bs, CONV_W - 1, CONV_CH // 128, 128)
    cw3 = conv_w.reshape(CONV_W, CONV_CH // 128, 128)
    pad = jnp.zeros((128 - GDN_HEADS,), F32)
    alog = jnp.concatenate([a_log, pad]).reshape(1, 128)
    dtb = jnp.concatenate([dt_bias, pad]).reshape(1, 128)
    nr = CONV_CH // 128
    o, cs_new, s_new = pl.pallas_call(
        _gdn_step_body,
        grid=(Bs,),
        in_specs=[
            pl.BlockSpec((1, nr, 128), lambda b: (b, 0, 0)),
            pl.BlockSpec((1, GDN_HEADS, 128), lambda b: (b, OFF_Z // 128 // GDN_HEADS, 0)),
            pl.BlockSpec((1, 8, 128), lambda b: (b, OFF_AB // 128 // 8, 0)),
            pl.BlockSpec((1, CONV_W - 1, nr, 128), lambda b: (b, 0, 0, 0)),
            pl.BlockSpec((1, GDN_HEADS, GDN_DK, GDN_DV), lambda b: (b, 0, 0, 0)),
            pl.BlockSpec((CONV_W, nr, 128), lambda b: (0, 0, 0)),
            pl.BlockSpec((1, 128), lambda b: (0, 0)),
            pl.BlockSpec((1, 128), lambda b: (0, 0)),
            pl.BlockSpec((1, GDN_DV), lambda b: (0, 0)),
        ],
        out_specs=[
            pl.BlockSpec((1, GDN_HEADS, GDN_DV), lambda b: (b, 0, 0)),
            pl.BlockSpec((1, CONV_W - 1, nr, 128), lambda b: (b, 0, 0, 0)),
            pl.BlockSpec((1, GDN_HEADS, GDN_DK, GDN_DV), lambda b: (b, 0, 0, 0)),
        ],
        out_shape=[
            jax.ShapeDtypeStruct((Bs, GDN_HEADS, GDN_DV), F32),
            jax.ShapeDtypeStruct((Bs, CONV_W - 1, nr, 128), F32),
            jax.ShapeDtypeStruct((Bs, GDN_HEADS, GDN_DK, GDN_DV), F32),
        ],
        compiler_params=_cparams(("parallel",)),
        name="gdn_step",
    )(p3, p3, p3, cs4, ssm_state, cw3, alog, dtb, gdn_norm.reshape(1, GDN_DV))
    return (o.reshape(Bs, GDN_V_W), cs_new.reshape(Bs, CONV_W - 1, CONV_CH), s_new)


def _qknorm_body(q_ref, k_ref, v_ref, qg_ref, kg_ref, qn_ref, kv_ref):
    for h in range(GROUP_HEADS):
        sl = slice(h * HEAD_DIM, (h + 1) * HEAD_DIM)
        q = q_ref[:, sl]
        qn = q * lax.rsqrt(jnp.mean(q * q, axis=-1, keepdims=True) + EPS) * qg_ref[...]
        qn_ref[:, sl] = qn * (HEAD_DIM ** -0.5)
        k = k_ref[:, sl]
        kv_ref[:, sl] = k * lax.rsqrt(jnp.mean(k * k, axis=-1, keepdims=True) + EPS) * kg_ref[...]
    kv_ref[:, GROUP_W:] = v_ref[...]


def qk_norm(proj, q_norm, k_norm):
    M = proj.shape[0]
    tm = min(M, 512)
    blk = GROUP_W
    return pl.pallas_call(
        _qknorm_body,
        grid=(M // tm, N_GROUPS),
        in_specs=[
            pl.BlockSpec((tm, blk), lambda i, g: (i, OFF_AQ // blk + g)),
            pl.BlockSpec((tm, blk), lambda i, g: (i, OFF_AK // blk + g)),
            pl.BlockSpec((tm, blk), lambda i, g: (i, OFF_AV // blk + g)),
            pl.BlockSpec((1, HEAD_DIM), lambda i, g: (0, 0)),
            pl.BlockSpec((1, HEAD_DIM), lambda i, g: (0, 0)),
        ],
        out_specs=[
            pl.BlockSpec((tm, blk), lambda i, g: (i, g)),
            pl.BlockSpec((tm, 2 * blk), lambda i, g: (i, g)),
        ],
        out_shape=[jax.ShapeDtypeStruct((M, ATTN_W), F32),
                   jax.ShapeDtypeStruct((M, N_GROUPS * 2 * GROUP_W), F32)],
        compiler_params=_cparams(("parallel", "parallel")),
        name="qk_norm",
    )(proj, proj, proj, q_norm.reshape(1, HEAD_DIM), k_norm.reshape(1, HEAD_DIM))


def _t5_bucket(dist):
    max_exact = N_BUCKETS // 2
    d = np.maximum(dist, max_exact).astype(np.float32)
    large = max_exact + (np.log(d / max_exact) / np.log(MAX_DISTANCE / max_exact)
                         * (N_BUCKETS - max_exact)).astype(np.int32)
    large = np.minimum(large, N_BUCKETS - 1)
    return np.where(dist < max_exact, dist, large).astype(np.int32)


def _group_bias(rel_bias, gi):
    window, dilation = ATTN_GROUPS[gi]
    n_keys = window // dilation + 1
    buckets = _t5_bucket(np.arange(n_keys) * dilation)
    return rel_bias[buckets][:, gi * GROUP_HEADS:(gi + 1) * GROUP_HEADS].T.astype(F32)


def _attn_prompt_body(T, q0_ref, q1_ref, q2_ref, k0_ref, v0_ref, k1_ref, v1_ref, k2_ref, v2_ref,
                      bias_ref, y_ref, m_s, l_s, o_s):
    QB = Q_BLOCK
    q_refs = (q0_ref, q1_ref, q2_ref)
    k_refs = (k0_ref, k1_ref, k2_ref)
    v_refs = (v0_ref, v1_ref, v2_ref)
    for gi, (window, d) in enumerate(ATTN_GROUPS):
        nq = T // (d * QB)
        for r in range(d):
            for qi in range(nq):
                rows = pl.ds(r + d * QB * qi, QB, stride=d) if d > 1 else pl.ds(QB * qi, QB)
                q = q_refs[gi][0, rows, :].astype(BF16)
                kc = k_refs[gi][0, rows, :]
                vc = v_refs[gi][0, rows, :]
                if qi > 0:
                    prev = (pl.ds(r + d * QB * (qi - 1), QB, stride=d) if d > 1
                            else pl.ds(QB * (qi - 1), QB))
                    kcat = jnp.concatenate([k_refs[gi][0, prev, :], kc], axis=0).astype(BF16)
                    vcat = jnp.concatenate([v_refs[gi][0, prev, :], vc], axis=0).astype(BF16)
                    bias = bias_ref[gi, 0]
                else:
                    kcat = kc.astype(BF16)
                    vcat = vc.astype(BF16)
                    bias = bias_ref[gi, 0, :, QB:]
                s = lax.dot_general(q, kcat, (((1,), (1,)), ((), ())),
                                    preferred_element_type=F32) + bias
                m_blk = jnp.max(s, axis=-1, keepdims=True)
                if gi == 0:
                    m_new = m_blk
                else:
                    m_old = m_s[rows, :][:, 0:1]
                    m_new = jnp.maximum(m_old, m_blk)
                p = jnp.exp(s - m_new)
                l_new = jnp.sum(p, axis=-1, keepdims=True)
                o_new = jnp.dot(p.astype(BF16), vcat, preferred_element_type=F32)
                if gi > 0:
                    alpha = jnp.exp(m_old - m_new)
                    l_new = l_new + alpha * l_s[rows, :][:, 0:1]
                    o_new = o_new + alpha * o_s[rows, :]
                m_s[rows, :] = jnp.broadcast_to(m_new, (QB, HEAD_DIM))
                l_s[rows, :] = jnp.broadcast_to(l_new, (QB, HEAD_DIM))
                o_s[rows, :] = o_new
    y_ref[0] = (o_s[...] / l_s[...]).astype(y_ref.dtype)


def attn_prompt(qn3, kv3, bias_all):
    B, T, _ = qn3.shape
    blk = (1, T, HEAD_DIM)

    def qspec(gi):
        return pl.BlockSpec(blk, lambda b, h: (b, 0, gi * GROUP_HEADS + h))

    def kvspec(gi, is_v):
        return pl.BlockSpec(blk, lambda b, h: (b, 0, (2 * gi + is_v) * GROUP_HEADS + h))

    in_specs = [qspec(0), qspec(1), qspec(2)]
    for gi in range(N_GROUPS):
        in_specs += [kvspec(gi, 0), kvspec(gi, 1)]
    in_specs.append(pl.BlockSpec((N_GROUPS, 1, Q_BLOCK, 2 * Q_BLOCK), lambda b, h: (0, h, 0, 0)))
    y = pl.pallas_call(
        functools.partial(_attn_prompt_body, T),
        grid=(B, GROUP_HEADS),
        in_specs=in_specs,
        out_specs=pl.BlockSpec(blk, lambda b, h: (b, 0, h)),
        out_shape=jax.ShapeDtypeStruct((B, T, GROUP_W), BF16),
        scratch_shapes=[pltpu.VMEM((T, HEAD_DIM), F32), pltpu.VMEM((T, HEAD_DIM), F32),
                        pltpu.VMEM((T, HEAD_DIM), F32)],
        compiler_params=_cparams(("parallel", "parallel")),
        name="attn_prompt",
    )(qn3, qn3, qn3, kv3, kv3, kv3, kv3, kv3, kv3, bias_all)
    return y.reshape(B * T, GROUP_W)


def _prompt_bias_matrix(rel_bias, gi):
    QB = Q_BLOCK
    bias = _group_bias(rel_bias, gi)
    row0 = jnp.concatenate([bias[:, ::-1], jnp.full((GROUP_HEADS, QB), NEG_INF, F32)], axis=1)
    flat = jnp.tile(row0, (1, QB))[:, :QB * 2 * QB]
    return flat.reshape(GROUP_HEADS, QB, 2 * QB)


def _attn_decode_body(q_ref, n0_ref, n1_ref, n2_ref, c0_ref, c1_ref, c2_ref, bc_ref, bn_ref, y_ref):
    new_refs = (n0_ref, n1_ref, n2_ref)
    cache_refs = (c0_ref, c1_ref, c2_ref)
    outs, lds = [], []
    for gi in range(N_GROUPS):
        q = q_ref[0, gi * GROUP_HEADS:(gi + 1) * GROUP_HEADS, :]
        kc = cache_refs[gi][:, 0]
        vc = cache_refs[gi][:, 1]
        kn = new_refs[gi][0, 0:GROUP_HEADS, :]
        vn = new_refs[gi][0, GROUP_HEADS:2 * GROUP_HEADS, :]
        s_c = jnp.sum(kc * q[None], axis=-1, keepdims=True) + bc_ref[gi, :, :, 0:1]
        s_n = jnp.sum(kn * q, axis=-1, keepdims=True) + bn_ref[gi, :, 0:1]
        m = jnp.maximum(jnp.max(s_c, axis=0), s_n)
        p_c = jnp.exp(s_c - m[None])
        p_n = jnp.exp(s_n - m)
        l = jnp.sum(p_c, axis=0) + p_n
        outs.append((jnp.sum(p_c * vc, axis=0) + p_n * vn) / l)
        lds.append(m + jnp.log(l))
    m = jnp.maximum(jnp.maximum(lds[0], lds[1]), lds[2])
    es = [jnp.exp(x - m) for x in lds]
    den = es[0] + es[1] + es[2]
    y_ref[0] = outs[0] * (es[0] / den) + outs[1] * (es[1] / den) + outs[2] * (es[2] / den)


def attn_decode(qn_s, kv_new, caches, rel_bias):
    Bs = qn_s.shape[0]
    q3 = qn_s.reshape(Bs, N_GROUPS * GROUP_HEADS, HEAD_DIM)
    new3 = [x.reshape(Bs, 2 * GROUP_HEADS, HEAD_DIM) for x in kv_new]
    cviews, bcs, bns = [], [], []
    for gi, (window, d) in enumerate(ATTN_GROUPS):
        nk = window // d
        assert caches[gi].shape[1] == window and nk == Q_BLOCK
        cviews.append(caches[gi].reshape(Bs, nk, d, 2, GROUP_HEADS, HEAD_DIM))
        bias = _group_bias(rel_bias, gi)
        bc = bias[:, nk - np.arange(nk)]
        bcs.append(jnp.broadcast_to(bc.T[:, :, None], (nk, GROUP_HEADS, HEAD_DIM)))
        bns.append(jnp.broadcast_to(bias[:, 0:1], (GROUP_HEADS, HEAD_DIM)))
    bc_all = jnp.stack(bcs, 0)
    bn_all = jnp.stack(bns, 0)
    nspec = pl.BlockSpec((1, 2 * GROUP_HEADS, HEAD_DIM), lambda b: (b, 0, 0))
    cspec = pl.BlockSpec((None, Q_BLOCK, None, 2, GROUP_HEADS, HEAD_DIM),
                         lambda b: (b, 0, 0, 0, 0, 0))
    y = pl.pallas_call(
        _attn_decode_body,
        grid=(Bs,),
        in_specs=[
            pl.BlockSpec((1, N_GROUPS * GROUP_HEADS, HEAD_DIM), lambda b: (b, 0, 0)),
            nspec, nspec, nspec, cspec, cspec, cspec,
            pl.BlockSpec((N_GROUPS, Q_BLOCK, GROUP_HEADS, HEAD_DIM), lambda b: (0, 0, 0, 0)),
            pl.BlockSpec((N_GROUPS, GROUP_HEADS, HEAD_DIM), lambda b: (0, 0, 0)),
        ],
        out_specs=pl.BlockSpec((1, GROUP_HEADS, HEAD_DIM), lambda b: (b, 0, 0)),
        out_shape=jax.ShapeDtypeStruct((Bs, GROUP_HEADS, HEAD_DIM), F32),
        compiler_params=_cparams(("parallel",)),
        name="attn_decode",
    )(q3, *new3, *cviews, bc_all, bn_all)
    return y.reshape(Bs, GROUP_W)


def _merge_body(ya_ref, yb_ref, wa_ref, wb_ref, ga_ref, gb_ref, o_ref):
    pa = jnp.dot(ya_ref[...], wa_ref[...], preferred_element_type=F32)
    pb = jnp.dot(yb_ref[...], wb_ref[...], preferred_element_type=F32)
    o_ref[...] = (_sigmoid(ga_ref[...]) * pa + _sigmoid(gb_ref[...]) * pb).astype(o_ref.dtype)


def branch_merge(ya, yb, wa, wb, proj):
    M = ya.shape[0]
    tm = min(M, 1024)
    tn = 512
    return pl.pallas_call(
        _merge_body,
        grid=(M // tm, D_MODEL // tn),
        in_specs=[
            pl.BlockSpec((tm, GDN_V_W), lambda i, j: (i, 0)),
            pl.BlockSpec((tm, GROUP_W), lambda i, j: (i, 0)),
            pl.BlockSpec((GDN_V_W, tn), lambda i, j: (0, j)),
            pl.BlockSpec((GROUP_W, tn), lambda i, j: (0, j)),
            pl.BlockSpec((tm, tn), lambda i, j: (i, OFF_GA // tn + j)),
            pl.BlockSpec((tm, tn), lambda i, j: (i, OFF_GB // tn + j)),
        ],
        out_specs=pl.BlockSpec((tm, tn), lambda i, j: (i, j)),
        out_shape=jax.ShapeDtypeStruct((M, D_MODEL), BF16),
        compiler_params=_cparams(("parallel", "parallel")),
        name="branch_merge",
    )(ya, yb, wa, wb, proj, proj)


def _extract_top(x, count):
    R = x.shape[0]
    rid = lax.broadcasted_iota(jnp.int32, x.shape, 0).astype(F32)
    vid = lax.broadcasted_iota(jnp.int32, (count, x.shape[1]), 0)
    vals = jnp.zeros((count, x.shape[1]), F32)
    for a in range(count):
        m = jnp.max(x, axis=0, keepdims=True)
        first = jnp.min(jnp.where(x == m, rid, float(R)), axis=0, keepdims=True)
        x = jnp.where(rid == first, -jnp.inf, x)
        vals = jnp.where(vid == a, m, vals)
    return vals, x == -jnp.inf


def _peer_topk_body(q_ref, sk_ref, s1_ref, s2_ref, e1_ref, e2_ref, tau_ref, cand_s):
    K = PEER_TOPK

    def head(h, carry):
        off = pl.multiple_of(h * (2 * N_KEYS), 2 * N_KEYS)
        q1 = q_ref[:, pl.ds(off, N_KEYS)]
        q2 = q_ref[:, pl.ds(pl.multiple_of(off + N_KEYS, N_KEYS), N_KEYS)]
        s1 = lax.dot_general(sk_ref[0], q1, (((1,), (1,)), ((), ())), precision=HIGHEST,
                             preferred_element_type=F32)
        s2 = lax.dot_general(sk_ref[1], q2, (((1,), (1,)), ((), ())), precision=HIGHEST,
                             preferred_element_type=F32)
        v1, sel1 = _extract_top(s1, K)
        v2, sel2 = _extract_top(s2, K)
        for a in range(K):
            cand_s[a * K:(a + 1) * K, :] = v2 + v1[a:a + 1, :]
        best, _ = _extract_top(cand_s[...], K)
        cmax = best[0:1, :]
        z = jnp.sum(jnp.exp(best - cmax), axis=0, keepdims=True)
        s1_ref[h] = s1
        s2_ref[h] = s2
        e1_ref[h] = jnp.where(sel1, jnp.exp(s1 - v1[0:1, :]), 0.0)
        e2_ref[h] = jnp.where(sel2, jnp.exp(s2 - v2[0:1, :]), 0.0) / z
        tau_ref[h] = best[K - 1:K, :]
        return carry

    lax.fori_loop(0, PEER_HEADS, head, 0)


def peer_topk(query, subkeys):
    M = query.shape[0]
    tm = min(M, 256)
    big = jax.ShapeDtypeStruct((PEER_HEADS, N_KEYS, M), F32)
    bspec = pl.BlockSpec((PEER_HEADS, N_KEYS, tm), lambda i: (0, 0, i))
    return pl.pallas_call(
        _peer_topk_body,
        grid=(M // tm,),
        in_specs=[pl.BlockSpec((tm, PEER_HEADS * 2 * N_KEYS), lambda i: (i, 0)),
                  pl.BlockSpec((2, N_KEYS, N_KEYS), lambda i: (0, 0, 0))],
        out_specs=[bspec, bspec, bspec, bspec,
                   pl.BlockSpec((PEER_HEADS, 1, tm), lambda i: (0, 0, i))],
        out_shape=[big, big, big, big, jax.ShapeDtypeStruct((PEER_HEADS, 1, M), F32)],
        scratch_shapes=[pltpu.VMEM((PEER_TOPK * PEER_TOPK, tm), F32)],
        compiler_params=_cparams(("parallel",)),
        name="peer_topk",
    )(query, subkeys)


def _gelu_exact(x):
    return 0.5 * x * (1.0 + lax.erf(x * (2.0 ** -0.5)))


def _peer_gate_body(tn, hf_ref, u_ref, s1_ref, s2_ref, e1_ref, e2_ref, tau_ref, w_ref, g_s):
    j = pl.program_id(1)
    tm = hf_ref.shape[0]

    for kk in range(tn // N_KEYS):
        i1 = j * (tn // N_KEYS) + kk
        grp = pl.multiple_of((i1 // 8) * 8, 8)
        rsel = lax.broadcasted_iota(jnp.int32, (8, 128), 0) == (i1 % 8)
        for tc in range(tm // 128):
            ts = slice(tc * 128, (tc + 1) * 128)
            acc = jnp.zeros((N_KEYS, 128), F32)
            for h in range(PEER_HEADS):
                s1row = jnp.sum(jnp.where(rsel, s1_ref[h, pl.ds(grp, 8), ts], 0.0),
                                axis=0, keepdims=True)
                e1row = jnp.sum(jnp.where(rsel, e1_ref[h, pl.ds(grp, 8), ts], 0.0),
                                axis=0, keepdims=True)
                c = s2_ref[h, :, ts] + s1row
                acc = acc + jnp.where(c >= tau_ref[h, :, ts], e2_ref[h, :, ts] * e1row, 0.0)
            g_s[ts, kk * N_KEYS:(kk + 1) * N_KEYS] = acc.T
    hmat = lax.dot_general(hf_ref[...], u_ref[...], (((1,), (1,)), ((), ())),
                           preferred_element_type=F32)
    w_ref[...] = (g_s[...] * _gelu_exact(hmat)).astype(w_ref.dtype)


def peer_gate_weights(hf, u_tab, s1, s2, e1, e2, tau, tm=512, tn=512):
    M = hf.shape[0]
    tm = min(M, tm)
    E = u_tab.shape[0]
    sspec = pl.BlockSpec((PEER_HEADS, N_KEYS, tm), lambda i, j: (0, 0, i))
    return pl.pallas_call(
        functools.partial(_peer_gate_body, tn),
        grid=(M // tm, E // tn),
        in_specs=[
            pl.BlockSpec((tm, D_MODEL), lambda i, j: (i, 0)),
            pl.BlockSpec((tn, D_MODEL), lambda i, j: (j, 0)),
            sspec, sspec, sspec, sspec,
            pl.BlockSpec((PEER_HEADS, 1, tm), lambda i, j: (0, 0, i)),
        ],
        out_specs=pl.BlockSpec((tm, tn), lambda i, j: (i, j)),
        out_shape=jax.ShapeDtypeStruct((M, E), BF16),
        scratch_shapes=[pltpu.VMEM((tm, tn), F32)],
        compiler_params=_cparams(("parallel", "arbitrary")),
        name="peer_gate",
    )(hf, u_tab, s1, s2, e1, e2, tau)


def _mm_kres_body(a_ref, b_ref, r_ref, o_ref):
    k = pl.program_id(2)

    @pl.when(k == 0)
    def _():
        o_ref[...] = r_ref[...]

    o_ref[...] += jnp.dot(a_ref[...], b_ref[...], preferred_element_type=F32)


def matmul_kres(a, b, residual, tm=1024, tn=1024, tk=2048, name="matmul_k"):
    M, K = a.shape
    N = b.shape[1]
    tm = min(tm, M)
    return pl.pallas_call(
        _mm_kres_body,
        grid=(M // tm, N // tn, K // tk),
        in_specs=[pl.BlockSpec((tm, tk), lambda i, j, k: (i, k)),
                  pl.BlockSpec((tk, tn), lambda i, j, k: (k, j)),
                  pl.BlockSpec((tm, tn), lambda i, j, k: (i, j))],
        out_specs=pl.BlockSpec((tm, tn), lambda i, j, k: (i, j)),
        out_shape=jax.ShapeDtypeStruct((M, N), F32),
        compiler_params=_cparams(("parallel", "parallel", "arbitrary")),
        name=name,
    )(a, b, residual)


def peer_ffn(x1, norm_ffn, wq_bf, subkeys, u_bf, v_bf):
    M = x1.shape[0]
    Mp = max(M, 128)
    if Mp != M:
        x1 = jnp.concatenate([x1, jnp.zeros((Mp - M, D_MODEL), F32)], axis=0)
    hf = rmsnorm_bf16(x1, norm_ffn)
    query = matmul(hf, wq_bf, name="peer_query")
    s1, s2, e1, e2, tau = peer_topk(query, subkeys)
    w = peer_gate_weights(hf, u_bf, s1, s2, e1, e2, tau)
    return matmul_kres(w, v_bf, x1, name="peer_out")[:M]


W_BLK = 512
N_AB = 2 * GDN_HEADS
J_AB = OFF_AB // W_BLK


def _prep_w_in_body(a_ref, b_ref, o_ref):
    j = pl.program_id(0)

    @pl.when(j < OFF_AQ // W_BLK)
    def _():
        o_ref[...] = a_ref[...].astype(o_ref.dtype)

    @pl.when(jnp.logical_and(j >= OFF_AQ // W_BLK, j < J_AB))
    def _():
        o_ref[0:W_BLK - N_AB, :] = a_ref[N_AB:, :].astype(o_ref.dtype)
        o_ref[W_BLK - N_AB:, :] = b_ref[...].astype(o_ref.dtype)

    @pl.when(j == J_AB)
    def _():
        o_ref[0:N_AB, :] = b_ref[...].astype(o_ref.dtype)
        o_ref[N_AB:, :] = jnp.zeros((W_BLK - N_AB, o_ref.shape[1]), o_ref.dtype)


def _prep_w_in(w_in_t):
    D = w_in_t.shape[1]
    j0 = OFF_AQ // W_BLK
    sub = W_BLK // N_AB
    return pl.pallas_call(
        _prep_w_in_body,
        grid=(N_PROJ // W_BLK,),
        in_specs=[
            pl.BlockSpec((W_BLK, D), lambda j: (jnp.where(j == J_AB, j0, j), 0)),
            pl.BlockSpec((N_AB, D), lambda j: (jnp.where(j == J_AB, j0 * sub,
                                                         jnp.where(j >= j0, (j + 1) * sub, 0)), 0)),
        ],
        out_specs=pl.BlockSpec((W_BLK, D), lambda j: (j, 0)),
        out_shape=jax.ShapeDtypeStruct((N_PROJ, D), BF16),
        compiler_params=_cparams(("parallel",)),
        name="prep_w_in",
    )(w_in_t, w_in_t)


def _gdn_group_params(a_log, dt_bias):
    pad = jnp.zeros((GDN_HG, 128 - GDN_HB), F32)
    alog = jnp.concatenate([a_log.reshape(GDN_HG, GDN_HB), pad], axis=1).reshape(GDN_HG, 1, 128)
    dtb = jnp.concatenate([dt_bias.reshape(GDN_HG, GDN_HB), pad], axis=1).reshape(GDN_HG, 1, 128)
    return alog, dtb


def _ab_grouped(proj, B, T):
    ab = proj[:, OFF_AB:OFF_AB + 2 * GDN_HEADS].reshape(B, T, 2, GDN_HG, GDN_HB)
    ab = ab.transpose(0, 3, 1, 2, 4).reshape(B, GDN_HG, T, 2 * GDN_HB)
    return jnp.concatenate([ab, jnp.zeros((B, GDN_HG, T, 128 - 2 * GDN_HB), F32)], axis=-1)


def kernel(x_prompt, x_sample, state_ssm, state_conv, cache_kv_g0, cache_kv_g1, cache_kv_g2, rel_bias, norm_mix, w_in, conv_w, a_log, dt_bias, gdn_norm, q_norm, k_norm, w_branch_a, w_branch_b, w_out, norm_ffn, peer_w_query, peer_subkeys, peer_u, peer_v):
    B, T, D = x_prompt.shape
    Bs = x_sample.shape[0]
    assert state_ssm.shape[0] == 1 and x_sample.shape[1] == 1
    caches = (cache_kv_g0[0], cache_kv_g1[0], cache_kv_g2[0])
    w_cat_t = _prep_w_in(jnp.swapaxes(w_in[0], 0, 1))
    wa_bf = w_branch_a[0].astype(BF16)
    wb_bf = w_branch_b[0].astype(BF16)
    wo_bf = w_out[0].astype(BF16)
    wq_bf = peer_w_query[0].astype(BF16)
    u_bf = peer_u[0].astype(BF16)
    v_bf = peer_v[0].astype(BF16)
    cw = conv_w[0]

    xp = x_prompt.reshape(B * T, D)
    xs = x_sample.reshape(Bs, D)

    proj = matmul_nt(rmsnorm_bf16(xp, norm_mix[0]), w_cat_t, name="in_proj")
    alog_g, dtb_g = _gdn_group_params(a_log[0], dt_bias[0])
    ya, ssm_p = gdn_prompt(proj.reshape(B, T, N_PROJ), _ab_grouped(proj, B, T), cw, alog_g, dtb_g,
                           gdn_norm[0])
    conv_p = proj.reshape(B, T, N_PROJ)[:, T - (CONV_W - 1):, :CONV_CH]
    qn, kv = qk_norm(proj, q_norm[0], k_norm[0])
    kv3 = kv.reshape(B, T, N_GROUPS * 2 * GROUP_W)
    kvs = [kv3[:, :, gi * 2 * GROUP_W:(gi + 1) * 2 * GROUP_W] for gi in range(N_GROUPS)]
    bias_all = jnp.stack([_prompt_bias_matrix(rel_bias, gi) for gi in range(N_GROUPS)], 0)
    yb = attn_prompt(qn.reshape(B, T, ATTN_W), kv3, bias_all)
    merged = branch_merge(ya.reshape(B * T, GDN_V_W), yb, wa_bf, wb_bf, proj)
    x1 = matmul(merged, wo_bf, residual=xp, name="out_proj")
    y_prompt = peer_ffn(x1, norm_ffn[0], wq_bf, peer_subkeys[0], u_bf, v_bf).reshape(B, T, D)
    p_kv = []
    for gi, (window, _) in enumerate(ATTN_GROUPS):
        keep = min(window, T)
        kvg = kvs[gi][:, T - keep:].reshape(B, keep, 2, GROUP_HEADS, HEAD_DIM)
        p_kv.append(kvg[None])

    proj_s = matmul_nt(rmsnorm_bf16(xs, norm_mix[0]), w_cat_t, name="in_proj_s")
    ya_s, conv_s, ssm_s = gdn_step(proj_s, state_conv[0], state_ssm[0], cw, a_log[0], dt_bias[0],
                                   gdn_norm[0])
    qn_s, kv_s = qk_norm(proj_s, q_norm[0], k_norm[0])
    kvs_s = [kv_s[:, gi * 2 * GROUP_W:(gi + 1) * 2 * GROUP_W] for gi in range(N_GROUPS)]
    yb_s = attn_decode(qn_s, kvs_s, caches, rel_bias)
    merged_s = branch_merge(ya_s.astype(BF16), yb_s.astype(BF16), wa_bf, wb_bf, proj_s)
    x1_s = matmul(merged_s, wo_bf, residual=xs, name="out_proj_s")
    y_sample = peer_ffn(x1_s, norm_ffn[0], wq_bf, peer_subkeys[0], u_bf, v_bf).reshape(Bs, 1, D)
    s_kv = [kvs_s[gi].reshape(1, Bs, 1, 2, GROUP_HEADS, HEAD_DIM) for gi in range(N_GROUPS)]

    return (y_prompt, y_sample,
            ssm_p[None], conv_p[None], p_kv[0], p_kv[1], p_kv[2],
            ssm_s[None], conv_s[None], s_kv[0], s_kv[1], s_kv[2])
```

```python
import functools

import numpy as np
import jax
import jax.numpy as jnp
from jax import lax
from jax.experimental import pallas as pl
from jax.experimental.pallas import tpu as pltpu

F32 = jnp.float32
BF16 = jnp.bfloat16
HIGHEST = lax.Precision.HIGHEST

D_MODEL = 4096
GDN_HEADS = 16
GDN_DK = 128
GDN_DV = 128
GDN_QK_W = GDN_HEADS * GDN_DK
GDN_V_W = GDN_HEADS * GDN_DV
CONV_W = 4
CONV_CH = 2 * GDN_QK_W + GDN_V_W
GDN_CHUNK = 64
ATTN_GROUPS = ((128, 1), (512, 4), (2048, 16))
N_GROUPS = 3
GROUP_HEADS = 4
HEAD_DIM = 128
GROUP_W = GROUP_HEADS * HEAD_DIM
ATTN_W = N_GROUPS * GROUP_W
Q_BLOCK = 128
N_BUCKETS = 32
MAX_DISTANCE = 2048
PEER_HEADS = 8
N_KEYS = 128
PEER_TOPK = 16
EPS = 1e-6
NEG_INF = -1e30

OFF_QKV = 0
OFF_Z = CONV_CH
OFF_AQ = OFF_Z + GDN_V_W
OFF_AK = OFF_AQ + ATTN_W
OFF_AV = OFF_AK + ATTN_W
OFF_GA = OFF_AV + ATTN_W
OFF_GB = OFF_GA + D_MODEL
OFF_AB = OFF_GB + D_MODEL
N_PROJ = 21504

GDN_HB = 8
GDN_HG = GDN_HEADS // GDN_HB
VMEM_LIMIT = 56 * 1024 * 1024


def _cparams(sem):
    return pltpu.CompilerParams(dimension_semantics=sem, vmem_limit_bytes=VMEM_LIMIT)


def _sigmoid(x):
    return 1.0 / (1.0 + jnp.exp(-x))


def _silu(x):
    return x * _sigmoid(x)


def _softplus(x):
    return jnp.maximum(x, 0.0) + jnp.log(1.0 + jnp.exp(-jnp.abs(x)))


def _bdot(a, b):
    return jnp.dot(a.astype(BF16), b.astype(BF16), preferred_element_type=F32)


def _bdot_nt(a, b):
    return lax.dot_general(a.astype(BF16), b.astype(BF16), (((1,), (1,)), ((), ())),
                           preferred_element_type=F32)


def _bdot_tn(a, b):
    return lax.dot_general(a.astype(BF16), b.astype(BF16), (((0,), (0,)), ((), ())),
                           preferred_element_type=F32)


def _hdot(a, b):
    return jnp.dot(a, b, precision=HIGHEST, preferred_element_type=F32)


def _split_bf16(a):
    hi = a.astype(BF16)
    lo = (a - hi.astype(F32)).astype(BF16)
    return hi, lo


def _dot3(a_parts, b_parts):
    a_hi, a_lo = a_parts
    b_hi, b_lo = b_parts
    return (jnp.dot(a_hi, b_hi, preferred_element_type=F32)
            + jnp.dot(a_lo, b_hi, preferred_element_type=F32)
            + jnp.dot(a_hi, b_lo, preferred_element_type=F32))


def _rmsnorm_body(x_ref, g_ref, o_ref):
    x = x_ref[...]
    ms = jnp.mean(x * x, axis=-1, keepdims=True)
    o_ref[...] = (x * lax.rsqrt(ms + EPS) * g_ref[...]).astype(o_ref.dtype)


def rmsnorm_bf16(x, gain):
    M, D = x.shape
    tm = min(M, 256)
    return pl.pallas_call(
        _rmsnorm_body,
        grid=(M // tm,),
        in_specs=[pl.BlockSpec((tm, D), lambda i: (i, 0)),
                  pl.BlockSpec((1, D), lambda i: (0, 0))],
        out_specs=pl.BlockSpec((tm, D), lambda i: (i, 0)),
        out_shape=jax.ShapeDtypeStruct((M, D), BF16),
        compiler_params=_cparams(("parallel",)),
        name="rmsnorm",
    )(x, gain.reshape(1, D))


def _mm_body(a_ref, b_ref, o_ref):
    o_ref[...] = jnp.dot(a_ref[...], b_ref[...], preferred_element_type=F32)


def _mm_res_body(a_ref, b_ref, r_ref, o_ref):
    o_ref[...] = r_ref[...] + jnp.dot(a_ref[...], b_ref[...], preferred_element_type=F32)


def matmul(a, b, residual=None, tm=1024, tn=512, name="matmul"):
    M, K = a.shape
    N = b.shape[1]
    tm = min(tm, M)
    tn = min(tn, N)
    in_specs = [pl.BlockSpec((tm, K), lambda i, j: (i, 0)),
                pl.BlockSpec((K, tn), lambda i, j: (0, j))]
    args = [a, b]
    body = _mm_body
    if residual is not None:
        in_specs.append(pl.BlockSpec((tm, tn), lambda i, j: (i, j)))
        args.append(residual)
        body = _mm_res_body
    return pl.pallas_call(
        body,
        grid=(M // tm, N // tn),
        in_specs=in_specs,
        out_specs=pl.BlockSpec((tm, tn), lambda i, j: (i, j)),
        out_shape=jax.ShapeDtypeStruct((M, N), F32),
        compiler_params=_cparams(("parallel", "parallel")),
        name=name,
    )(*args)


def _mm_nt_body(a_ref, bt_ref, o_ref):
    o_ref[...] = lax.dot_general(a_ref[...], bt_ref[...], (((1,), (1,)), ((), ())),
                                 preferred_element_type=F32)


def matmul_nt(a, bt, tm=1024, tn=512, name="matmul_nt"):
    M, K = a.shape
    N = bt.shape[0]
    tm = min(tm, M)
    return pl.pallas_call(
        _mm_nt_body,
        grid=(M // tm, N // tn),
        in_specs=[pl.BlockSpec((tm, K), lambda i, j: (i, 0)),
                  pl.BlockSpec((tn, K), lambda i, j: (j, 0))],
        out_specs=pl.BlockSpec((tm, tn), lambda i, j: (i, j)),
        out_shape=jax.ShapeDtypeStruct((M, N), F32),
        compiler_params=_cparams(("parallel", "parallel")),
        name=name,
    )(a, bt)


def _gdn_chunk_body(q_ref, k_ref, v_ref, z_ref, ab_ref, cwq_ref, cwk_ref, cwv_ref,
                    alog_ref, dtb_ref, gn_ref, o_ref, s_out_ref,
                    xq_s, xk_s, xv_s, state_s):
    n = pl.program_id(2)
    C = GDN_CHUNK
    HW = GDN_HB * GDN_DK

    @pl.when(n == 0)
    def _():
        xq_s[0:8, :] = jnp.zeros((8, HW), F32)
        xk_s[0:8, :] = jnp.zeros((8, HW), F32)
        xv_s[0:8, :] = jnp.zeros((8, HW), F32)
        state_s[...] = jnp.zeros(state_s.shape, F32)

    def conv(x_ref, xs, cw_ref):
        xs[8:8 + C, :] = x_ref[0]
        y = cw_ref[CONV_W - 1:CONV_W, :] * xs[8:8 + C, :]
        for w in range(CONV_W - 1):
            y = y + cw_ref[w:w + 1, :] * xs[5 + w:5 + w + C, :]
        xs[0:8, :] = xs[C:C + 8, :]
        return _silu(y)

    q_all = conv(q_ref, xq_s, cwq_ref)
    k_all = conv(k_ref, xk_s, cwk_ref)
    v_all = conv(v_ref, xv_s, cwv_ref)

    ab = ab_ref[0, 0]
    g_all = -jnp.exp(alog_ref[0]) * _softplus(ab + dtb_ref[0])
    beta_all = _sigmoid(ab)
    ri = lax.broadcasted_iota(jnp.int32, (C, C), 0)
    ci = lax.broadcasted_iota(jnp.int32, (C, C), 1)
    tri = ri >= ci
    strict = ri > ci
    gc_all = _hdot(tri.astype(F32), g_all)
    gc_t = jnp.concatenate([gc_all, jnp.zeros((128 - C, 128), F32)], axis=0).T

    H = range(GDN_HB)
    sls = [slice(j * GDN_DK, (j + 1) * GDN_DK) for j in H]
    qh, kh, kb, gcol, decay, rhs = [], [], [], [], [], []
    for j in H:
        q = q_all[:, sls[j]]
        k = k_all[:, sls[j]]
        q = q * lax.rsqrt(jnp.sum(q * q, axis=-1, keepdims=True) + EPS) * (GDN_DK ** -0.5)
        k = k * lax.rsqrt(jnp.sum(k * k, axis=-1, keepdims=True) + EPS)
        beta = beta_all[:, GDN_HB + j:GDN_HB + j + 1]
        gc = gc_all[:, j:j + 1]
        grow = gc_t[j:j + 1, :C]
        qh.append(q)
        kh.append(k)
        kb.append(k * beta)
        gcol.append(gc)
        decay.append(jnp.where(tri, jnp.exp(jnp.where(tri, gc - grow, 0.0)), 0.0))
        rhs.append(jnp.concatenate([v_all[:, sls[j]] * beta, kb[j] * jnp.exp(gc)], axis=-1))
    kq = [_bdot_nt(jnp.concatenate([kb[j], qh[j]], axis=0), kh[j]) for j in H]
    intra = [jnp.where(tri, kq[j][C:] * decay[j], 0.0) for j in H]
    acc = [-jnp.where(strict, kq[j][:C] * decay[j], 0.0) for j in H]
    parts = [_split_bf16(acc[j]) for j in H]
    pw = [_dot3(parts[j], parts[j]) for j in H]
    for r in range(5):
        pw_parts = [_split_bf16(pw[j]) for j in H]
        acc_parts = [_split_bf16(acc[j]) for j in H]
        acc = [acc[j] + pw[j] + _dot3(pw_parts[j], acc_parts[j]) for j in H]
        if r < 4:
            pw = [_dot3(pw_parts[j], pw_parts[j]) for j in H]
    sol = [rhs[j] + _dot3(_split_bf16(acc[j]), _split_bf16(rhs[j])) for j in H]
    S = [state_s[j] for j in H]
    wq_s = [_bdot(jnp.concatenate([sol[j][:, GDN_DV:], qh[j] * jnp.exp(gcol[j])], axis=0), S[j])
            for j in H]
    v_new = [sol[j][:, :GDN_DV] - wq_s[j][:C] for j in H]
    o = [wq_s[j][C:] + _bdot(intra[j], v_new[j]) for j in H]
    for j in H:
        g_last = gcol[j][C - 1:C, :]
        state_s[j] = (S[j] * jnp.exp(g_last)
                      + _bdot_tn(kh[j] * jnp.exp(g_last - gcol[j]), v_new[j]))
    for j in H:
        ms = jnp.mean(o[j] * o[j], axis=-1, keepdims=True)
        on = o[j] * lax.rsqrt(ms + EPS) * gn_ref[...]
        o_ref[0, :, sls[j]] = (on * _silu(z_ref[0, :, sls[j]])).astype(o_ref.dtype)

    @pl.when(n == pl.num_programs(2) - 1)
    def _():
        s_out_ref[0] = state_s[...]


def gdn_prompt(proj3, ab_g, conv_w, alog_g, dtb_g, gdn_norm):
    B, T, _ = proj3.shape
    C = GDN_CHUNK
    N = T // C
    HW = GDN_HB * GDN_DK
    nq = GDN_QK_W // HW

    def col(base):
        return lambda b, hg, n: (b, n, base + hg)

    def cwcol(base):
        return lambda b, hg, n: (0, base + hg)

    return pl.pallas_call(
        _gdn_chunk_body,
        grid=(B, GDN_HG, N),
        in_specs=[
            pl.BlockSpec((1, C, HW), col(0)),
            pl.BlockSpec((1, C, HW), col(nq)),
            pl.BlockSpec((1, C, HW), col(2 * nq)),
            pl.BlockSpec((1, C, HW), col(OFF_Z // HW)),
            pl.BlockSpec((1, 1, C, 128), lambda b, hg, n: (b, hg, n, 0)),
            pl.BlockSpec((CONV_W, HW), cwcol(0)),
            pl.BlockSpec((CONV_W, HW), cwcol(nq)),
            pl.BlockSpec((CONV_W, HW), cwcol(2 * nq)),
            pl.BlockSpec((1, 1, 128), lambda b, hg, n: (hg, 0, 0)),
            pl.BlockSpec((1, 1, 128), lambda b, hg, n: (hg, 0, 0)),
            pl.BlockSpec((1, GDN_DV), lambda b, hg, n: (0, 0)),
        ],
        out_specs=[
            pl.BlockSpec((1, C, HW), lambda b, hg, n: (b, n, hg)),
            pl.BlockSpec((1, GDN_HB, GDN_DK, GDN_DV), lambda b, hg, n: (b, hg, 0, 0)),
        ],
        out_shape=[
            jax.ShapeDtypeStruct((B, T, GDN_V_W), BF16),
            jax.ShapeDtypeStruct((B, GDN_HEADS, GDN_DK, GDN_DV), F32),
        ],
        scratch_shapes=[
            pltpu.VMEM((C + 8, HW), F32),
            pltpu.VMEM((C + 8, HW), F32),
            pltpu.VMEM((C + 8, HW), F32),
            pltpu.VMEM((GDN_HB, GDN_DK, GDN_DV), F32),
        ],
        compiler_params=_cparams(("parallel", "parallel", "arbitrary")),
        name="gdn_prompt",
    )(proj3, proj3, proj3, proj3, ab_g, conv_w, conv_w, conv_w, alog_g, dtb_g,
      gdn_norm.reshape(1, GDN_DV))


def _gdn_step_body(x_ref, z_ref, ab_ref, cs_ref, s_ref, cw_ref, alog_ref, dtb_ref, gn_ref,
                   o_ref, cs_out_ref, s_out_ref):
    x = x_ref[0]
    y = cw_ref[CONV_W - 1] * x
    for w in range(CONV_W - 1):
        y = y + cw_ref[w] * cs_ref[0, w]
    y = _silu(y)
    cs_out_ref[0, 0] = cs_ref[0, 1]
    cs_out_ref[0, 1] = cs_ref[0, 2]
    cs_out_ref[0, 2] = x
    r_ab = (OFF_AB // 128) % 8
    ab = ab_ref[0, r_ab:r_ab + 1, :]
    g_all = -jnp.exp(alog_ref[...]) * _softplus(ab + dtb_ref[...])
    beta_all = _sigmoid(ab)
    eye = (lax.broadcasted_iota(jnp.int32, (GDN_DK, GDN_DK), 0)
           == lax.broadcasted_iota(jnp.int32, (GDN_DK, GDN_DK), 1))

    def to_col(row):
        return jnp.sum(jnp.where(eye, jnp.broadcast_to(row, (GDN_DK, GDN_DK)), 0.0),
                       axis=1, keepdims=True)

    for h in range(GDN_HEADS):
        q = y[h:h + 1, :]
        k = y[GDN_HEADS + h:GDN_HEADS + h + 1, :]
        v = y[2 * GDN_HEADS + h:2 * GDN_HEADS + h + 1, :]
        q = q * lax.rsqrt(jnp.sum(q * q, axis=-1, keepdims=True) + EPS) * (GDN_DK ** -0.5)
        k = k * lax.rsqrt(jnp.sum(k * k, axis=-1, keepdims=True) + EPS)
        g = g_all[:, h:h + 1]
        beta = beta_all[:, GDN_HEADS + h:GDN_HEADS + h + 1]
        eg = jnp.exp(g)
        S = s_ref[0, h]
        kcol = to_col(k)
        qcol = to_col(q)
        k_s = jnp.sum(S * kcol, axis=0, keepdims=True)
        q_s = jnp.sum(S * qcol, axis=0, keepdims=True)
        v_new = v * beta - (beta * eg) * k_s
        o = eg * q_s + jnp.sum(q * k, axis=-1, keepdims=True) * v_new
        s_out_ref[0, h] = S * eg + kcol * v_new
        ms = jnp.mean(o * o, axis=-1, keepdims=True)
        on = o * lax.rsqrt(ms + EPS) * gn_ref[...]
        o_ref[0, h:h + 1, :] = on * _silu(z_ref[0, h:h + 1, :])


def gdn_step(proj_s, conv_state, ssm_state, conv_w, a_log, dt_bias, gdn_norm):
    Bs = proj_s.shape[0]
    nrow = N_PROJ // 128
    p3 = proj_s.reshape(Bs, nrow, 128)
    cs4 = conv_state.reshape(Bs, CONV_W - 1, CONV_CH // 128, 128)
    cw3 = conv_w.reshape(CONV_W, CONV_CH // 128, 128)
    pad = jnp.zeros((128 - GDN_HEADS,), F32)
    alog = jnp.concatenate([a_log, pad]).reshape(1, 128)
    dtb = jnp.concatenate([dt_bias, pad]).reshape(1, 128)
    nr = CONV_CH // 128
    o, cs_new, s_new = pl.pallas_call(
        _gdn_step_body,
        grid=(Bs,),
        in_specs=[
            pl.BlockSpec((1, nr, 128), lambda b: (b, 0, 0)),
            pl.BlockSpec((1, GDN_HEADS, 128), lambda b: (b, OFF_Z // 128 // GDN_HEADS, 0)),
            pl.BlockSpec((1, 8, 128), lambda b: (b, OFF_AB // 128 // 8, 0)),
            pl.BlockSpec((1, CONV_W - 1, nr, 128), lambda b: (b, 0, 0, 0)),
            pl.BlockSpec((1, GDN_HEADS, GDN_DK, GDN_DV), lambda b: (b, 0, 0, 0)),
            pl.BlockSpec((CONV_W, nr, 128), lambda b: (0, 0, 0)),
            pl.BlockSpec((1, 128), lambda b: (0, 0)),
            pl.BlockSpec((1, 128), lambda b: (0, 0)),
            pl.BlockSpec((1, GDN_DV), lambda b: (0, 0)),
        ],
        out_specs=[
            pl.BlockSpec((1, GDN_HEADS, GDN_DV), lambda b: (b, 0, 0)),
            pl.BlockSpec((1, CONV_W - 1, nr, 128), lambda b: (b, 0, 0, 0)),
            pl.BlockSpec((1, GDN_HEADS, GDN_DK, GDN_DV), lambda b: (b, 0, 0, 0)),
        ],
        out_shape=[
            jax.ShapeDtypeStruct((Bs, GDN_HEADS, GDN_DV), F32),
            jax.ShapeDtypeStruct((Bs, CONV_W - 1, nr, 128), F32),
            jax.ShapeDtypeStruct((Bs, GDN_HEADS, GDN_DK, GDN_DV), F32),
        ],
        compiler_params=_cparams(("parallel",)),
        name="gdn_step",
    )(p3, p3, p3, cs4, ssm_state, cw3, alog, dtb, gdn_norm.reshape(1, GDN_DV))
    return (o.reshape(Bs, GDN_V_W), cs_new.reshape(Bs, CONV_W - 1, CONV_CH), s_new)


def _qknorm_body(q_ref, k_ref, v_ref, qg_ref, kg_ref, qn_ref, kv_ref):
    for h in range(GROUP_HEADS):
        sl = slice(h * HEAD_DIM, (h + 1) * HEAD_DIM)
        q = q_ref[:, sl]
        qn = q * lax.rsqrt(jnp.mean(q * q, axis=-1, keepdims=True) + EPS) * qg_ref[...]
        qn_ref[:, sl] = qn * (HEAD_DIM ** -0.5)
        k = k_ref[:, sl]
        kv_ref[:, sl] = k * lax.rsqrt(jnp.mean(k * k, axis=-1, keepdims=True) + EPS) * kg_ref[...]
    kv_ref[:, GROUP_W:] = v_ref[...]


def qk_norm(proj, q_norm, k_norm):
    M = proj.shape[0]
    tm = min(M, 512)
    blk = GROUP_W
    return pl.pallas_call(
        _qknorm_body,
        grid=(M // tm, N_GROUPS),
        in_specs=[
            pl.BlockSpec((tm, blk), lambda i, g: (i, OFF_AQ // blk + g)),
            pl.BlockSpec((tm, blk), lambda i, g: (i, OFF_AK // blk + g)),
            pl.BlockSpec((tm, blk), lambda i, g: (i, OFF_AV // blk + g)),
            pl.BlockSpec((1, HEAD_DIM), lambda i, g: (0, 0)),
            pl.BlockSpec((1, HEAD_DIM), lambda i, g: (0, 0)),
        ],
        out_specs=[
            pl.BlockSpec((tm, blk), lambda i, g: (i, g)),
            pl.BlockSpec((tm, 2 * blk), lambda i, g: (i, g)),
        ],
        out_shape=[jax.ShapeDtypeStruct((M, ATTN_W), F32),
                   jax.ShapeDtypeStruct((M, N_GROUPS * 2 * GROUP_W), F32)],
        compiler_params=_cparams(("parallel", "parallel")),
        name="qk_norm",
    )(proj, proj, proj, q_norm.reshape(1, HEAD_DIM), k_norm.reshape(1, HEAD_DIM))


def _t5_bucket(dist):
    max_exact = N_BUCKETS // 2
    d = np.maximum(dist, max_exact).astype(np.float32)
    large = max_exact + (np.log(d / max_exact) / np.log(MAX_DISTANCE / max_exact)
                         * (N_BUCKETS - max_exact)).astype(np.int32)
    large = np.minimum(large, N_BUCKETS - 1)
    return np.where(dist < max_exact, dist, large).astype(np.int32)


def _group_bias(rel_bias, gi):
    window, dilation = ATTN_GROUPS[gi]
    n_keys = window // dilation + 1
    buckets = _t5_bucket(np.arange(n_keys) * dilation)
    return rel_bias[buckets][:, gi * GROUP_HEADS:(gi + 1) * GROUP_HEADS].T.astype(F32)


def _attn_prompt_body(T, q0_ref, q1_ref, q2_ref, k0_ref, v0_ref, k1_ref, v1_ref, k2_ref, v2_ref,
                      bias_ref, y_ref, m_s, l_s, o_s):
    QB = Q_BLOCK
    q_refs = (q0_ref, q1_ref, q2_ref)
    k_refs = (k0_ref, k1_ref, k2_ref)
    v_refs = (v0_ref, v1_ref, v2_ref)
    for gi, (window, d) in enumerate(ATTN_GROUPS):
        nq = T // (d * QB)
        for r in range(d):
            for qi in range(nq):
                rows = pl.ds(r + d * QB * qi, QB, stride=d) if d > 1 else pl.ds(QB * qi, QB)
                q = q_refs[gi][0, rows, :].astype(BF16)
                kc = k_refs[gi][0, rows, :]
                vc = v_refs[gi][0, rows, :]
                if qi > 0:
                    prev = (pl.ds(r + d * QB * (qi - 1), QB, stride=d) if d > 1
                            else pl.ds(QB * (qi - 1), QB))
                    kcat = jnp.concatenate([k_refs[gi][0, prev, :], kc], axis=0).astype(BF16)
                    vcat = jnp.concatenate([v_refs[gi][0, prev, :], vc], axis=0).astype(BF16)
                    bias = bias_ref[gi, 0]
                else:
                    kcat = kc.astype(BF16)
                    vcat = vc.astype(BF16)
                    bias = bias_ref[gi, 0, :, QB:]
                s = lax.dot_general(q, kcat, (((1,), (1,)), ((), ())),
                                    preferred_element_type=F32) + bias
                m_blk = jnp.max(s, axis=-1, keepdims=True)
                if gi == 0:
                    m_new = m_blk
                else:
                    m_old = m_s[rows, :][:, 0:1]
                    m_new = jnp.maximum(m_old, m_blk)
                p = jnp.exp(s - m_new)
                l_new = jnp.sum(p, axis=-1, keepdims=True)
                o_new = jnp.dot(p.astype(BF16), vcat, preferred_element_type=F32)
                if gi > 0:
                    alpha = jnp.exp(m_old - m_new)
                    l_new = l_new + alpha * l_s[rows, :][:, 0:1]
                    o_new = o_new + alpha * o_s[rows, :]
                m_s[rows, :] = jnp.broadcast_to(m_new, (QB, HEAD_DIM))
                l_s[rows, :] = jnp.broadcast_to(l_new, (QB, HEAD_DIM))
                o_s[rows, :] = o_new
    y_ref[0] = (o_s[...] / l_s[...]).astype(y_ref.dtype)


def attn_prompt(qn3, kv3, bias_all):
    B, T, _ = qn3.shape
    blk = (1, T, HEAD_DIM)

    def qspec(gi):
        return pl.BlockSpec(blk, lambda b, h: (b, 0, gi * GROUP_HEADS + h))

    def kvspec(gi, is_v):
        return pl.BlockSpec(blk, lambda b, h: (b, 0, (2 * gi + is_v) * GROUP_HEADS + h))

    in_specs = [qspec(0), qspec(1), qspec(2)]
    for gi in range(N_GROUPS):
        in_specs += [kvspec(gi, 0), kvspec(gi, 1)]
    in_specs.append(pl.BlockSpec((N_GROUPS, 1, Q_BLOCK, 2 * Q_BLOCK), lambda b, h: (0, h, 0, 0)))
    y = pl.pallas_call(
        functools.partial(_attn_prompt_body, T),
        grid=(B, GROUP_HEADS),
        in_specs=in_specs,
        out_specs=pl.BlockSpec(blk, lambda b, h: (b, 0, h)),
        out_shape=jax.ShapeDtypeStruct((B, T, GROUP_W), BF16),
        scratch_shapes=[pltpu.VMEM((T, HEAD_DIM), F32), pltpu.VMEM((T, HEAD_DIM), F32),
                        pltpu.VMEM((T, HEAD_DIM), F32)],
        compiler_params=_cparams(("parallel", "parallel")),
        name="attn_prompt",
    )(qn3, qn3, qn3, kv3, kv3, kv3, kv3, kv3, kv3, bias_all)
    return y.reshape(B * T, GROUP_W)


def _prompt_bias_matrix(rel_bias, gi):
    QB = Q_BLOCK
    bias = _group_bias(rel_bias, gi)
    row0 = jnp.concatenate([bias[:, ::-1], jnp.full((GROUP_HEADS, QB), NEG_INF, F32)], axis=1)
    flat = jnp.tile(row0, (1, QB))[:, :QB * 2 * QB]
    return flat.reshape(GROUP_HEADS, QB, 2 * QB)


def _attn_decode_body(q_ref, n0_ref, n1_ref, n2_ref, c0_ref, c1_ref, c2_ref, bc_ref, bn_ref, y_ref):
    new_refs = (n0_ref, n1_ref, n2_ref)
    cache_refs = (c0_ref, c1_ref, c2_ref)
    outs, lds = [], []
    for gi in range(N_GROUPS):
        q = q_ref[0, gi * GROUP_HEADS:(gi + 1) * GROUP_HEADS, :]
        kc = cache_refs[gi][:, 0]
        vc = cache_refs[gi][:, 1]
        kn = new_refs[gi][0, 0:GROUP_HEADS, :]
        vn = new_refs[gi][0, GROUP_HEADS:2 * GROUP_HEADS, :]
        s_c = jnp.sum(kc * q[None], axis=-1, keepdims=True) + bc_ref[gi, :, :, 0:1]
        s_n = jnp.sum(kn * q, axis=-1, keepdims=True) + bn_ref[gi, :, 0:1]
        m = jnp.maximum(jnp.max(s_c, axis=0), s_n)
        p_c = jnp.exp(s_c - m[None])
        p_n = jnp.exp(s_n - m)
        l = jnp.sum(p_c, axis=0) + p_n
        outs.append((jnp.sum(p_c * vc, axis=0) + p_n * vn) / l)
        lds.append(m + jnp.log(l))
    m = jnp.maximum(jnp.maximum(lds[0], lds[1]), lds[2])
    es = [jnp.exp(x - m) for x in lds]
    den = es[0] + es[1] + es[2]
    y_ref[0] = outs[0] * (es[0] / den) + outs[1] * (es[1] / den) + outs[2] * (es[2] / den)


def attn_decode(qn_s, kv_new, caches, rel_bias):
    Bs = qn_s.shape[0]
    q3 = qn_s.reshape(Bs, N_GROUPS * GROUP_HEADS, HEAD_DIM)
    new3 = [x.reshape(Bs, 2 * GROUP_HEADS, HEAD_DIM) for x in kv_new]
    cviews, bcs, bns = [], [], []
    for gi, (window, d) in enumerate(ATTN_GROUPS):
        nk = window // d
        assert caches[gi].shape[1] == window and nk == Q_BLOCK
        cviews.append(caches[gi].reshape(Bs, nk, d, 2, GROUP_HEADS, HEAD_DIM))
        bias = _group_bias(rel_bias, gi)
        bc = bias[:, nk - np.arange(nk)]
        bcs.append(jnp.broadcast_to(bc.T[:, :, None], (nk, GROUP_HEADS, HEAD_DIM)))
        bns.append(jnp.broadcast_to(bias[:, 0:1], (GROUP_HEADS, HEAD_DIM)))
    bc_all = jnp.stack(bcs, 0)
    bn_all = jnp.stack(bns, 0)
    nspec = pl.BlockSpec((1, 2 * GROUP_HEADS, HEAD_DIM), lambda b: (b, 0, 0))
    cspec = pl.BlockSpec((None, Q_BLOCK, None, 2, GROUP_HEADS, HEAD_DIM),
                         lambda b: (b, 0, 0, 0, 0, 0))
    y = pl.pallas_call(
        _attn_decode_body,
        grid=(Bs,),
        in_specs=[
            pl.BlockSpec((1, N_GROUPS * GROUP_HEADS, HEAD_DIM), lambda b: (b, 0, 0)),
            nspec, nspec, nspec, cspec, cspec, cspec,
            pl.BlockSpec((N_GROUPS, Q_BLOCK, GROUP_HEADS, HEAD_DIM), lambda b: (0, 0, 0, 0)),
            pl.BlockSpec((N_GROUPS, GROUP_HEADS, HEAD_DIM), lambda b: (0, 0, 0)),
        ],
        out_specs=pl.BlockSpec((1, GROUP_HEADS, HEAD_DIM), lambda b: (b, 0, 0)),
        out_shape=jax.ShapeDtypeStruct((Bs, GROUP_HEADS, HEAD_DIM), F32),
        compiler_params=_cparams(("parallel",)),
        name="attn_decode",
    )(q3, *new3, *cviews, bc_all, bn_all)
    return y.reshape(Bs, GROUP_W)


def _merge_body(ya_ref, yb_ref, wa_ref, wb_ref, ga_ref, gb_ref, o_ref):
    pa = jnp.dot(ya_ref[...], wa_ref[...], preferred_element_type=F32)
    pb = jnp.dot(yb_ref[...], wb_ref[...], preferred_element_type=F32)
    o_ref[...] = (_sigmoid(ga_ref[...]) * pa + _sigmoid(gb_ref[...]) * pb).astype(o_ref.dtype)


def branch_merge(ya, yb, wa, wb, proj):
    M = ya.shape[0]
    tm = min(M, 1024)
    tn = 512
    return pl.pallas_call(
        _merge_body,
        grid=(M // tm, D_MODEL // tn),
        in_specs=[
            pl.BlockSpec((tm, GDN_V_W), lambda i, j: (i, 0)),
            pl.BlockSpec((tm, GROUP_W), lambda i, j: (i, 0)),
            pl.BlockSpec((GDN_V_W, tn), lambda i, j: (0, j)),
            pl.BlockSpec((GROUP_W, tn), lambda i, j: (0, j)),
            pl.BlockSpec((tm, tn), lambda i, j: (i, OFF_GA // tn + j)),
            pl.BlockSpec((tm, tn), lambda i, j: (i, OFF_GB // tn + j)),
        ],
        out_specs=pl.BlockSpec((tm, tn), lambda i, j: (i, j)),
        out_shape=jax.ShapeDtypeStruct((M, D_MODEL), BF16),
        compiler_params=_cparams(("parallel", "parallel")),
        name="branch_merge",
    )(ya, yb, wa, wb, proj, proj)


def _extract_top(x, count):
    R = x.shape[0]
    rid = lax.broadcasted_iota(jnp.int32, x.shape, 0).astype(F32)
    vid = lax.broadcasted_iota(jnp.int32, (count, x.shape[1]), 0)
    vals = jnp.zeros((count, x.shape[1]), F32)
    rank = jnp.full(x.shape, float(count), F32)
    for a in range(count):
        m = jnp.max(x, axis=0, keepdims=True)
        first = jnp.min(jnp.where(x == m, rid, float(R)), axis=0, keepdims=True)
        hit = rid == first
        x = jnp.where(hit, -jnp.inf, x)
        rank = jnp.where(hit, float(a), rank)
        vals = jnp.where(vid == a, m, vals)
    return vals, rank


CAND_HALF = PEER_TOPK // 2
N_CAND = PEER_TOPK + (CAND_HALF - 1) * CAND_HALF + CAND_HALF


def _peer_topk_body(q_ref, sk_ref, cnt_ref, r2_ref, e1_ref, e2_ref, cand_s):
    K = PEER_TOPK
    KH = CAND_HALF

    def head(h, carry):
        off = pl.multiple_of(h * (2 * N_KEYS), 2 * N_KEYS)
        q1 = q_ref[:, pl.ds(off, N_KEYS)]
        q2 = q_ref[:, pl.ds(pl.multiple_of(off + N_KEYS, N_KEYS), N_KEYS)]
        s1 = lax.dot_general(sk_ref[0], q1, (((1,), (1,)), ((), ())), precision=HIGHEST,
                             preferred_element_type=F32)
        s2 = lax.dot_general(sk_ref[1], q2, (((1,), (1,)), ((), ())), precision=HIGHEST,
                             preferred_element_type=F32)
        v1, rank1 = _extract_top(s1, K)
        v2, rank2 = _extract_top(s2, K)
        cand_s[0:K, :] = v2 + v1[0:1, :]
        for a in range(1, KH):
            cand_s[K + (a - 1) * KH:K + a * KH, :] = v2[0:KH, :] + v1[a:a + 1, :]
        cand_s[N_CAND - KH:N_CAND, :] = v1[KH:K, :] + v2[0:1, :]
        best, crank = _extract_top(cand_s[...], K)
        picked = jnp.where(crank < float(K), 1.0, 0.0)
        vid = lax.broadcasted_iota(jnp.int32, (KH, picked.shape[1]), 0)
        low = jnp.where(vid == 0, jnp.sum(picked[0:K], axis=0, keepdims=True), 0.0)
        for a in range(1, KH):
            rows = picked[K + (a - 1) * KH:K + a * KH]
            low = jnp.where(vid == a, jnp.sum(rows, axis=0, keepdims=True), low)
        count = jnp.concatenate([low, picked[N_CAND - KH:N_CAND]], axis=0)
        cnt1 = jnp.zeros(rank1.shape, F32)
        for a in range(K):
            cnt1 = jnp.where(rank1 == float(a), count[a:a + 1, :], cnt1)
        cmax = best[0:1, :]
        z = jnp.sum(jnp.exp(best - cmax), axis=0, keepdims=True)
        cnt_ref[h] = cnt1
        r2_ref[h] = rank2
        e1_ref[h] = jnp.where(rank1 < float(K), jnp.exp(s1 - v1[0:1, :]), 0.0)
        e2_ref[h] = jnp.where(rank2 < float(K), jnp.exp(s2 - v2[0:1, :]), 0.0) / z
        return carry

    lax.fori_loop(0, PEER_HEADS, head, 0)


def peer_topk(query, subkeys):
    M = query.shape[0]
    tm = min(M, 256)
    big = jax.ShapeDtypeStruct((PEER_HEADS, N_KEYS, M), F32)
    bspec = pl.BlockSpec((PEER_HEADS, N_KEYS, tm), lambda i: (0, 0, i))
    return pl.pallas_call(
        _peer_topk_body,
        grid=(M // tm,),
        in_specs=[pl.BlockSpec((tm, PEER_HEADS * 2 * N_KEYS), lambda i: (i, 0)),
                  pl.BlockSpec((2, N_KEYS, N_KEYS), lambda i: (0, 0, 0))],
        out_specs=[bspec, bspec, bspec, bspec],
        out_shape=[big, big, big, big],
        scratch_shapes=[pltpu.VMEM((N_CAND, tm), F32)],
        compiler_params=_cparams(("parallel",)),
        name="peer_topk",
    )(query, subkeys)


def _gelu_exact(x):
    return 0.5 * x * (1.0 + lax.erf(x * (2.0 ** -0.5)))


GATE_ROWS = 8


def _peer_gate_body(hf_ref, u_ref, cnt_ref, r2_ref, e1_ref, e2_ref, w_ref, g_s):
    j = pl.program_id(1)
    tm = hf_ref.shape[0]
    grp = pl.multiple_of(j * GATE_ROWS, GATE_ROWS)

    for kk in range(GATE_ROWS):
        for tc in range(tm // 128):
            ts = slice(tc * 128, (tc + 1) * 128)
            acc = jnp.zeros((N_KEYS, 128), F32)
            for h in range(PEER_HEADS):
                cnt_row = cnt_ref[h, pl.ds(grp, GATE_ROWS), ts][kk:kk + 1, :]
                e1_row = e1_ref[h, pl.ds(grp, GATE_ROWS), ts][kk:kk + 1, :]
                acc = acc + jnp.where(r2_ref[h, :, ts] < cnt_row, e2_ref[h, :, ts] * e1_row, 0.0)
            g_s[ts, kk * N_KEYS:(kk + 1) * N_KEYS] = acc.T
    hmat = lax.dot_general(hf_ref[...], u_ref[...], (((1,), (1,)), ((), ())),
                           preferred_element_type=F32)
    w_ref[...] = (g_s[...] * _gelu_exact(hmat)).astype(w_ref.dtype)


def peer_gate_weights(hf, u_tab, cnt, r2, e1, e2, tm=512):
    M = hf.shape[0]
    tm = min(M, tm)
    tn = GATE_ROWS * N_KEYS
    E = u_tab.shape[0]
    sspec = pl.BlockSpec((PEER_HEADS, N_KEYS, tm), lambda i, j: (0, 0, i))
    return pl.pallas_call(
        _peer_gate_body,
        grid=(M // tm, E // tn),
        in_specs=[
            pl.BlockSpec((tm, D_MODEL), lambda i, j: (i, 0)),
            pl.BlockSpec((tn, D_MODEL), lambda i, j: (j, 0)),
            sspec, sspec, sspec, sspec,
        ],
        out_specs=pl.BlockSpec((tm, tn), lambda i, j: (i, j)),
        out_shape=jax.ShapeDtypeStruct((M, E), BF16),
        scratch_shapes=[pltpu.VMEM((tm, tn), F32)],
        compiler_params=_cparams(("parallel", "arbitrary")),
        name="peer_gate",
    )(hf, u_tab, cnt, r2, e1, e2)


def _mm_kres_body(a_ref, b_ref, r_ref, o_ref):
    k = pl.program_id(2)

    @pl.when(k == 0)
    def _():
        o_ref[...] = r_ref[...]

    o_ref[...] += jnp.dot(a_ref[...], b_ref[...], preferred_element_type=F32)


def matmul_kres(a, b, residual, tm=1024, tn=1024, tk=2048, name="matmul_k"):
    M, K = a.shape
    N = b.shape[1]
    tm = min(tm, M)
    return pl.pallas_call(
        _mm_kres_body,
        grid=(M // tm, N // tn, K // tk),
        in_specs=[pl.BlockSpec((tm, tk), lambda i, j, k: (i, k)),
                  pl.BlockSpec((tk, tn), lambda i, j, k: (k, j)),
                  pl.BlockSpec((tm, tn), lambda i, j, k: (i, j))],
        out_specs=pl.BlockSpec((tm, tn), lambda i, j, k: (i, j)),
        out_shape=jax.ShapeDtypeStruct((M, N), F32),
        compiler_params=_cparams(("parallel", "parallel", "arbitrary")),
        name=name,
    )(a, b, residual)


def peer_ffn(x1, norm_ffn, wq_bf, subkeys, u_bf, v_bf):
    M = x1.shape[0]
    Mp = max(M, 128)
    if Mp != M:
        x1 = jnp.concatenate([x1, jnp.zeros((Mp - M, D_MODEL), F32)], axis=0)
    hf = rmsnorm_bf16(x1, norm_ffn)
    query = matmul(hf, wq_bf, name="peer_query")
    cnt, r2, e1, e2 = peer_topk(query, subkeys)
    w = peer_gate_weights(hf, u_bf, cnt, r2, e1, e2)
    return matmul_kres(w, v_bf, x1, name="peer_out")[:M]


W_BLK = 512
N_AB = 2 * GDN_HEADS
J_AB = OFF_AB // W_BLK


def _prep_w_in_body(a_ref, b_ref, o_ref):
    j = pl.program_id(0)

    @pl.when(j < OFF_AQ // W_BLK)
    def _():
        o_ref[...] = a_ref[...].astype(o_ref.dtype)

    @pl.when(jnp.logical_and(j >= OFF_AQ // W_BLK, j < J_AB))
    def _():
        o_ref[0:W_BLK - N_AB, :] = a_ref[N_AB:, :].astype(o_ref.dtype)
        o_ref[W_BLK - N_AB:, :] = b_ref[...].astype(o_ref.dtype)

    @pl.when(j == J_AB)
    def _():
        o_ref[0:N_AB, :] = b_ref[...].astype(o_ref.dtype)
        o_ref[N_AB:, :] = jnp.zeros((W_BLK - N_AB, o_ref.shape[1]), o_ref.dtype)


def _prep_w_in(w_in_t):
    D = w_in_t.shape[1]
    j0 = OFF_AQ // W_BLK
    sub = W_BLK // N_AB
    return pl.pallas_call(
        _prep_w_in_body,
        grid=(N_PROJ // W_BLK,),
        in_specs=[
            pl.BlockSpec((W_BLK, D), lambda j: (jnp.where(j == J_AB, j0, j), 0)),
            pl.BlockSpec((N_AB, D), lambda j: (jnp.where(j == J_AB, j0 * sub,
                                                         jnp.where(j >= j0, (j + 1) * sub, 0)), 0)),
        ],
        out_specs=pl.BlockSpec((W_BLK, D), lambda j: (j, 0)),
        out_shape=jax.ShapeDtypeStruct((N_PROJ, D), BF16),
        compiler_params=_cparams(("parallel",)),
        name="prep_w_in",
    )(w_in_t, w_in_t)


def _gdn_group_params(a_log, dt_bias):
    pad = jnp.zeros((GDN_HG, 128 - GDN_HB), F32)
    alog = jnp.concatenate([a_log.reshape(GDN_HG, GDN_HB), pad], axis=1).reshape(GDN_HG, 1, 128)
    dtb = jnp.concatenate([dt_bias.reshape(GDN_HG, GDN_HB), pad], axis=1).reshape(GDN_HG, 1, 128)
    return alog, dtb


def _ab_grouped(proj, B, T):
    ab = proj[:, OFF_AB:OFF_AB + 2 * GDN_HEADS].reshape(B, T, 2, GDN_HG, GDN_HB)
    ab = ab.transpose(0, 3, 1, 2, 4).reshape(B, GDN_HG, T, 2 * GDN_HB)
    return jnp.concatenate([ab, jnp.zeros((B, GDN_HG, T, 128 - 2 * GDN_HB), F32)], axis=-1)


def kernel(x_prompt, x_sample, state_ssm, state_conv, cache_kv_g0, cache_kv_g1, cache_kv_g2, rel_bias, norm_mix, w_in, conv_w, a_log, dt_bias, gdn_norm, q_norm, k_norm, w_branch_a, w_branch_b, w_out, norm_ffn, peer_w_query, peer_subkeys, peer_u, peer_v):
    B, T, D = x_prompt.shape
    Bs = x_sample.shape[0]
    assert state_ssm.shape[0] == 1 and x_sample.shape[1] == 1
    caches = (cache_kv_g0[0], cache_kv_g1[0], cache_kv_g2[0])
    w_cat_t = _prep_w_in(jnp.swapaxes(w_in[0], 0, 1))
    wa_bf = w_branch_a[0].astype(BF16)
    wb_bf = w_branch_b[0].astype(BF16)
    wo_bf = w_out[0].astype(BF16)
    wq_bf = peer_w_query[0].astype(BF16)
    u_bf = peer_u[0].astype(BF16)
    v_bf = peer_v[0].astype(BF16)
    cw = conv_w[0]

    xp = x_prompt.reshape(B * T, D)
    xs = x_sample.reshape(Bs, D)

    proj = matmul_nt(rmsnorm_bf16(xp, norm_mix[0]), w_cat_t, name="in_proj")
    alog_g, dtb_g = _gdn_group_params(a_log[0], dt_bias[0])
    ya, ssm_p = gdn_prompt(proj.reshape(B, T, N_PROJ), _ab_grouped(proj, B, T), cw, alog_g, dtb_g,
                           gdn_norm[0])
    conv_p = proj.reshape(B, T, N_PROJ)[:, T - (CONV_W - 1):, :CONV_CH]
    qn, kv = qk_norm(proj, q_norm[0], k_norm[0])
    kv3 = kv.reshape(B, T, N_GROUPS * 2 * GROUP_W)
    kvs = [kv3[:, :, gi * 2 * GROUP_W:(gi + 1) * 2 * GROUP_W] for gi in range(N_GROUPS)]
    bias_all = jnp.stack([_prompt_bias_matrix(rel_bias, gi) for gi in range(N_GROUPS)], 0)
    yb = attn_prompt(qn.reshape(B, T, ATTN_W), kv3, bias_all)
    merged = branch_merge(ya.reshape(B * T, GDN_V_W), yb, wa_bf, wb_bf, proj)
    x1 = matmul(merged, wo_bf, residual=xp, name="out_proj")
    y_prompt = peer_ffn(x1, norm_ffn[0], wq_bf, peer_subkeys[0], u_bf, v_bf).reshape(B, T, D)
    p_kv = []
    for gi, (window, _) in enumerate(ATTN_GROUPS):
        keep = min(window, T)
        kvg = kvs[gi][:, T - keep:].reshape(B, keep, 2, GROUP_HEADS, HEAD_DIM)
        p_kv.append(kvg[None])

    proj_s = matmul_nt(rmsnorm_bf16(xs, norm_mix[0]), w_cat_t, name="in_proj_s")
    ya_s, conv_s, ssm_s = gdn_step(proj_s, state_conv[0], state_ssm[0], cw, a_log[0], dt_bias[0],
                                   gdn_norm[0])
    qn_s, kv_s = qk_norm(proj_s, q_norm[0], k_norm[0])
    kvs_s = [kv_s[:, gi * 2 * GROUP_W:(gi + 1) * 2 * GROUP_W] for gi in range(N_GROUPS)]
    yb_s = attn_decode(qn_s, kvs_s, caches, rel_bias)
    merged_s = branch_merge(ya_s.astype(BF16), yb_s.astype(BF16), wa_bf, wb_bf, proj_s)
    x1_s = matmul(merged_s, wo_bf, residual=xs, name="out_proj_s")
    y_sample = peer_ffn(x1_s, norm_ffn[0], wq_bf, peer_subkeys[0], u_bf, v_bf).reshape(Bs, 1, D)
    s_kv = [kvs_s[gi].reshape(1, Bs, 1, 2, GROUP_HEADS, HEAD_DIM) for gi in range(N_GROUPS)]

    return (y_prompt, y_sample,
            ssm_p[None], conv_p[None], p_kv[0], p_kv[1], p_kv[2],
            ssm_s[None], conv_s[None], s_kv[0], s_kv[1], s_kv[2])
```

```python
import functools

import numpy as np
import jax
import jax.numpy as jnp
from jax import lax
from jax.experimental import pallas as pl
from jax.experimental.pallas import tpu as pltpu

F32 = jnp.float32
BF16 = jnp.bfloat16
HIGHEST = lax.Precision.HIGHEST

D_MODEL = 4096
GDN_HEADS = 16
GDN_DK = 128
GDN_DV = 128
GDN_QK_W = GDN_HEADS * GDN_DK
GDN_V_W = GDN_HEADS * GDN_DV
CONV_W = 4
CONV_CH = 2 * GDN_QK_W + GDN_V_W
GDN_CHUNK = 64
ATTN_GROUPS = ((128, 1), (512, 4), (2048, 16))
N_GROUPS = 3
GROUP_HEADS = 4
HEAD_DIM = 128
GROUP_W = GROUP_HEADS * HEAD_DIM
ATTN_W = N_GROUPS * GROUP_W
Q_BLOCK = 128
N_BUCKETS = 32
MAX_DISTANCE = 2048
PEER_HEADS = 8
N_KEYS = 128
PEER_TOPK = 16
EPS = 1e-6
NEG_INF = -1e30

OFF_QKV = 0
OFF_Z = CONV_CH
OFF_AQ = OFF_Z + GDN_V_W
OFF_AK = OFF_AQ + ATTN_W
OFF_AV = OFF_AK + ATTN_W
OFF_GA = OFF_AV + ATTN_W
OFF_GB = OFF_GA + D_MODEL
OFF_AB = OFF_GB + D_MODEL
N_PROJ = 21504

GDN_HB = 8
GDN_HG = GDN_HEADS // GDN_HB
VMEM_LIMIT = 56 * 1024 * 1024


def _cparams(sem):
    return pltpu.CompilerParams(dimension_semantics=sem, vmem_limit_bytes=VMEM_LIMIT)


def _sigmoid(x):
    return 1.0 / (1.0 + jnp.exp(-x))


def _silu(x):
    return x * _sigmoid(x)


def _softplus(x):
    return jnp.maximum(x, 0.0) + jnp.log(1.0 + jnp.exp(-jnp.abs(x)))


def _bdot(a, b):
    return jnp.dot(a.astype(BF16), b.astype(BF16), preferred_element_type=F32)


def _bdot_nt(a, b):
    return lax.dot_general(a.astype(BF16), b.astype(BF16), (((1,), (1,)), ((), ())),
                           preferred_element_type=F32)


def _bdot_tn(a, b):
    return lax.dot_general(a.astype(BF16), b.astype(BF16), (((0,), (0,)), ((), ())),
                           preferred_element_type=F32)


def _hdot(a, b):
    return jnp.dot(a, b, precision=HIGHEST, preferred_element_type=F32)


def _split_bf16(a):
    hi = a.astype(BF16)
    lo = (a - hi.astype(F32)).astype(BF16)
    return hi, lo


def _dot3(a_parts, b_parts):
    a_hi, a_lo = a_parts
    b_hi, b_lo = b_parts
    return (jnp.dot(a_hi, b_hi, preferred_element_type=F32)
            + jnp.dot(a_lo, b_hi, preferred_element_type=F32)
            + jnp.dot(a_hi, b_lo, preferred_element_type=F32))


def _rmsnorm_body(x_ref, g_ref, o_ref):
    x = x_ref[...]
    ms = jnp.mean(x * x, axis=-1, keepdims=True)
    o_ref[...] = (x * lax.rsqrt(ms + EPS) * g_ref[...]).astype(o_ref.dtype)


def rmsnorm_bf16(x, gain):
    M, D = x.shape
    tm = min(M, 256)
    return pl.pallas_call(
        _rmsnorm_body,
        grid=(M // tm,),
        in_specs=[pl.BlockSpec((tm, D), lambda i: (i, 0)),
                  pl.BlockSpec((1, D), lambda i: (0, 0))],
        out_specs=pl.BlockSpec((tm, D), lambda i: (i, 0)),
        out_shape=jax.ShapeDtypeStruct((M, D), BF16),
        compiler_params=_cparams(("parallel",)),
        name="rmsnorm",
    )(x, gain.reshape(1, D))


def _mm_body(a_ref, b_ref, o_ref):
    o_ref[...] = jnp.dot(a_ref[...], b_ref[...], preferred_element_type=F32)


def _mm_res_body(a_ref, b_ref, r_ref, o_ref):
    o_ref[...] = r_ref[...] + jnp.dot(a_ref[...], b_ref[...], preferred_element_type=F32)


def matmul(a, b, residual=None, tm=1024, tn=512, name="matmul"):
    M, K = a.shape
    N = b.shape[1]
    tm = min(tm, M)
    tn = min(tn, N)
    in_specs = [pl.BlockSpec((tm, K), lambda i, j: (i, 0)),
                pl.BlockSpec((K, tn), lambda i, j: (0, j))]
    args = [a, b]
    body = _mm_body
    if residual is not None:
        in_specs.append(pl.BlockSpec((tm, tn), lambda i, j: (i, j)))
        args.append(residual)
        body = _mm_res_body
    return pl.pallas_call(
        body,
        grid=(M // tm, N // tn),
        in_specs=in_specs,
        out_specs=pl.BlockSpec((tm, tn), lambda i, j: (i, j)),
        out_shape=jax.ShapeDtypeStruct((M, N), F32),
        compiler_params=_cparams(("parallel", "parallel")),
        name=name,
    )(*args)


def _mm_nt_body(a_ref, bt_ref, o_ref):
    o_ref[...] = lax.dot_general(a_ref[...], bt_ref[...], (((1,), (1,)), ((), ())),
                                 preferred_element_type=F32)


def matmul_nt(a, bt, tm=1024, tn=512, name="matmul_nt"):
    M, K = a.shape
    N = bt.shape[0]
    tm = min(tm, M)
    return pl.pallas_call(
        _mm_nt_body,
        grid=(M // tm, N // tn),
        in_specs=[pl.BlockSpec((tm, K), lambda i, j: (i, 0)),
                  pl.BlockSpec((tn, K), lambda i, j: (j, 0))],
        out_specs=pl.BlockSpec((tm, tn), lambda i, j: (i, j)),
        out_shape=jax.ShapeDtypeStruct((M, N), F32),
        compiler_params=_cparams(("parallel", "parallel")),
        name=name,
    )(a, bt)


def _gdn_chunk_body(q_ref, k_ref, v_ref, z_ref, ab_ref, cwq_ref, cwk_ref, cwv_ref,
                    alog_ref, dtb_ref, gn_ref, o_ref, s_out_ref,
                    xq_s, xk_s, xv_s, state_s):
    n = pl.program_id(2)
    C = GDN_CHUNK
    HW = GDN_HB * GDN_DK

    @pl.when(n == 0)
    def _():
        xq_s[0:8, :] = jnp.zeros((8, HW), F32)
        xk_s[0:8, :] = jnp.zeros((8, HW), F32)
        xv_s[0:8, :] = jnp.zeros((8, HW), F32)
        state_s[...] = jnp.zeros(state_s.shape, F32)

    def conv(x_ref, xs, cw_ref):
        xs[8:8 + C, :] = x_ref[0]
        y = cw_ref[CONV_W - 1:CONV_W, :] * xs[8:8 + C, :]
        for w in range(CONV_W - 1):
            y = y + cw_ref[w:w + 1, :] * xs[5 + w:5 + w + C, :]
        xs[0:8, :] = xs[C:C + 8, :]
        return _silu(y)

    q_all = conv(q_ref, xq_s, cwq_ref)
    k_all = conv(k_ref, xk_s, cwk_ref)
    v_all = conv(v_ref, xv_s, cwv_ref)

    ab = ab_ref[0, 0]
    g_all = -jnp.exp(alog_ref[0]) * _softplus(ab + dtb_ref[0])
    beta_all = _sigmoid(ab)
    ri = lax.broadcasted_iota(jnp.int32, (C, C), 0)
    ci = lax.broadcasted_iota(jnp.int32, (C, C), 1)
    tri = ri >= ci
    strict = ri > ci
    gc_all = _hdot(tri.astype(F32), g_all)
    gc_t = jnp.concatenate([gc_all, jnp.zeros((128 - C, 128), F32)], axis=0).T

    H = range(GDN_HB)
    sls = [slice(j * GDN_DK, (j + 1) * GDN_DK) for j in H]
    qh, kh, kb, gcol, decay, rhs = [], [], [], [], [], []
    for j in H:
        q = q_all[:, sls[j]]
        k = k_all[:, sls[j]]
        q = q * lax.rsqrt(jnp.sum(q * q, axis=-1, keepdims=True) + EPS) * (GDN_DK ** -0.5)
        k = k * lax.rsqrt(jnp.sum(k * k, axis=-1, keepdims=True) + EPS)
        beta = beta_all[:, GDN_HB + j:GDN_HB + j + 1]
        gc = gc_all[:, j:j + 1]
        grow = gc_t[j:j + 1, :C]
        qh.append(q)
        kh.append(k)
        kb.append(k * beta)
        gcol.append(gc)
        decay.append(jnp.where(tri, jnp.exp(jnp.where(tri, gc - grow, 0.0)), 0.0))
        rhs.append(jnp.concatenate([v_all[:, sls[j]] * beta, kb[j] * jnp.exp(gc)], axis=-1))
    kq = [_bdot_nt(jnp.concatenate([kb[j], qh[j]], axis=0), kh[j]) for j in H]
    intra = [jnp.where(tri, kq[j][C:] * decay[j], 0.0) for j in H]
    acc = [-jnp.where(strict, kq[j][:C] * decay[j], 0.0) for j in H]
    parts = [_split_bf16(acc[j]) for j in H]
    pw = [_dot3(parts[j], parts[j]) for j in H]
    for r in range(5):
        pw_parts = [_split_bf16(pw[j]) for j in H]
        acc_parts = [_split_bf16(acc[j]) for j in H]
        acc = [acc[j] + pw[j] + _dot3(pw_parts[j], acc_parts[j]) for j in H]
        if r < 4:
            pw = [_dot3(pw_parts[j], pw_parts[j]) for j in H]
    sol = [rhs[j] + _dot3(_split_bf16(acc[j]), _split_bf16(rhs[j])) for j in H]
    S = [state_s[j] for j in H]
    wq_s = [_bdot(jnp.concatenate([sol[j][:, GDN_DV:], qh[j] * jnp.exp(gcol[j])], axis=0), S[j])
            for j in H]
    v_new = [sol[j][:, :GDN_DV] - wq_s[j][:C] for j in H]
    o = [wq_s[j][C:] + _bdot(intra[j], v_new[j]) for j in H]
    for j in H:
        g_last = gcol[j][C - 1:C, :]
        state_s[j] = (S[j] * jnp.exp(g_last)
                      + _bdot_tn(kh[j] * jnp.exp(g_last - gcol[j]), v_new[j]))
    for j in H:
        ms = jnp.mean(o[j] * o[j], axis=-1, keepdims=True)
        on = o[j] * lax.rsqrt(ms + EPS) * gn_ref[...]
        o_ref[0, :, sls[j]] = (on * _silu(z_ref[0, :, sls[j]])).astype(o_ref.dtype)

    @pl.when(n == pl.num_programs(2) - 1)
    def _():
        s_out_ref[0] = state_s[...]


def gdn_prompt(proj3, ab_g, conv_w, alog_g, dtb_g, gdn_norm):
    B, T, _ = proj3.shape
    C = GDN_CHUNK
    N = T // C
    HW = GDN_HB * GDN_DK
    nq = GDN_QK_W // HW

    def col(base):
        return lambda b, hg, n: (b, n, base + hg)

    def cwcol(base):
        return lambda b, hg, n: (0, base + hg)

    return pl.pallas_call(
        _gdn_chunk_body,
        grid=(B, GDN_HG, N),
        in_specs=[
            pl.BlockSpec((1, C, HW), col(0)),
            pl.BlockSpec((1, C, HW), col(nq)),
            pl.BlockSpec((1, C, HW), col(2 * nq)),
            pl.BlockSpec((1, C, HW), col(OFF_Z // HW)),
            pl.BlockSpec((1, 1, C, 128), lambda b, hg, n: (b, hg, n, 0)),
            pl.BlockSpec((CONV_W, HW), cwcol(0)),
            pl.BlockSpec((CONV_W, HW), cwcol(nq)),
            pl.BlockSpec((CONV_W, HW), cwcol(2 * nq)),
            pl.BlockSpec((1, 1, 128), lambda b, hg, n: (hg, 0, 0)),
            pl.BlockSpec((1, 1, 128), lambda b, hg, n: (hg, 0, 0)),
            pl.BlockSpec((1, GDN_DV), lambda b, hg, n: (0, 0)),
        ],
        out_specs=[
            pl.BlockSpec((1, C, HW), lambda b, hg, n: (b, n, hg)),
            pl.BlockSpec((1, GDN_HB, GDN_DK, GDN_DV), lambda b, hg, n: (b, hg, 0, 0)),
        ],
        out_shape=[
            jax.ShapeDtypeStruct((B, T, GDN_V_W), BF16),
            jax.ShapeDtypeStruct((B, GDN_HEADS, GDN_DK, GDN_DV), F32),
        ],
        scratch_shapes=[
            pltpu.VMEM((C + 8, HW), F32),
            pltpu.VMEM((C + 8, HW), F32),
            pltpu.VMEM((C + 8, HW), F32),
            pltpu.VMEM((GDN_HB, GDN_DK, GDN_DV), F32),
        ],
        compiler_params=_cparams(("parallel", "parallel", "arbitrary")),
        name="gdn_prompt",
    )(proj3, proj3, proj3, proj3, ab_g, conv_w, conv_w, conv_w, alog_g, dtb_g,
      gdn_norm.reshape(1, GDN_DV))


def _gdn_step_body(x_ref, z_ref, ab_ref, cs_ref, s_ref, cw_ref, alog_ref, dtb_ref, gn_ref,
                   o_ref, cs_out_ref, s_out_ref):
    x = x_ref[0]
    y = cw_ref[CONV_W - 1] * x
    for w in range(CONV_W - 1):
        y = y + cw_ref[w] * cs_ref[0, w]
    y = _silu(y)
    cs_out_ref[0, 0] = cs_ref[0, 1]
    cs_out_ref[0, 1] = cs_ref[0, 2]
    cs_out_ref[0, 2] = x
    r_ab = (OFF_AB // 128) % 8
    ab = ab_ref[0, r_ab:r_ab + 1, :]
    g_all = -jnp.exp(alog_ref[...]) * _softplus(ab + dtb_ref[...])
    beta_all = _sigmoid(ab)
    eye = (lax.broadcasted_iota(jnp.int32, (GDN_DK, GDN_DK), 0)
           == lax.broadcasted_iota(jnp.int32, (GDN_DK, GDN_DK), 1))

    def to_col(row):
        return jnp.sum(jnp.where(eye, jnp.broadcast_to(row, (GDN_DK, GDN_DK)), 0.0),
                       axis=1, keepdims=True)

    for h in range(GDN_HEADS):
        q = y[h:h + 1, :]
        k = y[GDN_HEADS + h:GDN_HEADS + h + 1, :]
        v = y[2 * GDN_HEADS + h:2 * GDN_HEADS + h + 1, :]
        q = q * lax.rsqrt(jnp.sum(q * q, axis=-1, keepdims=True) + EPS) * (GDN_DK ** -0.5)
        k = k * lax.rsqrt(jnp.sum(k * k, axis=-1, keepdims=True) + EPS)
        g = g_all[:, h:h + 1]
        beta = beta_all[:, GDN_HEADS + h:GDN_HEADS + h + 1]
        eg = jnp.exp(g)
        S = s_ref[0, h]
        kcol = to_col(k)
        qcol = to_col(q)
        k_s = jnp.sum(S * kcol, axis=0, keepdims=True)
        q_s = jnp.sum(S * qcol, axis=0, keepdims=True)
        v_new = v * beta - (beta * eg) * k_s
        o = eg * q_s + jnp.sum(q * k, axis=-1, keepdims=True) * v_new
        s_out_ref[0, h] = S * eg + kcol * v_new
        ms = jnp.mean(o * o, axis=-1, keepdims=True)
        on = o * lax.rsqrt(ms + EPS) * gn_ref[...]
        o_ref[0, h:h + 1, :] = on * _silu(z_ref[0, h:h + 1, :])


def gdn_step(proj_s, conv_state, ssm_state, conv_w, a_log, dt_bias, gdn_norm):
    Bs = proj_s.shape[0]
    nrow = N_PROJ // 128
    p3 = proj_s.reshape(Bs, nrow, 128)
    cs4 = conv_state.reshape(Bs, CONV_W - 1, CONV_CH // 128, 128)
    cw3 = conv_w.reshape(CONV_W, CONV_CH // 128, 128)
    pad = jnp.zeros((128 - GDN_HEADS,), F32)
    alog = jnp.concatenate([a_log, pad]).reshape(1, 128)
    dtb = jnp.concatenate([dt_bias, pad]).reshape(1, 128)
    nr = CONV_CH // 128
    o, cs_new, s_new = pl.pallas_call(
        _gdn_step_body,
        grid=(Bs,),
        in_specs=[
            pl.BlockSpec((1, nr, 128), lambda b: (b, 0, 0)),
            pl.BlockSpec((1, GDN_HEADS, 128), lambda b: (b, OFF_Z // 128 // GDN_HEADS, 0)),
            pl.BlockSpec((1, 8, 128), lambda b: (b, OFF_AB // 128 // 8, 0)),
            pl.BlockSpec((1, CONV_W - 1, nr, 128), lambda b: (b, 0, 0, 0)),
            pl.BlockSpec((1, GDN_HEADS, GDN_DK, GDN_DV), lambda b: (b, 0, 0, 0)),
            pl.BlockSpec((CONV_W, nr, 128), lambda b: (0, 0, 0)),
            pl.BlockSpec((1, 128), lambda b: (0, 0)),
            pl.BlockSpec((1, 128), lambda b: (0, 0)),
            pl.BlockSpec((1, GDN_DV), lambda b: (0, 0)),
        ],
        out_specs=[
            pl.BlockSpec((1, GDN_HEADS, GDN_DV), lambda b: (b, 0, 0)),
            pl.BlockSpec((1, CONV_W - 1, nr, 128), lambda b: (b, 0, 0, 0)),
            pl.BlockSpec((1, GDN_HEADS, GDN_DK, GDN_DV), lambda b: (b, 0, 0, 0)),
        ],
        out_shape=[
            jax.ShapeDtypeStruct((Bs, GDN_HEADS, GDN_DV), F32),
            jax.ShapeDtypeStruct((Bs, CONV_W - 1, nr, 128), F32),
            jax.ShapeDtypeStruct((Bs, GDN_HEADS, GDN_DK, GDN_DV), F32),
        ],
        compiler_params=_cparams(("parallel",)),
        name="gdn_step",
    )(p3, p3, p3, cs4, ssm_state, cw3, alog, dtb, gdn_norm.reshape(1, GDN_DV))
    return (o.reshape(Bs, GDN_V_W), cs_new.reshape(Bs, CONV_W - 1, CONV_CH), s_new)


def _qknorm_body(q_ref, k_ref, v_ref, qg_ref, kg_ref, qn_ref, kv_ref):
    for h in range(GROUP_HEADS):
        sl = slice(h * HEAD_DIM, (h + 1) * HEAD_DIM)
        q = q_ref[:, sl]
        qn = q * lax.rsqrt(jnp.mean(q * q, axis=-1, keepdims=True) + EPS) * qg_ref[...]
        qn_ref[:, sl] = qn * (HEAD_DIM ** -0.5)
        k = k_ref[:, sl]
        kv_ref[:, sl] = k * lax.rsqrt(jnp.mean(k * k, axis=-1, keepdims=True) + EPS) * kg_ref[...]
    kv_ref[:, GROUP_W:] = v_ref[...]


def qk_norm(proj, q_norm, k_norm):
    M = proj.shape[0]
    tm = min(M, 512)
    blk = GROUP_W
    return pl.pallas_call(
        _qknorm_body,
        grid=(M // tm, N_GROUPS),
        in_specs=[
            pl.BlockSpec((tm, blk), lambda i, g: (i, OFF_AQ // blk + g)),
            pl.BlockSpec((tm, blk), lambda i, g: (i, OFF_AK // blk + g)),
            pl.BlockSpec((tm, blk), lambda i, g: (i, OFF_AV // blk + g)),
            pl.BlockSpec((1, HEAD_DIM), lambda i, g: (0, 0)),
            pl.BlockSpec((1, HEAD_DIM), lambda i, g: (0, 0)),
        ],
        out_specs=[
            pl.BlockSpec((tm, blk), lambda i, g: (i, g)),
            pl.BlockSpec((tm, 2 * blk), lambda i, g: (i, g)),
        ],
        out_shape=[jax.ShapeDtypeStruct((M, ATTN_W), F32),
                   jax.ShapeDtypeStruct((M, N_GROUPS * 2 * GROUP_W), F32)],
        compiler_params=_cparams(("parallel", "parallel")),
        name="qk_norm",
    )(proj, proj, proj, q_norm.reshape(1, HEAD_DIM), k_norm.reshape(1, HEAD_DIM))


def _t5_bucket(dist):
    max_exact = N_BUCKETS // 2
    d = np.maximum(dist, max_exact).astype(np.float32)
    large = max_exact + (np.log(d / max_exact) / np.log(MAX_DISTANCE / max_exact)
                         * (N_BUCKETS - max_exact)).astype(np.int32)
    large = np.minimum(large, N_BUCKETS - 1)
    return np.where(dist < max_exact, dist, large).astype(np.int32)


def _group_bias(rel_bias, gi):
    window, dilation = ATTN_GROUPS[gi]
    n_keys = window // dilation + 1
    buckets = _t5_bucket(np.arange(n_keys) * dilation)
    return rel_bias[buckets][:, gi * GROUP_HEADS:(gi + 1) * GROUP_HEADS].T.astype(F32)


def _attn_prompt_body(T, q0_ref, q1_ref, q2_ref, k0_ref, v0_ref, k1_ref, v1_ref, k2_ref, v2_ref,
                      bias_ref, y_ref, m_s, l_s, o_s):
    QB = Q_BLOCK
    q_refs = (q0_ref, q1_ref, q2_ref)
    k_refs = (k0_ref, k1_ref, k2_ref)
    v_refs = (v0_ref, v1_ref, v2_ref)
    for gi, (window, d) in enumerate(ATTN_GROUPS):
        nq = T // (d * QB)
        for r in range(d):
            for qi in range(nq):
                rows = pl.ds(r + d * QB * qi, QB, stride=d) if d > 1 else pl.ds(QB * qi, QB)
                q = q_refs[gi][0, rows, :].astype(BF16)
                kc = k_refs[gi][0, rows, :]
                vc = v_refs[gi][0, rows, :]
                if qi > 0:
                    prev = (pl.ds(r + d * QB * (qi - 1), QB, stride=d) if d > 1
                            else pl.ds(QB * (qi - 1), QB))
                    kcat = jnp.concatenate([k_refs[gi][0, prev, :], kc], axis=0).astype(BF16)
                    vcat = jnp.concatenate([v_refs[gi][0, prev, :], vc], axis=0).astype(BF16)
                    bias = bias_ref[gi, 0]
                else:
                    kcat = kc.astype(BF16)
                    vcat = vc.astype(BF16)
                    bias = bias_ref[gi, 0, :, QB:]
                s = lax.dot_general(q, kcat, (((1,), (1,)), ((), ())),
                                    preferred_element_type=F32) + bias
                m_blk = jnp.max(s, axis=-1, keepdims=True)
                if gi == 0:
                    m_new = m_blk
                else:
                    m_old = m_s[rows, :][:, 0:1]
                    m_new = jnp.maximum(m_old, m_blk)
                p = jnp.exp(s - m_new)
                l_new = jnp.sum(p, axis=-1, keepdims=True)
                o_new = jnp.dot(p.astype(BF16), vcat, preferred_element_type=F32)
                if gi > 0:
                    alpha = jnp.exp(m_old - m_new)
                    l_new = l_new + alpha * l_s[rows, :][:, 0:1]
                    o_new = o_new + alpha * o_s[rows, :]
                m_s[rows, :] = jnp.broadcast_to(m_new, (QB, HEAD_DIM))
                l_s[rows, :] = jnp.broadcast_to(l_new, (QB, HEAD_DIM))
                o_s[rows, :] = o_new
    y_ref[0] = (o_s[...] / l_s[...]).astype(y_ref.dtype)


def attn_prompt(qn3, kv3, bias_all):
    B, T, _ = qn3.shape
    blk = (1, T, HEAD_DIM)

    def qspec(gi):
        return pl.BlockSpec(blk, lambda b, h: (b, 0, gi * GROUP_HEADS + h))

    def kvspec(gi, is_v):
        return pl.BlockSpec(blk, lambda b, h: (b, 0, (2 * gi + is_v) * GROUP_HEADS + h))

    in_specs = [qspec(0), qspec(1), qspec(2)]
    for gi in range(N_GROUPS):
        in_specs += [kvspec(gi, 0), kvspec(gi, 1)]
    in_specs.append(pl.BlockSpec((N_GROUPS, 1, Q_BLOCK, 2 * Q_BLOCK), lambda b, h: (0, h, 0, 0)))
    y = pl.pallas_call(
        functools.partial(_attn_prompt_body, T),
        grid=(B, GROUP_HEADS),
        in_specs=in_specs,
        out_specs=pl.BlockSpec(blk, lambda b, h: (b, 0, h)),
        out_shape=jax.ShapeDtypeStruct((B, T, GROUP_W), BF16),
        scratch_shapes=[pltpu.VMEM((T, HEAD_DIM), F32), pltpu.VMEM((T, HEAD_DIM), F32),
                        pltpu.VMEM((T, HEAD_DIM), F32)],
        compiler_params=_cparams(("parallel", "parallel")),
        name="attn_prompt",
    )(qn3, qn3, qn3, kv3, kv3, kv3, kv3, kv3, kv3, bias_all)
    return y.reshape(B * T, GROUP_W)


def _prompt_bias_matrix(rel_bias, gi):
    QB = Q_BLOCK
    bias = _group_bias(rel_bias, gi)
    row0 = jnp.concatenate([bias[:, ::-1], jnp.full((GROUP_HEADS, QB), NEG_INF, F32)], axis=1)
    flat = jnp.tile(row0, (1, QB))[:, :QB * 2 * QB]
    return flat.reshape(GROUP_HEADS, QB, 2 * QB)


def _attn_decode_body(q_ref, n0_ref, n1_ref, n2_ref, c0_ref, c1_ref, c2_ref, bc_ref, bn_ref, y_ref):
    new_refs = (n0_ref, n1_ref, n2_ref)
    cache_refs = (c0_ref, c1_ref, c2_ref)
    outs, lds = [], []
    for gi in range(N_GROUPS):
        q = q_ref[0, gi * GROUP_HEADS:(gi + 1) * GROUP_HEADS, :]
        kc = cache_refs[gi][:, 0]
        vc = cache_refs[gi][:, 1]
        kn = new_refs[gi][0, 0:GROUP_HEADS, :]
        vn = new_refs[gi][0, GROUP_HEADS:2 * GROUP_HEADS, :]
        s_c = jnp.sum(kc * q[None], axis=-1, keepdims=True) + bc_ref[gi, :, :, 0:1]
        s_n = jnp.sum(kn * q, axis=-1, keepdims=True) + bn_ref[gi, :, 0:1]
        m = jnp.maximum(jnp.max(s_c, axis=0), s_n)
        p_c = jnp.exp(s_c - m[None])
        p_n = jnp.exp(s_n - m)
        l = jnp.sum(p_c, axis=0) + p_n
        outs.append((jnp.sum(p_c * vc, axis=0) + p_n * vn) / l)
        lds.append(m + jnp.log(l))
    m = jnp.maximum(jnp.maximum(lds[0], lds[1]), lds[2])
    es = [jnp.exp(x - m) for x in lds]
    den = es[0] + es[1] + es[2]
    y_ref[0] = outs[0] * (es[0] / den) + outs[1] * (es[1] / den) + outs[2] * (es[2] / den)


def attn_decode(qn_s, kv_new, caches, rel_bias):
    Bs = qn_s.shape[0]
    q3 = qn_s.reshape(Bs, N_GROUPS * GROUP_HEADS, HEAD_DIM)
    new3 = [x.reshape(Bs, 2 * GROUP_HEADS, HEAD_DIM) for x in kv_new]
    cviews, bcs, bns = [], [], []
    for gi, (window, d) in enumerate(ATTN_GROUPS):
        nk = window // d
        assert caches[gi].shape[1] == window and nk == Q_BLOCK
        cviews.append(caches[gi].reshape(Bs, nk, d, 2, GROUP_HEADS, HEAD_DIM))
        bias = _group_bias(rel_bias, gi)
        bc = bias[:, nk - np.arange(nk)]
        bcs.append(jnp.broadcast_to(bc.T[:, :, None], (nk, GROUP_HEADS, HEAD_DIM)))
        bns.append(jnp.broadcast_to(bias[:, 0:1], (GROUP_HEADS, HEAD_DIM)))
    bc_all = jnp.stack(bcs, 0)
    bn_all = jnp.stack(bns, 0)
    nspec = pl.BlockSpec((1, 2 * GROUP_HEADS, HEAD_DIM), lambda b: (b, 0, 0))
    cspec = pl.BlockSpec((None, Q_BLOCK, None, 2, GROUP_HEADS, HEAD_DIM),
                         lambda b: (b, 0, 0, 0, 0, 0))
    y = pl.pallas_call(
        _attn_decode_body,
        grid=(Bs,),
        in_specs=[
            pl.BlockSpec((1, N_GROUPS * GROUP_HEADS, HEAD_DIM), lambda b: (b, 0, 0)),
            nspec, nspec, nspec, cspec, cspec, cspec,
            pl.BlockSpec((N_GROUPS, Q_BLOCK, GROUP_HEADS, HEAD_DIM), lambda b: (0, 0, 0, 0)),
            pl.BlockSpec((N_GROUPS, GROUP_HEADS, HEAD_DIM), lambda b: (0, 0, 0)),
        ],
        out_specs=pl.BlockSpec((1, GROUP_HEADS, HEAD_DIM), lambda b: (b, 0, 0)),
        out_shape=jax.ShapeDtypeStruct((Bs, GROUP_HEADS, HEAD_DIM), F32),
        compiler_params=_cparams(("parallel",)),
        name="attn_decode",
    )(q3, *new3, *cviews, bc_all, bn_all)
    return y.reshape(Bs, GROUP_W)


def _merge_body(ya_ref, yb_ref, wa_ref, wb_ref, ga_ref, gb_ref, o_ref):
    pa = jnp.dot(ya_ref[...], wa_ref[...], preferred_element_type=F32)
    pb = jnp.dot(yb_ref[...], wb_ref[...], preferred_element_type=F32)
    o_ref[...] = (_sigmoid(ga_ref[...]) * pa + _sigmoid(gb_ref[...]) * pb).astype(o_ref.dtype)


def branch_merge(ya, yb, wa, wb, proj):
    M = ya.shape[0]
    tm = min(M, 1024)
    tn = 512
    return pl.pallas_call(
        _merge_body,
        grid=(M // tm, D_MODEL // tn),
        in_specs=[
            pl.BlockSpec((tm, GDN_V_W), lambda i, j: (i, 0)),
            pl.BlockSpec((tm, GROUP_W), lambda i, j: (i, 0)),
            pl.BlockSpec((GDN_V_W, tn), lambda i, j: (0, j)),
            pl.BlockSpec((GROUP_W, tn), lambda i, j: (0, j)),
            pl.BlockSpec((tm, tn), lambda i, j: (i, OFF_GA // tn + j)),
            pl.BlockSpec((tm, tn), lambda i, j: (i, OFF_GB // tn + j)),
        ],
        out_specs=pl.BlockSpec((tm, tn), lambda i, j: (i, j)),
        out_shape=jax.ShapeDtypeStruct((M, D_MODEL), BF16),
        compiler_params=_cparams(("parallel", "parallel")),
        name="branch_merge",
    )(ya, yb, wa, wb, proj, proj)


def _extract_top(x, count, tie_safe):
    R = x.shape[0]
    rid = lax.broadcasted_iota(jnp.int32, x.shape, 0).astype(F32)
    vid = lax.broadcasted_iota(jnp.int32, (count, x.shape[1]), 0)
    vals = jnp.zeros((count, x.shape[1]), F32)
    rank = jnp.full(x.shape, float(count), F32)
    for a in range(count):
        m = jnp.max(x, axis=0, keepdims=True)
        hit = x == m
        if tie_safe:
            first = jnp.min(jnp.where(hit, rid, float(R)), axis=0, keepdims=True)
            hit = rid == first
        x = jnp.where(hit, -jnp.inf, x)
        rank = jnp.where(hit, float(a), rank)
        vals = jnp.where(vid == a, m, vals)
    return vals, rank


CAND_HALF = PEER_TOPK // 2
N_CAND = PEER_TOPK + (CAND_HALF - 1) * CAND_HALF + CAND_HALF


def _peer_topk_body(q_ref, sk_ref, cnt_ref, r2_ref, e1_ref, e2_ref, cand_s, s1_s, s2_s):
    K = PEER_TOPK
    KH = CAND_HALF

    def select(h, s1, s2, tie_safe):
        v1, rank1 = _extract_top(s1, K, tie_safe)
        v2, rank2 = _extract_top(s2, K, tie_safe)
        cand_s[0:K, :] = v2 + v1[0:1, :]
        for a in range(1, KH):
            cand_s[K + (a - 1) * KH:K + a * KH, :] = v2[0:KH, :] + v1[a:a + 1, :]
        cand_s[N_CAND - KH:N_CAND, :] = v1[KH:K, :] + v2[0:1, :]
        best, crank = _extract_top(cand_s[...], K, tie_safe)
        picked = jnp.where(crank < float(K), 1.0, 0.0)
        vid = lax.broadcasted_iota(jnp.int32, (KH, picked.shape[1]), 0)
        low = jnp.where(vid == 0, jnp.sum(picked[0:K], axis=0, keepdims=True), 0.0)
        for a in range(1, KH):
            rows = picked[K + (a - 1) * KH:K + a * KH]
            low = jnp.where(vid == a, jnp.sum(rows, axis=0, keepdims=True), low)
        count = jnp.concatenate([low, picked[N_CAND - KH:N_CAND]], axis=0)
        cnt1 = jnp.zeros(rank1.shape, F32)
        for a in range(K):
            cnt1 = jnp.where(rank1 == float(a), count[a:a + 1, :], cnt1)
        cmax = best[0:1, :]
        z = jnp.sum(jnp.exp(best - cmax), axis=0, keepdims=True)
        in1 = jnp.where(rank1 < float(K), 1.0, 0.0)
        in2 = jnp.where(rank2 < float(K), 1.0, 0.0)
        cnt_ref[h] = cnt1
        r2_ref[h] = rank2
        e1_ref[h] = in1 * jnp.exp(s1 - v1[0:1, :])
        e2_ref[h] = in2 * jnp.exp(s2 - v2[0:1, :]) / z
        return (jnp.sum(in1, axis=0, keepdims=True) + jnp.sum(in2, axis=0, keepdims=True)
                + jnp.sum(picked, axis=0, keepdims=True) - 3.0 * K)

    def head(h, carry):
        off = pl.multiple_of(h * (2 * N_KEYS), 2 * N_KEYS)
        q1 = q_ref[:, pl.ds(off, N_KEYS)]
        q2 = q_ref[:, pl.ds(pl.multiple_of(off + N_KEYS, N_KEYS), N_KEYS)]
        s1_s[...] = lax.dot_general(sk_ref[0], q1, (((1,), (1,)), ((), ())), precision=HIGHEST,
                                    preferred_element_type=F32)
        s2_s[...] = lax.dot_general(sk_ref[1], q2, (((1,), (1,)), ((), ())), precision=HIGHEST,
                                    preferred_element_type=F32)
        extra = select(h, s1_s[...], s2_s[...], tie_safe=False)

        @pl.when(jnp.max(extra) > 0.0)
        def _():
            select(h, s1_s[...], s2_s[...], tie_safe=True)

        return carry

    lax.fori_loop(0, PEER_HEADS, head, 0)


def peer_topk(query, subkeys):
    M = query.shape[0]
    tm = min(M, 256)
    big = jax.ShapeDtypeStruct((PEER_HEADS, N_KEYS, M), F32)
    bspec = pl.BlockSpec((PEER_HEADS, N_KEYS, tm), lambda i: (0, 0, i))
    return pl.pallas_call(
        _peer_topk_body,
        grid=(M // tm,),
        in_specs=[pl.BlockSpec((tm, PEER_HEADS * 2 * N_KEYS), lambda i: (i, 0)),
                  pl.BlockSpec((2, N_KEYS, N_KEYS), lambda i: (0, 0, 0))],
        out_specs=[bspec, bspec, bspec, bspec],
        out_shape=[big, big, big, big],
        scratch_shapes=[pltpu.VMEM((N_CAND, tm), F32), pltpu.VMEM((N_KEYS, tm), F32),
                        pltpu.VMEM((N_KEYS, tm), F32)],
        compiler_params=_cparams(("parallel",)),
        name="peer_topk",
    )(query, subkeys)


def _gelu_exact(x):
    return 0.5 * x * (1.0 + lax.erf(x * (2.0 ** -0.5)))


GATE_ROWS = 8


def _peer_gate_body(hf_ref, u_ref, cnt_ref, r2_ref, e1_ref, e2_ref, w_ref, g_s):
    j = pl.program_id(1)
    tm = hf_ref.shape[0]
    grp = pl.multiple_of(j * GATE_ROWS, GATE_ROWS)

    for kk in range(GATE_ROWS):
        for tc in range(tm // 128):
            ts = slice(tc * 128, (tc + 1) * 128)
            acc = jnp.zeros((N_KEYS, 128), F32)
            for h in range(PEER_HEADS):
                cnt_row = cnt_ref[h, pl.ds(grp, GATE_ROWS), ts][kk:kk + 1, :]
                e1_row = e1_ref[h, pl.ds(grp, GATE_ROWS), ts][kk:kk + 1, :]
                acc = acc + jnp.where(r2_ref[h, :, ts] < cnt_row, e2_ref[h, :, ts] * e1_row, 0.0)
            g_s[ts, kk * N_KEYS:(kk + 1) * N_KEYS] = acc.T
    hmat = lax.dot_general(hf_ref[...], u_ref[...], (((1,), (1,)), ((), ())),
                           preferred_element_type=F32)
    w_ref[...] = (g_s[...] * _gelu_exact(hmat)).astype(w_ref.dtype)


def peer_gate_weights(hf, u_tab, cnt, r2, e1, e2, tm=512):
    M = hf.shape[0]
    tm = min(M, tm)
    tn = GATE_ROWS * N_KEYS
    E = u_tab.shape[0]
    sspec = pl.BlockSpec((PEER_HEADS, N_KEYS, tm), lambda i, j: (0, 0, i))
    return pl.pallas_call(
        _peer_gate_body,
        grid=(M // tm, E // tn),
        in_specs=[
            pl.BlockSpec((tm, D_MODEL), lambda i, j: (i, 0)),
            pl.BlockSpec((tn, D_MODEL), lambda i, j: (j, 0)),
            sspec, sspec, sspec, sspec,
        ],
        out_specs=pl.BlockSpec((tm, tn), lambda i, j: (i, j)),
        out_shape=jax.ShapeDtypeStruct((M, E), BF16),
        scratch_shapes=[pltpu.VMEM((tm, tn), F32)],
        compiler_params=_cparams(("parallel", "arbitrary")),
        name="peer_gate",
    )(hf, u_tab, cnt, r2, e1, e2)


def _mm_kres_body(a_ref, b_ref, r_ref, o_ref):
    k = pl.program_id(2)

    @pl.when(k == 0)
    def _():
        o_ref[...] = r_ref[...]

    o_ref[...] += jnp.dot(a_ref[...], b_ref[...], preferred_element_type=F32)


def matmul_kres(a, b, residual, tm=1024, tn=1024, tk=2048, name="matmul_k"):
    M, K = a.shape
    N = b.shape[1]
    tm = min(tm, M)
    return pl.pallas_call(
        _mm_kres_body,
        grid=(M // tm, N // tn, K // tk),
        in_specs=[pl.BlockSpec((tm, tk), lambda i, j, k: (i, k)),
                  pl.BlockSpec((tk, tn), lambda i, j, k: (k, j)),
                  pl.BlockSpec((tm, tn), lambda i, j, k: (i, j))],
        out_specs=pl.BlockSpec((tm, tn), lambda i, j, k: (i, j)),
        out_shape=jax.ShapeDtypeStruct((M, N), F32),
        compiler_params=_cparams(("parallel", "parallel", "arbitrary")),
        name=name,
    )(a, b, residual)


def peer_ffn(x1, norm_ffn, wq_bf, subkeys, u_bf, v_bf):
    M = x1.shape[0]
    Mp = max(M, 128)
    if Mp != M:
        x1 = jnp.concatenate([x1, jnp.zeros((Mp - M, D_MODEL), F32)], axis=0)
    hf = rmsnorm_bf16(x1, norm_ffn)
    query = matmul(hf, wq_bf, name="peer_query")
    cnt, r2, e1, e2 = peer_topk(query, subkeys)
    w = peer_gate_weights(hf, u_bf, cnt, r2, e1, e2)
    return matmul_kres(w, v_bf, x1, name="peer_out")[:M]


W_BLK = 512
N_AB = 2 * GDN_HEADS
J_AB = OFF_AB // W_BLK


def _prep_w_in_body(a_ref, b_ref, o_ref):
    j = pl.program_id(0)

    @pl.when(j < OFF_AQ // W_BLK)
    def _():
        o_ref[...] = a_ref[...].astype(o_ref.dtype)

    @pl.when(jnp.logical_and(j >= OFF_AQ // W_BLK, j < J_AB))
    def _():
        o_ref[0:W_BLK - N_AB, :] = a_ref[N_AB:, :].astype(o_ref.dtype)
        o_ref[W_BLK - N_AB:, :] = b_ref[...].astype(o_ref.dtype)

    @pl.when(j == J_AB)
    def _():
        o_ref[0:N_AB, :] = b_ref[...].astype(o_ref.dtype)
        o_ref[N_AB:, :] = jnp.zeros((W_BLK - N_AB, o_ref.shape[1]), o_ref.dtype)


def _prep_w_in(w_in_t):
    D = w_in_t.shape[1]
    j0 = OFF_AQ // W_BLK
    sub = W_BLK // N_AB
    return pl.pallas_call(
        _prep_w_in_body,
        grid=(N_PROJ // W_BLK,),
        in_specs=[
            pl.BlockSpec((W_BLK, D), lambda j: (jnp.where(j == J_AB, j0, j), 0)),
            pl.BlockSpec((N_AB, D), lambda j: (jnp.where(j == J_AB, j0 * sub,
                                                         jnp.where(j >= j0, (j + 1) * sub, 0)), 0)),
        ],
        out_specs=pl.BlockSpec((W_BLK, D), lambda j: (j, 0)),
        out_shape=jax.ShapeDtypeStruct((N_PROJ, D), BF16),
        compiler_params=_cparams(("parallel",)),
        name="prep_w_in",
    )(w_in_t, w_in_t)


def _gdn_group_params(a_log, dt_bias):
    pad = jnp.zeros((GDN_HG, 128 - GDN_HB), F32)
    alog = jnp.concatenate([a_log.reshape(GDN_HG, GDN_HB), pad], axis=1).reshape(GDN_HG, 1, 128)
    dtb = jnp.concatenate([dt_bias.reshape(GDN_HG, GDN_HB), pad], axis=1).reshape(GDN_HG, 1, 128)
    return alog, dtb


def _ab_grouped(proj, B, T):
    ab = proj[:, OFF_AB:OFF_AB + 2 * GDN_HEADS].reshape(B, T, 2, GDN_HG, GDN_HB)
    ab = ab.transpose(0, 3, 1, 2, 4).reshape(B, GDN_HG, T, 2 * GDN_HB)
    return jnp.concatenate([ab, jnp.zeros((B, GDN_HG, T, 128 - 2 * GDN_HB), F32)], axis=-1)


def kernel(x_prompt, x_sample, state_ssm, state_conv, cache_kv_g0, cache_kv_g1, cache_kv_g2, rel_bias, norm_mix, w_in, conv_w, a_log, dt_bias, gdn_norm, q_norm, k_norm, w_branch_a, w_branch_b, w_out, norm_ffn, peer_w_query, peer_subkeys, peer_u, peer_v):
    B, T, D = x_prompt.shape
    Bs = x_sample.shape[0]
    assert state_ssm.shape[0] == 1 and x_sample.shape[1] == 1
    caches = (cache_kv_g0[0], cache_kv_g1[0], cache_kv_g2[0])
    w_cat_t = _prep_w_in(jnp.swapaxes(w_in[0], 0, 1))
    wa_bf = w_branch_a[0].astype(BF16)
    wb_bf = w_branch_b[0].astype(BF16)
    wo_bf = w_out[0].astype(BF16)
    wq_bf = peer_w_query[0].astype(BF16)
    u_bf = peer_u[0].astype(BF16)
    v_bf = peer_v[0].astype(BF16)
    cw = conv_w[0]

    xp = x_prompt.reshape(B * T, D)
    xs = x_sample.reshape(Bs, D)

    proj = matmul_nt(rmsnorm_bf16(xp, norm_mix[0]), w_cat_t, name="in_proj")
    alog_g, dtb_g = _gdn_group_params(a_log[0], dt_bias[0])
    ya, ssm_p = gdn_prompt(proj.reshape(B, T, N_PROJ), _ab_grouped(proj, B, T), cw, alog_g, dtb_g,
                           gdn_norm[0])
    conv_p = proj.reshape(B, T, N_PROJ)[:, T - (CONV_W - 1):, :CONV_CH]
    qn, kv = qk_norm(proj, q_norm[0], k_norm[0])
    kv3 = kv.reshape(B, T, N_GROUPS * 2 * GROUP_W)
    kvs = [kv3[:, :, gi * 2 * GROUP_W:(gi + 1) * 2 * GROUP_W] for gi in range(N_GROUPS)]
    bias_all = jnp.stack([_prompt_bias_matrix(rel_bias, gi) for gi in range(N_GROUPS)], 0)
    yb = attn_prompt(qn.reshape(B, T, ATTN_W), kv3, bias_all)
    merged = branch_merge(ya.reshape(B * T, GDN_V_W), yb, wa_bf, wb_bf, proj)
    x1 = matmul(merged, wo_bf, residual=xp, name="out_proj")
    y_prompt = peer_ffn(x1, norm_ffn[0], wq_bf, peer_subkeys[0], u_bf, v_bf).reshape(B, T, D)
    p_kv = []
    for gi, (window, _) in enumerate(ATTN_GROUPS):
        keep = min(window, T)
        kvg = kvs[gi][:, T - keep:].reshape(B, keep, 2, GROUP_HEADS, HEAD_DIM)
        p_kv.append(kvg[None])

    proj_s = matmul_nt(rmsnorm_bf16(xs, norm_mix[0]), w_cat_t, name="in_proj_s")
    ya_s, conv_s, ssm_s = gdn_step(proj_s, state_conv[0], state_ssm[0], cw, a_log[0], dt_bias[0],
                                   gdn_norm[0])
    qn_s, kv_s = qk_norm(proj_s, q_norm[0], k_norm[0])
    kvs_s = [kv_s[:, gi * 2 * GROUP_W:(gi + 1) * 2 * GROUP_W] for gi in range(N_GROUPS)]
    yb_s = attn_decode(qn_s, kvs_s, caches, rel_bias)
    merged_s = branch_merge(ya_s.astype(BF16), yb_s.astype(BF16), wa_bf, wb_bf, proj_s)
    x1_s = matmul(merged_s, wo_bf, residual=xs, name="out_proj_s")
    y_sample = peer_ffn(x1_s, norm_ffn[0], wq_bf, peer_subkeys[0], u_bf, v_bf).reshape(Bs, 1, D)
    s_kv = [kvs_s[gi].reshape(1, Bs, 1, 2, GROUP_HEADS, HEAD_DIM) for gi in range(N_GROUPS)]

    return (y_prompt, y_sample,
            ssm_p[None], conv_p[None], p_kv[0], p_kv[1], p_kv[2],
            ssm_s[None], conv_s[None], s_kv[0], s_kv[1], s_kv[2])
```

```python
import functools

import numpy as np
import jax
import jax.numpy as jnp
from jax import lax
from jax.experimental import pallas as pl
from jax.experimental.pallas import tpu as pltpu

F32 = jnp.float32
BF16 = jnp.bfloat16
HIGHEST = lax.Precision.HIGHEST

D_MODEL = 4096
GDN_HEADS = 16
GDN_DK = 128
GDN_DV = 128
GDN_QK_W = GDN_HEADS * GDN_DK
GDN_V_W = GDN_HEADS * GDN_DV
CONV_W = 4
CONV_CH = 2 * GDN_QK_W + GDN_V_W
GDN_CHUNK = 64
ATTN_GROUPS = ((128, 1), (512, 4), (2048, 16))
N_GROUPS = 3
GROUP_HEADS = 4
HEAD_DIM = 128
GROUP_W = GROUP_HEADS * HEAD_DIM
ATTN_W = N_GROUPS * GROUP_W
Q_BLOCK = 128
N_BUCKETS = 32
MAX_DISTANCE = 2048
PEER_HEADS = 8
N_KEYS = 128
PEER_TOPK = 16
EPS = 1e-6
NEG_INF = -1e30

OFF_QKV = 0
OFF_Z = CONV_CH
OFF_AQ = OFF_Z + GDN_V_W
OFF_AK = OFF_AQ + ATTN_W
OFF_AV = OFF_AK + ATTN_W
OFF_GA = OFF_AV + ATTN_W
OFF_GB = OFF_GA + D_MODEL
OFF_AB = OFF_GB + D_MODEL
N_PROJ = 21504

GDN_HB = 8
GDN_HG = GDN_HEADS // GDN_HB
VMEM_LIMIT = 56 * 1024 * 1024


def _cparams(sem):
    return pltpu.CompilerParams(dimension_semantics=sem, vmem_limit_bytes=VMEM_LIMIT)


def _sigmoid(x):
    return 1.0 / (1.0 + jnp.exp(-x))


def _silu(x):
    return x * _sigmoid(x)


def _softplus(x):
    return jnp.maximum(x, 0.0) + jnp.log(1.0 + jnp.exp(-jnp.abs(x)))


def _bdot(a, b):
    return jnp.dot(a.astype(BF16), b.astype(BF16), preferred_element_type=F32)


def _bdot_nt(a, b):
    return lax.dot_general(a.astype(BF16), b.astype(BF16), (((1,), (1,)), ((), ())),
                           preferred_element_type=F32)


def _bdot_tn(a, b):
    return lax.dot_general(a.astype(BF16), b.astype(BF16), (((0,), (0,)), ((), ())),
                           preferred_element_type=F32)


def _hdot(a, b):
    return jnp.dot(a, b, precision=HIGHEST, preferred_element_type=F32)


def _split_bf16(a):
    hi = a.astype(BF16)
    lo = (a - hi.astype(F32)).astype(BF16)
    return hi, lo


def _dot3(a_parts, b_parts):
    a_hi, a_lo = a_parts
    b_hi, b_lo = b_parts
    return (jnp.dot(a_hi, b_hi, preferred_element_type=F32)
            + jnp.dot(a_lo, b_hi, preferred_element_type=F32)
            + jnp.dot(a_hi, b_lo, preferred_element_type=F32))


def _rmsnorm_body(x_ref, g_ref, o_ref):
    x = x_ref[...]
    ms = jnp.mean(x * x, axis=-1, keepdims=True)
    o_ref[...] = (x * lax.rsqrt(ms + EPS) * g_ref[...]).astype(o_ref.dtype)


def rmsnorm_bf16(x, gain):
    M, D = x.shape
    tm = min(M, 256)
    return pl.pallas_call(
        _rmsnorm_body,
        grid=(M // tm,),
        in_specs=[pl.BlockSpec((tm, D), lambda i: (i, 0)),
                  pl.BlockSpec((1, D), lambda i: (0, 0))],
        out_specs=pl.BlockSpec((tm, D), lambda i: (i, 0)),
        out_shape=jax.ShapeDtypeStruct((M, D), BF16),
        compiler_params=_cparams(("parallel",)),
        name="rmsnorm",
    )(x, gain.reshape(1, D))


def _mm_body(a_ref, b_ref, o_ref):
    o_ref[...] = jnp.dot(a_ref[...], b_ref[...], preferred_element_type=F32)


def _mm_res_body(a_ref, b_ref, r_ref, o_ref):
    o_ref[...] = r_ref[...] + jnp.dot(a_ref[...], b_ref[...], preferred_element_type=F32)


def matmul(a, b, residual=None, tm=1024, tn=1024, name="matmul"):
    M, K = a.shape
    N = b.shape[1]
    tm = min(tm, M)
    tn = min(tn, N)
    in_specs = [pl.BlockSpec((tm, K), lambda i, j: (i, 0)),
                pl.BlockSpec((K, tn), lambda i, j: (0, j))]
    args = [a, b]
    body = _mm_body
    if residual is not None:
        in_specs.append(pl.BlockSpec((tm, tn), lambda i, j: (i, j)))
        args.append(residual)
        body = _mm_res_body
    return pl.pallas_call(
        body,
        grid=(M // tm, N // tn),
        in_specs=in_specs,
        out_specs=pl.BlockSpec((tm, tn), lambda i, j: (i, j)),
        out_shape=jax.ShapeDtypeStruct((M, N), F32),
        compiler_params=_cparams(("parallel", "parallel")),
        name=name,
    )(*args)


def _mm_nt_body(a_ref, bt_ref, o_ref):
    o_ref[...] = lax.dot_general(a_ref[...], bt_ref[...], (((1,), (1,)), ((), ())),
                                 preferred_element_type=F32)


def matmul_nt(a, bt, tm=1024, tn=1024, name="matmul_nt"):
    M, K = a.shape
    N = bt.shape[0]
    tm = min(tm, M)
    return pl.pallas_call(
        _mm_nt_body,
        grid=(M // tm, N // tn),
        in_specs=[pl.BlockSpec((tm, K), lambda i, j: (i, 0)),
                  pl.BlockSpec((tn, K), lambda i, j: (j, 0))],
        out_specs=pl.BlockSpec((tm, tn), lambda i, j: (i, j)),
        out_shape=jax.ShapeDtypeStruct((M, N), F32),
        compiler_params=_cparams(("parallel", "parallel")),
        name=name,
    )(a, bt)


def _gdn_chunk_body(q_ref, k_ref, v_ref, z_ref, ab_ref, cwq_ref, cwk_ref, cwv_ref,
                    alog_ref, dtb_ref, gn_ref, o_ref, s_out_ref,
                    xq_s, xk_s, xv_s, state_s):
    n = pl.program_id(2)
    C = GDN_CHUNK
    HW = GDN_HB * GDN_DK

    @pl.when(n == 0)
    def _():
        xq_s[0:8, :] = jnp.zeros((8, HW), F32)
        xk_s[0:8, :] = jnp.zeros((8, HW), F32)
        xv_s[0:8, :] = jnp.zeros((8, HW), F32)
        state_s[...] = jnp.zeros(state_s.shape, F32)

    def conv(x_ref, xs, cw_ref):
        xs[8:8 + C, :] = x_ref[0]
        y = cw_ref[CONV_W - 1:CONV_W, :] * xs[8:8 + C, :]
        for w in range(CONV_W - 1):
            y = y + cw_ref[w:w + 1, :] * xs[5 + w:5 + w + C, :]
        xs[0:8, :] = xs[C:C + 8, :]
        return _silu(y)

    q_all = conv(q_ref, xq_s, cwq_ref)
    k_all = conv(k_ref, xk_s, cwk_ref)
    v_all = conv(v_ref, xv_s, cwv_ref)

    ab = ab_ref[0, 0]
    g_all = -jnp.exp(alog_ref[0]) * _softplus(ab + dtb_ref[0])
    beta_all = _sigmoid(ab)
    ri = lax.broadcasted_iota(jnp.int32, (C, C), 0)
    ci = lax.broadcasted_iota(jnp.int32, (C, C), 1)
    tri = ri >= ci
    strict = ri > ci
    gc_all = _hdot(tri.astype(F32), g_all)
    gc_t = jnp.concatenate([gc_all, jnp.zeros((128 - C, 128), F32)], axis=0).T

    H = range(GDN_HB)
    sls = [slice(j * GDN_DK, (j + 1) * GDN_DK) for j in H]
    qh, kh, kb, gcol, decay, rhs = [], [], [], [], [], []
    for j in H:
        q = q_all[:, sls[j]]
        k = k_all[:, sls[j]]
        q = q * lax.rsqrt(jnp.sum(q * q, axis=-1, keepdims=True) + EPS) * (GDN_DK ** -0.5)
        k = k * lax.rsqrt(jnp.sum(k * k, axis=-1, keepdims=True) + EPS)
        beta = beta_all[:, GDN_HB + j:GDN_HB + j + 1]
        gc = gc_all[:, j:j + 1]
        grow = gc_t[j:j + 1, :C]
        qh.append(q)
        kh.append(k)
        kb.append(k * beta)
        gcol.append(gc)
        decay.append(jnp.where(tri, jnp.exp(jnp.where(tri, gc - grow, 0.0)), 0.0))
        rhs.append(jnp.concatenate([v_all[:, sls[j]] * beta, kb[j] * jnp.exp(gc)], axis=-1))
    kq = [_bdot_nt(jnp.concatenate([kb[j], qh[j]], axis=0), kh[j]) for j in H]
    intra = [jnp.where(tri, kq[j][C:] * decay[j], 0.0) for j in H]
    acc = [-jnp.where(strict, kq[j][:C] * decay[j], 0.0) for j in H]
    parts = [_split_bf16(acc[j]) for j in H]
    pw = [_dot3(parts[j], parts[j]) for j in H]
    for r in range(5):
        pw_parts = [_split_bf16(pw[j]) for j in H]
        acc_parts = [_split_bf16(acc[j]) for j in H]
        acc = [acc[j] + pw[j] + _dot3(pw_parts[j], acc_parts[j]) for j in H]
        if r < 4:
            pw = [_dot3(pw_parts[j], pw_parts[j]) for j in H]
    sol = [rhs[j] + _dot3(_split_bf16(acc[j]), _split_bf16(rhs[j])) for j in H]
    S = [state_s[j] for j in H]
    wq_s = [_bdot(jnp.concatenate([sol[j][:, GDN_DV:], qh[j] * jnp.exp(gcol[j])], axis=0), S[j])
            for j in H]
    v_new = [sol[j][:, :GDN_DV] - wq_s[j][:C] for j in H]
    o = [wq_s[j][C:] + _bdot(intra[j], v_new[j]) for j in H]
    for j in H:
        g_last = gcol[j][C - 1:C, :]
        state_s[j] = (S[j] * jnp.exp(g_last)
                      + _bdot_tn(kh[j] * jnp.exp(g_last - gcol[j]), v_new[j]))
    for j in H:
        ms = jnp.mean(o[j] * o[j], axis=-1, keepdims=True)
        on = o[j] * lax.rsqrt(ms + EPS) * gn_ref[...]
        o_ref[0, :, sls[j]] = (on * _silu(z_ref[0, :, sls[j]])).astype(o_ref.dtype)

    @pl.when(n == pl.num_programs(2) - 1)
    def _():
        s_out_ref[0] = state_s[...]


def gdn_prompt(proj3, ab_g, conv_w, alog_g, dtb_g, gdn_norm):
    B, T, _ = proj3.shape
    C = GDN_CHUNK
    N = T // C
    HW = GDN_HB * GDN_DK
    nq = GDN_QK_W // HW

    def col(base):
        return lambda b, hg, n: (b, n, base + hg)

    def cwcol(base):
        return lambda b, hg, n: (0, base + hg)

    return pl.pallas_call(
        _gdn_chunk_body,
        grid=(B, GDN_HG, N),
        in_specs=[
            pl.BlockSpec((1, C, HW), col(0)),
            pl.BlockSpec((1, C, HW), col(nq)),
            pl.BlockSpec((1, C, HW), col(2 * nq)),
            pl.BlockSpec((1, C, HW), col(OFF_Z // HW)),
            pl.BlockSpec((1, 1, C, 128), lambda b, hg, n: (b, hg, n, 0)),
            pl.BlockSpec((CONV_W, HW), cwcol(0)),
            pl.BlockSpec((CONV_W, HW), cwcol(nq)),
            pl.BlockSpec((CONV_W, HW), cwcol(2 * nq)),
            pl.BlockSpec((1, 1, 128), lambda b, hg, n: (hg, 0, 0)),
            pl.BlockSpec((1, 1, 128), lambda b, hg, n: (hg, 0, 0)),
            pl.BlockSpec((1, GDN_DV), lambda b, hg, n: (0, 0)),
        ],
        out_specs=[
            pl.BlockSpec((1, C, HW), lambda b, hg, n: (b, n, hg)),
            pl.BlockSpec((1, GDN_HB, GDN_DK, GDN_DV), lambda b, hg, n: (b, hg, 0, 0)),
        ],
        out_shape=[
            jax.ShapeDtypeStruct((B, T, GDN_V_W), BF16),
            jax.ShapeDtypeStruct((B, GDN_HEADS, GDN_DK, GDN_DV), F32),
        ],
        scratch_shapes=[
            pltpu.VMEM((C + 8, HW), F32),
            pltpu.VMEM((C + 8, HW), F32),
            pltpu.VMEM((C + 8, HW), F32),
            pltpu.VMEM((GDN_HB, GDN_DK, GDN_DV), F32),
        ],
        compiler_params=_cparams(("parallel", "parallel", "arbitrary")),
        name="gdn_prompt",
    )(proj3, proj3, proj3, proj3, ab_g, conv_w, conv_w, conv_w, alog_g, dtb_g,
      gdn_norm.reshape(1, GDN_DV))


def _gdn_step_body(x_ref, z_ref, ab_ref, cs_ref, s_ref, cw_ref, alog_ref, dtb_ref, gn_ref,
                   o_ref, cs_out_ref, s_out_ref):
    x = x_ref[0]
    y = cw_ref[CONV_W - 1] * x
    for w in range(CONV_W - 1):
        y = y + cw_ref[w] * cs_ref[0, w]
    y = _silu(y)
    cs_out_ref[0, 0] = cs_ref[0, 1]
    cs_out_ref[0, 1] = cs_ref[0, 2]
    cs_out_ref[0, 2] = x
    r_ab = (OFF_AB // 128) % 8
    ab = ab_ref[0, r_ab:r_ab + 1, :]
    g_all = -jnp.exp(alog_ref[...]) * _softplus(ab + dtb_ref[...])
    beta_all = _sigmoid(ab)
    eye = (lax.broadcasted_iota(jnp.int32, (GDN_DK, GDN_DK), 0)
           == lax.broadcasted_iota(jnp.int32, (GDN_DK, GDN_DK), 1))

    def to_col(row):
        return jnp.sum(jnp.where(eye, jnp.broadcast_to(row, (GDN_DK, GDN_DK)), 0.0),
                       axis=1, keepdims=True)

    for h in range(GDN_HEADS):
        q = y[h:h + 1, :]
        k = y[GDN_HEADS + h:GDN_HEADS + h + 1, :]
        v = y[2 * GDN_HEADS + h:2 * GDN_HEADS + h + 1, :]
        q = q * lax.rsqrt(jnp.sum(q * q, axis=-1, keepdims=True) + EPS) * (GDN_DK ** -0.5)
        k = k * lax.rsqrt(jnp.sum(k * k, axis=-1, keepdims=True) + EPS)
        g = g_all[:, h:h + 1]
        beta = beta_all[:, GDN_HEADS + h:GDN_HEADS + h + 1]
        eg = jnp.exp(g)
        S = s_ref[0, h]
        kcol = to_col(k)
        qcol = to_col(q)
        k_s = jnp.sum(S * kcol, axis=0, keepdims=True)
        q_s = jnp.sum(S * qcol, axis=0, keepdims=True)
        v_new = v * beta - (beta * eg) * k_s
        o = eg * q_s + jnp.sum(q * k, axis=-1, keepdims=True) * v_new
        s_out_ref[0, h] = S * eg + kcol * v_new
        ms = jnp.mean(o * o, axis=-1, keepdims=True)
        on = o * lax.rsqrt(ms + EPS) * gn_ref[...]
        o_ref[0, h:h + 1, :] = on * _silu(z_ref[0, h:h + 1, :])


def gdn_step(proj_s, conv_state, ssm_state, conv_w, a_log, dt_bias, gdn_norm):
    Bs = proj_s.shape[0]
    nrow = N_PROJ // 128
    p3 = proj_s.reshape(Bs, nrow, 128)
    cs4 = conv_state.reshape(Bs, CONV_W - 1, CONV_CH // 128, 128)
    cw3 = conv_w.reshape(CONV_W, CONV_CH // 128, 128)
    pad = jnp.zeros((128 - GDN_HEADS,), F32)
    alog = jnp.concatenate([a_log, pad]).reshape(1, 128)
    dtb = jnp.concatenate([dt_bias, pad]).reshape(1, 128)
    nr = CONV_CH // 128
    o, cs_new, s_new = pl.pallas_call(
        _gdn_step_body,
        grid=(Bs,),
        in_specs=[
            pl.BlockSpec((1, nr, 128), lambda b: (b, 0, 0)),
            pl.BlockSpec((1, GDN_HEADS, 128), lambda b: (b, OFF_Z // 128 // GDN_HEADS, 0)),
            pl.BlockSpec((1, 8, 128), lambda b: (b, OFF_AB // 128 // 8, 0)),
            pl.BlockSpec((1, CONV_W - 1, nr, 128), lambda b: (b, 0, 0, 0)),
            pl.BlockSpec((1, GDN_HEADS, GDN_DK, GDN_DV), lambda b: (b, 0, 0, 0)),
            pl.BlockSpec((CONV_W, nr, 128), lambda b: (0, 0, 0)),
            pl.BlockSpec((1, 128), lambda b: (0, 0)),
            pl.BlockSpec((1, 128), lambda b: (0, 0)),
            pl.BlockSpec((1, GDN_DV), lambda b: (0, 0)),
        ],
        out_specs=[
            pl.BlockSpec((1, GDN_HEADS, GDN_DV), lambda b: (b, 0, 0)),
            pl.BlockSpec((1, CONV_W - 1, nr, 128), lambda b: (b, 0, 0, 0)),
            pl.BlockSpec((1, GDN_HEADS, GDN_DK, GDN_DV), lambda b: (b, 0, 0, 0)),
        ],
        out_shape=[
            jax.ShapeDtypeStruct((Bs, GDN_HEADS, GDN_DV), F32),
            jax.ShapeDtypeStruct((Bs, CONV_W - 1, nr, 128), F32),
            jax.ShapeDtypeStruct((Bs, GDN_HEADS, GDN_DK, GDN_DV), F32),
        ],
        compiler_params=_cparams(("parallel",)),
        name="gdn_step",
    )(p3, p3, p3, cs4, ssm_state, cw3, alog, dtb, gdn_norm.reshape(1, GDN_DV))
    return (o.reshape(Bs, GDN_V_W), cs_new.reshape(Bs, CONV_W - 1, CONV_CH), s_new)


def _qknorm_body(q_ref, k_ref, v_ref, qg_ref, kg_ref, qn_ref, kv_ref):
    for h in range(GROUP_HEADS):
        sl = slice(h * HEAD_DIM, (h + 1) * HEAD_DIM)
        q = q_ref[:, sl]
        qn = q * lax.rsqrt(jnp.mean(q * q, axis=-1, keepdims=True) + EPS) * qg_ref[...]
        qn_ref[:, sl] = qn * (HEAD_DIM ** -0.5)
        k = k_ref[:, sl]
        kv_ref[:, sl] = k * lax.rsqrt(jnp.mean(k * k, axis=-1, keepdims=True) + EPS) * kg_ref[...]
    kv_ref[:, GROUP_W:] = v_ref[...]


def qk_norm(proj, q_norm, k_norm):
    M = proj.shape[0]
    tm = min(M, 512)
    blk = GROUP_W
    return pl.pallas_call(
        _qknorm_body,
        grid=(M // tm, N_GROUPS),
        in_specs=[
            pl.BlockSpec((tm, blk), lambda i, g: (i, OFF_AQ // blk + g)),
            pl.BlockSpec((tm, blk), lambda i, g: (i, OFF_AK // blk + g)),
            pl.BlockSpec((tm, blk), lambda i, g: (i, OFF_AV // blk + g)),
            pl.BlockSpec((1, HEAD_DIM), lambda i, g: (0, 0)),
            pl.BlockSpec((1, HEAD_DIM), lambda i, g: (0, 0)),
        ],
        out_specs=[
            pl.BlockSpec((tm, blk), lambda i, g: (i, g)),
            pl.BlockSpec((tm, 2 * blk), lambda i, g: (i, g)),
        ],
        out_shape=[jax.ShapeDtypeStruct((M, ATTN_W), F32),
                   jax.ShapeDtypeStruct((M, N_GROUPS * 2 * GROUP_W), F32)],
        compiler_params=_cparams(("parallel", "parallel")),
        name="qk_norm",
    )(proj, proj, proj, q_norm.reshape(1, HEAD_DIM), k_norm.reshape(1, HEAD_DIM))


def _t5_bucket(dist):
    max_exact = N_BUCKETS // 2
    d = np.maximum(dist, max_exact).astype(np.float32)
    large = max_exact + (np.log(d / max_exact) / np.log(MAX_DISTANCE / max_exact)
                         * (N_BUCKETS - max_exact)).astype(np.int32)
    large = np.minimum(large, N_BUCKETS - 1)
    return np.where(dist < max_exact, dist, large).astype(np.int32)


def _group_bias(rel_bias, gi):
    window, dilation = ATTN_GROUPS[gi]
    n_keys = window // dilation + 1
    buckets = _t5_bucket(np.arange(n_keys) * dilation)
    return rel_bias[buckets][:, gi * GROUP_HEADS:(gi + 1) * GROUP_HEADS].T.astype(F32)


def _attn_prompt_body(T, q0_ref, q1_ref, q2_ref, k0_ref, v0_ref, k1_ref, v1_ref, k2_ref, v2_ref,
                      bias_ref, y_ref, m_s, l_s, o_s):
    QB = Q_BLOCK
    q_refs = (q0_ref, q1_ref, q2_ref)
    k_refs = (k0_ref, k1_ref, k2_ref)
    v_refs = (v0_ref, v1_ref, v2_ref)
    for gi, (window, d) in enumerate(ATTN_GROUPS):
        nq = T // (d * QB)
        for r in range(d):
            for qi in range(nq):
                rows = pl.ds(r + d * QB * qi, QB, stride=d) if d > 1 else pl.ds(QB * qi, QB)
                q = q_refs[gi][0, rows, :].astype(BF16)
                kc = k_refs[gi][0, rows, :]
                vc = v_refs[gi][0, rows, :]
                if qi > 0:
                    prev = (pl.ds(r + d * QB * (qi - 1), QB, stride=d) if d > 1
                            else pl.ds(QB * (qi - 1), QB))
                    kcat = jnp.concatenate([k_refs[gi][0, prev, :], kc], axis=0).astype(BF16)
                    vcat = jnp.concatenate([v_refs[gi][0, prev, :], vc], axis=0).astype(BF16)
                    bias = bias_ref[gi, 0]
                else:
                    kcat = kc.astype(BF16)
                    vcat = vc.astype(BF16)
                    bias = bias_ref[gi, 0, :, QB:]
                s = lax.dot_general(q, kcat, (((1,), (1,)), ((), ())),
                                    preferred_element_type=F32) + bias
                m_blk = jnp.max(s, axis=-1, keepdims=True)
                if gi == 0:
                    m_new = m_blk
                else:
                    m_old = m_s[rows, :][:, 0:1]
                    m_new = jnp.maximum(m_old, m_blk)
                p = jnp.exp(s - m_new)
                l_new = jnp.sum(p, axis=-1, keepdims=True)
                o_new = jnp.dot(p.astype(BF16), vcat, preferred_element_type=F32)
                if gi > 0:
                    alpha = jnp.exp(m_old - m_new)
                    l_new = l_new + alpha * l_s[rows, :][:, 0:1]
                    o_new = o_new + alpha * o_s[rows, :]
                m_s[rows, :] = jnp.broadcast_to(m_new, (QB, HEAD_DIM))
                l_s[rows, :] = jnp.broadcast_to(l_new, (QB, HEAD_DIM))
                o_s[rows, :] = o_new
    y_ref[0] = (o_s[...] / l_s[...]).astype(y_ref.dtype)


def attn_prompt(qn3, kv3, bias_all):
    B, T, _ = qn3.shape
    blk = (1, T, HEAD_DIM)

    def qspec(gi):
        return pl.BlockSpec(blk, lambda b, h: (b, 0, gi * GROUP_HEADS + h))

    def kvspec(gi, is_v):
        return pl.BlockSpec(blk, lambda b, h: (b, 0, (2 * gi + is_v) * GROUP_HEADS + h))

    in_specs = [qspec(0), qspec(1), qspec(2)]
    for gi in range(N_GROUPS):
        in_specs += [kvspec(gi, 0), kvspec(gi, 1)]
    in_specs.append(pl.BlockSpec((N_GROUPS, 1, Q_BLOCK, 2 * Q_BLOCK), lambda b, h: (0, h, 0, 0)))
    y = pl.pallas_call(
        functools.partial(_attn_prompt_body, T),
        grid=(B, GROUP_HEADS),
        in_specs=in_specs,
        out_specs=pl.BlockSpec(blk, lambda b, h: (b, 0, h)),
        out_shape=jax.ShapeDtypeStruct((B, T, GROUP_W), BF16),
        scratch_shapes=[pltpu.VMEM((T, HEAD_DIM), F32), pltpu.VMEM((T, HEAD_DIM), F32),
                        pltpu.VMEM((T, HEAD_DIM), F32)],
        compiler_params=_cparams(("parallel", "parallel")),
        name="attn_prompt",
    )(qn3, qn3, qn3, kv3, kv3, kv3, kv3, kv3, kv3, bias_all)
    return y.reshape(B * T, GROUP_W)


def _prompt_bias_matrix(rel_bias, gi):
    QB = Q_BLOCK
    bias = _group_bias(rel_bias, gi)
    row0 = jnp.concatenate([bias[:, ::-1], jnp.full((GROUP_HEADS, QB), NEG_INF, F32)], axis=1)
    flat = jnp.tile(row0, (1, QB))[:, :QB * 2 * QB]
    return flat.reshape(GROUP_HEADS, QB, 2 * QB)


def _attn_decode_body(q_ref, n0_ref, n1_ref, n2_ref, c0_ref, c1_ref, c2_ref, bc_ref, bn_ref, y_ref):
    new_refs = (n0_ref, n1_ref, n2_ref)
    cache_refs = (c0_ref, c1_ref, c2_ref)
    outs, lds = [], []
    for gi in range(N_GROUPS):
        q = q_ref[0, gi * GROUP_HEADS:(gi + 1) * GROUP_HEADS, :]
        kc = cache_refs[gi][:, 0]
        vc = cache_refs[gi][:, 1]
        kn = new_refs[gi][0, 0:GROUP_HEADS, :]
        vn = new_refs[gi][0, GROUP_HEADS:2 * GROUP_HEADS, :]
        s_c = jnp.sum(kc * q[None], axis=-1, keepdims=True) + bc_ref[gi, :, :, 0:1]
        s_n = jnp.sum(kn * q, axis=-1, keepdims=True) + bn_ref[gi, :, 0:1]
        m = jnp.maximum(jnp.max(s_c, axis=0), s_n)
        p_c = jnp.exp(s_c - m[None])
        p_n = jnp.exp(s_n - m)
        l = jnp.sum(p_c, axis=0) + p_n
        outs.append((jnp.sum(p_c * vc, axis=0) + p_n * vn) / l)
        lds.append(m + jnp.log(l))
    m = jnp.maximum(jnp.maximum(lds[0], lds[1]), lds[2])
    es = [jnp.exp(x - m) for x in lds]
    den = es[0] + es[1] + es[2]
    y_ref[0] = outs[0] * (es[0] / den) + outs[1] * (es[1] / den) + outs[2] * (es[2] / den)


def attn_decode(qn_s, kv_new, caches, rel_bias):
    Bs = qn_s.shape[0]
    q3 = qn_s.reshape(Bs, N_GROUPS * GROUP_HEADS, HEAD_DIM)
    new3 = [x.reshape(Bs, 2 * GROUP_HEADS, HEAD_DIM) for x in kv_new]
    cviews, bcs, bns = [], [], []
    for gi, (window, d) in enumerate(ATTN_GROUPS):
        nk = window // d
        assert caches[gi].shape[1] == window and nk == Q_BLOCK
        cviews.append(caches[gi].reshape(Bs, nk, d, 2, GROUP_HEADS, HEAD_DIM))
        bias = _group_bias(rel_bias, gi)
        bc = bias[:, nk - np.arange(nk)]
        bcs.append(jnp.broadcast_to(bc.T[:, :, None], (nk, GROUP_HEADS, HEAD_DIM)))
        bns.append(jnp.broadcast_to(bias[:, 0:1], (GROUP_HEADS, HEAD_DIM)))
    bc_all = jnp.stack(bcs, 0)
    bn_all = jnp.stack(bns, 0)
    nspec = pl.BlockSpec((1, 2 * GROUP_HEADS, HEAD_DIM), lambda b: (b, 0, 0))
    cspec = pl.BlockSpec((None, Q_BLOCK, None, 2, GROUP_HEADS, HEAD_DIM),
                         lambda b: (b, 0, 0, 0, 0, 0))
    y = pl.pallas_call(
        _attn_decode_body,
        grid=(Bs,),
        in_specs=[
            pl.BlockSpec((1, N_GROUPS * GROUP_HEADS, HEAD_DIM), lambda b: (b, 0, 0)),
            nspec, nspec, nspec, cspec, cspec, cspec,
            pl.BlockSpec((N_GROUPS, Q_BLOCK, GROUP_HEADS, HEAD_DIM), lambda b: (0, 0, 0, 0)),
            pl.BlockSpec((N_GROUPS, GROUP_HEADS, HEAD_DIM), lambda b: (0, 0, 0)),
        ],
        out_specs=pl.BlockSpec((1, GROUP_HEADS, HEAD_DIM), lambda b: (b, 0, 0)),
        out_shape=jax.ShapeDtypeStruct((Bs, GROUP_HEADS, HEAD_DIM), F32),
        compiler_params=_cparams(("parallel",)),
        name="attn_decode",
    )(q3, *new3, *cviews, bc_all, bn_all)
    return y.reshape(Bs, GROUP_W)


def _merge_body(ya_ref, yb_ref, wa_ref, wb_ref, ga_ref, gb_ref, o_ref):
    pa = jnp.dot(ya_ref[...], wa_ref[...], preferred_element_type=F32)
    pb = jnp.dot(yb_ref[...], wb_ref[...], preferred_element_type=F32)
    o_ref[...] = (_sigmoid(ga_ref[...]) * pa + _sigmoid(gb_ref[...]) * pb).astype(o_ref.dtype)


def branch_merge(ya, yb, wa, wb, proj):
    M = ya.shape[0]
    tm = min(M, 1024)
    tn = 512
    return pl.pallas_call(
        _merge_body,
        grid=(M // tm, D_MODEL // tn),
        in_specs=[
            pl.BlockSpec((tm, GDN_V_W), lambda i, j: (i, 0)),
            pl.BlockSpec((tm, GROUP_W), lambda i, j: (i, 0)),
            pl.BlockSpec((GDN_V_W, tn), lambda i, j: (0, j)),
            pl.BlockSpec((GROUP_W, tn), lambda i, j: (0, j)),
            pl.BlockSpec((tm, tn), lambda i, j: (i, OFF_GA // tn + j)),
            pl.BlockSpec((tm, tn), lambda i, j: (i, OFF_GB // tn + j)),
        ],
        out_specs=pl.BlockSpec((tm, tn), lambda i, j: (i, j)),
        out_shape=jax.ShapeDtypeStruct((M, D_MODEL), BF16),
        compiler_params=_cparams(("parallel", "parallel")),
        name="branch_merge",
    )(ya, yb, wa, wb, proj, proj)


def _extract_top(x, count, tie_safe):
    R = x.shape[0]
    rid = lax.broadcasted_iota(jnp.int32, x.shape, 0).astype(F32)
    vid = lax.broadcasted_iota(jnp.int32, (count, x.shape[1]), 0)
    vals = jnp.zeros((count, x.shape[1]), F32)
    rank = jnp.full(x.shape, float(count), F32)
    for a in range(count):
        m = jnp.max(x, axis=0, keepdims=True)
        hit = x == m
        if tie_safe:
            first = jnp.min(jnp.where(hit, rid, float(R)), axis=0, keepdims=True)
            hit = rid == first
        x = jnp.where(hit, -jnp.inf, x)
        rank = jnp.where(hit, float(a), rank)
        vals = jnp.where(vid == a, m, vals)
    return vals, rank


CAND_HALF = PEER_TOPK // 2
N_CAND = PEER_TOPK + (CAND_HALF - 1) * CAND_HALF + CAND_HALF


def _peer_topk_body(q_ref, sk_ref, cnt_ref, r2_ref, e1_ref, e2_ref, cand_s, s1_s, s2_s):
    K = PEER_TOPK
    KH = CAND_HALF

    def select(h, s1, s2, tie_safe):
        v1, rank1 = _extract_top(s1, K, tie_safe)
        v2, rank2 = _extract_top(s2, K, tie_safe)
        cand_s[0:K, :] = v2 + v1[0:1, :]
        for a in range(1, KH):
            cand_s[K + (a - 1) * KH:K + a * KH, :] = v2[0:KH, :] + v1[a:a + 1, :]
        cand_s[N_CAND - KH:N_CAND, :] = v1[KH:K, :] + v2[0:1, :]
        best, crank = _extract_top(cand_s[...], K, tie_safe)
        picked = jnp.where(crank < float(K), 1.0, 0.0)
        vid = lax.broadcasted_iota(jnp.int32, (KH, picked.shape[1]), 0)
        low = jnp.where(vid == 0, jnp.sum(picked[0:K], axis=0, keepdims=True), 0.0)
        for a in range(1, KH):
            rows = picked[K + (a - 1) * KH:K + a * KH]
            low = jnp.where(vid == a, jnp.sum(rows, axis=0, keepdims=True), low)
        count = jnp.concatenate([low, picked[N_CAND - KH:N_CAND]], axis=0)
        cnt1 = jnp.zeros(rank1.shape, F32)
        for a in range(K):
            cnt1 = jnp.where(rank1 == float(a), count[a:a + 1, :], cnt1)
        cmax = best[0:1, :]
        z = jnp.sum(jnp.exp(best - cmax), axis=0, keepdims=True)
        in1 = jnp.where(rank1 < float(K), 1.0, 0.0)
        in2 = jnp.where(rank2 < float(K), 1.0, 0.0)
        cnt_ref[h] = cnt1
        r2_ref[h] = rank2
        e1_ref[h] = in1 * jnp.exp(s1 - v1[0:1, :])
        e2_ref[h] = in2 * jnp.exp(s2 - v2[0:1, :]) / z
        return (jnp.sum(in1, axis=0, keepdims=True) + jnp.sum(in2, axis=0, keepdims=True)
                + jnp.sum(picked, axis=0, keepdims=True) - 3.0 * K)

    def head(h, carry):
        off = pl.multiple_of(h * (2 * N_KEYS), 2 * N_KEYS)
        q1 = q_ref[:, pl.ds(off, N_KEYS)]
        q2 = q_ref[:, pl.ds(pl.multiple_of(off + N_KEYS, N_KEYS), N_KEYS)]
        s1_s[...] = lax.dot_general(sk_ref[0], q1, (((1,), (1,)), ((), ())), precision=HIGHEST,
                                    preferred_element_type=F32)
        s2_s[...] = lax.dot_general(sk_ref[1], q2, (((1,), (1,)), ((), ())), precision=HIGHEST,
                                    preferred_element_type=F32)
        extra = select(h, s1_s[...], s2_s[...], tie_safe=False)

        @pl.when(jnp.max(extra) > 0.0)
        def _():
            select(h, s1_s[...], s2_s[...], tie_safe=True)

        return carry

    lax.fori_loop(0, PEER_HEADS, head, 0)


def peer_topk(query, subkeys):
    M = query.shape[0]
    tm = min(M, 256)
    big = jax.ShapeDtypeStruct((PEER_HEADS, N_KEYS, M), F32)
    bspec = pl.BlockSpec((PEER_HEADS, N_KEYS, tm), lambda i: (0, 0, i))
    return pl.pallas_call(
        _peer_topk_body,
        grid=(M // tm,),
        in_specs=[pl.BlockSpec((tm, PEER_HEADS * 2 * N_KEYS), lambda i: (i, 0)),
                  pl.BlockSpec((2, N_KEYS, N_KEYS), lambda i: (0, 0, 0))],
        out_specs=[bspec, bspec, bspec, bspec],
        out_shape=[big, big, big, big],
        scratch_shapes=[pltpu.VMEM((N_CAND, tm), F32), pltpu.VMEM((N_KEYS, tm), F32),
                        pltpu.VMEM((N_KEYS, tm), F32)],
        compiler_params=_cparams(("parallel",)),
        name="peer_topk",
    )(query, subkeys)


def _gelu_exact(x):
    return 0.5 * x * (1.0 + lax.erf(x * (2.0 ** -0.5)))


GATE_ROWS = 8


def _peer_gate_body(hf_ref, u_ref, cnt_ref, r2_ref, e1_ref, e2_ref, w_ref, g_s):
    j = pl.program_id(1)
    tm = hf_ref.shape[0]
    grp = pl.multiple_of(j * GATE_ROWS, GATE_ROWS)

    for kk in range(GATE_ROWS):
        for tc in range(tm // 128):
            ts = slice(tc * 128, (tc + 1) * 128)
            acc = jnp.zeros((N_KEYS, 128), F32)
            for h in range(PEER_HEADS):
                cnt_row = cnt_ref[h, pl.ds(grp, GATE_ROWS), ts][kk:kk + 1, :]
                e1_row = e1_ref[h, pl.ds(grp, GATE_ROWS), ts][kk:kk + 1, :]
                acc = acc + jnp.where(r2_ref[h, :, ts] < cnt_row, e2_ref[h, :, ts] * e1_row, 0.0)
            g_s[ts, kk * N_KEYS:(kk + 1) * N_KEYS] = acc.T
    hmat = lax.dot_general(hf_ref[...], u_ref[...], (((1,), (1,)), ((), ())),
                           preferred_element_type=F32)
    w_ref[...] = (g_s[...] * _gelu_exact(hmat)).astype(w_ref.dtype)


def peer_gate_weights(hf, u_tab, cnt, r2, e1, e2, tm=1024):
    M = hf.shape[0]
    tm = min(M, tm)
    tn = GATE_ROWS * N_KEYS
    E = u_tab.shape[0]
    once = pl.Buffered(1)
    sspec = pl.BlockSpec((PEER_HEADS, N_KEYS, tm), lambda i, j: (0, 0, i), pipeline_mode=once)
    return pl.pallas_call(
        _peer_gate_body,
        grid=(M // tm, E // tn),
        in_specs=[
            pl.BlockSpec((tm, D_MODEL), lambda i, j: (i, 0), pipeline_mode=once),
            pl.BlockSpec((tn, D_MODEL), lambda i, j: (j, 0)),
            sspec, sspec, sspec, sspec,
        ],
        out_specs=pl.BlockSpec((tm, tn), lambda i, j: (i, j)),
        out_shape=jax.ShapeDtypeStruct((M, E), BF16),
        scratch_shapes=[pltpu.VMEM((tm, tn), F32)],
        compiler_params=_cparams(("parallel", "arbitrary")),
        name="peer_gate",
    )(hf, u_tab, cnt, r2, e1, e2)


def _mm_kres_body(a_ref, b_ref, r_ref, o_ref):
    k = pl.program_id(2)

    @pl.when(k == 0)
    def _():
        o_ref[...] = r_ref[...]

    o_ref[...] += jnp.dot(a_ref[...], b_ref[...], preferred_element_type=F32)


def matmul_kres(a, b, residual, tm=1024, tn=1024, tk=4096, name="matmul_k"):
    M, K = a.shape
    N = b.shape[1]
    tm = min(tm, M)
    return pl.pallas_call(
        _mm_kres_body,
        grid=(M // tm, N // tn, K // tk),
        in_specs=[pl.BlockSpec((tm, tk), lambda i, j, k: (i, k)),
                  pl.BlockSpec((tk, tn), lambda i, j, k: (k, j)),
                  pl.BlockSpec((tm, tn), lambda i, j, k: (i, j))],
        out_specs=pl.BlockSpec((tm, tn), lambda i, j, k: (i, j)),
        out_shape=jax.ShapeDtypeStruct((M, N), F32),
        compiler_params=_cparams(("parallel", "parallel", "arbitrary")),
        name=name,
    )(a, b, residual)


def peer_ffn(x1, norm_ffn, wq_bf, subkeys, u_bf, v_bf):
    M = x1.shape[0]
    Mp = max(M, 128)
    if Mp != M:
        x1 = jnp.concatenate([x1, jnp.zeros((Mp - M, D_MODEL), F32)], axis=0)
    hf = rmsnorm_bf16(x1, norm_ffn)
    query = matmul(hf, wq_bf, name="peer_query")
    cnt, r2, e1, e2 = peer_topk(query, subkeys)
    w = peer_gate_weights(hf, u_bf, cnt, r2, e1, e2)
    return matmul_kres(w, v_bf, x1, name="peer_out")[:M]


W_BLK = 512
N_AB = 2 * GDN_HEADS
J_AB = OFF_AB // W_BLK


def _prep_w_in_body(a_ref, b_ref, o_ref):
    j = pl.program_id(0)

    @pl.when(j < OFF_AQ // W_BLK)
    def _():
        o_ref[...] = a_ref[...].astype(o_ref.dtype)

    @pl.when(jnp.logical_and(j >= OFF_AQ // W_BLK, j < J_AB))
    def _():
        o_ref[0:W_BLK - N_AB, :] = a_ref[N_AB:, :].astype(o_ref.dtype)
        o_ref[W_BLK - N_AB:, :] = b_ref[...].astype(o_ref.dtype)

    @pl.when(j == J_AB)
    def _():
        o_ref[0:N_AB, :] = b_ref[...].astype(o_ref.dtype)
        o_ref[N_AB:, :] = jnp.zeros((W_BLK - N_AB, o_ref.shape[1]), o_ref.dtype)


def _prep_w_in(w_in_t):
    D = w_in_t.shape[1]
    j0 = OFF_AQ // W_BLK
    sub = W_BLK // N_AB
    return pl.pallas_call(
        _prep_w_in_body,
        grid=(N_PROJ // W_BLK,),
        in_specs=[
            pl.BlockSpec((W_BLK, D), lambda j: (jnp.where(j == J_AB, j0, j), 0)),
            pl.BlockSpec((N_AB, D), lambda j: (jnp.where(j == J_AB, j0 * sub,
                                                         jnp.where(j >= j0, (j + 1) * sub, 0)), 0)),
        ],
        out_specs=pl.BlockSpec((W_BLK, D), lambda j: (j, 0)),
        out_shape=jax.ShapeDtypeStruct((N_PROJ, D), BF16),
        compiler_params=_cparams(("parallel",)),
        name="prep_w_in",
    )(w_in_t, w_in_t)


def _gdn_group_params(a_log, dt_bias):
    pad = jnp.zeros((GDN_HG, 128 - GDN_HB), F32)
    alog = jnp.concatenate([a_log.reshape(GDN_HG, GDN_HB), pad], axis=1).reshape(GDN_HG, 1, 128)
    dtb = jnp.concatenate([dt_bias.reshape(GDN_HG, GDN_HB), pad], axis=1).reshape(GDN_HG, 1, 128)
    return alog, dtb


def _ab_grouped(proj, B, T):
    ab = proj[:, OFF_AB:OFF_AB + 2 * GDN_HEADS].reshape(B, T, 2, GDN_HG, GDN_HB)
    ab = ab.transpose(0, 3, 1, 2, 4).reshape(B, GDN_HG, T, 2 * GDN_HB)
    return jnp.concatenate([ab, jnp.zeros((B, GDN_HG, T, 128 - 2 * GDN_HB), F32)], axis=-1)


def kernel(x_prompt, x_sample, state_ssm, state_conv, cache_kv_g0, cache_kv_g1, cache_kv_g2, rel_bias, norm_mix, w_in, conv_w, a_log, dt_bias, gdn_norm, q_norm, k_norm, w_branch_a, w_branch_b, w_out, norm_ffn, peer_w_query, peer_subkeys, peer_u, peer_v):
    B, T, D = x_prompt.shape
    Bs = x_sample.shape[0]
    assert state_ssm.shape[0] == 1 and x_sample.shape[1] == 1
    caches = (cache_kv_g0[0], cache_kv_g1[0], cache_kv_g2[0])
    w_cat_t = _prep_w_in(jnp.swapaxes(w_in[0], 0, 1))
    wa_bf = w_branch_a[0].astype(BF16)
    wb_bf = w_branch_b[0].astype(BF16)
    wo_bf = w_out[0].astype(BF16)
    wq_bf = peer_w_query[0].astype(BF16)
    u_bf = peer_u[0].astype(BF16)
    v_bf = peer_v[0].astype(BF16)
    cw = conv_w[0]

    xp = x_prompt.reshape(B * T, D)
    xs = x_sample.reshape(Bs, D)

    proj = matmul_nt(rmsnorm_bf16(xp, norm_mix[0]), w_cat_t, name="in_proj")
    alog_g, dtb_g = _gdn_group_params(a_log[0], dt_bias[0])
    ya, ssm_p = gdn_prompt(proj.reshape(B, T, N_PROJ), _ab_grouped(proj, B, T), cw, alog_g, dtb_g,
                           gdn_norm[0])
    conv_p = proj.reshape(B, T, N_PROJ)[:, T - (CONV_W - 1):, :CONV_CH]
    qn, kv = qk_norm(proj, q_norm[0], k_norm[0])
    kv3 = kv.reshape(B, T, N_GROUPS * 2 * GROUP_W)
    kvs = [kv3[:, :, gi * 2 * GROUP_W:(gi + 1) * 2 * GROUP_W] for gi in range(N_GROUPS)]
    bias_all = jnp.stack([_prompt_bias_matrix(rel_bias, gi) for gi in range(N_GROUPS)], 0)
    yb = attn_prompt(qn.reshape(B, T, ATTN_W), kv3, bias_all)
    merged = branch_merge(ya.reshape(B * T, GDN_V_W), yb, wa_bf, wb_bf, proj)
    x1 = matmul(merged, wo_bf, residual=xp, name="out_proj")
    y_prompt = peer_ffn(x1, norm_ffn[0], wq_bf, peer_subkeys[0], u_bf, v_bf).reshape(B, T, D)
    p_kv = []
    for gi, (window, _) in enumerate(ATTN_GROUPS):
        keep = min(window, T)
        kvg = kvs[gi][:, T - keep:].reshape(B, keep, 2, GROUP_HEADS, HEAD_DIM)
        p_kv.append(kvg[None])

    proj_s = matmul_nt(rmsnorm_bf16(xs, norm_mix[0]), w_cat_t, name="in_proj_s")
    ya_s, conv_s, ssm_s = gdn_step(proj_s, state_conv[0], state_ssm[0], cw, a_log[0], dt_bias[0],
                                   gdn_norm[0])
    qn_s, kv_s = qk_norm(proj_s, q_norm[0], k_norm[0])
    kvs_s = [kv_s[:, gi * 2 * GROUP_W:(gi + 1) * 2 * GROUP_W] for gi in range(N_GROUPS)]
    yb_s = attn_decode(qn_s, kvs_s, caches, rel_bias)
    merged_s = branch_merge(ya_s.astype(BF16), yb_s.astype(BF16), wa_bf, wb_bf, proj_s)
    x1_s = matmul(merged_s, wo_bf, residual=xs, name="out_proj_s")
    y_sample = peer_ffn(x1_s, norm_ffn[0], wq_bf, peer_subkeys[0], u_bf, v_bf).reshape(Bs, 1, D)
    s_kv = [kvs_s[gi].reshape(1, Bs, 1, 2, GROUP_HEADS, HEAD_DIM) for gi in range(N_GROUPS)]

    return (y_prompt, y_sample,
            ssm_p[None], conv_p[None], p_kv[0], p_kv[1], p_kv[2],
            ssm_s[None], conv_s[None], s_kv[0], s_kv[1], s_kv[2])
```

```python
import functools

import numpy as np
import jax
import jax.numpy as jnp
from jax import lax
from jax.experimental import pallas as pl
from jax.experimental.pallas import tpu as pltpu

F32 = jnp.float32
BF16 = jnp.bfloat16
HIGHEST = lax.Precision.HIGHEST

D_MODEL = 4096
GDN_HEADS = 16
GDN_DK = 128
GDN_DV = 128
GDN_QK_W = GDN_HEADS * GDN_DK
GDN_V_W = GDN_HEADS * GDN_DV
CONV_W = 4
CONV_CH = 2 * GDN_QK_W + GDN_V_W
GDN_CHUNK = 64
ATTN_GROUPS = ((128, 1), (512, 4), (2048, 16))
N_GROUPS = 3
GROUP_HEADS = 4
HEAD_DIM = 128
GROUP_W = GROUP_HEADS * HEAD_DIM
ATTN_W = N_GROUPS * GROUP_W
Q_BLOCK = 128
N_BUCKETS = 32
MAX_DISTANCE = 2048
PEER_HEADS = 8
N_KEYS = 128
PEER_TOPK = 16
EPS = 1e-6
NEG_INF = -1e30

OFF_QKV = 0
OFF_Z = CONV_CH
OFF_AQ = OFF_Z + GDN_V_W
OFF_AK = OFF_AQ + ATTN_W
OFF_AV = OFF_AK + ATTN_W
OFF_GA = OFF_AV + ATTN_W
OFF_GB = OFF_GA + D_MODEL
OFF_AB = OFF_GB + D_MODEL
N_PROJ = 21504

GDN_HB = 16
GDN_HG = GDN_HEADS // GDN_HB
VMEM_LIMIT = 56 * 1024 * 1024


def _cparams(sem):
    return pltpu.CompilerParams(dimension_semantics=sem, vmem_limit_bytes=VMEM_LIMIT)


def _sigmoid(x):
    return 1.0 / (1.0 + jnp.exp(-x))


def _silu(x):
    return x * _sigmoid(x)


def _softplus(x):
    return jnp.maximum(x, 0.0) + jnp.log(1.0 + jnp.exp(-jnp.abs(x)))


def _bdot(a, b):
    return jnp.dot(a.astype(BF16), b.astype(BF16), preferred_element_type=F32)


def _bdot_nt(a, b):
    return lax.dot_general(a.astype(BF16), b.astype(BF16), (((1,), (1,)), ((), ())),
                           preferred_element_type=F32)


def _bdot_tn(a, b):
    return lax.dot_general(a.astype(BF16), b.astype(BF16), (((0,), (0,)), ((), ())),
                           preferred_element_type=F32)


def _hdot(a, b):
    return jnp.dot(a, b, precision=HIGHEST, preferred_element_type=F32)


def _split_bf16(a):
    hi = a.astype(BF16)
    lo = (a - hi.astype(F32)).astype(BF16)
    return hi, lo


def _dot3(a_parts, b_parts):
    a_hi, a_lo = a_parts
    b_hi, b_lo = b_parts
    return (jnp.dot(a_hi, b_hi, preferred_element_type=F32)
            + jnp.dot(a_lo, b_hi, preferred_element_type=F32)
            + jnp.dot(a_hi, b_lo, preferred_element_type=F32))


def _rmsnorm_body(x_ref, g_ref, o_ref):
    x = x_ref[...]
    ms = jnp.mean(x * x, axis=-1, keepdims=True)
    o_ref[...] = (x * lax.rsqrt(ms + EPS) * g_ref[...]).astype(o_ref.dtype)


def rmsnorm_bf16(x, gain):
    M, D = x.shape
    tm = min(M, 256)
    return pl.pallas_call(
        _rmsnorm_body,
        grid=(M // tm,),
        in_specs=[pl.BlockSpec((tm, D), lambda i: (i, 0)),
                  pl.BlockSpec((1, D), lambda i: (0, 0))],
        out_specs=pl.BlockSpec((tm, D), lambda i: (i, 0)),
        out_shape=jax.ShapeDtypeStruct((M, D), BF16),
        compiler_params=_cparams(("parallel",)),
        name="rmsnorm",
    )(x, gain.reshape(1, D))


def _mm_body(a_ref, b_ref, o_ref):
    o_ref[...] = jnp.dot(a_ref[...], b_ref[...], preferred_element_type=F32)


def _mm_res_body(a_ref, b_ref, r_ref, o_ref):
    o_ref[...] = r_ref[...] + jnp.dot(a_ref[...], b_ref[...], preferred_element_type=F32)


def matmul(a, b, residual=None, tm=1024, tn=1024, name="matmul"):
    M, K = a.shape
    N = b.shape[1]
    tm = min(tm, M)
    tn = min(tn, N)
    in_specs = [pl.BlockSpec((tm, K), lambda i, j: (i, 0)),
                pl.BlockSpec((K, tn), lambda i, j: (0, j))]
    args = [a, b]
    body = _mm_body
    if residual is not None:
        in_specs.append(pl.BlockSpec((tm, tn), lambda i, j: (i, j)))
        args.append(residual)
        body = _mm_res_body
    return pl.pallas_call(
        body,
        grid=(M // tm, N // tn),
        in_specs=in_specs,
        out_specs=pl.BlockSpec((tm, tn), lambda i, j: (i, j)),
        out_shape=jax.ShapeDtypeStruct((M, N), F32),
        compiler_params=_cparams(("parallel", "parallel")),
        name=name,
    )(*args)


def _mm_nt2_body(a_ref, a2_ref, bt_ref, o_ref, o2_ref):
    dims = (((1,), (1,)), ((), ()))
    o_ref[...] = lax.dot_general(a_ref[...], bt_ref[...], dims, preferred_element_type=F32)

    @pl.when(pl.program_id(1) == 0)
    def _():
        o2_ref[...] = lax.dot_general(a2_ref[...], bt_ref[...], dims, preferred_element_type=F32)


def matmul_nt2(a, a2, bt, tm=1024, tn=1024, name="matmul_nt2"):
    M, K = a.shape
    M2 = a2.shape[0]
    N = bt.shape[0]
    tm = min(tm, M)
    return pl.pallas_call(
        _mm_nt2_body,
        grid=(N // tn, M // tm),
        in_specs=[pl.BlockSpec((tm, K), lambda j, i: (i, 0)),
                  pl.BlockSpec((M2, K), lambda j, i: (0, 0)),
                  pl.BlockSpec((tn, K), lambda j, i: (j, 0))],
        out_specs=[pl.BlockSpec((tm, tn), lambda j, i: (i, j)),
                   pl.BlockSpec((M2, tn), lambda j, i: (0, j))],
        out_shape=[jax.ShapeDtypeStruct((M, N), F32), jax.ShapeDtypeStruct((M2, N), F32)],
        compiler_params=_cparams(("parallel", "arbitrary")),
        name=name,
    )(a, a2, bt)


def _gdn_chunk_body(q_ref, k_ref, v_ref, z_ref, ab_ref, cwq_ref, cwk_ref, cwv_ref,
                    alog_ref, dtb_ref, gn_ref, o_ref, s_out_ref,
                    xq_s, xk_s, xv_s, state_s):
    n = pl.program_id(2)
    C = GDN_CHUNK
    HW = GDN_HB * GDN_DK

    @pl.when(n == 0)
    def _():
        xq_s[0:8, :] = jnp.zeros((8, HW), F32)
        xk_s[0:8, :] = jnp.zeros((8, HW), F32)
        xv_s[0:8, :] = jnp.zeros((8, HW), F32)
        state_s[...] = jnp.zeros(state_s.shape, F32)

    def conv(x_ref, xs, cw_ref):
        xs[8:8 + C, :] = x_ref[0]
        y = cw_ref[CONV_W - 1:CONV_W, :] * xs[8:8 + C, :]
        for w in range(CONV_W - 1):
            y = y + cw_ref[w:w + 1, :] * xs[5 + w:5 + w + C, :]
        xs[0:8, :] = xs[C:C + 8, :]
        return _silu(y)

    q_all = conv(q_ref, xq_s, cwq_ref)
    k_all = conv(k_ref, xk_s, cwk_ref)
    v_all = conv(v_ref, xv_s, cwv_ref)

    ab = ab_ref[0, 0]
    g_all = -jnp.exp(alog_ref[0]) * _softplus(ab + dtb_ref[0])
    beta_all = _sigmoid(ab)
    ri = lax.broadcasted_iota(jnp.int32, (C, C), 0)
    ci = lax.broadcasted_iota(jnp.int32, (C, C), 1)
    tri = ri >= ci
    strict = ri > ci
    gc_all = _hdot(tri.astype(F32), g_all)
    gc_t = jnp.concatenate([gc_all, jnp.zeros((128 - C, 128), F32)], axis=0).T

    H = range(GDN_HB)
    sls = [slice(j * GDN_DK, (j + 1) * GDN_DK) for j in H]
    qh, kh, kb, gcol, decay, rhs = [], [], [], [], [], []
    for j in H:
        q = q_all[:, sls[j]]
        k = k_all[:, sls[j]]
        q = q * lax.rsqrt(jnp.sum(q * q, axis=-1, keepdims=True) + EPS) * (GDN_DK ** -0.5)
        k = k * lax.rsqrt(jnp.sum(k * k, axis=-1, keepdims=True) + EPS)
        beta = beta_all[:, GDN_HB + j:GDN_HB + j + 1]
        gc = gc_all[:, j:j + 1]
        grow = gc_t[j:j + 1, :C]
        qh.append(q)
        kh.append(k)
        kb.append(k * beta)
        gcol.append(gc)
        decay.append(jnp.where(tri, jnp.exp(jnp.where(tri, gc - grow, 0.0)), 0.0))
        rhs.append(jnp.concatenate([v_all[:, sls[j]] * beta, kb[j] * jnp.exp(gc)], axis=-1))
    kq = [_bdot_nt(jnp.concatenate([kb[j], qh[j]], axis=0), kh[j]) for j in H]
    intra = [jnp.where(tri, kq[j][C:] * decay[j], 0.0) for j in H]
    acc = [-jnp.where(strict, kq[j][:C] * decay[j], 0.0) for j in H]
    parts = [_split_bf16(acc[j]) for j in H]
    pw = [_dot3(parts[j], parts[j]) for j in H]
    for r in range(5):
        pw_parts = [_split_bf16(pw[j]) for j in H]
        acc_parts = [_split_bf16(acc[j]) for j in H]
        acc = [acc[j] + pw[j] + _dot3(pw_parts[j], acc_parts[j]) for j in H]
        if r < 4:
            pw = [_dot3(pw_parts[j], pw_parts[j]) for j in H]
    sol = [rhs[j] + _dot3(_split_bf16(acc[j]), _split_bf16(rhs[j])) for j in H]
    S = [state_s[j] for j in H]
    wq_s = [_bdot(jnp.concatenate([sol[j][:, GDN_DV:], qh[j] * jnp.exp(gcol[j])], axis=0), S[j])
            for j in H]
    v_new = [sol[j][:, :GDN_DV] - wq_s[j][:C] for j in H]
    o = [wq_s[j][C:] + _bdot(intra[j], v_new[j]) for j in H]
    for j in H:
        g_last = gcol[j][C - 1:C, :]
        state_s[j] = (S[j] * jnp.exp(g_last)
                      + _bdot_tn(kh[j] * jnp.exp(g_last - gcol[j]), v_new[j]))
    for j in H:
        ms = jnp.mean(o[j] * o[j], axis=-1, keepdims=True)
        on = o[j] * lax.rsqrt(ms + EPS) * gn_ref[...]
        o_ref[0, :, sls[j]] = (on * _silu(z_ref[0, :, sls[j]])).astype(o_ref.dtype)

    @pl.when(n == pl.num_programs(2) - 1)
    def _():
        s_out_ref[0] = state_s[...]


def gdn_prompt(proj3, ab_g, conv_w, alog_g, dtb_g, gdn_norm):
    B, T, _ = proj3.shape
    C = GDN_CHUNK
    N = T // C
    HW = GDN_HB * GDN_DK
    nq = GDN_QK_W // HW

    def col(base):
        return lambda b, hg, n: (b, n, base + hg)

    def cwcol(base):
        return lambda b, hg, n: (0, base + hg)

    return pl.pallas_call(
        _gdn_chunk_body,
        grid=(B, GDN_HG, N),
        in_specs=[
            pl.BlockSpec((1, C, HW), col(0)),
            pl.BlockSpec((1, C, HW), col(nq)),
            pl.BlockSpec((1, C, HW), col(2 * nq)),
            pl.BlockSpec((1, C, HW), col(OFF_Z // HW)),
            pl.BlockSpec((1, 1, C, 128), lambda b, hg, n: (b, hg, n, 0)),
            pl.BlockSpec((CONV_W, HW), cwcol(0)),
            pl.BlockSpec((CONV_W, HW), cwcol(nq)),
            pl.BlockSpec((CONV_W, HW), cwcol(2 * nq)),
            pl.BlockSpec((1, 1, 128), lambda b, hg, n: (hg, 0, 0)),
            pl.BlockSpec((1, 1, 128), lambda b, hg, n: (hg, 0, 0)),
            pl.BlockSpec((1, GDN_DV), lambda b, hg, n: (0, 0)),
        ],
        out_specs=[
            pl.BlockSpec((1, C, HW), lambda b, hg, n: (b, n, hg)),
            pl.BlockSpec((1, GDN_HB, GDN_DK, GDN_DV), lambda b, hg, n: (b, hg, 0, 0)),
        ],
        out_shape=[
            jax.ShapeDtypeStruct((B, T, GDN_V_W), BF16),
            jax.ShapeDtypeStruct((B, GDN_HEADS, GDN_DK, GDN_DV), F32),
        ],
        scratch_shapes=[
            pltpu.VMEM((C + 8, HW), F32),
            pltpu.VMEM((C + 8, HW), F32),
            pltpu.VMEM((C + 8, HW), F32),
            pltpu.VMEM((GDN_HB, GDN_DK, GDN_DV), F32),
        ],
        compiler_params=_cparams(("parallel", "parallel", "arbitrary")),
        name="gdn_prompt",
    )(proj3, proj3, proj3, proj3, ab_g, conv_w, conv_w, conv_w, alog_g, dtb_g,
      gdn_norm.reshape(1, GDN_DV))


def _gdn_step_body(x_ref, z_ref, ab_ref, cs_ref, s_ref, cw_ref, alog_ref, dtb_ref, gn_ref,
                   o_ref, cs_out_ref, s_out_ref):
    x = x_ref[0]
    y = cw_ref[CONV_W - 1] * x
    for w in range(CONV_W - 1):
        y = y + cw_ref[w] * cs_ref[0, w]
    y = _silu(y)
    cs_out_ref[0, 0] = cs_ref[0, 1]
    cs_out_ref[0, 1] = cs_ref[0, 2]
    cs_out_ref[0, 2] = x
    r_ab = (OFF_AB // 128) % 8
    ab = ab_ref[0, r_ab:r_ab + 1, :]
    g_all = -jnp.exp(alog_ref[...]) * _softplus(ab + dtb_ref[...])
    beta_all = _sigmoid(ab)
    eye = (lax.broadcasted_iota(jnp.int32, (GDN_DK, GDN_DK), 0)
           == lax.broadcasted_iota(jnp.int32, (GDN_DK, GDN_DK), 1))

    def to_col(row):
        return jnp.sum(jnp.where(eye, jnp.broadcast_to(row, (GDN_DK, GDN_DK)), 0.0),
                       axis=1, keepdims=True)

    for h in range(GDN_HEADS):
        q = y[h:h + 1, :]
        k = y[GDN_HEADS + h:GDN_HEADS + h + 1, :]
        v = y[2 * GDN_HEADS + h:2 * GDN_HEADS + h + 1, :]
        q = q * lax.rsqrt(jnp.sum(q * q, axis=-1, keepdims=True) + EPS) * (GDN_DK ** -0.5)
        k = k * lax.rsqrt(jnp.sum(k * k, axis=-1, keepdims=True) + EPS)
        g = g_all[:, h:h + 1]
        beta = beta_all[:, GDN_HEADS + h:GDN_HEADS + h + 1]
        eg = jnp.exp(g)
        S = s_ref[0, h]
        kcol = to_col(k)
        qcol = to_col(q)
        k_s = jnp.sum(S * kcol, axis=0, keepdims=True)
        q_s = jnp.sum(S * qcol, axis=0, keepdims=True)
        v_new = v * beta - (beta * eg) * k_s
        o = eg * q_s + jnp.sum(q * k, axis=-1, keepdims=True) * v_new
        s_out_ref[0, h] = S * eg + kcol * v_new
        ms = jnp.mean(o * o, axis=-1, keepdims=True)
        on = o * lax.rsqrt(ms + EPS) * gn_ref[...]
        o_ref[0, h:h + 1, :] = on * _silu(z_ref[0, h:h + 1, :])


def gdn_step(proj_s, conv_state, ssm_state, conv_w, a_log, dt_bias, gdn_norm):
    Bs = proj_s.shape[0]
    nrow = N_PROJ // 128
    p3 = proj_s.reshape(Bs, nrow, 128)
    cs4 = conv_state.reshape(Bs, CONV_W - 1, CONV_CH // 128, 128)
    cw3 = conv_w.reshape(CONV_W, CONV_CH // 128, 128)
    pad = jnp.zeros((128 - GDN_HEADS,), F32)
    alog = jnp.concatenate([a_log, pad]).reshape(1, 128)
    dtb = jnp.concatenate([dt_bias, pad]).reshape(1, 128)
    nr = CONV_CH // 128
    o, cs_new, s_new = pl.pallas_call(
        _gdn_step_body,
        grid=(Bs,),
        in_specs=[
            pl.BlockSpec((1, nr, 128), lambda b: (b, 0, 0)),
            pl.BlockSpec((1, GDN_HEADS, 128), lambda b: (b, OFF_Z // 128 // GDN_HEADS, 0)),
            pl.BlockSpec((1, 8, 128), lambda b: (b, OFF_AB // 128 // 8, 0)),
            pl.BlockSpec((1, CONV_W - 1, nr, 128), lambda b: (b, 0, 0, 0)),
            pl.BlockSpec((1, GDN_HEADS, GDN_DK, GDN_DV), lambda b: (b, 0, 0, 0)),
            pl.BlockSpec((CONV_W, nr, 128), lambda b: (0, 0, 0)),
            pl.BlockSpec((1, 128), lambda b: (0, 0)),
            pl.BlockSpec((1, 128), lambda b: (0, 0)),
            pl.BlockSpec((1, GDN_DV), lambda b: (0, 0)),
        ],
        out_specs=[
            pl.BlockSpec((1, GDN_HEADS, GDN_DV), lambda b: (b, 0, 0)),
            pl.BlockSpec((1, CONV_W - 1, nr, 128), lambda b: (b, 0, 0, 0)),
            pl.BlockSpec((1, GDN_HEADS, GDN_DK, GDN_DV), lambda b: (b, 0, 0, 0)),
        ],
        out_shape=[
            jax.ShapeDtypeStruct((Bs, GDN_HEADS, GDN_DV), F32),
            jax.ShapeDtypeStruct((Bs, CONV_W - 1, nr, 128), F32),
            jax.ShapeDtypeStruct((Bs, GDN_HEADS, GDN_DK, GDN_DV), F32),
        ],
        compiler_params=_cparams(("parallel",)),
        name="gdn_step",
    )(p3, p3, p3, cs4, ssm_state, cw3, alog, dtb, gdn_norm.reshape(1, GDN_DV))
    return (o.reshape(Bs, GDN_V_W), cs_new.reshape(Bs, CONV_W - 1, CONV_CH), s_new)


def _qknorm_body(q_ref, k_ref, v_ref, qg_ref, kg_ref, qn_ref, kv_ref):
    for h in range(GROUP_HEADS):
        sl = slice(h * HEAD_DIM, (h + 1) * HEAD_DIM)
        q = q_ref[:, sl]
        qn = q * lax.rsqrt(jnp.mean(q * q, axis=-1, keepdims=True) + EPS) * qg_ref[...]
        qn_ref[:, sl] = qn * (HEAD_DIM ** -0.5)
        k = k_ref[:, sl]
        kv_ref[:, sl] = k * lax.rsqrt(jnp.mean(k * k, axis=-1, keepdims=True) + EPS) * kg_ref[...]
    kv_ref[:, GROUP_W:] = v_ref[...]


def qk_norm(proj, q_norm, k_norm):
    M = proj.shape[0]
    tm = min(M, 512)
    blk = GROUP_W
    return pl.pallas_call(
        _qknorm_body,
        grid=(M // tm, N_GROUPS),
        in_specs=[
            pl.BlockSpec((tm, blk), lambda i, g: (i, OFF_AQ // blk + g)),
            pl.BlockSpec((tm, blk), lambda i, g: (i, OFF_AK // blk + g)),
            pl.BlockSpec((tm, blk), lambda i, g: (i, OFF_AV // blk + g)),
            pl.BlockSpec((1, HEAD_DIM), lambda i, g: (0, 0)),
            pl.BlockSpec((1, HEAD_DIM), lambda i, g: (0, 0)),
        ],
        out_specs=[
            pl.BlockSpec((tm, blk), lambda i, g: (i, g)),
            pl.BlockSpec((tm, 2 * blk), lambda i, g: (i, g)),
        ],
        out_shape=[jax.ShapeDtypeStruct((M, ATTN_W), F32),
                   jax.ShapeDtypeStruct((M, N_GROUPS * 2 * GROUP_W), F32)],
        compiler_params=_cparams(("parallel", "parallel")),
        name="qk_norm",
    )(proj, proj, proj, q_norm.reshape(1, HEAD_DIM), k_norm.reshape(1, HEAD_DIM))


def _t5_bucket(dist):
    max_exact = N_BUCKETS // 2
    d = np.maximum(dist, max_exact).astype(np.float32)
    large = max_exact + (np.log(d / max_exact) / np.log(MAX_DISTANCE / max_exact)
                         * (N_BUCKETS - max_exact)).astype(np.int32)
    large = np.minimum(large, N_BUCKETS - 1)
    return np.where(dist < max_exact, dist, large).astype(np.int32)


def _group_bias(rel_bias, gi):
    window, dilation = ATTN_GROUPS[gi]
    n_keys = window // dilation + 1
    buckets = _t5_bucket(np.arange(n_keys) * dilation)
    return rel_bias[buckets][:, gi * GROUP_HEADS:(gi + 1) * GROUP_HEADS].T.astype(F32)


def _attn_prompt_body(T, q0_ref, q1_ref, q2_ref, k0_ref, v0_ref, k1_ref, v1_ref, k2_ref, v2_ref,
                      bias_ref, y_ref, m_s, l_s, o_s):
    QB = Q_BLOCK
    q_refs = (q0_ref, q1_ref, q2_ref)
    k_refs = (k0_ref, k1_ref, k2_ref)
    v_refs = (v0_ref, v1_ref, v2_ref)
    for gi, (window, d) in enumerate(ATTN_GROUPS):
        nq = T // (d * QB)
        for r in range(d):
            for qi in range(nq):
                rows = pl.ds(r + d * QB * qi, QB, stride=d) if d > 1 else pl.ds(QB * qi, QB)
                q = q_refs[gi][0, rows, :].astype(BF16)
                kc = k_refs[gi][0, rows, :]
                vc = v_refs[gi][0, rows, :]
                if qi > 0:
                    prev = (pl.ds(r + d * QB * (qi - 1), QB, stride=d) if d > 1
                            else pl.ds(QB * (qi - 1), QB))
                    kcat = jnp.concatenate([k_refs[gi][0, prev, :], kc], axis=0).astype(BF16)
                    vcat = jnp.concatenate([v_refs[gi][0, prev, :], vc], axis=0).astype(BF16)
                    bias = bias_ref[gi, 0]
                else:
                    kcat = kc.astype(BF16)
                    vcat = vc.astype(BF16)
                    bias = bias_ref[gi, 0, :, QB:]
                s = lax.dot_general(q, kcat, (((1,), (1,)), ((), ())),
                                    preferred_element_type=F32) + bias
                m_blk = jnp.max(s, axis=-1, keepdims=True)
                if gi == 0:
                    m_new = m_blk
                else:
                    m_old = m_s[rows, :][:, 0:1]
                    m_new = jnp.maximum(m_old, m_blk)
                p = jnp.exp(s - m_new)
                l_new = jnp.sum(p, axis=-1, keepdims=True)
                o_new = jnp.dot(p.astype(BF16), vcat, preferred_element_type=F32)
                if gi > 0:
                    alpha = jnp.exp(m_old - m_new)
                    l_new = l_new + alpha * l_s[rows, :][:, 0:1]
                    o_new = o_new + alpha * o_s[rows, :]
                m_s[rows, :] = jnp.broadcast_to(m_new, (QB, HEAD_DIM))
                l_s[rows, :] = jnp.broadcast_to(l_new, (QB, HEAD_DIM))
                o_s[rows, :] = o_new
    y_ref[0] = (o_s[...] / l_s[...]).astype(y_ref.dtype)


def attn_prompt(qn3, kv3, bias_all):
    B, T, _ = qn3.shape
    blk = (1, T, HEAD_DIM)

    def qspec(gi):
        return pl.BlockSpec(blk, lambda b, h: (b, 0, gi * GROUP_HEADS + h))

    def kvspec(gi, is_v):
        return pl.BlockSpec(blk, lambda b, h: (b, 0, (2 * gi + is_v) * GROUP_HEADS + h))

    in_specs = [qspec(0), qspec(1), qspec(2)]
    for gi in range(N_GROUPS):
        in_specs += [kvspec(gi, 0), kvspec(gi, 1)]
    in_specs.append(pl.BlockSpec((N_GROUPS, 1, Q_BLOCK, 2 * Q_BLOCK), lambda b, h: (0, h, 0, 0)))
    y = pl.pallas_call(
        functools.partial(_attn_prompt_body, T),
        grid=(B, GROUP_HEADS),
        in_specs=in_specs,
        out_specs=pl.BlockSpec(blk, lambda b, h: (b, 0, h)),
        out_shape=jax.ShapeDtypeStruct((B, T, GROUP_W), BF16),
        scratch_shapes=[pltpu.VMEM((T, HEAD_DIM), F32), pltpu.VMEM((T, HEAD_DIM), F32),
                        pltpu.VMEM((T, HEAD_DIM), F32)],
        compiler_params=_cparams(("parallel", "parallel")),
        name="attn_prompt",
    )(qn3, qn3, qn3, kv3, kv3, kv3, kv3, kv3, kv3, bias_all)
    return y.reshape(B * T, GROUP_W)


def _prompt_bias_matrix(rel_bias, gi):
    QB = Q_BLOCK
    bias = _group_bias(rel_bias, gi)
    row0 = jnp.concatenate([bias[:, ::-1], jnp.full((GROUP_HEADS, QB), NEG_INF, F32)], axis=1)
    flat = jnp.tile(row0, (1, QB))[:, :QB * 2 * QB]
    return flat.reshape(GROUP_HEADS, QB, 2 * QB)


def _attn_decode_body(q_ref, n0_ref, n1_ref, n2_ref, c0_ref, c1_ref, c2_ref, bc_ref, bn_ref, y_ref):
    new_refs = (n0_ref, n1_ref, n2_ref)
    cache_refs = (c0_ref, c1_ref, c2_ref)
    outs, lds = [], []
    for gi in range(N_GROUPS):
        q = q_ref[0, gi * GROUP_HEADS:(gi + 1) * GROUP_HEADS, :]
        kc = cache_refs[gi][:, 0]
        vc = cache_refs[gi][:, 1]
        kn = new_refs[gi][0, 0:GROUP_HEADS, :]
        vn = new_refs[gi][0, GROUP_HEADS:2 * GROUP_HEADS, :]
        s_c = jnp.sum(kc * q[None], axis=-1, keepdims=True) + bc_ref[gi, :, :, 0:1]
        s_n = jnp.sum(kn * q, axis=-1, keepdims=True) + bn_ref[gi, :, 0:1]
        m = jnp.maximum(jnp.max(s_c, axis=0), s_n)
        p_c = jnp.exp(s_c - m[None])
        p_n = jnp.exp(s_n - m)
        l = jnp.sum(p_c, axis=0) + p_n
        outs.append((jnp.sum(p_c * vc, axis=0) + p_n * vn) / l)
        lds.append(m + jnp.log(l))
    m = jnp.maximum(jnp.maximum(lds[0], lds[1]), lds[2])
    es = [jnp.exp(x - m) for x in lds]
    den = es[0] + es[1] + es[2]
    y_ref[0] = outs[0] * (es[0] / den) + outs[1] * (es[1] / den) + outs[2] * (es[2] / den)


def attn_decode(qn_s, kv_new, caches, rel_bias):
    Bs = qn_s.shape[0]
    q3 = qn_s.reshape(Bs, N_GROUPS * GROUP_HEADS, HEAD_DIM)
    new3 = [x.reshape(Bs, 2 * GROUP_HEADS, HEAD_DIM) for x in kv_new]
    cviews, bcs, bns = [], [], []
    for gi, (window, d) in enumerate(ATTN_GROUPS):
        nk = window // d
        assert caches[gi].shape[1] == window and nk == Q_BLOCK
        cviews.append(caches[gi].reshape(Bs, nk, d, 2, GROUP_HEADS, HEAD_DIM))
        bias = _group_bias(rel_bias, gi)
        bc = bias[:, nk - np.arange(nk)]
        bcs.append(jnp.broadcast_to(bc.T[:, :, None], (nk, GROUP_HEADS, HEAD_DIM)))
        bns.append(jnp.broadcast_to(bias[:, 0:1], (GROUP_HEADS, HEAD_DIM)))
    bc_all = jnp.stack(bcs, 0)
    bn_all = jnp.stack(bns, 0)
    nspec = pl.BlockSpec((1, 2 * GROUP_HEADS, HEAD_DIM), lambda b: (b, 0, 0))
    cspec = pl.BlockSpec((None, Q_BLOCK, None, 2, GROUP_HEADS, HEAD_DIM),
                         lambda b: (b, 0, 0, 0, 0, 0))
    y = pl.pallas_call(
        _attn_decode_body,
        grid=(Bs,),
        in_specs=[
            pl.BlockSpec((1, N_GROUPS * GROUP_HEADS, HEAD_DIM), lambda b: (b, 0, 0)),
            nspec, nspec, nspec, cspec, cspec, cspec,
            pl.BlockSpec((N_GROUPS, Q_BLOCK, GROUP_HEADS, HEAD_DIM), lambda b: (0, 0, 0, 0)),
            pl.BlockSpec((N_GROUPS, GROUP_HEADS, HEAD_DIM), lambda b: (0, 0, 0)),
        ],
        out_specs=pl.BlockSpec((1, GROUP_HEADS, HEAD_DIM), lambda b: (b, 0, 0)),
        out_shape=jax.ShapeDtypeStruct((Bs, GROUP_HEADS, HEAD_DIM), F32),
        compiler_params=_cparams(("parallel",)),
        name="attn_decode",
    )(q3, *new3, *cviews, bc_all, bn_all)
    return y.reshape(Bs, GROUP_W)


def _merge_body(ya_ref, yb_ref, wa_ref, wb_ref, ga_ref, gb_ref, o_ref):
    pa = jnp.dot(ya_ref[...], wa_ref[...], preferred_element_type=F32)
    pb = jnp.dot(yb_ref[...], wb_ref[...], preferred_element_type=F32)
    o_ref[...] = (_sigmoid(ga_ref[...]) * pa + _sigmoid(gb_ref[...]) * pb).astype(o_ref.dtype)


def branch_merge(ya, yb, wa, wb, proj):
    M = ya.shape[0]
    tm = min(M, 1024)
    tn = 512
    return pl.pallas_call(
        _merge_body,
        grid=(M // tm, D_MODEL // tn),
        in_specs=[
            pl.BlockSpec((tm, GDN_V_W), lambda i, j: (i, 0)),
            pl.BlockSpec((tm, GROUP_W), lambda i, j: (i, 0)),
            pl.BlockSpec((GDN_V_W, tn), lambda i, j: (0, j)),
            pl.BlockSpec((GROUP_W, tn), lambda i, j: (0, j)),
            pl.BlockSpec((tm, tn), lambda i, j: (i, OFF_GA // tn + j)),
            pl.BlockSpec((tm, tn), lambda i, j: (i, OFF_GB // tn + j)),
        ],
        out_specs=pl.BlockSpec((tm, tn), lambda i, j: (i, j)),
        out_shape=jax.ShapeDtypeStruct((M, D_MODEL), BF16),
        compiler_params=_cparams(("parallel", "parallel")),
        name="branch_merge",
    )(ya, yb, wa, wb, proj, proj)


def _extract_top(x, count, tie_safe):
    R = x.shape[0]
    rid = lax.broadcasted_iota(jnp.int32, x.shape, 0).astype(F32)
    vid = lax.broadcasted_iota(jnp.int32, (count, x.shape[1]), 0)
    vals = jnp.zeros((count, x.shape[1]), F32)
    rank = jnp.full(x.shape, float(count), F32)
    for a in range(count):
        m = jnp.max(x, axis=0, keepdims=True)
        hit = x == m
        if tie_safe:
            first = jnp.min(jnp.where(hit, rid, float(R)), axis=0, keepdims=True)
            hit = rid == first
        x = jnp.where(hit, -jnp.inf, x)
        rank = jnp.where(hit, float(a), rank)
        vals = jnp.where(vid == a, m, vals)
    return vals, rank


CAND_HALF = PEER_TOPK // 2
N_CAND = PEER_TOPK + (CAND_HALF - 1) * CAND_HALF + CAND_HALF


def _peer_topk_body(q_ref, sk_ref, cnt_ref, r2_ref, e1_ref, e2_ref, cand_s, s1_s, s2_s):
    K = PEER_TOPK
    KH = CAND_HALF

    def select(h, s1, s2, tie_safe):
        v1, rank1 = _extract_top(s1, K, tie_safe)
        v2, rank2 = _extract_top(s2, K, tie_safe)
        cand_s[0:K, :] = v2 + v1[0:1, :]
        for a in range(1, KH):
            cand_s[K + (a - 1) * KH:K + a * KH, :] = v2[0:KH, :] + v1[a:a + 1, :]
        cand_s[N_CAND - KH:N_CAND, :] = v1[KH:K, :] + v2[0:1, :]
        best, crank = _extract_top(cand_s[...], K, tie_safe)
        picked = jnp.where(crank < float(K), 1.0, 0.0)
        vid = lax.broadcasted_iota(jnp.int32, (KH, picked.shape[1]), 0)
        low = jnp.where(vid == 0, jnp.sum(picked[0:K], axis=0, keepdims=True), 0.0)
        for a in range(1, KH):
            rows = picked[K + (a - 1) * KH:K + a * KH]
            low = jnp.where(vid == a, jnp.sum(rows, axis=0, keepdims=True), low)
        count = jnp.concatenate([low, picked[N_CAND - KH:N_CAND]], axis=0)
        cnt1 = jnp.zeros(rank1.shape, F32)
        for a in range(K):
            cnt1 = jnp.where(rank1 == float(a), count[a:a + 1, :], cnt1)
        cmax = best[0:1, :]
        z = jnp.sum(jnp.exp(best - cmax), axis=0, keepdims=True)
        in1 = jnp.where(rank1 < float(K), 1.0, 0.0)
        in2 = jnp.where(rank2 < float(K), 1.0, 0.0)
        cnt_ref[h] = cnt1
        r2_ref[h] = rank2
        e1_ref[h] = in1 * jnp.exp(s1 - v1[0:1, :])
        e2_ref[h] = in2 * jnp.exp(s2 - v2[0:1, :]) / z
        return (jnp.sum(in1, axis=0, keepdims=True) + jnp.sum(in2, axis=0, keepdims=True)
                + jnp.sum(picked, axis=0, keepdims=True) - 3.0 * K)

    def head(h, carry):
        off = pl.multiple_of(h * (2 * N_KEYS), 2 * N_KEYS)
        q1 = q_ref[:, pl.ds(off, N_KEYS)]
        q2 = q_ref[:, pl.ds(pl.multiple_of(off + N_KEYS, N_KEYS), N_KEYS)]
        s1_s[...] = lax.dot_general(sk_ref[0], q1, (((1,), (1,)), ((), ())), precision=HIGHEST,
                                    preferred_element_type=F32)
        s2_s[...] = lax.dot_general(sk_ref[1], q2, (((1,), (1,)), ((), ())), precision=HIGHEST,
                                    preferred_element_type=F32)
        extra = select(h, s1_s[...], s2_s[...], tie_safe=False)

        @pl.when(jnp.max(extra) > 0.0)
        def _():
            select(h, s1_s[...], s2_s[...], tie_safe=True)

        return carry

    lax.fori_loop(0, PEER_HEADS, head, 0)


def peer_topk(query, subkeys):
    M = query.shape[0]
    tm = min(M, 256)
    big = jax.ShapeDtypeStruct((PEER_HEADS, N_KEYS, M), F32)
    bspec = pl.BlockSpec((PEER_HEADS, N_KEYS, tm), lambda i: (0, 0, i))
    return pl.pallas_call(
        _peer_topk_body,
        grid=(M // tm,),
        in_specs=[pl.BlockSpec((tm, PEER_HEADS * 2 * N_KEYS), lambda i: (i, 0)),
                  pl.BlockSpec((2, N_KEYS, N_KEYS), lambda i: (0, 0, 0))],
        out_specs=[bspec, bspec, bspec, bspec],
        out_shape=[big, big, big, big],
        scratch_shapes=[pltpu.VMEM((N_CAND, tm), F32), pltpu.VMEM((N_KEYS, tm), F32),
                        pltpu.VMEM((N_KEYS, tm), F32)],
        compiler_params=_cparams(("parallel",)),
        name="peer_topk",
    )(query, subkeys)


def _gelu_exact(x):
    return 0.5 * x * (1.0 + lax.erf(x * (2.0 ** -0.5)))


GATE_ROWS = 8


def _peer_gate_body(hf_ref, u_ref, cnt_ref, r2_ref, e1_ref, e2_ref, w_ref, g_s):
    j = pl.program_id(1)
    tm = hf_ref.shape[0]
    grp = pl.multiple_of(j * GATE_ROWS, GATE_ROWS)

    for kk in range(GATE_ROWS):
        for tc in range(tm // 128):
            ts = slice(tc * 128, (tc + 1) * 128)
            acc = jnp.zeros((N_KEYS, 128), F32)
            for h in range(PEER_HEADS):
                cnt_row = cnt_ref[h, pl.ds(grp, GATE_ROWS), ts][kk:kk + 1, :]
                e1_row = e1_ref[h, pl.ds(grp, GATE_ROWS), ts][kk:kk + 1, :]
                acc = acc + jnp.where(r2_ref[h, :, ts] < cnt_row, e2_ref[h, :, ts] * e1_row, 0.0)
            g_s[ts, kk * N_KEYS:(kk + 1) * N_KEYS] = acc.T
    hmat = lax.dot_general(hf_ref[...], u_ref[...], (((1,), (1,)), ((), ())),
                           preferred_element_type=F32)
    w_ref[...] = (g_s[...] * _gelu_exact(hmat)).astype(w_ref.dtype)


def peer_gate_weights(hf, u_tab, cnt, r2, e1, e2, tm=1024):
    M = hf.shape[0]
    tm = min(M, tm)
    tn = GATE_ROWS * N_KEYS
    E = u_tab.shape[0]
    once = pl.Buffered(1)
    sspec = pl.BlockSpec((PEER_HEADS, N_KEYS, tm), lambda i, j: (0, 0, i), pipeline_mode=once)
    return pl.pallas_call(
        _peer_gate_body,
        grid=(M // tm, E // tn),
        in_specs=[
            pl.BlockSpec((tm, D_MODEL), lambda i, j: (i, 0), pipeline_mode=once),
            pl.BlockSpec((tn, D_MODEL), lambda i, j: (j, 0)),
            sspec, sspec, sspec, sspec,
        ],
        out_specs=pl.BlockSpec((tm, tn), lambda i, j: (i, j)),
        out_shape=jax.ShapeDtypeStruct((M, E), BF16),
        scratch_shapes=[pltpu.VMEM((tm, tn), F32)],
        compiler_params=_cparams(("parallel", "arbitrary")),
        name="peer_gate",
    )(hf, u_tab, cnt, r2, e1, e2)


def _mm_kres2_body(a_ref, a2_ref, b_ref, r_ref, r2_ref, o_ref, o2_ref):
    i = pl.program_id(1)
    k = pl.program_id(2)

    @pl.when(k == 0)
    def _():
        o_ref[...] = r_ref[...]

    o_ref[...] += jnp.dot(a_ref[...], b_ref[...], preferred_element_type=F32)

    @pl.when(jnp.logical_and(i == 0, k == 0))
    def _():
        o2_ref[...] = r2_ref[...]

    @pl.when(i == 0)
    def _():
        o2_ref[...] += jnp.dot(a2_ref[...], b_ref[...], preferred_element_type=F32)


def matmul_kres2(a, a2, b, residual, residual2, tm=1024, tn=1024, tk=2048, name="matmul_k2"):
    M, K = a.shape
    M2 = a2.shape[0]
    N = b.shape[1]
    tm = min(tm, M)
    return pl.pallas_call(
        _mm_kres2_body,
        grid=(N // tn, M // tm, K // tk),
        in_specs=[pl.BlockSpec((tm, tk), lambda j, i, k: (i, k)),
                  pl.BlockSpec((M2, tk), lambda j, i, k: (0, k)),
                  pl.BlockSpec((tk, tn), lambda j, i, k: (k, j)),
                  pl.BlockSpec((tm, tn), lambda j, i, k: (i, j)),
                  pl.BlockSpec((M2, tn), lambda j, i, k: (0, j))],
        out_specs=[pl.BlockSpec((tm, tn), lambda j, i, k: (i, j)),
                   pl.BlockSpec((M2, tn), lambda j, i, k: (0, j))],
        out_shape=[jax.ShapeDtypeStruct((M, N), F32), jax.ShapeDtypeStruct((M2, N), F32)],
        compiler_params=_cparams(("parallel", "arbitrary", "arbitrary")),
        name=name,
    )(a, a2, b, residual, residual2)


def peer_gate_input(x1, norm_ffn, wq_bf, subkeys, u_bf):
    hf = rmsnorm_bf16(x1, norm_ffn)
    query = matmul(hf, wq_bf, name="peer_query")
    cnt, r2, e1, e2 = peer_topk(query, subkeys)
    return peer_gate_weights(hf, u_bf, cnt, r2, e1, e2)


def peer_ffn(x1, x1_s, norm_ffn, wq_bf, subkeys, u_bf, v_bf):
    Ms = x1_s.shape[0]
    Mp = max(Ms, 128)
    if Mp != Ms:
        x1_s = jnp.concatenate([x1_s, jnp.zeros((Mp - Ms, D_MODEL), F32)], axis=0)
    w = peer_gate_input(x1, norm_ffn, wq_bf, subkeys, u_bf)
    w_s = peer_gate_input(x1_s, norm_ffn, wq_bf, subkeys, u_bf)
    y, y_s = matmul_kres2(w, w_s, v_bf, x1, x1_s, name="peer_out")
    return y, y_s[:Ms]


W_BLK = 512
N_AB = 2 * GDN_HEADS
J_AB = OFF_AB // W_BLK


def _prep_w_in_body(a_ref, b_ref, o_ref):
    j = pl.program_id(0)

    @pl.when(j < OFF_AQ // W_BLK)
    def _():
        o_ref[...] = a_ref[...].astype(o_ref.dtype)

    @pl.when(jnp.logical_and(j >= OFF_AQ // W_BLK, j < J_AB))
    def _():
        o_ref[0:W_BLK - N_AB, :] = a_ref[N_AB:, :].astype(o_ref.dtype)
        o_ref[W_BLK - N_AB:, :] = b_ref[...].astype(o_ref.dtype)

    @pl.when(j == J_AB)
    def _():
        o_ref[0:N_AB, :] = b_ref[...].astype(o_ref.dtype)
        o_ref[N_AB:, :] = jnp.zeros((W_BLK - N_AB, o_ref.shape[1]), o_ref.dtype)


def _prep_w_in(w_in_t):
    D = w_in_t.shape[1]
    j0 = OFF_AQ // W_BLK
    sub = W_BLK // N_AB
    return pl.pallas_call(
        _prep_w_in_body,
        grid=(N_PROJ // W_BLK,),
        in_specs=[
            pl.BlockSpec((W_BLK, D), lambda j: (jnp.where(j == J_AB, j0, j), 0)),
            pl.BlockSpec((N_AB, D), lambda j: (jnp.where(j == J_AB, j0 * sub,
                                                         jnp.where(j >= j0, (j + 1) * sub, 0)), 0)),
        ],
        out_specs=pl.BlockSpec((W_BLK, D), lambda j: (j, 0)),
        out_shape=jax.ShapeDtypeStruct((N_PROJ, D), BF16),
        compiler_params=_cparams(("parallel",)),
        name="prep_w_in",
    )(w_in_t, w_in_t)


def _gdn_group_params(a_log, dt_bias):
    pad = jnp.zeros((GDN_HG, 128 - GDN_HB), F32)
    alog = jnp.concatenate([a_log.reshape(GDN_HG, GDN_HB), pad], axis=1).reshape(GDN_HG, 1, 128)
    dtb = jnp.concatenate([dt_bias.reshape(GDN_HG, GDN_HB), pad], axis=1).reshape(GDN_HG, 1, 128)
    return alog, dtb


def _ab_grouped(proj, B, T):
    ab = proj[:, OFF_AB:OFF_AB + 2 * GDN_HEADS].reshape(B, T, 2, GDN_HG, GDN_HB)
    ab = ab.transpose(0, 3, 1, 2, 4).reshape(B, GDN_HG, T, 2 * GDN_HB)
    return jnp.concatenate([ab, jnp.zeros((B, GDN_HG, T, 128 - 2 * GDN_HB), F32)], axis=-1)


def kernel(x_prompt, x_sample, state_ssm, state_conv, cache_kv_g0, cache_kv_g1, cache_kv_g2, rel_bias, norm_mix, w_in, conv_w, a_log, dt_bias, gdn_norm, q_norm, k_norm, w_branch_a, w_branch_b, w_out, norm_ffn, peer_w_query, peer_subkeys, peer_u, peer_v):
    B, T, D = x_prompt.shape
    Bs = x_sample.shape[0]
    assert state_ssm.shape[0] == 1 and x_sample.shape[1] == 1
    caches = (cache_kv_g0[0], cache_kv_g1[0], cache_kv_g2[0])
    w_cat_t = _prep_w_in(jnp.swapaxes(w_in[0], 0, 1))
    wa_bf = w_branch_a[0].astype(BF16)
    wb_bf = w_branch_b[0].astype(BF16)
    wo_bf = w_out[0].astype(BF16)
    wq_bf = peer_w_query[0].astype(BF16)
    u_bf = peer_u[0].astype(BF16)
    v_bf = peer_v[0].astype(BF16)
    cw = conv_w[0]

    xp = x_prompt.reshape(B * T, D)
    xs = x_sample.reshape(Bs, D)

    proj, proj_s = matmul_nt2(rmsnorm_bf16(xp, norm_mix[0]), rmsnorm_bf16(xs, norm_mix[0]),
                              w_cat_t, name="in_proj")
    alog_g, dtb_g = _gdn_group_params(a_log[0], dt_bias[0])
    ya, ssm_p = gdn_prompt(proj.reshape(B, T, N_PROJ), _ab_grouped(proj, B, T), cw, alog_g, dtb_g,
                           gdn_norm[0])
    conv_p = proj.reshape(B, T, N_PROJ)[:, T - (CONV_W - 1):, :CONV_CH]
    qn, kv = qk_norm(proj, q_norm[0], k_norm[0])
    kv3 = kv.reshape(B, T, N_GROUPS * 2 * GROUP_W)
    kvs = [kv3[:, :, gi * 2 * GROUP_W:(gi + 1) * 2 * GROUP_W] for gi in range(N_GROUPS)]
    bias_all = jnp.stack([_prompt_bias_matrix(rel_bias, gi) for gi in range(N_GROUPS)], 0)
    yb = attn_prompt(qn.reshape(B, T, ATTN_W), kv3, bias_all)
    merged = branch_merge(ya.reshape(B * T, GDN_V_W), yb, wa_bf, wb_bf, proj)
    x1 = matmul(merged, wo_bf, residual=xp, name="out_proj")
    p_kv = []
    for gi, (window, _) in enumerate(ATTN_GROUPS):
        keep = min(window, T)
        kvg = kvs[gi][:, T - keep:].reshape(B, keep, 2, GROUP_HEADS, HEAD_DIM)
        p_kv.append(kvg[None])

    ya_s, conv_s, ssm_s = gdn_step(proj_s, state_conv[0], state_ssm[0], cw, a_log[0], dt_bias[0],
                                   gdn_norm[0])
    qn_s, kv_s = qk_norm(proj_s, q_norm[0], k_norm[0])
    kvs_s = [kv_s[:, gi * 2 * GROUP_W:(gi + 1) * 2 * GROUP_W] for gi in range(N_GROUPS)]
    yb_s = attn_decode(qn_s, kvs_s, caches, rel_bias)
    merged_s = branch_merge(ya_s.astype(BF16), yb_s.astype(BF16), wa_bf, wb_bf, proj_s)
    x1_s = matmul(merged_s, wo_bf, residual=xs, name="out_proj_s")
    y_prompt, y_sample = peer_ffn(x1, x1_s, norm_ffn[0], wq_bf, peer_subkeys[0], u_bf, v_bf)
    y_prompt = y_prompt.reshape(B, T, D)
    y_sample = y_sample.reshape(Bs, 1, D)
    s_kv =[kvs_s[gi].reshape(1, Bs, 1, 2, GROUP_HEADS, HEAD_DIM) for gi in range(N_GROUPS)]

    return (y_prompt, y_sample,
            ssm_p[None], conv_p[None], p_kv[0], p_kv[1], p_kv[2],
            ssm_s[None], conv_s[None], s_kv[0], s_kv[1], s_kv[2])
```

```python
import functools

import numpy as np
import jax
import jax.numpy as jnp
from jax import lax
from jax.experimental import pallas as pl
from jax.experimental.pallas import tpu as pltpu

F32 = jnp.float32
BF16 = jnp.bfloat16
HIGHEST = lax.Precision.HIGHEST

D_MODEL = 4096
GDN_HEADS = 16
GDN_DK = 128
GDN_DV = 128
GDN_QK_W = GDN_HEADS * GDN_DK
GDN_V_W = GDN_HEADS * GDN_DV
CONV_W = 4
CONV_CH = 2 * GDN_QK_W + GDN_V_W
GDN_CHUNK = 64
ATTN_GROUPS = ((128, 1), (512, 4), (2048, 16))
N_GROUPS = 3
GROUP_HEADS = 4
HEAD_DIM = 128
GROUP_W = GROUP_HEADS * HEAD_DIM
ATTN_W = N_GROUPS * GROUP_W
Q_BLOCK = 128
N_BUCKETS = 32
MAX_DISTANCE = 2048
PEER_HEADS = 8
N_KEYS = 128
PEER_TOPK = 16
EPS = 1e-6
NEG_INF = -1e30

OFF_QKV = 0
OFF_Z = CONV_CH
OFF_AQ = OFF_Z + GDN_V_W
OFF_AK = OFF_AQ + ATTN_W
OFF_AV = OFF_AK + ATTN_W
OFF_GA = OFF_AV + ATTN_W
OFF_GB = OFF_GA + D_MODEL
OFF_AB = OFF_GB + D_MODEL
N_PROJ = 21504

GDN_HB = 16
GDN_HG = GDN_HEADS // GDN_HB
VMEM_LIMIT = 56 * 1024 * 1024


def _cparams(sem):
    return pltpu.CompilerParams(dimension_semantics=sem, vmem_limit_bytes=VMEM_LIMIT)


def _sigmoid(x):
    return 0.5 * jnp.tanh(0.5 * x) + 0.5


def _silu(x):
    return x * _sigmoid(x)


def _softplus(x):
    return jnp.maximum(x, 0.0) + jnp.log(1.0 + jnp.exp(-jnp.abs(x)))


def _bdot(a, b):
    return jnp.dot(a.astype(BF16), b.astype(BF16), preferred_element_type=F32)


def _bdot_nt(a, b):
    return lax.dot_general(a.astype(BF16), b.astype(BF16), (((1,), (1,)), ((), ())),
                           preferred_element_type=F32)


def _bdot_tn(a, b):
    return lax.dot_general(a.astype(BF16), b.astype(BF16), (((0,), (0,)), ((), ())),
                           preferred_element_type=F32)


def _hdot(a, b):
    return jnp.dot(a, b, precision=HIGHEST, preferred_element_type=F32)


def _split_bf16(a):
    hi = a.astype(BF16)
    lo = (a - hi.astype(F32)).astype(BF16)
    return hi, lo


def _dot3(a_parts, b_parts):
    a_hi, a_lo = a_parts
    b_hi, b_lo = b_parts
    return (jnp.dot(a_hi, b_hi, preferred_element_type=F32)
            + jnp.dot(a_lo, b_hi, preferred_element_type=F32)
            + jnp.dot(a_hi, b_lo, preferred_element_type=F32))


def _rmsnorm_body(x_ref, g_ref, o_ref):
    x = x_ref[...]
    ms = jnp.mean(x * x, axis=-1, keepdims=True)
    o_ref[...] = (x * lax.rsqrt(ms + EPS) * g_ref[...]).astype(o_ref.dtype)


def rmsnorm_bf16(x, gain):
    M, D = x.shape
    tm = min(M, 256)
    return pl.pallas_call(
        _rmsnorm_body,
        grid=(M // tm,),
        in_specs=[pl.BlockSpec((tm, D), lambda i: (i, 0)),
                  pl.BlockSpec((1, D), lambda i: (0, 0))],
        out_specs=pl.BlockSpec((tm, D), lambda i: (i, 0)),
        out_shape=jax.ShapeDtypeStruct((M, D), BF16),
        compiler_params=_cparams(("parallel",)),
        name="rmsnorm",
    )(x, gain.reshape(1, D))


def _mm_body(a_ref, b_ref, o_ref):
    o_ref[...] = jnp.dot(a_ref[...], b_ref[...], preferred_element_type=F32)


def _mm_res_body(a_ref, b_ref, r_ref, o_ref):
    o_ref[...] = r_ref[...] + jnp.dot(a_ref[...], b_ref[...], preferred_element_type=F32)


def matmul(a, b, residual=None, tm=1024, tn=1024, name="matmul"):
    M, K = a.shape
    N = b.shape[1]
    tm = min(tm, M)
    tn = min(tn, N)
    in_specs = [pl.BlockSpec((tm, K), lambda i, j: (i, 0)),
                pl.BlockSpec((K, tn), lambda i, j: (0, j))]
    args = [a, b]
    body = _mm_body
    if residual is not None:
        in_specs.append(pl.BlockSpec((tm, tn), lambda i, j: (i, j)))
        args.append(residual)
        body = _mm_res_body
    return pl.pallas_call(
        body,
        grid=(M // tm, N // tn),
        in_specs=in_specs,
        out_specs=pl.BlockSpec((tm, tn), lambda i, j: (i, j)),
        out_shape=jax.ShapeDtypeStruct((M, N), F32),
        compiler_params=_cparams(("parallel", "parallel")),
        name=name,
    )(*args)


def _mm_nt2_body(a_ref, a2_ref, bt_ref, o_ref, o2_ref):
    dims = (((1,), (1,)), ((), ()))
    o_ref[...] = lax.dot_general(a_ref[...], bt_ref[...], dims, preferred_element_type=F32)

    @pl.when(pl.program_id(1) == 0)
    def _():
        o2_ref[...] = lax.dot_general(a2_ref[...], bt_ref[...], dims, preferred_element_type=F32)


def matmul_nt2(a, a2, bt, tm=1024, tn=1024, name="matmul_nt2"):
    M, K = a.shape
    M2 = a2.shape[0]
    N = bt.shape[0]
    tm = min(tm, M)
    return pl.pallas_call(
        _mm_nt2_body,
        grid=(N // tn, M // tm),
        in_specs=[pl.BlockSpec((tm, K), lambda j, i: (i, 0)),
                  pl.BlockSpec((M2, K), lambda j, i: (0, 0)),
                  pl.BlockSpec((tn, K), lambda j, i: (j, 0))],
        out_specs=[pl.BlockSpec((tm, tn), lambda j, i: (i, j)),
                   pl.BlockSpec((M2, tn), lambda j, i: (0, j))],
        out_shape=[jax.ShapeDtypeStruct((M, N), F32), jax.ShapeDtypeStruct((M2, N), F32)],
        compiler_params=_cparams(("parallel", "arbitrary")),
        name=name,
    )(a, a2, bt)


def _gdn_chunk_body(q_ref, k_ref, v_ref, z_ref, ab_ref, cwq_ref, cwk_ref, cwv_ref,
                    alog_ref, dtb_ref, gn_ref, o_ref, s_out_ref,
                    xq_s, xk_s, xv_s, state_s):
    n = pl.program_id(2)
    C = GDN_CHUNK
    HW = GDN_HB * GDN_DK

    @pl.when(n == 0)
    def _():
        xq_s[0:8, :] = jnp.zeros((8, HW), F32)
        xk_s[0:8, :] = jnp.zeros((8, HW), F32)
        xv_s[0:8, :] = jnp.zeros((8, HW), F32)
        state_s[...] = jnp.zeros(state_s.shape, F32)

    def conv(x_ref, xs, cw_ref):
        xs[8:8 + C, :] = x_ref[0]
        y = cw_ref[CONV_W - 1:CONV_W, :] * xs[8:8 + C, :]
        for w in range(CONV_W - 1):
            y = y + cw_ref[w:w + 1, :] * xs[5 + w:5 + w + C, :]
        xs[0:8, :] = xs[C:C + 8, :]
        return _silu(y)

    q_all = conv(q_ref, xq_s, cwq_ref)
    k_all = conv(k_ref, xk_s, cwk_ref)
    v_all = conv(v_ref, xv_s, cwv_ref)

    ab = ab_ref[0, 0]
    g_all = -jnp.exp(alog_ref[0]) * _softplus(ab + dtb_ref[0])
    beta_all = _sigmoid(ab)
    ri = lax.broadcasted_iota(jnp.int32, (C, C), 0)
    ci = lax.broadcasted_iota(jnp.int32, (C, C), 1)
    tri = ri >= ci
    strict = ri > ci
    gc_all = _hdot(tri.astype(F32), g_all)
    gc_t = jnp.concatenate([gc_all, jnp.zeros((128 - C, 128), F32)], axis=0).T

    H = range(GDN_HB)
    sls = [slice(j * GDN_DK, (j + 1) * GDN_DK) for j in H]
    qh, kh, kb, gcol, decay, rhs = [], [], [], [], [], []
    for j in H:
        q = q_all[:, sls[j]]
        k = k_all[:, sls[j]]
        q = q * lax.rsqrt(jnp.sum(q * q, axis=-1, keepdims=True) + EPS) * (GDN_DK ** -0.5)
        k = k * lax.rsqrt(jnp.sum(k * k, axis=-1, keepdims=True) + EPS)
        beta = beta_all[:, GDN_HB + j:GDN_HB + j + 1]
        gc = gc_all[:, j:j + 1]
        grow = gc_t[j:j + 1, :C]
        qh.append(q)
        kh.append(k)
        kb.append(k * beta)
        gcol.append(gc)
        decay.append(jnp.where(tri, jnp.exp(jnp.where(tri, gc - grow, 0.0)), 0.0))
        rhs.append(jnp.concatenate([v_all[:, sls[j]] * beta, kb[j] * jnp.exp(gc)], axis=-1))
    kq = [_bdot_nt(jnp.concatenate([kb[j], qh[j]], axis=0), kh[j]) for j in H]
    intra = [jnp.where(tri, kq[j][C:] * decay[j], 0.0) for j in H]
    acc = [-jnp.where(strict, kq[j][:C] * decay[j], 0.0) for j in H]
    parts = [_split_bf16(acc[j]) for j in H]
    pw = [_dot3(parts[j], parts[j]) for j in H]
    for r in range(5):
        pw_parts = [_split_bf16(pw[j]) for j in H]
        acc_parts = [_split_bf16(acc[j]) for j in H]
        acc = [acc[j] + pw[j] + _dot3(pw_parts[j], acc_parts[j]) for j in H]
        if r < 4:
            pw = [_dot3(pw_parts[j], pw_parts[j]) for j in H]
    sol = [rhs[j] + _dot3(_split_bf16(acc[j]), _split_bf16(rhs[j])) for j in H]
    S = [state_s[j] for j in H]
    wq_s = [_bdot(jnp.concatenate([sol[j][:, GDN_DV:], qh[j] * jnp.exp(gcol[j])], axis=0), S[j])
            for j in H]
    v_new = [sol[j][:, :GDN_DV] - wq_s[j][:C] for j in H]
    o = [wq_s[j][C:] + _bdot(intra[j], v_new[j]) for j in H]
    for j in H:
        g_last = gcol[j][C - 1:C, :]
        state_s[j] = (S[j] * jnp.exp(g_last)
                      + _bdot_tn(kh[j] * jnp.exp(g_last - gcol[j]), v_new[j]))
    for j in H:
        ms = jnp.mean(o[j] * o[j], axis=-1, keepdims=True)
        on = o[j] * lax.rsqrt(ms + EPS) * gn_ref[...]
        o_ref[0, :, sls[j]] = (on * _silu(z_ref[0, :, sls[j]])).astype(o_ref.dtype)

    @pl.when(n == pl.num_programs(2) - 1)
    def _():
        s_out_ref[0] = state_s[...]


def gdn_prompt(proj3, ab_g, conv_w, alog_g, dtb_g, gdn_norm):
    B, T, _ = proj3.shape
    C = GDN_CHUNK
    N = T // C
    HW = GDN_HB * GDN_DK
    nq = GDN_QK_W // HW

    def col(base):
        return lambda b, hg, n: (b, n, base + hg)

    def cwcol(base):
        return lambda b, hg, n: (0, base + hg)

    return pl.pallas_call(
        _gdn_chunk_body,
        grid=(B, GDN_HG, N),
        in_specs=[
            pl.BlockSpec((1, C, HW), col(0)),
            pl.BlockSpec((1, C, HW), col(nq)),
            pl.BlockSpec((1, C, HW), col(2 * nq)),
            pl.BlockSpec((1, C, HW), col(OFF_Z // HW)),
            pl.BlockSpec((1, 1, C, 128), lambda b, hg, n: (b, hg, n, 0)),
            pl.BlockSpec((CONV_W, HW), cwcol(0)),
            pl.BlockSpec((CONV_W, HW), cwcol(nq)),
            pl.BlockSpec((CONV_W, HW), cwcol(2 * nq)),
            pl.BlockSpec((1, 1, 128), lambda b, hg, n: (hg, 0, 0)),
            pl.BlockSpec((1, 1, 128), lambda b, hg, n: (hg, 0, 0)),
            pl.BlockSpec((1, GDN_DV), lambda b, hg, n: (0, 0)),
        ],
        out_specs=[
            pl.BlockSpec((1, C, HW), lambda b, hg, n: (b, n, hg)),
            pl.BlockSpec((1, GDN_HB, GDN_DK, GDN_DV), lambda b, hg, n: (b, hg, 0, 0)),
        ],
        out_shape=[
            jax.ShapeDtypeStruct((B, T, GDN_V_W), BF16),
            jax.ShapeDtypeStruct((B, GDN_HEADS, GDN_DK, GDN_DV), F32),
        ],
        scratch_shapes=[
            pltpu.VMEM((C + 8, HW), F32),
            pltpu.VMEM((C + 8, HW), F32),
            pltpu.VMEM((C + 8, HW), F32),
            pltpu.VMEM((GDN_HB, GDN_DK, GDN_DV), F32),
        ],
        compiler_params=_cparams(("parallel", "parallel", "arbitrary")),
        name="gdn_prompt",
    )(proj3, proj3, proj3, proj3, ab_g, conv_w, conv_w, conv_w, alog_g, dtb_g,
      gdn_norm.reshape(1, GDN_DV))


def _gdn_step_body(x_ref, z_ref, ab_ref, cs_ref, s_ref, cw_ref, alog_ref, dtb_ref, gn_ref,
                   o_ref, cs_out_ref, s_out_ref):
    x = x_ref[0]
    y = cw_ref[CONV_W - 1] * x
    for w in range(CONV_W - 1):
        y = y + cw_ref[w] * cs_ref[0, w]
    y = _silu(y)
    cs_out_ref[0, 0] = cs_ref[0, 1]
    cs_out_ref[0, 1] = cs_ref[0, 2]
    cs_out_ref[0, 2] = x
    r_ab = (OFF_AB // 128) % 8
    ab = ab_ref[0, r_ab:r_ab + 1, :]
    g_all = -jnp.exp(alog_ref[...]) * _softplus(ab + dtb_ref[...])
    beta_all = _sigmoid(ab)
    eye = (lax.broadcasted_iota(jnp.int32, (GDN_DK, GDN_DK), 0)
           == lax.broadcasted_iota(jnp.int32, (GDN_DK, GDN_DK), 1))

    def to_col(row):
        return jnp.sum(jnp.where(eye, jnp.broadcast_to(row, (GDN_DK, GDN_DK)), 0.0),
                       axis=1, keepdims=True)

    for h in range(GDN_HEADS):
        q = y[h:h + 1, :]
        k = y[GDN_HEADS + h:GDN_HEADS + h + 1, :]
        v = y[2 * GDN_HEADS + h:2 * GDN_HEADS + h + 1, :]
        q = q * lax.rsqrt(jnp.sum(q * q, axis=-1, keepdims=True) + EPS) * (GDN_DK ** -0.5)
        k = k * lax.rsqrt(jnp.sum(k * k, axis=-1, keepdims=True) + EPS)
        g = g_all[:, h:h + 1]
        beta = beta_all[:, GDN_HEADS + h:GDN_HEADS + h + 1]
        eg = jnp.exp(g)
        S = s_ref[0, h]
        kcol = to_col(k)
        qcol = to_col(q)
        k_s = jnp.sum(S * kcol, axis=0, keepdims=True)
        q_s = jnp.sum(S * qcol, axis=0, keepdims=True)
        v_new = v * beta - (beta * eg) * k_s
        o = eg * q_s + jnp.sum(q * k, axis=-1, keepdims=True) * v_new
        s_out_ref[0, h] = S * eg + kcol * v_new
        ms = jnp.mean(o * o, axis=-1, keepdims=True)
        on = o * lax.rsqrt(ms + EPS) * gn_ref[...]
        o_ref[0, h:h + 1, :] = on * _silu(z_ref[0, h:h + 1, :])


def gdn_step(proj_s, conv_state, ssm_state, conv_w, a_log, dt_bias, gdn_norm):
    Bs = proj_s.shape[0]
    nrow = N_PROJ // 128
    p3 = proj_s.reshape(Bs, nrow, 128)
    cs4 = conv_state.reshape(Bs, CONV_W - 1, CONV_CH // 128, 128)
    cw3 = conv_w.reshape(CONV_W, CONV_CH // 128, 128)
    pad = jnp.zeros((128 - GDN_HEADS,), F32)
    alog = jnp.concatenate([a_log, pad]).reshape(1, 128)
    dtb = jnp.concatenate([dt_bias, pad]).reshape(1, 128)
    nr = CONV_CH // 128
    o, cs_new, s_new = pl.pallas_call(
        _gdn_step_body,
        grid=(Bs,),
        in_specs=[
            pl.BlockSpec((1, nr, 128), lambda b: (b, 0, 0)),
            pl.BlockSpec((1, GDN_HEADS, 128), lambda b: (b, OFF_Z // 128 // GDN_HEADS, 0)),
            pl.BlockSpec((1, 8, 128), lambda b: (b, OFF_AB // 128 // 8, 0)),
            pl.BlockSpec((1, CONV_W - 1, nr, 128), lambda b: (b, 0, 0, 0)),
            pl.BlockSpec((1, GDN_HEADS, GDN_DK, GDN_DV), lambda b: (b, 0, 0, 0)),
            pl.BlockSpec((CONV_W, nr, 128), lambda b: (0, 0, 0)),
            pl.BlockSpec((1, 128), lambda b: (0, 0)),
            pl.BlockSpec((1, 128), lambda b: (0, 0)),
            pl.BlockSpec((1, GDN_DV), lambda b: (0, 0)),
        ],
        out_specs=[
            pl.BlockSpec((1, GDN_HEADS, GDN_DV), lambda b: (b, 0, 0)),
            pl.BlockSpec((1, CONV_W - 1, nr, 128), lambda b: (b, 0, 0, 0)),
            pl.BlockSpec((1, GDN_HEADS, GDN_DK, GDN_DV), lambda b: (b, 0, 0, 0)),
        ],
        out_shape=[
            jax.ShapeDtypeStruct((Bs, GDN_HEADS, GDN_DV), F32),
            jax.ShapeDtypeStruct((Bs, CONV_W - 1, nr, 128), F32),
            jax.ShapeDtypeStruct((Bs, GDN_HEADS, GDN_DK, GDN_DV), F32),
        ],
        compiler_params=_cparams(("parallel",)),
        name="gdn_step",
    )(p3, p3, p3, cs4, ssm_state, cw3, alog, dtb, gdn_norm.reshape(1, GDN_DV))
    return (o.reshape(Bs, GDN_V_W), cs_new.reshape(Bs, CONV_W - 1, CONV_CH), s_new)


def _qknorm_body(q_ref, k_ref, v_ref, qg_ref, kg_ref, qn_ref, kv_ref):
    for h in range(GROUP_HEADS):
        sl = slice(h * HEAD_DIM, (h + 1) * HEAD_DIM)
        q = q_ref[:, sl]
        qn = q * lax.rsqrt(jnp.mean(q * q, axis=-1, keepdims=True) + EPS) * qg_ref[...]
        qn_ref[:, sl] = (qn * (HEAD_DIM ** -0.5)).astype(qn_ref.dtype)
        k = k_ref[:, sl]
        kv_ref[:, sl] = k * lax.rsqrt(jnp.mean(k * k, axis=-1, keepdims=True) + EPS) * kg_ref[...]
    kv_ref[:, GROUP_W:] = v_ref[...]


def qk_norm(proj, q_norm, k_norm, q_dtype):
    M = proj.shape[0]
    tm = min(M, 512)
    blk = GROUP_W
    return pl.pallas_call(
        _qknorm_body,
        grid=(M // tm, N_GROUPS),
        in_specs=[
            pl.BlockSpec((tm, blk), lambda i, g: (i, OFF_AQ // blk + g)),
            pl.BlockSpec((tm, blk), lambda i, g: (i, OFF_AK // blk + g)),
            pl.BlockSpec((tm, blk), lambda i, g: (i, OFF_AV // blk + g)),
            pl.BlockSpec((1, HEAD_DIM), lambda i, g: (0, 0)),
            pl.BlockSpec((1, HEAD_DIM), lambda i, g: (0, 0)),
        ],
        out_specs=[
            pl.BlockSpec((tm, blk), lambda i, g: (i, g)),
            pl.BlockSpec((tm, 2 * blk), lambda i, g: (i, g)),
        ],
        out_shape=[jax.ShapeDtypeStruct((M, ATTN_W), q_dtype),
                   jax.ShapeDtypeStruct((M, N_GROUPS * 2 * GROUP_W), F32)],
        compiler_params=_cparams(("parallel", "parallel")),
        name="qk_norm",
    )(proj, proj, proj, q_norm.reshape(1, HEAD_DIM), k_norm.reshape(1, HEAD_DIM))


def _t5_bucket(dist):
    max_exact = N_BUCKETS // 2
    d = np.maximum(dist, max_exact).astype(np.float32)
    large = max_exact + (np.log(d / max_exact) / np.log(MAX_DISTANCE / max_exact)
                         * (N_BUCKETS - max_exact)).astype(np.int32)
    large = np.minimum(large, N_BUCKETS - 1)
    return np.where(dist < max_exact, dist, large).astype(np.int32)


def _group_bias(rel_bias, gi):
    window, dilation = ATTN_GROUPS[gi]
    n_keys = window // dilation + 1
    buckets = _t5_bucket(np.arange(n_keys) * dilation)
    return rel_bias[buckets][:, gi * GROUP_HEADS:(gi + 1) * GROUP_HEADS].T.astype(F32)


def _attn_prompt_body(T, q0_ref, q1_ref, q2_ref, k0_ref, v0_ref, k1_ref, v1_ref, k2_ref, v2_ref,
                      bias_ref, y_ref, m_s, l_s, o_s):
    QB = Q_BLOCK
    q_refs = (q0_ref, q1_ref, q2_ref)
    k_refs = (k0_ref, k1_ref, k2_ref)
    v_refs = (v0_ref, v1_ref, v2_ref)
    for gi, (window, d) in enumerate(ATTN_GROUPS):
        nq = T // (d * QB)
        for r in range(d):
            for qi in range(nq):
                rows = pl.ds(r + d * QB * qi, QB, stride=d) if d > 1 else pl.ds(QB * qi, QB)
                q = q_refs[gi][0, rows, :].astype(BF16)
                kc = k_refs[gi][0, rows, :]
                vc = v_refs[gi][0, rows, :]
                if qi > 0:
                    prev = (pl.ds(r + d * QB * (qi - 1), QB, stride=d) if d > 1
                            else pl.ds(QB * (qi - 1), QB))
                    kcat = jnp.concatenate([k_refs[gi][0, prev, :], kc], axis=0).astype(BF16)
                    vcat = jnp.concatenate([v_refs[gi][0, prev, :], vc], axis=0).astype(BF16)
                    bias = bias_ref[gi, 0]
                else:
                    kcat = kc.astype(BF16)
                    vcat = vc.astype(BF16)
                    bias = bias_ref[gi, 0, :, QB:]
                s = lax.dot_general(q, kcat, (((1,), (1,)), ((), ())),
                                    preferred_element_type=F32) + bias
                m_blk = jnp.max(s, axis=-1, keepdims=True)
                if gi == 0:
                    m_new = m_blk
                else:
                    m_old = m_s[rows, :][:, 0:1]
                    m_new = jnp.maximum(m_old, m_blk)
                p = jnp.exp(s - m_new)
                l_new = jnp.sum(p, axis=-1, keepdims=True)
                o_new = jnp.dot(p.astype(BF16), vcat, preferred_element_type=F32)
                if gi > 0:
                    alpha = jnp.exp(m_old - m_new)
                    l_new = l_new + alpha * l_s[rows, :][:, 0:1]
                    o_new = o_new + alpha * o_s[rows, :]
                m_s[rows, :] = jnp.broadcast_to(m_new, (QB, HEAD_DIM))
                l_s[rows, :] = jnp.broadcast_to(l_new, (QB, HEAD_DIM))
                o_s[rows, :] = o_new
    y_ref[0] = (o_s[...] / l_s[...]).astype(y_ref.dtype)


def attn_prompt(qn3, kv3, bias_all):
    B, T, _ = qn3.shape
    blk = (1, T, HEAD_DIM)

    def qspec(gi):
        return pl.BlockSpec(blk, lambda b, h: (b, 0, gi * GROUP_HEADS + h))

    def kvspec(gi, is_v):
        return pl.BlockSpec(blk, lambda b, h: (b, 0, (2 * gi + is_v) * GROUP_HEADS + h))

    in_specs = [qspec(0), qspec(1), qspec(2)]
    for gi in range(N_GROUPS):
        in_specs += [kvspec(gi, 0), kvspec(gi, 1)]
    in_specs.append(pl.BlockSpec((N_GROUPS, 1, Q_BLOCK, 2 * Q_BLOCK), lambda b, h: (0, h, 0, 0)))
    y = pl.pallas_call(
        functools.partial(_attn_prompt_body, T),
        grid=(B, GROUP_HEADS),
        in_specs=in_specs,
        out_specs=pl.BlockSpec(blk, lambda b, h: (b, 0, h)),
        out_shape=jax.ShapeDtypeStruct((B, T, GROUP_W), BF16),
        scratch_shapes=[pltpu.VMEM((T, HEAD_DIM), F32), pltpu.VMEM((T, HEAD_DIM), F32),
                        pltpu.VMEM((T, HEAD_DIM), F32)],
        compiler_params=_cparams(("parallel", "parallel")),
        name="attn_prompt",
    )(qn3, qn3, qn3, kv3, kv3, kv3, kv3, kv3, kv3, bias_all)
    return y.reshape(B * T, GROUP_W)


def _prompt_bias_matrix(rel_bias, gi):
    QB = Q_BLOCK
    bias = _group_bias(rel_bias, gi)
    row0 = jnp.concatenate([bias[:, ::-1], jnp.full((GROUP_HEADS, QB), NEG_INF, F32)], axis=1)
    flat = jnp.tile(row0, (1, QB))[:, :QB * 2 * QB]
    return flat.reshape(GROUP_HEADS, QB, 2 * QB)


def _attn_decode_body(q_ref, n0_ref, n1_ref, n2_ref, c0_ref, c1_ref, c2_ref, bc_ref, bn_ref, y_ref):
    new_refs = (n0_ref, n1_ref, n2_ref)
    cache_refs = (c0_ref, c1_ref, c2_ref)
    outs, lds = [], []
    for gi in range(N_GROUPS):
        q = q_ref[0, gi * GROUP_HEADS:(gi + 1) * GROUP_HEADS, :]
        kc = cache_refs[gi][:, 0]
        vc = cache_refs[gi][:, 1]
        kn = new_refs[gi][0, 0:GROUP_HEADS, :]
        vn = new_refs[gi][0, GROUP_HEADS:2 * GROUP_HEADS, :]
        s_c = jnp.sum(kc * q[None], axis=-1, keepdims=True) + bc_ref[gi, :, :, 0:1]
        s_n = jnp.sum(kn * q, axis=-1, keepdims=True) + bn_ref[gi, :, 0:1]
        m = jnp.maximum(jnp.max(s_c, axis=0), s_n)
        p_c = jnp.exp(s_c - m[None])
        p_n = jnp.exp(s_n - m)
        l = jnp.sum(p_c, axis=0) + p_n
        outs.append((jnp.sum(p_c * vc, axis=0) + p_n * vn) / l)
        lds.append(m + jnp.log(l))
    m = jnp.maximum(jnp.maximum(lds[0], lds[1]), lds[2])
    es = [jnp.exp(x - m) for x in lds]
    den = es[0] + es[1] + es[2]
    y_ref[0] = outs[0] * (es[0] / den) + outs[1] * (es[1] / den) + outs[2] * (es[2] / den)


def attn_decode(qn_s, kv_new, caches, rel_bias):
    Bs = qn_s.shape[0]
    q3 = qn_s.reshape(Bs, N_GROUPS * GROUP_HEADS, HEAD_DIM)
    new3 = [x.reshape(Bs, 2 * GROUP_HEADS, HEAD_DIM) for x in kv_new]
    cviews, bcs, bns = [], [], []
    for gi, (window, d) in enumerate(ATTN_GROUPS):
        nk = window // d
        assert caches[gi].shape[1] == window and nk == Q_BLOCK
        cviews.append(caches[gi].reshape(Bs, nk, d, 2, GROUP_HEADS, HEAD_DIM))
        bias = _group_bias(rel_bias, gi)
        bc = bias[:, nk - np.arange(nk)]
        bcs.append(jnp.broadcast_to(bc.T[:, :, None], (nk, GROUP_HEADS, HEAD_DIM)))
        bns.append(jnp.broadcast_to(bias[:, 0:1], (GROUP_HEADS, HEAD_DIM)))
    bc_all = jnp.stack(bcs, 0)
    bn_all = jnp.stack(bns, 0)
    nspec = pl.BlockSpec((1, 2 * GROUP_HEADS, HEAD_DIM), lambda b: (b, 0, 0))
    cspec = pl.BlockSpec((None, Q_BLOCK, None, 2, GROUP_HEADS, HEAD_DIM),
                         lambda b: (b, 0, 0, 0, 0, 0))
    y = pl.pallas_call(
        _attn_decode_body,
        grid=(Bs,),
        in_specs=[
            pl.BlockSpec((1, N_GROUPS * GROUP_HEADS, HEAD_DIM), lambda b: (b, 0, 0)),
            nspec, nspec, nspec, cspec, cspec, cspec,
            pl.BlockSpec((N_GROUPS, Q_BLOCK, GROUP_HEADS, HEAD_DIM), lambda b: (0, 0, 0, 0)),
            pl.BlockSpec((N_GROUPS, GROUP_HEADS, HEAD_DIM), lambda b: (0, 0, 0)),
        ],
        out_specs=pl.BlockSpec((1, GROUP_HEADS, HEAD_DIM), lambda b: (b, 0, 0)),
        out_shape=jax.ShapeDtypeStruct((Bs, GROUP_HEADS, HEAD_DIM), F32),
        compiler_params=_cparams(("parallel",)),
        name="attn_decode",
    )(q3, *new3, *cviews, bc_all, bn_all)
    return y.reshape(Bs, GROUP_W)


def _merge_body(ya_ref, yb_ref, wa_ref, wb_ref, ga_ref, gb_ref, o_ref):
    pa = jnp.dot(ya_ref[...], wa_ref[...], preferred_element_type=F32)
    pb = jnp.dot(yb_ref[...], wb_ref[...], preferred_element_type=F32)
    o_ref[...] = (_sigmoid(ga_ref[...]) * pa + _sigmoid(gb_ref[...]) * pb).astype(o_ref.dtype)


def branch_merge(ya, yb, wa, wb, proj):
    M = ya.shape[0]
    tm = min(M, 1024)
    tn = 512
    return pl.pallas_call(
        _merge_body,
        grid=(M // tm, D_MODEL // tn),
        in_specs=[
            pl.BlockSpec((tm, GDN_V_W), lambda i, j: (i, 0)),
            pl.BlockSpec((tm, GROUP_W), lambda i, j: (i, 0)),
            pl.BlockSpec((GDN_V_W, tn), lambda i, j: (0, j)),
            pl.BlockSpec((GROUP_W, tn), lambda i, j: (0, j)),
            pl.BlockSpec((tm, tn), lambda i, j: (i, OFF_GA // tn + j)),
            pl.BlockSpec((tm, tn), lambda i, j: (i, OFF_GB // tn + j)),
        ],
        out_specs=pl.BlockSpec((tm, tn), lambda i, j: (i, j)),
        out_shape=jax.ShapeDtypeStruct((M, D_MODEL), BF16),
        compiler_params=_cparams(("parallel", "parallel")),
        name="branch_merge",
    )(ya, yb, wa, wb, proj, proj)


def _extract_top(x, count, tie_safe):
    R = x.shape[0]
    rid = lax.broadcasted_iota(jnp.int32, x.shape, 0).astype(F32)
    vid = lax.broadcasted_iota(jnp.int32, (count, x.shape[1]), 0)
    vals = jnp.zeros((count, x.shape[1]), F32)
    rank = jnp.full(x.shape, float(count), F32)
    for a in range(count):
        m = jnp.max(x, axis=0, keepdims=True)
        hit = x == m
        if tie_safe:
            first = jnp.min(jnp.where(hit, rid, float(R)), axis=0, keepdims=True)
            hit = rid == first
        x = jnp.where(hit, -jnp.inf, x)
        rank = jnp.where(hit, float(a), rank)
        vals = jnp.where(vid == a, m, vals)
    return vals, rank


CAND_HALF = PEER_TOPK // 2
N_CAND = PEER_TOPK + (CAND_HALF - 1) * CAND_HALF + CAND_HALF


def _peer_topk_body(q_ref, sk_ref, cnt_ref, r2_ref, e1_ref, e2_ref, cand_s, s1_s, s2_s):
    K = PEER_TOPK
    KH = CAND_HALF

    def select(h, s1, s2, tie_safe):
        v1, rank1 = _extract_top(s1, K, tie_safe)
        v2, rank2 = _extract_top(s2, K, tie_safe)
        cand_s[0:K, :] = v2 + v1[0:1, :]
        for a in range(1, KH):
            cand_s[K + (a - 1) * KH:K + a * KH, :] = v2[0:KH, :] + v1[a:a + 1, :]
        cand_s[N_CAND - KH:N_CAND, :] = v1[KH:K, :] + v2[0:1, :]
        best, crank = _extract_top(cand_s[...], K, tie_safe)
        picked = jnp.where(crank < float(K), 1.0, 0.0)
        vid = lax.broadcasted_iota(jnp.int32, (KH, picked.shape[1]), 0)
        low = jnp.where(vid == 0, jnp.sum(picked[0:K], axis=0, keepdims=True), 0.0)
        for a in range(1, KH):
            rows = picked[K + (a - 1) * KH:K + a * KH]
            low = jnp.where(vid == a, jnp.sum(rows, axis=0, keepdims=True), low)
        count = jnp.concatenate([low, picked[N_CAND - KH:N_CAND]], axis=0)
        cnt1 = jnp.zeros(rank1.shape, F32)
        for a in range(K):
            cnt1 = jnp.where(rank1 == float(a), count[a:a + 1, :], cnt1)
        cmax = best[0:1, :]
        z = jnp.sum(jnp.exp(best - cmax), axis=0, keepdims=True)
        in1 = jnp.where(rank1 < float(K), 1.0, 0.0)
        in2 = jnp.where(rank2 < float(K), 1.0, 0.0)
        cnt_ref[h] = cnt1
        r2_ref[h] = rank2
        e1_ref[h] = in1 * jnp.exp(s1 - v1[0:1, :])
        e2_ref[h] = in2 * jnp.exp(s2 - v2[0:1, :]) / z * 0.5
        return (jnp.sum(in1, axis=0, keepdims=True) + jnp.sum(in2, axis=0, keepdims=True)
                + jnp.sum(picked, axis=0, keepdims=True) - 3.0 * K)

    def head(h, carry):
        off = pl.multiple_of(h * (2 * N_KEYS), 2 * N_KEYS)
        q1 = q_ref[:, pl.ds(off, N_KEYS)]
        q2 = q_ref[:, pl.ds(pl.multiple_of(off + N_KEYS, N_KEYS), N_KEYS)]
        s1_s[...] = lax.dot_general(sk_ref[0], q1, (((1,), (1,)), ((), ())), precision=HIGHEST,
                                    preferred_element_type=F32)
        s2_s[...] = lax.dot_general(sk_ref[1], q2, (((1,), (1,)), ((), ())), precision=HIGHEST,
                                    preferred_element_type=F32)
        extra = select(h, s1_s[...], s2_s[...], tie_safe=False)

        @pl.when(jnp.max(extra) > 0.0)
        def _():
            select(h, s1_s[...], s2_s[...], tie_safe=True)

        return carry

    lax.fori_loop(0, PEER_HEADS, head, 0)


def peer_topk(query, subkeys):
    M = query.shape[0]
    tm = min(M, 512)
    big = jax.ShapeDtypeStruct((PEER_HEADS, N_KEYS, M), F32)
    bspec = pl.BlockSpec((PEER_HEADS, N_KEYS, tm), lambda i: (0, 0, i))
    return pl.pallas_call(
        _peer_topk_body,
        grid=(M // tm,),
        in_specs=[pl.BlockSpec((tm, PEER_HEADS * 2 * N_KEYS), lambda i: (i, 0)),
                  pl.BlockSpec((2, N_KEYS, N_KEYS), lambda i: (0, 0, 0))],
        out_specs=[bspec, bspec, bspec, bspec],
        out_shape=[big, big, big, big],
        scratch_shapes=[pltpu.VMEM((N_CAND, tm), F32), pltpu.VMEM((N_KEYS, tm), F32),
                        pltpu.VMEM((N_KEYS, tm), F32)],
        compiler_params=_cparams(("parallel",)),
        name="peer_topk",
    )(query, subkeys)


def _gelu_x2(x):
    return x * (1.0 + lax.erf(x * (2.0 ** -0.5)))


GATE_ROWS = 8


def _peer_gate_body(hf_ref, u_ref, cnt_ref, r2_ref, e1_ref, e2_ref, w_ref, g_s):
    j = pl.program_id(1)
    tm = hf_ref.shape[0]
    grp = pl.multiple_of(j * GATE_ROWS, GATE_ROWS)

    for kk in range(GATE_ROWS):
        for tc in range(tm // 128):
            ts = slice(tc * 128, (tc + 1) * 128)
            acc = jnp.zeros((N_KEYS, 128), F32)
            for h in range(PEER_HEADS):
                cnt_row = cnt_ref[h, pl.ds(grp, GATE_ROWS), ts][kk:kk + 1, :]
                e1_row = e1_ref[h, pl.ds(grp, GATE_ROWS), ts][kk:kk + 1, :]
                acc = acc + jnp.where(r2_ref[h, :, ts] < cnt_row, e2_ref[h, :, ts] * e1_row, 0.0)
            g_s[ts, kk * N_KEYS:(kk + 1) * N_KEYS] = acc.T
    hmat = lax.dot_general(hf_ref[...], u_ref[...], (((1,), (1,)), ((), ())),
                           preferred_element_type=F32)
    w_ref[...] = (g_s[...] * _gelu_x2(hmat)).astype(w_ref.dtype)


def peer_gate_weights(hf, u_tab, cnt, r2, e1, e2, tm=1024):
    M = hf.shape[0]
    tm = min(M, tm)
    tn = GATE_ROWS * N_KEYS
    E = u_tab.shape[0]
    once = pl.Buffered(1)
    sspec = pl.BlockSpec((PEER_HEADS, N_KEYS, tm), lambda i, j: (0, 0, i), pipeline_mode=once)
    return pl.pallas_call(
        _peer_gate_body,
        grid=(M // tm, E // tn),
        in_specs=[
            pl.BlockSpec((tm, D_MODEL), lambda i, j: (i, 0), pipeline_mode=once),
            pl.BlockSpec((tn, D_MODEL), lambda i, j: (j, 0)),
            sspec, sspec, sspec, sspec,
        ],
        out_specs=pl.BlockSpec((tm, tn), lambda i, j: (i, j)),
        out_shape=jax.ShapeDtypeStruct((M, E), BF16),
        scratch_shapes=[pltpu.VMEM((tm, tn), F32)],
        compiler_params=_cparams(("parallel", "arbitrary")),
        name="peer_gate",
    )(hf, u_tab, cnt, r2, e1, e2)


def _mm_kres_body(a_ref, b_ref, r_ref, o_ref):
    k = pl.program_id(2)

    @pl.when(k == 0)
    def _():
        o_ref[...] = r_ref[...]

    o_ref[...] += jnp.dot(a_ref[...], b_ref[...], preferred_element_type=F32)


def matmul_kres(a, b, residual, tm=1024, tn=1024, tk=4096, name="matmul_k"):
    M, K = a.shape
    N = b.shape[1]
    tm = min(tm, M)
    return pl.pallas_call(
        _mm_kres_body,
        grid=(M // tm, N // tn, K // tk),
        in_specs=[pl.BlockSpec((tm, tk), lambda i, j, k: (i, k)),
                  pl.BlockSpec((tk, tn), lambda i, j, k: (k, j)),
                  pl.BlockSpec((tm, tn), lambda i, j, k: (i, j))],
        out_specs=pl.BlockSpec((tm, tn), lambda i, j, k: (i, j)),
        out_shape=jax.ShapeDtypeStruct((M, N), F32),
        compiler_params=_cparams(("parallel", "parallel", "arbitrary")),
        name=name,
    )(a, b, residual)


def peer_ffn(x1, norm_ffn, wq_bf, subkeys, u_bf, v_bf):
    M = x1.shape[0]
    Mp = max(M, 128)
    if Mp != M:
        x1 = jnp.concatenate([x1, jnp.zeros((Mp - M, D_MODEL), F32)], axis=0)
    hf = rmsnorm_bf16(x1, norm_ffn)
    query = matmul(hf, wq_bf, name="peer_query")
    cnt, r2, e1, e2 = peer_topk(query, subkeys)
    w = peer_gate_weights(hf, u_bf, cnt, r2, e1, e2)
    return matmul_kres(w, v_bf, x1, name="peer_out")[:M]


W_BLK = 512
N_AB = 2 * GDN_HEADS
J_AB = OFF_AB // W_BLK


def _prep_w_in_body(a_ref, b_ref, o_ref):
    j = pl.program_id(0)

    @pl.when(j < OFF_AQ // W_BLK)
    def _():
        o_ref[...] = a_ref[...].astype(o_ref.dtype)

    @pl.when(jnp.logical_and(j >= OFF_AQ // W_BLK, j < J_AB))
    def _():
        o_ref[0:W_BLK - N_AB, :] = a_ref[N_AB:, :].astype(o_ref.dtype)
        o_ref[W_BLK - N_AB:, :] = b_ref[...].astype(o_ref.dtype)

    @pl.when(j == J_AB)
    def _():
        o_ref[0:N_AB, :] = b_ref[...].astype(o_ref.dtype)
        o_ref[N_AB:, :] = jnp.zeros((W_BLK - N_AB, o_ref.shape[1]), o_ref.dtype)


def _prep_w_in(w_in_t):
    D = w_in_t.shape[1]
    j0 = OFF_AQ // W_BLK
    sub = W_BLK // N_AB
    return pl.pallas_call(
        _prep_w_in_body,
        grid=(N_PROJ // W_BLK,),
        in_specs=[
            pl.BlockSpec((W_BLK, D), lambda j: (jnp.where(j == J_AB, j0, j), 0)),
            pl.BlockSpec((N_AB, D), lambda j: (jnp.where(j == J_AB, j0 * sub,
                                                         jnp.where(j >= j0, (j + 1) * sub, 0)), 0)),
        ],
        out_specs=pl.BlockSpec((W_BLK, D), lambda j: (j, 0)),
        out_shape=jax.ShapeDtypeStruct((N_PROJ, D), BF16),
        compiler_params=_cparams(("parallel",)),
        name="prep_w_in",
    )(w_in_t, w_in_t)


def _gdn_group_params(a_log, dt_bias):
    pad = jnp.zeros((GDN_HG, 128 - GDN_HB), F32)
    alog = jnp.concatenate([a_log.reshape(GDN_HG, GDN_HB), pad], axis=1).reshape(GDN_HG, 1, 128)
    dtb = jnp.concatenate([dt_bias.reshape(GDN_HG, GDN_HB), pad], axis=1).reshape(GDN_HG, 1, 128)
    return alog, dtb


def _ab_grouped(proj, B, T):
    ab = proj[:, OFF_AB:OFF_AB + 2 * GDN_HEADS].reshape(B, T, 2, GDN_HG, GDN_HB)
    ab = ab.transpose(0, 3, 1, 2, 4).reshape(B, GDN_HG, T, 2 * GDN_HB)
    return jnp.concatenate([ab, jnp.zeros((B, GDN_HG, T, 128 - 2 * GDN_HB), F32)], axis=-1)


def kernel(x_prompt, x_sample, state_ssm, state_conv, cache_kv_g0, cache_kv_g1, cache_kv_g2, rel_bias, norm_mix, w_in, conv_w, a_log, dt_bias, gdn_norm, q_norm, k_norm, w_branch_a, w_branch_b, w_out, norm_ffn, peer_w_query, peer_subkeys, peer_u, peer_v):
    B, T, D = x_prompt.shape
    Bs = x_sample.shape[0]
    assert state_ssm.shape[0] == 1 and x_sample.shape[1] == 1
    caches = (cache_kv_g0[0], cache_kv_g1[0], cache_kv_g2[0])
    w_cat_t = _prep_w_in(jnp.swapaxes(w_in[0], 0, 1))
    wa_bf = w_branch_a[0].astype(BF16)
    wb_bf = w_branch_b[0].astype(BF16)
    wo_bf = w_out[0].astype(BF16)
    wq_bf = peer_w_query[0].astype(BF16)
    u_bf = peer_u[0].astype(BF16)
    v_bf = peer_v[0].astype(BF16)
    cw = conv_w[0]

    xp = x_prompt.reshape(B * T, D)
    xs = x_sample.reshape(Bs, D)

    proj, proj_s = matmul_nt2(rmsnorm_bf16(xp, norm_mix[0]), rmsnorm_bf16(xs, norm_mix[0]),
                              w_cat_t, name="in_proj")
    alog_g, dtb_g = _gdn_group_params(a_log[0], dt_bias[0])
    ya, ssm_p = gdn_prompt(proj.reshape(B, T, N_PROJ), _ab_grouped(proj, B, T), cw, alog_g, dtb_g,
                           gdn_norm[0])
    conv_p = proj.reshape(B, T, N_PROJ)[:, T - (CONV_W - 1):, :CONV_CH]
    qn, kv = qk_norm(proj, q_norm[0], k_norm[0], F32)
    kv3 = kv.reshape(B, T, N_GROUPS * 2 * GROUP_W)
    kvs = [kv3[:, :, gi * 2 * GROUP_W:(gi + 1) * 2 * GROUP_W] for gi in range(N_GROUPS)]
    bias_all = jnp.stack([_prompt_bias_matrix(rel_bias, gi) for gi in range(N_GROUPS)], 0)
    yb = attn_prompt(qn.reshape(B, T, ATTN_W), kv3, bias_all)
    merged = branch_merge(ya.reshape(B * T, GDN_V_W), yb, wa_bf, wb_bf, proj)
    x1 = matmul(merged, wo_bf, residual=xp, name="out_proj")
    p_kv = []
    for gi, (window, _) in enumerate(ATTN_GROUPS):
        keep = min(window, T)
        kvg = kvs[gi][:, T - keep:].reshape(B, keep, 2, GROUP_HEADS, HEAD_DIM)
        p_kv.append(kvg[None])

    ya_s, conv_s, ssm_s = gdn_step(proj_s, state_conv[0], state_ssm[0], cw, a_log[0], dt_bias[0],
                                   gdn_norm[0])
    qn_s, kv_s = qk_norm(proj_s, q_norm[0], k_norm[0], F32)
    kvs_s = [kv_s[:, gi * 2 * GROUP_W:(gi + 1) * 2 * GROUP_W] for gi in range(N_GROUPS)]
    yb_s = attn_decode(qn_s, kvs_s, caches, rel_bias)
    merged_s = branch_merge(ya_s.astype(BF16), yb_s.astype(BF16), wa_bf, wb_bf, proj_s)
    x1_s = matmul(merged_s, wo_bf, residual=xs, name="out_proj_s")
    y_prompt = peer_ffn(x1, norm_ffn[0], wq_bf, peer_subkeys[0], u_bf, v_bf).reshape(B, T, D)
    y_sample = peer_ffn(x1_s, norm_ffn[0], wq_bf, peer_subkeys[0], u_bf, v_bf).reshape(Bs, 1, D)
    s_kv =[kvs_s[gi].reshape(1, Bs, 1, 2, GROUP_HEADS, HEAD_DIM) for gi in range(N_GROUPS)]

    return (y_prompt, y_sample,
            ssm_p[None], conv_p[None], p_kv[0], p_kv[1], p_kv[2],
            ssm_s[None], conv_s[None], s_kv[0], s_kv[1], s_kv[2])
```

```python
import functools

import numpy as np
import jax
import jax.numpy as jnp
from jax import lax
from jax.experimental import pallas as pl
from jax.experimental.pallas import tpu as pltpu

F32 = jnp.float32
BF16 = jnp.bfloat16
HIGHEST = lax.Precision.HIGHEST

D_MODEL = 4096
GDN_HEADS = 16
GDN_DK = 128
GDN_DV = 128
GDN_QK_W = GDN_HEADS * GDN_DK
GDN_V_W = GDN_HEADS * GDN_DV
CONV_W = 4
CONV_CH = 2 * GDN_QK_W + GDN_V_W
GDN_CHUNK = 64
ATTN_GROUPS = ((128, 1), (512, 4), (2048, 16))
N_GROUPS = 3
GROUP_HEADS = 4
HEAD_DIM = 128
GROUP_W = GROUP_HEADS * HEAD_DIM
ATTN_W = N_GROUPS * GROUP_W
Q_BLOCK = 128
N_BUCKETS = 32
MAX_DISTANCE = 2048
PEER_HEADS = 8
N_KEYS = 128
PEER_TOPK = 16
EPS = 1e-6
NEG_INF = -1e30

OFF_QKV = 0
OFF_Z = CONV_CH
OFF_AQ = OFF_Z + GDN_V_W
OFF_AK = OFF_AQ + ATTN_W
OFF_AV = OFF_AK + ATTN_W
OFF_GA = OFF_AV + ATTN_W
OFF_GB = OFF_GA + D_MODEL
OFF_AB = OFF_GB + D_MODEL
N_PROJ = 21504

GDN_HB = 16
GDN_HG = GDN_HEADS // GDN_HB
VMEM_LIMIT = 56 * 1024 * 1024


def _cparams(sem):
    return pltpu.CompilerParams(dimension_semantics=sem, vmem_limit_bytes=VMEM_LIMIT)


def _sigmoid(x):
    return 0.5 * jnp.tanh(0.5 * x) + 0.5


def _silu(x):
    return x * _sigmoid(x)


def _softplus(x):
    return jnp.maximum(x, 0.0) + jnp.log(1.0 + jnp.exp(-jnp.abs(x)))


def _bdot(a, b):
    return jnp.dot(a.astype(BF16), b.astype(BF16), preferred_element_type=F32)


def _bdot_nt(a, b):
    return lax.dot_general(a.astype(BF16), b.astype(BF16), (((1,), (1,)), ((), ())),
                           preferred_element_type=F32)


def _bdot_tn(a, b):
    return lax.dot_general(a.astype(BF16), b.astype(BF16), (((0,), (0,)), ((), ())),
                           preferred_element_type=F32)


def _hdot(a, b):
    return jnp.dot(a, b, precision=HIGHEST, preferred_element_type=F32)


def _split_bf16(a):
    hi = a.astype(BF16)
    lo = (a - hi.astype(F32)).astype(BF16)
    return hi, lo


def _dot3(a_parts, b_parts):
    a_hi, a_lo = a_parts
    b_hi, b_lo = b_parts
    return (jnp.dot(a_hi, b_hi, preferred_element_type=F32)
            + jnp.dot(a_lo, b_hi, preferred_element_type=F32)
            + jnp.dot(a_hi, b_lo, preferred_element_type=F32))


def _rmsnorm_body(x_ref, g_ref, o_ref):
    x = x_ref[...]
    ms = jnp.mean(x * x, axis=-1, keepdims=True)
    o_ref[...] = (x * lax.rsqrt(ms + EPS) * g_ref[...]).astype(o_ref.dtype)


def rmsnorm_bf16(x, gain):
    M, D = x.shape
    tm = min(M, 256)
    return pl.pallas_call(
        _rmsnorm_body,
        grid=(M // tm,),
        in_specs=[pl.BlockSpec((tm, D), lambda i: (i, 0)),
                  pl.BlockSpec((1, D), lambda i: (0, 0))],
        out_specs=pl.BlockSpec((tm, D), lambda i: (i, 0)),
        out_shape=jax.ShapeDtypeStruct((M, D), BF16),
        compiler_params=_cparams(("parallel",)),
        name="rmsnorm",
    )(x, gain.reshape(1, D))


def _mm_body(a_ref, b_ref, o_ref):
    o_ref[...] = jnp.dot(a_ref[...], b_ref[...], preferred_element_type=F32)


def _mm_res_body(a_ref, b_ref, r_ref, o_ref):
    o_ref[...] = r_ref[...] + jnp.dot(a_ref[...], b_ref[...], preferred_element_type=F32)


def matmul(a, b, residual=None, tm=1024, tn=1024, name="matmul"):
    M, K = a.shape
    N = b.shape[1]
    tm = min(tm, M)
    tn = min(tn, N)
    in_specs = [pl.BlockSpec((tm, K), lambda i, j: (i, 0)),
                pl.BlockSpec((K, tn), lambda i, j: (0, j))]
    args = [a, b]
    body = _mm_body
    if residual is not None:
        in_specs.append(pl.BlockSpec((tm, tn), lambda i, j: (i, j)))
        args.append(residual)
        body = _mm_res_body
    return pl.pallas_call(
        body,
        grid=(M // tm, N // tn),
        in_specs=in_specs,
        out_specs=pl.BlockSpec((tm, tn), lambda i, j: (i, j)),
        out_shape=jax.ShapeDtypeStruct((M, N), F32),
        compiler_params=_cparams(("parallel", "parallel")),
        name=name,
    )(*args)


def _gdn_chunk_body(q_ref, k_ref, v_ref, z_ref, ab_ref, cwq_ref, cwk_ref, cwv_ref,
                    alog_ref, dtb_ref, gn_ref, o_ref, s_out_ref,
                    xq_s, xk_s, xv_s, state_s):
    n = pl.program_id(2)
    C = GDN_CHUNK
    HW = GDN_HB * GDN_DK

    @pl.when(n == 0)
    def _():
        xq_s[0:8, :] = jnp.zeros((8, HW), F32)
        xk_s[0:8, :] = jnp.zeros((8, HW), F32)
        xv_s[0:8, :] = jnp.zeros((8, HW), F32)
        state_s[...] = jnp.zeros(state_s.shape, F32)

    def conv(x_ref, xs, cw_ref):
        xs[8:8 + C, :] = x_ref[0]
        y = cw_ref[CONV_W - 1:CONV_W, :] * xs[8:8 + C, :]
        for w in range(CONV_W - 1):
            y = y + cw_ref[w:w + 1, :] * xs[5 + w:5 + w + C, :]
        xs[0:8, :] = xs[C:C + 8, :]
        return _silu(y)

    q_all = conv(q_ref, xq_s, cwq_ref)
    k_all = conv(k_ref, xk_s, cwk_ref)
    v_all = conv(v_ref, xv_s, cwv_ref)

    ab = ab_ref[0, 0]
    g_all = -jnp.exp(alog_ref[0]) * _softplus(ab + dtb_ref[0])
    beta_all = _sigmoid(ab)
    ri = lax.broadcasted_iota(jnp.int32, (C, C), 0)
    ci = lax.broadcasted_iota(jnp.int32, (C, C), 1)
    tri = ri >= ci
    strict = ri > ci
    gc_all = _hdot(tri.astype(F32), g_all)
    gc_t = jnp.concatenate([gc_all, jnp.zeros((128 - C, 128), F32)], axis=0).T

    H = range(GDN_HB)
    sls = [slice(j * GDN_DK, (j + 1) * GDN_DK) for j in H]
    qh, kh, kb, gcol, decay, rhs = [], [], [], [], [], []
    for j in H:
        q = q_all[:, sls[j]]
        k = k_all[:, sls[j]]
        q = q * lax.rsqrt(jnp.sum(q * q, axis=-1, keepdims=True) + EPS) * (GDN_DK ** -0.5)
        k = k * lax.rsqrt(jnp.sum(k * k, axis=-1, keepdims=True) + EPS)
        beta = beta_all[:, GDN_HB + j:GDN_HB + j + 1]
        gc = gc_all[:, j:j + 1]
        grow = gc_t[j:j + 1, :C]
        qh.append(q)
        kh.append(k)
        kb.append(k * beta)
        gcol.append(gc)
        decay.append(jnp.where(tri, jnp.exp(jnp.where(tri, gc - grow, 0.0)), 0.0))
        rhs.append(jnp.concatenate([v_all[:, sls[j]] * beta, kb[j] * jnp.exp(gc)], axis=-1))
    kq = [_bdot_nt(jnp.concatenate([kb[j], qh[j]], axis=0), kh[j]) for j in H]
    intra = [jnp.where(tri, kq[j][C:] * decay[j], 0.0) for j in H]
    acc = [-jnp.where(strict, kq[j][:C] * decay[j], 0.0) for j in H]
    parts = [_split_bf16(acc[j]) for j in H]
    pw = [_dot3(parts[j], parts[j]) for j in H]
    for r in range(5):
        pw_parts = [_split_bf16(pw[j]) for j in H]
        acc_parts = [_split_bf16(acc[j]) for j in H]
        acc = [acc[j] + pw[j] + _dot3(pw_parts[j], acc_parts[j]) for j in H]
        if r < 4:
            pw = [_dot3(pw_parts[j], pw_parts[j]) for j in H]
    sol = [rhs[j] + _dot3(_split_bf16(acc[j]), _split_bf16(rhs[j])) for j in H]
    S = [state_s[j] for j in H]
    wq_s = [_bdot(jnp.concatenate([sol[j][:, GDN_DV:], qh[j] * jnp.exp(gcol[j])], axis=0), S[j])
            for j in H]
    v_new = [sol[j][:, :GDN_DV] - wq_s[j][:C] for j in H]
    o = [wq_s[j][C:] + _bdot(intra[j], v_new[j]) for j in H]
    for j in H:
        g_last = gcol[j][C - 1:C, :]
        state_s[j] = (S[j] * jnp.exp(g_last)
                      + _bdot_tn(kh[j] * jnp.exp(g_last - gcol[j]), v_new[j]))
    for j in H:
        ms = jnp.mean(o[j] * o[j], axis=-1, keepdims=True)
        on = o[j] * lax.rsqrt(ms + EPS) * gn_ref[...]
        o_ref[0, :, sls[j]] = (on * _silu(z_ref[0, :, sls[j]])).astype(o_ref.dtype)

    @pl.when(n == pl.num_programs(2) - 1)
    def _():
        s_out_ref[0] = state_s[...]


def gdn_prompt(proj3, ab_g, conv_w, alog_g, dtb_g, gdn_norm):
    B, T, _ = proj3.shape
    C = GDN_CHUNK
    N = T // C
    HW = GDN_HB * GDN_DK
    nq = GDN_QK_W // HW

    def col(base):
        return lambda b, hg, n: (b, n, base + hg)

    def cwcol(base):
        return lambda b, hg, n: (0, base + hg)

    return pl.pallas_call(
        _gdn_chunk_body,
        grid=(B, GDN_HG, N),
        in_specs=[
            pl.BlockSpec((1, C, HW), col(0)),
            pl.BlockSpec((1, C, HW), col(nq)),
            pl.BlockSpec((1, C, HW), col(2 * nq)),
            pl.BlockSpec((1, C, HW), col(OFF_Z // HW)),
            pl.BlockSpec((1, 1, C, 128), lambda b, hg, n: (b, hg, n, 0)),
            pl.BlockSpec((CONV_W, HW), cwcol(0)),
            pl.BlockSpec((CONV_W, HW), cwcol(nq)),
            pl.BlockSpec((CONV_W, HW), cwcol(2 * nq)),
            pl.BlockSpec((1, 1, 128), lambda b, hg, n: (hg, 0, 0)),
            pl.BlockSpec((1, 1, 128), lambda b, hg, n: (hg, 0, 0)),
            pl.BlockSpec((1, GDN_DV), lambda b, hg, n: (0, 0)),
        ],
        out_specs=[
            pl.BlockSpec((1, C, HW), lambda b, hg, n: (b, n, hg)),
            pl.BlockSpec((1, GDN_HB, GDN_DK, GDN_DV), lambda b, hg, n: (b, hg, 0, 0)),
        ],
        out_shape=[
            jax.ShapeDtypeStruct((B, T, GDN_V_W), BF16),
            jax.ShapeDtypeStruct((B, GDN_HEADS, GDN_DK, GDN_DV), F32),
        ],
        scratch_shapes=[
            pltpu.VMEM((C + 8, HW), F32),
            pltpu.VMEM((C + 8, HW), F32),
            pltpu.VMEM((C + 8, HW), F32),
            pltpu.VMEM((GDN_HB, GDN_DK, GDN_DV), F32),
        ],
        compiler_params=_cparams(("parallel", "parallel", "arbitrary")),
        name="gdn_prompt",
    )(proj3, proj3, proj3, proj3, ab_g, conv_w, conv_w, conv_w, alog_g, dtb_g,
      gdn_norm.reshape(1, GDN_DV))


def _gdn_step_body(x_ref, z_ref, ab_ref, cs_ref, s_ref, cw_ref, alog_ref, dtb_ref, gn_ref,
                   o_ref, cs_out_ref, s_out_ref):
    x = x_ref[0]
    y = cw_ref[CONV_W - 1] * x
    for w in range(CONV_W - 1):
        y = y + cw_ref[w] * cs_ref[0, w]
    y = _silu(y)
    cs_out_ref[0, 0] = cs_ref[0, 1]
    cs_out_ref[0, 1] = cs_ref[0, 2]
    cs_out_ref[0, 2] = x
    r_ab = (OFF_AB // 128) % 8
    ab = ab_ref[0, r_ab:r_ab + 1, :]
    g_all = -jnp.exp(alog_ref[...]) * _softplus(ab + dtb_ref[...])
    beta_all = _sigmoid(ab)
    eye = (lax.broadcasted_iota(jnp.int32, (GDN_DK, GDN_DK), 0)
           == lax.broadcasted_iota(jnp.int32, (GDN_DK, GDN_DK), 1))

    def to_col(row):
        return jnp.sum(jnp.where(eye, jnp.broadcast_to(row, (GDN_DK, GDN_DK)), 0.0),
                       axis=1, keepdims=True)

    for h in range(GDN_HEADS):
        q = y[h:h + 1, :]
        k = y[GDN_HEADS + h:GDN_HEADS + h + 1, :]
        v = y[2 * GDN_HEADS + h:2 * GDN_HEADS + h + 1, :]
        q = q * lax.rsqrt(jnp.sum(q * q, axis=-1, keepdims=True) + EPS) * (GDN_DK ** -0.5)
        k = k * lax.rsqrt(jnp.sum(k * k, axis=-1, keepdims=True) + EPS)
        g = g_all[:, h:h + 1]
        beta = beta_all[:, GDN_HEADS + h:GDN_HEADS + h + 1]
        eg = jnp.exp(g)
        S = s_ref[0, h]
        kcol = to_col(k)
        qcol = to_col(q)
        k_s = jnp.sum(S * kcol, axis=0, keepdims=True)
        q_s = jnp.sum(S * qcol, axis=0, keepdims=True)
        v_new = v * beta - (beta * eg) * k_s
        o = eg * q_s + jnp.sum(q * k, axis=-1, keepdims=True) * v_new
        s_out_ref[0, h] = S * eg + kcol * v_new
        ms = jnp.mean(o * o, axis=-1, keepdims=True)
        on = o * lax.rsqrt(ms + EPS) * gn_ref[...]
        o_ref[0, h:h + 1, :] = on * _silu(z_ref[0, h:h + 1, :])


def gdn_step(proj_s, conv_state, ssm_state, conv_w, a_log, dt_bias, gdn_norm):
    Bs = proj_s.shape[0]
    nrow = N_PROJ // 128
    p3 = proj_s.reshape(Bs, nrow, 128)
    cs4 = conv_state.reshape(Bs, CONV_W - 1, CONV_CH // 128, 128)
    cw3 = conv_w.reshape(CONV_W, CONV_CH // 128, 128)
    pad = jnp.zeros((128 - GDN_HEADS,), F32)
    alog = jnp.concatenate([a_log, pad]).reshape(1, 128)
    dtb = jnp.concatenate([dt_bias, pad]).reshape(1, 128)
    nr = CONV_CH // 128
    o, cs_new, s_new = pl.pallas_call(
        _gdn_step_body,
        grid=(Bs,),
        in_specs=[
            pl.BlockSpec((1, nr, 128), lambda b: (b, 0, 0)),
            pl.BlockSpec((1, GDN_HEADS, 128), lambda b: (b, OFF_Z // 128 // GDN_HEADS, 0)),
            pl.BlockSpec((1, 8, 128), lambda b: (b, OFF_AB // 128 // 8, 0)),
            pl.BlockSpec((1, CONV_W - 1, nr, 128), lambda b: (b, 0, 0, 0)),
            pl.BlockSpec((1, GDN_HEADS, GDN_DK, GDN_DV), lambda b: (b, 0, 0, 0)),
            pl.BlockSpec((CONV_W, nr, 128), lambda b: (0, 0, 0)),
            pl.BlockSpec((1, 128), lambda b: (0, 0)),
            pl.BlockSpec((1, 128), lambda b: (0, 0)),
            pl.BlockSpec((1, GDN_DV), lambda b: (0, 0)),
        ],
        out_specs=[
            pl.BlockSpec((1, GDN_HEADS, GDN_DV), lambda b: (b, 0, 0)),
            pl.BlockSpec((1, CONV_W - 1, nr, 128), lambda b: (b, 0, 0, 0)),
            pl.BlockSpec((1, GDN_HEADS, GDN_DK, GDN_DV), lambda b: (b, 0, 0, 0)),
        ],
        out_shape=[
            jax.ShapeDtypeStruct((Bs, GDN_HEADS, GDN_DV), F32),
            jax.ShapeDtypeStruct((Bs, CONV_W - 1, nr, 128), F32),
            jax.ShapeDtypeStruct((Bs, GDN_HEADS, GDN_DK, GDN_DV), F32),
        ],
        compiler_params=_cparams(("parallel",)),
        name="gdn_step",
    )(p3, p3, p3, cs4, ssm_state, cw3, alog, dtb, gdn_norm.reshape(1, GDN_DV))
    return (o.reshape(Bs, GDN_V_W), cs_new.reshape(Bs, CONV_W - 1, CONV_CH), s_new)


def _qknorm_body(q_ref, k_ref, v_ref, qg_ref, kg_ref, qn_ref, kv_ref):
    for h in range(GROUP_HEADS):
        sl = slice(h * HEAD_DIM, (h + 1) * HEAD_DIM)
        q = q_ref[:, sl]
        qn = q * lax.rsqrt(jnp.mean(q * q, axis=-1, keepdims=True) + EPS) * qg_ref[...]
        qn_ref[:, sl] = (qn * (HEAD_DIM ** -0.5)).astype(qn_ref.dtype)
        k = k_ref[:, sl]
        kv_ref[:, sl] = k * lax.rsqrt(jnp.mean(k * k, axis=-1, keepdims=True) + EPS) * kg_ref[...]
    kv_ref[:, GROUP_W:] = v_ref[...]


def qk_norm(proj, q_norm, k_norm, q_dtype):
    M = proj.shape[0]
    tm = min(M, 512)
    blk = GROUP_W
    return pl.pallas_call(
        _qknorm_body,
        grid=(M // tm, N_GROUPS),
        in_specs=[
            pl.BlockSpec((tm, blk), lambda i, g: (i, OFF_AQ // blk + g)),
            pl.BlockSpec((tm, blk), lambda i, g: (i, OFF_AK // blk + g)),
            pl.BlockSpec((tm, blk), lambda i, g: (i, OFF_AV // blk + g)),
            pl.BlockSpec((1, HEAD_DIM), lambda i, g: (0, 0)),
            pl.BlockSpec((1, HEAD_DIM), lambda i, g: (0, 0)),
        ],
        out_specs=[
            pl.BlockSpec((tm, blk), lambda i, g: (i, g)),
            pl.BlockSpec((tm, 2 * blk), lambda i, g: (i, g)),
        ],
        out_shape=[jax.ShapeDtypeStruct((M, ATTN_W), q_dtype),
                   jax.ShapeDtypeStruct((M, N_GROUPS * 2 * GROUP_W), F32)],
        compiler_params=_cparams(("parallel", "parallel")),
        name="qk_norm",
    )(proj, proj, proj, q_norm.reshape(1, HEAD_DIM), k_norm.reshape(1, HEAD_DIM))


def _t5_bucket(dist):
    max_exact = N_BUCKETS // 2
    d = np.maximum(dist, max_exact).astype(np.float32)
    large = max_exact + (np.log(d / max_exact) / np.log(MAX_DISTANCE / max_exact)
                         * (N_BUCKETS - max_exact)).astype(np.int32)
    large = np.minimum(large, N_BUCKETS - 1)
    return np.where(dist < max_exact, dist, large).astype(np.int32)


def _group_bias(rel_bias, gi):
    window, dilation = ATTN_GROUPS[gi]
    n_keys = window // dilation + 1
    buckets = _t5_bucket(np.arange(n_keys) * dilation)
    return rel_bias[buckets][:, gi * GROUP_HEADS:(gi + 1) * GROUP_HEADS].T.astype(F32)


def _attn_prompt_body(T, q0_ref, q1_ref, q2_ref, k0_ref, v0_ref, k1_ref, v1_ref, k2_ref, v2_ref,
                      bias_ref, y_ref, m_s, l_s, o_s):
    QB = Q_BLOCK
    q_refs = (q0_ref, q1_ref, q2_ref)
    k_refs = (k0_ref, k1_ref, k2_ref)
    v_refs = (v0_ref, v1_ref, v2_ref)
    for gi, (window, d) in enumerate(ATTN_GROUPS):
        nq = T // (d * QB)
        for r in range(d):
            for qi in range(nq):
                rows = pl.ds(r + d * QB * qi, QB, stride=d) if d > 1 else pl.ds(QB * qi, QB)
                q = q_refs[gi][0, rows, :].astype(BF16)
                kc = k_refs[gi][0, rows, :]
                vc = v_refs[gi][0, rows, :]
                if qi > 0:
                    prev = (pl.ds(r + d * QB * (qi - 1), QB, stride=d) if d > 1
                            else pl.ds(QB * (qi - 1), QB))
                    kcat = jnp.concatenate([k_refs[gi][0, prev, :], kc], axis=0).astype(BF16)
                    vcat = jnp.concatenate([v_refs[gi][0, prev, :], vc], axis=0).astype(BF16)
                    bias = bias_ref[gi, 0]
                else:
                    kcat = kc.astype(BF16)
                    vcat = vc.astype(BF16)
                    bias = bias_ref[gi, 0, :, QB:]
                s = lax.dot_general(q, kcat, (((1,), (1,)), ((), ())),
                                    preferred_element_type=F32) + bias
                m_blk = jnp.max(s, axis=-1, keepdims=True)
                if gi == 0:
                    m_new = m_blk
                else:
                    m_old = m_s[rows, :][:, 0:1]
                    m_new = jnp.maximum(m_old, m_blk)
                p = jnp.exp(s - m_new)
                l_new = jnp.sum(p, axis=-1, keepdims=True)
                o_new = jnp.dot(p.astype(BF16), vcat, preferred_element_type=F32)
                if gi > 0:
                    alpha = jnp.exp(m_old - m_new)
                    l_new = l_new + alpha * l_s[rows, :][:, 0:1]
                    o_new = o_new + alpha * o_s[rows, :]
                m_s[rows, :] = jnp.broadcast_to(m_new, (QB, HEAD_DIM))
                l_s[rows, :] = jnp.broadcast_to(l_new, (QB, HEAD_DIM))
                o_s[rows, :] = o_new
    y_ref[0] = (o_s[...] / l_s[...]).astype(y_ref.dtype)


def attn_prompt(qn3, kv3, bias_all):
    B, T, _ = qn3.shape
    blk = (1, T, HEAD_DIM)

    def qspec(gi):
        return pl.BlockSpec(blk, lambda b, h: (b, 0, gi * GROUP_HEADS + h))

    def kvspec(gi, is_v):
        return pl.BlockSpec(blk, lambda b, h: (b, 0, (2 * gi + is_v) * GROUP_HEADS + h))

    in_specs = [qspec(0), qspec(1), qspec(2)]
    for gi in range(N_GROUPS):
        in_specs += [kvspec(gi, 0), kvspec(gi, 1)]
    in_specs.append(pl.BlockSpec((N_GROUPS, 1, Q_BLOCK, 2 * Q_BLOCK), lambda b, h: (0, h, 0, 0)))
    y = pl.pallas_call(
        functools.partial(_attn_prompt_body, T),
        grid=(B, GROUP_HEADS),
        in_specs=in_specs,
        out_specs=pl.BlockSpec(blk, lambda b, h: (b, 0, h)),
        out_shape=jax.ShapeDtypeStruct((B, T, GROUP_W), BF16),
        scratch_shapes=[pltpu.VMEM((T, HEAD_DIM), F32), pltpu.VMEM((T, HEAD_DIM), F32),
                        pltpu.VMEM((T, HEAD_DIM), F32)],
        compiler_params=_cparams(("parallel", "parallel")),
        name="attn_prompt",
    )(qn3, qn3, qn3, kv3, kv3, kv3, kv3, kv3, kv3, bias_all)
    return y.reshape(B * T, GROUP_W)


def _prompt_bias_matrix(rel_bias, gi):
    QB = Q_BLOCK
    bias = _group_bias(rel_bias, gi)
    row0 = jnp.concatenate([bias[:, ::-1], jnp.full((GROUP_HEADS, QB), NEG_INF, F32)], axis=1)
    flat = jnp.tile(row0, (1, QB))[:, :QB * 2 * QB]
    return flat.reshape(GROUP_HEADS, QB, 2 * QB)


def _attn_decode_body(q_ref, n0_ref, n1_ref, n2_ref, c0_ref, c1_ref, c2_ref, bc_ref, bn_ref, y_ref):
    new_refs = (n0_ref, n1_ref, n2_ref)
    cache_refs = (c0_ref, c1_ref, c2_ref)
    outs, lds = [], []
    for gi in range(N_GROUPS):
        q = q_ref[0, gi * GROUP_HEADS:(gi + 1) * GROUP_HEADS, :]
        kc = cache_refs[gi][:, 0]
        vc = cache_refs[gi][:, 1]
        kn = new_refs[gi][0, 0:GROUP_HEADS, :]
        vn = new_refs[gi][0, GROUP_HEADS:2 * GROUP_HEADS, :]
        s_c = jnp.sum(kc * q[None], axis=-1, keepdims=True) + bc_ref[gi, :, :, 0:1]
        s_n = jnp.sum(kn * q, axis=-1, keepdims=True) + bn_ref[gi, :, 0:1]
        m = jnp.maximum(jnp.max(s_c, axis=0), s_n)
        p_c = jnp.exp(s_c - m[None])
        p_n = jnp.exp(s_n - m)
        l = jnp.sum(p_c, axis=0) + p_n
        outs.append((jnp.sum(p_c * vc, axis=0) + p_n * vn) / l)
        lds.append(m + jnp.log(l))
    m = jnp.maximum(jnp.maximum(lds[0], lds[1]), lds[2])
    es = [jnp.exp(x - m) for x in lds]
    den = es[0] + es[1] + es[2]
    y_ref[0] = outs[0] * (es[0] / den) + outs[1] * (es[1] / den) + outs[2] * (es[2] / den)


def attn_decode(qn_s, kv_new, caches, rel_bias):
    Bs = qn_s.shape[0]
    q3 = qn_s.reshape(Bs, N_GROUPS * GROUP_HEADS, HEAD_DIM)
    new3 = [x.reshape(Bs, 2 * GROUP_HEADS, HEAD_DIM) for x in kv_new]
    cviews, bcs, bns = [], [], []
    for gi, (window, d) in enumerate(ATTN_GROUPS):
        nk = window // d
        assert caches[gi].shape[1] == window and nk == Q_BLOCK
        cviews.append(caches[gi].reshape(Bs, nk, d, 2, GROUP_HEADS, HEAD_DIM))
        bias = _group_bias(rel_bias, gi)
        bc = bias[:, nk - np.arange(nk)]
        bcs.append(jnp.broadcast_to(bc.T[:, :, None], (nk, GROUP_HEADS, HEAD_DIM)))
        bns.append(jnp.broadcast_to(bias[:, 0:1], (GROUP_HEADS, HEAD_DIM)))
    bc_all = jnp.stack(bcs, 0)
    bn_all = jnp.stack(bns, 0)
    nspec = pl.BlockSpec((1, 2 * GROUP_HEADS, HEAD_DIM), lambda b: (b, 0, 0))
    cspec = pl.BlockSpec((None, Q_BLOCK, None, 2, GROUP_HEADS, HEAD_DIM),
                         lambda b: (b, 0, 0, 0, 0, 0))
    y = pl.pallas_call(
        _attn_decode_body,
        grid=(Bs,),
        in_specs=[
            pl.BlockSpec((1, N_GROUPS * GROUP_HEADS, HEAD_DIM), lambda b: (b, 0, 0)),
            nspec, nspec, nspec, cspec, cspec, cspec,
            pl.BlockSpec((N_GROUPS, Q_BLOCK, GROUP_HEADS, HEAD_DIM), lambda b: (0, 0, 0, 0)),
            pl.BlockSpec((N_GROUPS, GROUP_HEADS, HEAD_DIM), lambda b: (0, 0, 0)),
        ],
        out_specs=pl.BlockSpec((1, GROUP_HEADS, HEAD_DIM), lambda b: (b, 0, 0)),
        out_shape=jax.ShapeDtypeStruct((Bs, GROUP_HEADS, HEAD_DIM), F32),
        compiler_params=_cparams(("parallel",)),
        name="attn_decode",
    )(q3, *new3, *cviews, bc_all, bn_all)
    return y.reshape(Bs, GROUP_W)


def _merge_body(ya_ref, yb_ref, wa_ref, wb_ref, ga_ref, gb_ref, o_ref):
    pa = jnp.dot(ya_ref[...], wa_ref[...], preferred_element_type=F32)
    pb = jnp.dot(yb_ref[...], wb_ref[...], preferred_element_type=F32)
    o_ref[...] = (_sigmoid(ga_ref[...]) * pa + _sigmoid(gb_ref[...]) * pb).astype(o_ref.dtype)


def branch_merge(ya, yb, wa, wb, proj):
    M = ya.shape[0]
    tm = min(M, 1024)
    tn = 512
    return pl.pallas_call(
        _merge_body,
        grid=(M // tm, D_MODEL // tn),
        in_specs=[
            pl.BlockSpec((tm, GDN_V_W), lambda i, j: (i, 0)),
            pl.BlockSpec((tm, GROUP_W), lambda i, j: (i, 0)),
            pl.BlockSpec((GDN_V_W, tn), lambda i, j: (0, j)),
            pl.BlockSpec((GROUP_W, tn), lambda i, j: (0, j)),
            pl.BlockSpec((tm, tn), lambda i, j: (i, OFF_GA // tn + j)),
            pl.BlockSpec((tm, tn), lambda i, j: (i, OFF_GB // tn + j)),
        ],
        out_specs=pl.BlockSpec((tm, tn), lambda i, j: (i, j)),
        out_shape=jax.ShapeDtypeStruct((M, D_MODEL), BF16),
        compiler_params=_cparams(("parallel", "parallel")),
        name="branch_merge",
    )(ya, yb, wa, wb, proj, proj)


def _extract_top(x, count, tie_safe):
    R = x.shape[0]
    rid = lax.broadcasted_iota(jnp.int32, x.shape, 0).astype(F32)
    vid = lax.broadcasted_iota(jnp.int32, (count, x.shape[1]), 0)
    vals = jnp.zeros((count, x.shape[1]), F32)
    rank = jnp.full(x.shape, float(count), F32)
    for a in range(count):
        m = jnp.max(x, axis=0, keepdims=True)
        hit = x == m
        if tie_safe:
            first = jnp.min(jnp.where(hit, rid, float(R)), axis=0, keepdims=True)
            hit = rid == first
        x = jnp.where(hit, -jnp.inf, x)
        rank = jnp.where(hit, float(a), rank)
        vals = jnp.where(vid == a, m, vals)
    return vals, rank


CAND_HALF = PEER_TOPK // 2
N_CAND = PEER_TOPK + (CAND_HALF - 1) * CAND_HALF + CAND_HALF


def _peer_topk_body(q_ref, sk_ref, cnt_ref, r2_ref, e1_ref, e2_ref, cand_s, s1_s, s2_s):
    K = PEER_TOPK
    KH = CAND_HALF

    def select(h, s1, s2, tie_safe):
        v1, rank1 = _extract_top(s1, K, tie_safe)
        v2, rank2 = _extract_top(s2, K, tie_safe)
        cand_s[0:K, :] = v2 + v1[0:1, :]
        for a in range(1, KH):
            cand_s[K + (a - 1) * KH:K + a * KH, :] = v2[0:KH, :] + v1[a:a + 1, :]
        cand_s[N_CAND - KH:N_CAND, :] = v1[KH:K, :] + v2[0:1, :]
        best, crank = _extract_top(cand_s[...], K, tie_safe)
        picked = jnp.where(crank < float(K), 1.0, 0.0)
        vid = lax.broadcasted_iota(jnp.int32, (KH, picked.shape[1]), 0)
        low = jnp.where(vid == 0, jnp.sum(picked[0:K], axis=0, keepdims=True), 0.0)
        for a in range(1, KH):
            rows = picked[K + (a - 1) * KH:K + a * KH]
            low = jnp.where(vid == a, jnp.sum(rows, axis=0, keepdims=True), low)
        count = jnp.concatenate([low, picked[N_CAND - KH:N_CAND]], axis=0)
        cnt1 = jnp.zeros(rank1.shape, F32)
        for a in range(K):
            cnt1 = jnp.where(rank1 == float(a), count[a:a + 1, :], cnt1)
        cmax = best[0:1, :]
        z = jnp.sum(jnp.exp(best - cmax), axis=0, keepdims=True)
        in1 = jnp.where(rank1 < float(K), 1.0, 0.0)
        in2 = jnp.where(rank2 < float(K), 1.0, 0.0)
        cnt_ref[h] = cnt1
        r2_ref[h] = rank2
        e1_ref[h] = in1 * jnp.exp(s1 - v1[0:1, :])
        e2_ref[h] = in2 * jnp.exp(s2 - v2[0:1, :]) / z * 0.5
        return (jnp.sum(in1, axis=0, keepdims=True) + jnp.sum(in2, axis=0, keepdims=True)
                + jnp.sum(picked, axis=0, keepdims=True) - 3.0 * K)

    def head(h, carry):
        off = pl.multiple_of(h * (2 * N_KEYS), 2 * N_KEYS)
        q1 = q_ref[:, pl.ds(off, N_KEYS)]
        q2 = q_ref[:, pl.ds(pl.multiple_of(off + N_KEYS, N_KEYS), N_KEYS)]
        s1_s[...] = lax.dot_general(sk_ref[0], q1, (((1,), (1,)), ((), ())), precision=HIGHEST,
                                    preferred_element_type=F32)
        s2_s[...] = lax.dot_general(sk_ref[1], q2, (((1,), (1,)), ((), ())), precision=HIGHEST,
                                    preferred_element_type=F32)
        extra = select(h, s1_s[...], s2_s[...], tie_safe=False)

        @pl.when(jnp.max(extra) > 0.0)
        def _():
            select(h, s1_s[...], s2_s[...], tie_safe=True)

        return carry

    lax.fori_loop(0, PEER_HEADS, head, 0)


def peer_topk(query, subkeys):
    M = query.shape[0]
    tm = min(M, 512)
    big = jax.ShapeDtypeStruct((PEER_HEADS, N_KEYS, M), F32)
    bspec = pl.BlockSpec((PEER_HEADS, N_KEYS, tm), lambda i: (0, 0, i))
    return pl.pallas_call(
        _peer_topk_body,
        grid=(M // tm,),
        in_specs=[pl.BlockSpec((tm, PEER_HEADS * 2 * N_KEYS), lambda i: (i, 0)),
                  pl.BlockSpec((2, N_KEYS, N_KEYS), lambda i: (0, 0, 0))],
        out_specs=[bspec, bspec, bspec, bspec],
        out_shape=[big, big, big, big],
        scratch_shapes=[pltpu.VMEM((N_CAND, tm), F32), pltpu.VMEM((N_KEYS, tm), F32),
                        pltpu.VMEM((N_KEYS, tm), F32)],
        compiler_params=_cparams(("parallel",)),
        name="peer_topk",
    )(query, subkeys)


def _gelu_x2(x):
    return x * (1.0 + lax.erf(x * (2.0 ** -0.5)))


GATE_ROWS = 8


def _peer_gate_body(hf_ref, u_ref, cnt_ref, r2_ref, e1_ref, e2_ref, w_ref, g_s):
    j = pl.program_id(1)
    tm = hf_ref.shape[0]
    grp = pl.multiple_of(j * GATE_ROWS, GATE_ROWS)

    for kk in range(GATE_ROWS):
        for tc in range(tm // 128):
            ts = slice(tc * 128, (tc + 1) * 128)
            acc = None
            for h in range(PEER_HEADS):
                cnt_row = cnt_ref[h, pl.ds(grp, GATE_ROWS), ts][kk:kk + 1, :]
                e1_row = e1_ref[h, pl.ds(grp, GATE_ROWS), ts][kk:kk + 1, :]
                term = jnp.where(r2_ref[h, :, ts] < cnt_row, e2_ref[h, :, ts] * e1_row, 0.0)
                acc = term if acc is None else acc + term
            g_s[ts, kk * N_KEYS:(kk + 1) * N_KEYS] = acc.T
    hmat = lax.dot_general(hf_ref[...], u_ref[...], (((1,), (1,)), ((), ())),
                           preferred_element_type=F32)
    w_ref[...] = (g_s[...] * _gelu_x2(hmat)).astype(w_ref.dtype)


def peer_gate_weights(hf, u_tab, cnt, r2, e1, e2, tm=1024):
    M = hf.shape[0]
    tm = min(M, tm)
    tn = GATE_ROWS * N_KEYS
    E = u_tab.shape[0]
    once = pl.Buffered(1)
    sspec = pl.BlockSpec((PEER_HEADS, N_KEYS, tm), lambda i, j: (0, 0, i), pipeline_mode=once)
    return pl.pallas_call(
        _peer_gate_body,
        grid=(M // tm, E // tn),
        in_specs=[
            pl.BlockSpec((tm, D_MODEL), lambda i, j: (i, 0), pipeline_mode=once),
            pl.BlockSpec((tn, D_MODEL), lambda i, j: (j, 0)),
            sspec, sspec, sspec, sspec,
        ],
        out_specs=pl.BlockSpec((tm, tn), lambda i, j: (i, j)),
        out_shape=jax.ShapeDtypeStruct((M, E), BF16),
        scratch_shapes=[pltpu.VMEM((tm, tn), F32)],
        compiler_params=_cparams(("parallel", "arbitrary")),
        name="peer_gate",
    )(hf, u_tab, cnt, r2, e1, e2)


def _mm_kres_body(a_ref, b_ref, r_ref, o_ref):
    k = pl.program_id(2)

    @pl.when(k == 0)
    def _():
        o_ref[...] = r_ref[...]

    o_ref[...] += jnp.dot(a_ref[...], b_ref[...], preferred_element_type=F32)


def matmul_kres(a, b, residual, tm=1024, tn=1024, tk=4096, name="matmul_k"):
    M, K = a.shape
    N = b.shape[1]
    tm = min(tm, M)
    return pl.pallas_call(
        _mm_kres_body,
        grid=(M // tm, N // tn, K // tk),
        in_specs=[pl.BlockSpec((tm, tk), lambda i, j, k: (i, k)),
                  pl.BlockSpec((tk, tn), lambda i, j, k: (k, j)),
                  pl.BlockSpec((tm, tn), lambda i, j, k: (i, j))],
        out_specs=pl.BlockSpec((tm, tn), lambda i, j, k: (i, j)),
        out_shape=jax.ShapeDtypeStruct((M, N), F32),
        compiler_params=_cparams(("parallel", "parallel", "arbitrary")),
        name=name,
    )(a, b, residual)


def peer_ffn(x1, norm_ffn, wq_bf, subkeys, u_bf, v_bf):
    M = x1.shape[0]
    Mp = max(M, 128)
    if Mp != M:
        x1 = jnp.concatenate([x1, jnp.zeros((Mp - M, D_MODEL), F32)], axis=0)
    hf = rmsnorm_bf16(x1, norm_ffn)
    query = matmul(hf, wq_bf, name="peer_query")
    cnt, r2, e1, e2 = peer_topk(query, subkeys)
    w = peer_gate_weights(hf, u_bf, cnt, r2, e1, e2)
    return matmul_kres(w, v_bf, x1, name="peer_out")[:M]


W_BLK = 512
N_AB = 2 * GDN_HEADS
J_AB = OFF_AB // W_BLK


def _in_proj_body(a_ref, a2_ref, wa_ref, wb_ref, o_ref, o2_ref, w_s):
    j = pl.program_id(0)
    i = pl.program_id(1)
    dims = (((1,), (1,)), ((), ()))

    @pl.when(jnp.logical_and(i == 0, j < OFF_AQ // W_BLK))
    def _():
        w_s[...] = wa_ref[...].astype(w_s.dtype)

    @pl.when(jnp.logical_and(i == 0, jnp.logical_and(j >= OFF_AQ // W_BLK, j < J_AB)))
    def _():
        w_s[0:W_BLK - N_AB, :] = wa_ref[N_AB:, :].astype(w_s.dtype)
        w_s[W_BLK - N_AB:, :] = wb_ref[...].astype(w_s.dtype)

    @pl.when(jnp.logical_and(i == 0, j == J_AB))
    def _():
        w_s[0:N_AB, :] = wb_ref[...].astype(w_s.dtype)
        w_s[N_AB:, :] = jnp.zeros((W_BLK - N_AB, w_s.shape[1]), w_s.dtype)

    o_ref[...] = lax.dot_general(a_ref[...], w_s[...], dims, preferred_element_type=F32)

    @pl.when(i == 0)
    def _():
        o2_ref[...] = lax.dot_general(a2_ref[...], w_s[...], dims, preferred_element_type=F32)


def in_proj(a, a2, w_in_t, tm=1024):
    M, D = a.shape
    M2 = a2.shape[0]
    tm = min(tm, M)
    j0 = OFF_AQ // W_BLK
    sub = W_BLK // N_AB
    return pl.pallas_call(
        _in_proj_body,
        grid=(N_PROJ // W_BLK, M // tm),
        in_specs=[
            pl.BlockSpec((tm, D), lambda j, i: (i, 0)),
            pl.BlockSpec((M2, D), lambda j, i: (0, 0)),
            pl.BlockSpec((W_BLK, D), lambda j, i: (jnp.where(j == J_AB, j0, j), 0)),
            pl.BlockSpec((N_AB, D), lambda j, i: (jnp.where(j == J_AB, j0 * sub,
                                                            jnp.where(j >= j0, (j + 1) * sub, 0)), 0)),
        ],
        out_specs=[pl.BlockSpec((tm, W_BLK), lambda j, i: (i, j)),
                   pl.BlockSpec((M2, W_BLK), lambda j, i: (0, j))],
        out_shape=[jax.ShapeDtypeStruct((M, N_PROJ), F32), jax.ShapeDtypeStruct((M2, N_PROJ), F32)],
        scratch_shapes=[pltpu.VMEM((W_BLK, D), BF16)],
        compiler_params=_cparams(("parallel", "arbitrary")),
        name="in_proj",
    )(a, a2, w_in_t, w_in_t)


def _gdn_group_params(a_log, dt_bias):
    pad = jnp.zeros((GDN_HG, 128 - GDN_HB), F32)
    alog = jnp.concatenate([a_log.reshape(GDN_HG, GDN_HB), pad], axis=1).reshape(GDN_HG, 1, 128)
    dtb = jnp.concatenate([dt_bias.reshape(GDN_HG, GDN_HB), pad], axis=1).reshape(GDN_HG, 1, 128)
    return alog, dtb


def _ab_grouped(proj, B, T):
    ab = proj[:, OFF_AB:OFF_AB + 2 * GDN_HEADS].reshape(B, T, 2, GDN_HG, GDN_HB)
    ab = ab.transpose(0, 3, 1, 2, 4).reshape(B, GDN_HG, T, 2 * GDN_HB)
    return jnp.concatenate([ab, jnp.zeros((B, GDN_HG, T, 128 - 2 * GDN_HB), F32)], axis=-1)


def kernel(x_prompt, x_sample, state_ssm, state_conv, cache_kv_g0, cache_kv_g1, cache_kv_g2, rel_bias, norm_mix, w_in, conv_w, a_log, dt_bias, gdn_norm, q_norm, k_norm, w_branch_a, w_branch_b, w_out, norm_ffn, peer_w_query, peer_subkeys, peer_u, peer_v):
    B, T, D = x_prompt.shape
    Bs = x_sample.shape[0]
    assert state_ssm.shape[0] == 1 and x_sample.shape[1] == 1
    caches = (cache_kv_g0[0], cache_kv_g1[0], cache_kv_g2[0])
    wa_bf = w_branch_a[0].astype(BF16)
    wb_bf = w_branch_b[0].astype(BF16)
    wo_bf = w_out[0].astype(BF16)
    wq_bf = peer_w_query[0].astype(BF16)
    u_bf = peer_u[0].astype(BF16)
    v_bf = peer_v[0].astype(BF16)
    cw = conv_w[0]

    xp = x_prompt.reshape(B * T, D)
    xs = x_sample.reshape(Bs, D)

    proj, proj_s = in_proj(rmsnorm_bf16(xp, norm_mix[0]), rmsnorm_bf16(xs, norm_mix[0]),
                           jnp.swapaxes(w_in[0], 0, 1))
    alog_g, dtb_g = _gdn_group_params(a_log[0], dt_bias[0])
    ya, ssm_p = gdn_prompt(proj.reshape(B, T, N_PROJ), _ab_grouped(proj, B, T), cw, alog_g, dtb_g,
                           gdn_norm[0])
    conv_p = proj.reshape(B, T, N_PROJ)[:, T - (CONV_W - 1):, :CONV_CH]
    qn, kv = qk_norm(proj, q_norm[0], k_norm[0], F32)
    kv3 = kv.reshape(B, T, N_GROUPS * 2 * GROUP_W)
    kvs = [kv3[:, :, gi * 2 * GROUP_W:(gi + 1) * 2 * GROUP_W] for gi in range(N_GROUPS)]
    bias_all = jnp.stack([_prompt_bias_matrix(rel_bias, gi) for gi in range(N_GROUPS)], 0)
    yb = attn_prompt(qn.reshape(B, T, ATTN_W), kv3, bias_all)
    merged = branch_merge(ya.reshape(B * T, GDN_V_W), yb, wa_bf, wb_bf, proj)
    x1 = matmul(merged, wo_bf, residual=xp, name="out_proj")
    p_kv = []
    for gi, (window, _) in enumerate(ATTN_GROUPS):
        keep = min(window, T)
        kvg = kvs[gi][:, T - keep:].reshape(B, keep, 2, GROUP_HEADS, HEAD_DIM)
        p_kv.append(kvg[None])

    ya_s, conv_s, ssm_s = gdn_step(proj_s, state_conv[0], state_ssm[0], cw, a_log[0], dt_bias[0],
                                   gdn_norm[0])
    qn_s, kv_s = qk_norm(proj_s, q_norm[0], k_norm[0], F32)
    kvs_s = [kv_s[:, gi * 2 * GROUP_W:(gi + 1) * 2 * GROUP_W] for gi in range(N_GROUPS)]
    yb_s = attn_decode(qn_s, kvs_s, caches, rel_bias)
    merged_s = branch_merge(ya_s.astype(BF16), yb_s.astype(BF16), wa_bf, wb_bf, proj_s)
    x1_s = matmul(merged_s, wo_bf, residual=xs, name="out_proj_s")
    y_prompt = peer_ffn(x1, norm_ffn[0], wq_bf, peer_subkeys[0], u_bf, v_bf).reshape(B, T, D)
    y_sample = peer_ffn(x1_s, norm_ffn[0], wq_bf, peer_subkeys[0], u_bf, v_bf).reshape(Bs, 1, D)
    s_kv =[kvs_s[gi].reshape(1, Bs, 1, 2, GROUP_HEADS, HEAD_DIM) for gi in range(N_GROUPS)]

    return (y_prompt, y_sample,
            ssm_p[None], conv_p[None], p_kv[0], p_kv[1], p_kv[2],
            ssm_s[None], conv_s[None], s_kv[0], s_kv[1], s_kv[2])
```

```python
import functools

import numpy as np
import jax
import jax.numpy as jnp
from jax import lax
from jax.experimental import pallas as pl
from jax.experimental.pallas import tpu as pltpu

F32 = jnp.float32
BF16 = jnp.bfloat16
HIGHEST = lax.Precision.HIGHEST

D_MODEL = 4096
GDN_HEADS = 16
GDN_DK = 128
GDN_DV = 128
GDN_QK_W = GDN_HEADS * GDN_DK
GDN_V_W = GDN_HEADS * GDN_DV
CONV_W = 4
CONV_CH = 2 * GDN_QK_W + GDN_V_W
GDN_CHUNK = 64
ATTN_GROUPS = ((128, 1), (512, 4), (2048, 16))
N_GROUPS = 3
GROUP_HEADS = 4
HEAD_DIM = 128
GROUP_W = GROUP_HEADS * HEAD_DIM
ATTN_W = N_GROUPS * GROUP_W
Q_BLOCK = 128
N_BUCKETS = 32
MAX_DISTANCE = 2048
PEER_HEADS = 8
N_KEYS = 128
PEER_TOPK = 16
EPS = 1e-6
NEG_INF = -1e30

OFF_QKV = 0
OFF_Z = CONV_CH
OFF_AQ = OFF_Z + GDN_V_W
OFF_AK = OFF_AQ + ATTN_W
OFF_AV = OFF_AK + ATTN_W
OFF_GA = OFF_AV + ATTN_W
OFF_GB = OFF_GA + D_MODEL
OFF_AB = OFF_GB + D_MODEL
N_PROJ = 21504

GDN_HB = 16
GDN_HG = GDN_HEADS // GDN_HB
VMEM_LIMIT = 56 * 1024 * 1024


def _cparams(sem):
    return pltpu.CompilerParams(dimension_semantics=sem, vmem_limit_bytes=VMEM_LIMIT)


def _sigmoid(x):
    return 0.5 * jnp.tanh(0.5 * x) + 0.5


def _silu(x):
    return x * _sigmoid(x)


def _softplus(x):
    return jnp.maximum(x, 0.0) + jnp.log(1.0 + jnp.exp(-jnp.abs(x)))


def _bdot(a, b):
    return jnp.dot(a.astype(BF16), b.astype(BF16), preferred_element_type=F32)


def _bdot_nt(a, b):
    return lax.dot_general(a.astype(BF16), b.astype(BF16), (((1,), (1,)), ((), ())),
                           preferred_element_type=F32)


def _bdot_tn(a, b):
    return lax.dot_general(a.astype(BF16), b.astype(BF16), (((0,), (0,)), ((), ())),
                           preferred_element_type=F32)


def _hdot(a, b):
    return jnp.dot(a, b, precision=HIGHEST, preferred_element_type=F32)


def _split_bf16(a):
    hi = a.astype(BF16)
    lo = (a - hi.astype(F32)).astype(BF16)
    return hi, lo


def _dot3(a_parts, b_parts):
    a_hi, a_lo = a_parts
    b_hi, b_lo = b_parts
    return (jnp.dot(a_hi, b_hi, preferred_element_type=F32)
            + jnp.dot(a_lo, b_hi, preferred_element_type=F32)
            + jnp.dot(a_hi, b_lo, preferred_element_type=F32))


def _rmsnorm_body(x_ref, g_ref, o_ref):
    x = x_ref[...]
    ms = jnp.mean(x * x, axis=-1, keepdims=True)
    o_ref[...] = (x * lax.rsqrt(ms + EPS) * g_ref[...]).astype(o_ref.dtype)


def rmsnorm_bf16(x, gain):
    M, D = x.shape
    tm = min(M, 512)
    return pl.pallas_call(
        _rmsnorm_body,
        grid=(M // tm,),
        in_specs=[pl.BlockSpec((tm, D), lambda i: (i, 0)),
                  pl.BlockSpec((1, D), lambda i: (0, 0))],
        out_specs=pl.BlockSpec((tm, D), lambda i: (i, 0)),
        out_shape=jax.ShapeDtypeStruct((M, D), BF16),
        compiler_params=_cparams(("parallel",)),
        name="rmsnorm",
    )(x, gain.reshape(1, D))


def _mm_body(a_ref, b_ref, o_ref):
    o_ref[...] = jnp.dot(a_ref[...], b_ref[...], preferred_element_type=F32)


def _mm_res_body(a_ref, b_ref, r_ref, o_ref):
    o_ref[...] = r_ref[...] + jnp.dot(a_ref[...], b_ref[...], preferred_element_type=F32)


def matmul(a, b, residual=None, tm=1024, tn=1024, name="matmul"):
    M, K = a.shape
    N = b.shape[1]
    tm = min(tm, M)
    tn = min(tn, N)
    in_specs = [pl.BlockSpec((tm, K), lambda i, j: (i, 0)),
                pl.BlockSpec((K, tn), lambda i, j: (0, j))]
    args = [a, b]
    body = _mm_body
    if residual is not None:
        in_specs.append(pl.BlockSpec((tm, tn), lambda i, j: (i, j)))
        args.append(residual)
        body = _mm_res_body
    return pl.pallas_call(
        body,
        grid=(M // tm, N // tn),
        in_specs=in_specs,
        out_specs=pl.BlockSpec((tm, tn), lambda i, j: (i, j)),
        out_shape=jax.ShapeDtypeStruct((M, N), F32),
        compiler_params=_cparams(("parallel", "parallel")),
        name=name,
    )(*args)


def _mm_nt2_body(a_ref, a2_ref, bt_ref, o_ref, o2_ref):
    dims = (((1,), (1,)), ((), ()))
    o_ref[...] = lax.dot_general(a_ref[...], bt_ref[...], dims, preferred_element_type=F32)

    @pl.when(pl.program_id(1) == 0)
    def _():
        o2_ref[...] = lax.dot_general(a2_ref[...], bt_ref[...], dims, preferred_element_type=F32)


def matmul_nt2(a, a2, bt, tm=1024, tn=1024, name="matmul_nt2"):
    M, K = a.shape
    M2 = a2.shape[0]
    N = bt.shape[0]
    tm = min(tm, M)
    return pl.pallas_call(
        _mm_nt2_body,
        grid=(N // tn, M // tm),
        in_specs=[pl.BlockSpec((tm, K), lambda j, i: (i, 0)),
                  pl.BlockSpec((M2, K), lambda j, i: (0, 0)),
                  pl.BlockSpec((tn, K), lambda j, i: (j, 0))],
        out_specs=[pl.BlockSpec((tm, tn), lambda j, i: (i, j)),
                   pl.BlockSpec((M2, tn), lambda j, i: (0, j))],
        out_shape=[jax.ShapeDtypeStruct((M, N), F32), jax.ShapeDtypeStruct((M2, N), F32)],
        compiler_params=_cparams(("parallel", "arbitrary")),
        name=name,
    )(a, a2, bt)


def _gdn_chunk_body(q_ref, k_ref, v_ref, z_ref, ab_ref, cwq_ref, cwk_ref, cwv_ref,
                    alog_ref, dtb_ref, gn_ref, o_ref, s_out_ref,
                    xq_s, xk_s, xv_s, state_s):
    n = pl.program_id(2)
    C = GDN_CHUNK
    HW = GDN_HB * GDN_DK

    @pl.when(n == 0)
    def _():
        xq_s[0:8, :] = jnp.zeros((8, HW), F32)
        xk_s[0:8, :] = jnp.zeros((8, HW), F32)
        xv_s[0:8, :] = jnp.zeros((8, HW), F32)
        state_s[...] = jnp.zeros(state_s.shape, F32)

    def conv(x_ref, xs, cw_ref):
        xs[8:8 + C, :] = x_ref[0]
        y = cw_ref[CONV_W - 1:CONV_W, :] * xs[8:8 + C, :]
        for w in range(CONV_W - 1):
            y = y + cw_ref[w:w + 1, :] * xs[5 + w:5 + w + C, :]
        xs[0:8, :] = xs[C:C + 8, :]
        return _silu(y)

    q_all = conv(q_ref, xq_s, cwq_ref)
    k_all = conv(k_ref, xk_s, cwk_ref)
    v_all = conv(v_ref, xv_s, cwv_ref)

    ab = ab_ref[0, 0]
    g_all = -jnp.exp(alog_ref[0]) * _softplus(ab + dtb_ref[0])
    beta_all = _sigmoid(ab)
    ri = lax.broadcasted_iota(jnp.int32, (C, C), 0)
    ci = lax.broadcasted_iota(jnp.int32, (C, C), 1)
    tri = ri >= ci
    strict = ri > ci
    rowid = lax.broadcasted_iota(jnp.int32, (C, 128), 0)
    gc_all = g_all
    shift = 1
    while shift < C:
        gc_all = gc_all + jnp.where(rowid >= shift, pltpu.roll(gc_all, shift, axis=0), 0.0)
        shift *= 2
    gc_t = jnp.concatenate([gc_all, jnp.zeros((128 - C, 128), F32)], axis=0).T

    H = range(GDN_HB)
    sls = [slice(j * GDN_DK, (j + 1) * GDN_DK) for j in H]
    qh, kh, kb, gcol, decay, rhs = [], [], [], [], [], []
    for j in H:
        q = q_all[:, sls[j]]
        k = k_all[:, sls[j]]
        q = q * lax.rsqrt(jnp.sum(q * q, axis=-1, keepdims=True) + EPS) * (GDN_DK ** -0.5)
        k = k * lax.rsqrt(jnp.sum(k * k, axis=-1, keepdims=True) + EPS)
        beta = beta_all[:, GDN_HB + j:GDN_HB + j + 1]
        gc = gc_all[:, j:j + 1]
        grow = gc_t[j:j + 1, :C]
        qh.append(q)
        kh.append(k)
        kb.append(k * beta)
        gcol.append(gc)
        decay.append(jnp.where(tri, jnp.exp(jnp.where(tri, gc - grow, 0.0)), 0.0))
        rhs.append(jnp.concatenate([v_all[:, sls[j]] * beta, kb[j] * jnp.exp(gc)], axis=-1))
    kq = [_bdot_nt(jnp.concatenate([kb[j], qh[j]], axis=0), kh[j]) for j in H]
    intra = [jnp.where(tri, kq[j][C:] * decay[j], 0.0) for j in H]
    acc = [-jnp.where(strict, kq[j][:C] * decay[j], 0.0) for j in H]
    parts = [_split_bf16(acc[j]) for j in H]
    pw = [_dot3(parts[j], parts[j]) for j in H]
    for r in range(5):
        pw_parts = [_split_bf16(pw[j]) for j in H]
        acc_parts = [_split_bf16(acc[j]) for j in H]
        acc = [acc[j] + pw[j] + _dot3(pw_parts[j], acc_parts[j]) for j in H]
        if r < 4:
            pw = [_dot3(pw_parts[j], pw_parts[j]) for j in H]
    sol = [rhs[j] + _dot3(_split_bf16(acc[j]), _split_bf16(rhs[j])) for j in H]
    S = [state_s[j] for j in H]
    wq_s = [_bdot(jnp.concatenate([sol[j][:, GDN_DV:], qh[j] * jnp.exp(gcol[j])], axis=0), S[j])
            for j in H]
    v_new = [sol[j][:, :GDN_DV] - wq_s[j][:C] for j in H]
    o = [wq_s[j][C:] + _bdot(intra[j], v_new[j]) for j in H]
    for j in H:
        g_last = gcol[j][C - 1:C, :]
        state_s[j] = (S[j] * jnp.exp(g_last)
                      + _bdot_tn(kh[j] * jnp.exp(g_last - gcol[j]), v_new[j]))
    for j in H:
        ms = jnp.mean(o[j] * o[j], axis=-1, keepdims=True)
        on = o[j] * lax.rsqrt(ms + EPS) * gn_ref[...]
        o_ref[0, :, sls[j]] = (on * _silu(z_ref[0, :, sls[j]])).astype(o_ref.dtype)

    @pl.when(n == pl.num_programs(2) - 1)
    def _():
        s_out_ref[0] = state_s[...]


def gdn_prompt(proj3, ab_g, conv_w, alog_g, dtb_g, gdn_norm):
    B, T, _ = proj3.shape
    C = GDN_CHUNK
    N = T // C
    HW = GDN_HB * GDN_DK
    nq = GDN_QK_W // HW

    def col(base):
        return lambda b, hg, n: (b, n, base + hg)

    def cwcol(base):
        return lambda b, hg, n: (0, base + hg)

    return pl.pallas_call(
        _gdn_chunk_body,
        grid=(B, GDN_HG, N),
        in_specs=[
            pl.BlockSpec((1, C, HW), col(0)),
            pl.BlockSpec((1, C, HW), col(nq)),
            pl.BlockSpec((1, C, HW), col(2 * nq)),
            pl.BlockSpec((1, C, HW), col(OFF_Z // HW)),
            pl.BlockSpec((1, 1, C, 128), lambda b, hg, n: (b, hg, n, 0)),
            pl.BlockSpec((CONV_W, HW), cwcol(0)),
            pl.BlockSpec((CONV_W, HW), cwcol(nq)),
            pl.BlockSpec((CONV_W, HW), cwcol(2 * nq)),
            pl.BlockSpec((1, 1, 128), lambda b, hg, n: (hg, 0, 0)),
            pl.BlockSpec((1, 1, 128), lambda b, hg, n: (hg, 0, 0)),
            pl.BlockSpec((1, GDN_DV), lambda b, hg, n: (0, 0)),
        ],
        out_specs=[
            pl.BlockSpec((1, C, HW), lambda b, hg, n: (b, n, hg)),
            pl.BlockSpec((1, GDN_HB, GDN_DK, GDN_DV), lambda b, hg, n: (b, hg, 0, 0)),
        ],
        out_shape=[
            jax.ShapeDtypeStruct((B, T, GDN_V_W), BF16),
            jax.ShapeDtypeStruct((B, GDN_HEADS, GDN_DK, GDN_DV), F32),
        ],
        scratch_shapes=[
            pltpu.VMEM((C + 8, HW), F32),
            pltpu.VMEM((C + 8, HW), F32),
            pltpu.VMEM((C + 8, HW), F32),
            pltpu.VMEM((GDN_HB, GDN_DK, GDN_DV), F32),
        ],
        compiler_params=_cparams(("parallel", "parallel", "arbitrary")),
        name="gdn_prompt",
    )(proj3, proj3, proj3, proj3, ab_g, conv_w, conv_w, conv_w, alog_g, dtb_g,
      gdn_norm.reshape(1, GDN_DV))


def _gdn_step_body(x_ref, z_ref, ab_ref, cs_ref, s_ref, cw_ref, alog_ref, dtb_ref, gn_ref,
                   o_ref, cs_out_ref, s_out_ref):
    x = x_ref[0]
    y = cw_ref[CONV_W - 1] * x
    for w in range(CONV_W - 1):
        y = y + cw_ref[w] * cs_ref[0, w]
    y = _silu(y)
    cs_out_ref[0, 0] = cs_ref[0, 1]
    cs_out_ref[0, 1] = cs_ref[0, 2]
    cs_out_ref[0, 2] = x
    r_ab = (OFF_AB // 128) % 8
    ab = ab_ref[0, r_ab:r_ab + 1, :]
    g_all = -jnp.exp(alog_ref[...]) * _softplus(ab + dtb_ref[...])
    beta_all = _sigmoid(ab)
    eye = (lax.broadcasted_iota(jnp.int32, (GDN_DK, GDN_DK), 0)
           == lax.broadcasted_iota(jnp.int32, (GDN_DK, GDN_DK), 1))

    def to_col(row):
        return jnp.sum(jnp.where(eye, jnp.broadcast_to(row, (GDN_DK, GDN_DK)), 0.0),
                       axis=1, keepdims=True)

    for h in range(GDN_HEADS):
        q = y[h:h + 1, :]
        k = y[GDN_HEADS + h:GDN_HEADS + h + 1, :]
        v = y[2 * GDN_HEADS + h:2 * GDN_HEADS + h + 1, :]
        q = q * lax.rsqrt(jnp.sum(q * q, axis=-1, keepdims=True) + EPS) * (GDN_DK ** -0.5)
        k = k * lax.rsqrt(jnp.sum(k * k, axis=-1, keepdims=True) + EPS)
        g = g_all[:, h:h + 1]
        beta = beta_all[:, GDN_HEADS + h:GDN_HEADS + h + 1]
        eg = jnp.exp(g)
        S = s_ref[0, h]
        kcol = to_col(k)
        qcol = to_col(q)
        k_s = jnp.sum(S * kcol, axis=0, keepdims=True)
        q_s = jnp.sum(S * qcol, axis=0, keepdims=True)
        v_new = v * beta - (beta * eg) * k_s
        o = eg * q_s + jnp.sum(q * k, axis=-1, keepdims=True) * v_new
        s_out_ref[0, h] = S * eg + kcol * v_new
        ms = jnp.mean(o * o, axis=-1, keepdims=True)
        on = o * lax.rsqrt(ms + EPS) * gn_ref[...]
        o_ref[0, h:h + 1, :] = on * _silu(z_ref[0, h:h + 1, :])


def gdn_step(proj_s, conv_state, ssm_state, conv_w, a_log, dt_bias, gdn_norm):
    Bs = proj_s.shape[0]
    nrow = N_PROJ // 128
    p3 = proj_s.reshape(Bs, nrow, 128)
    cs4 = conv_state.reshape(Bs, CONV_W - 1, CONV_CH // 128, 128)
    cw3 = conv_w.reshape(CONV_W, CONV_CH // 128, 128)
    pad = jnp.zeros((128 - GDN_HEADS,), F32)
    alog = jnp.concatenate([a_log, pad]).reshape(1, 128)
    dtb = jnp.concatenate([dt_bias, pad]).reshape(1, 128)
    nr = CONV_CH // 128
    o, cs_new, s_new = pl.pallas_call(
        _gdn_step_body,
        grid=(Bs,),
        in_specs=[
            pl.BlockSpec((1, nr, 128), lambda b: (b, 0, 0)),
            pl.BlockSpec((1, GDN_HEADS, 128), lambda b: (b, OFF_Z // 128 // GDN_HEADS, 0)),
            pl.BlockSpec((1, 8, 128), lambda b: (b, OFF_AB // 128 // 8, 0)),
            pl.BlockSpec((1, CONV_W - 1, nr, 128), lambda b: (b, 0, 0, 0)),
            pl.BlockSpec((1, GDN_HEADS, GDN_DK, GDN_DV), lambda b: (b, 0, 0, 0)),
            pl.BlockSpec((CONV_W, nr, 128), lambda b: (0, 0, 0)),
            pl.BlockSpec((1, 128), lambda b: (0, 0)),
            pl.BlockSpec((1, 128), lambda b: (0, 0)),
            pl.BlockSpec((1, GDN_DV), lambda b: (0, 0)),
        ],
        out_specs=[
            pl.BlockSpec((1, GDN_HEADS, GDN_DV), lambda b: (b, 0, 0)),
            pl.BlockSpec((1, CONV_W - 1, nr, 128), lambda b: (b, 0, 0, 0)),
            pl.BlockSpec((1, GDN_HEADS, GDN_DK, GDN_DV), lambda b: (b, 0, 0, 0)),
        ],
        out_shape=[
            jax.ShapeDtypeStruct((Bs, GDN_HEADS, GDN_DV), F32),
            jax.ShapeDtypeStruct((Bs, CONV_W - 1, nr, 128), F32),
            jax.ShapeDtypeStruct((Bs, GDN_HEADS, GDN_DK, GDN_DV), F32),
        ],
        compiler_params=_cparams(("parallel",)),
        name="gdn_step",
    )(p3, p3, p3, cs4, ssm_state, cw3, alog, dtb, gdn_norm.reshape(1, GDN_DV))
    return (o.reshape(Bs, GDN_V_W), cs_new.reshape(Bs, CONV_W - 1, CONV_CH), s_new)


def _qknorm_body(q_ref, k_ref, v_ref, qg_ref, kg_ref, qn_ref, kv_ref):
    for h in range(GROUP_HEADS):
        sl = slice(h * HEAD_DIM, (h + 1) * HEAD_DIM)
        q = q_ref[:, sl]
        qn = q * lax.rsqrt(jnp.mean(q * q, axis=-1, keepdims=True) + EPS) * qg_ref[...]
        qn_ref[:, sl] = (qn * (HEAD_DIM ** -0.5)).astype(qn_ref.dtype)
        k = k_ref[:, sl]
        kv_ref[:, sl] = k * lax.rsqrt(jnp.mean(k * k, axis=-1, keepdims=True) + EPS) * kg_ref[...]
    kv_ref[:, GROUP_W:] = v_ref[...]


def qk_norm(proj, q_norm, k_norm, q_dtype):
    M = proj.shape[0]
    tm = min(M, 512)
    blk = GROUP_W
    return pl.pallas_call(
        _qknorm_body,
        grid=(M // tm, N_GROUPS),
        in_specs=[
            pl.BlockSpec((tm, blk), lambda i, g: (i, OFF_AQ // blk + g)),
            pl.BlockSpec((tm, blk), lambda i, g: (i, OFF_AK // blk + g)),
            pl.BlockSpec((tm, blk), lambda i, g: (i, OFF_AV // blk + g)),
            pl.BlockSpec((1, HEAD_DIM), lambda i, g: (0, 0)),
            pl.BlockSpec((1, HEAD_DIM), lambda i, g: (0, 0)),
        ],
        out_specs=[
            pl.BlockSpec((tm, blk), lambda i, g: (i, g)),
            pl.BlockSpec((tm, 2 * blk), lambda i, g: (i, g)),
        ],
        out_shape=[jax.ShapeDtypeStruct((M, ATTN_W), q_dtype),
                   jax.ShapeDtypeStruct((M, N_GROUPS * 2 * GROUP_W), F32)],
        compiler_params=_cparams(("parallel", "parallel")),
        name="qk_norm",
    )(proj, proj, proj, q_norm.reshape(1, HEAD_DIM), k_norm.reshape(1, HEAD_DIM))


def _t5_bucket(dist):
    max_exact = N_BUCKETS // 2
    d = np.maximum(dist, max_exact).astype(np.float32)
    large = max_exact + (np.log(d / max_exact) / np.log(MAX_DISTANCE / max_exact)
                         * (N_BUCKETS - max_exact)).astype(np.int32)
    large = np.minimum(large, N_BUCKETS - 1)
    return np.where(dist < max_exact, dist, large).astype(np.int32)


def _group_bias(rel_bias, gi):
    window, dilation = ATTN_GROUPS[gi]
    n_keys = window // dilation + 1
    buckets = _t5_bucket(np.arange(n_keys) * dilation)
    return rel_bias[buckets][:, gi * GROUP_HEADS:(gi + 1) * GROUP_HEADS].T.astype(F32)


def _attn_prompt_body(T, q0_ref, q1_ref, q2_ref, k0_ref, v0_ref, k1_ref, v1_ref, k2_ref, v2_ref,
                      bias_ref, y_ref, m_s, l_s, o_s):
    QB = Q_BLOCK
    q_refs = (q0_ref, q1_ref, q2_ref)
    k_refs = (k0_ref, k1_ref, k2_ref)
    v_refs = (v0_ref, v1_ref, v2_ref)
    for gi, (window, d) in enumerate(ATTN_GROUPS):
        nq = T // (d * QB)
        for r in range(d):
            for qi in range(nq):
                rows = pl.ds(r + d * QB * qi, QB, stride=d) if d > 1 else pl.ds(QB * qi, QB)
                q = q_refs[gi][0, rows, :].astype(BF16)
                kc = k_refs[gi][0, rows, :]
                vc = v_refs[gi][0, rows, :]
                if qi > 0:
                    prev = (pl.ds(r + d * QB * (qi - 1), QB, stride=d) if d > 1
                            else pl.ds(QB * (qi - 1), QB))
                    kcat = jnp.concatenate([k_refs[gi][0, prev, :], kc], axis=0).astype(BF16)
                    vcat = jnp.concatenate([v_refs[gi][0, prev, :], vc], axis=0).astype(BF16)
                    bias = bias_ref[gi, 0]
                else:
                    kcat = kc.astype(BF16)
                    vcat = vc.astype(BF16)
                    bias = bias_ref[gi, 0, :, QB:]
                s = lax.dot_general(q, kcat, (((1,), (1,)), ((), ())),
                                    preferred_element_type=F32) + bias
                m_blk = jnp.max(s, axis=-1, keepdims=True)
                if gi == 0:
                    m_new = m_blk
                else:
                    m_old = m_s[rows, :][:, 0:1]
                    m_new = jnp.maximum(m_old, m_blk)
                p = jnp.exp(s - m_new)
                l_new = jnp.sum(p, axis=-1, keepdims=True)
                o_new = jnp.dot(p.astype(BF16), vcat, preferred_element_type=F32)
                if gi > 0:
                    alpha = jnp.exp(m_old - m_new)
                    l_new = l_new + alpha * l_s[rows, :][:, 0:1]
                    o_new = o_new + alpha * o_s[rows, :]
                m_s[rows, :] = jnp.broadcast_to(m_new, (QB, HEAD_DIM))
                l_s[rows, :] = jnp.broadcast_to(l_new, (QB, HEAD_DIM))
                o_s[rows, :] = o_new
    y_ref[0] = (o_s[...] / l_s[...]).astype(y_ref.dtype)


def attn_prompt(qn3, kv3, bias_all):
    B, T, _ = qn3.shape
    blk = (1, T, HEAD_DIM)

    def qspec(gi):
        return pl.BlockSpec(blk, lambda b, h: (b, 0, gi * GROUP_HEADS + h))

    def kvspec(gi, is_v):
        return pl.BlockSpec(blk, lambda b, h: (b, 0, (2 * gi + is_v) * GROUP_HEADS + h))

    in_specs = [qspec(0), qspec(1), qspec(2)]
    for gi in range(N_GROUPS):
        in_specs += [kvspec(gi, 0), kvspec(gi, 1)]
    in_specs.append(pl.BlockSpec((N_GROUPS, 1, Q_BLOCK, 2 * Q_BLOCK), lambda b, h: (0, h, 0, 0)))
    y = pl.pallas_call(
        functools.partial(_attn_prompt_body, T),
        grid=(B, GROUP_HEADS),
        in_specs=in_specs,
        out_specs=pl.BlockSpec(blk, lambda b, h: (b, 0, h)),
        out_shape=jax.ShapeDtypeStruct((B, T, GROUP_W), BF16),
        scratch_shapes=[pltpu.VMEM((T, HEAD_DIM), F32), pltpu.VMEM((T, HEAD_DIM), F32),
                        pltpu.VMEM((T, HEAD_DIM), F32)],
        compiler_params=_cparams(("parallel", "parallel")),
        name="attn_prompt",
    )(qn3, qn3, qn3, kv3, kv3, kv3, kv3, kv3, kv3, bias_all)
    return y.reshape(B * T, GROUP_W)


def _prompt_bias_matrix(rel_bias, gi):
    QB = Q_BLOCK
    bias = _group_bias(rel_bias, gi)
    row0 = jnp.concatenate([bias[:, ::-1], jnp.full((GROUP_HEADS, QB), NEG_INF, F32)], axis=1)
    flat = jnp.tile(row0, (1, QB))[:, :QB * 2 * QB]
    return flat.reshape(GROUP_HEADS, QB, 2 * QB)


def _attn_decode_body(q_ref, n0_ref, n1_ref, n2_ref, c0_ref, c1_ref, c2_ref, bc_ref, bn_ref, y_ref):
    new_refs = (n0_ref, n1_ref, n2_ref)
    cache_refs = (c0_ref, c1_ref, c2_ref)
    outs, lds = [], []
    for gi in range(N_GROUPS):
        q = q_ref[0, gi * GROUP_HEADS:(gi + 1) * GROUP_HEADS, :]
        kc = cache_refs[gi][:, 0]
        vc = cache_refs[gi][:, 1]
        kn = new_refs[gi][0, 0:GROUP_HEADS, :]
        vn = new_refs[gi][0, GROUP_HEADS:2 * GROUP_HEADS, :]
        s_c = jnp.sum(kc * q[None], axis=-1, keepdims=True) + bc_ref[gi, :, :, 0:1]
        s_n = jnp.sum(kn * q, axis=-1, keepdims=True) + bn_ref[gi, :, 0:1]
        m = jnp.maximum(jnp.max(s_c, axis=0), s_n)
        p_c = jnp.exp(s_c - m[None])
        p_n = jnp.exp(s_n - m)
        l = jnp.sum(p_c, axis=0) + p_n
        outs.append((jnp.sum(p_c * vc, axis=0) + p_n * vn) / l)
        lds.append(m + jnp.log(l))
    m = jnp.maximum(jnp.maximum(lds[0], lds[1]), lds[2])
    es = [jnp.exp(x - m) for x in lds]
    den = es[0] + es[1] + es[2]
    y_ref[0] = outs[0] * (es[0] / den) + outs[1] * (es[1] / den) + outs[2] * (es[2] / den)


def attn_decode(qn_s, kv_new, caches, rel_bias):
    Bs = qn_s.shape[0]
    q3 = qn_s.reshape(Bs, N_GROUPS * GROUP_HEADS, HEAD_DIM)
    new3 = [x.reshape(Bs, 2 * GROUP_HEADS, HEAD_DIM) for x in kv_new]
    cviews, bcs, bns = [], [], []
    for gi, (window, d) in enumerate(ATTN_GROUPS):
        nk = window // d
        assert caches[gi].shape[1] == window and nk == Q_BLOCK
        cviews.append(caches[gi].reshape(Bs, nk, d, 2, GROUP_HEADS, HEAD_DIM))
        bias = _group_bias(rel_bias, gi)
        bc = bias[:, nk - np.arange(nk)]
        bcs.append(jnp.broadcast_to(bc.T[:, :, None], (nk, GROUP_HEADS, HEAD_DIM)))
        bns.append(jnp.broadcast_to(bias[:, 0:1], (GROUP_HEADS, HEAD_DIM)))
    bc_all = jnp.stack(bcs, 0)
    bn_all = jnp.stack(bns, 0)
    nspec = pl.BlockSpec((1, 2 * GROUP_HEADS, HEAD_DIM), lambda b: (b, 0, 0))
    cspec = pl.BlockSpec((None, Q_BLOCK, None, 2, GROUP_HEADS, HEAD_DIM),
                         lambda b: (b, 0, 0, 0, 0, 0))
    y = pl.pallas_call(
        _attn_decode_body,
        grid=(Bs,),
        in_specs=[
            pl.BlockSpec((1, N_GROUPS * GROUP_HEADS, HEAD_DIM), lambda b: (b, 0, 0)),
            nspec, nspec, nspec, cspec, cspec, cspec,
            pl.BlockSpec((N_GROUPS, Q_BLOCK, GROUP_HEADS, HEAD_DIM), lambda b: (0, 0, 0, 0)),
            pl.BlockSpec((N_GROUPS, GROUP_HEADS, HEAD_DIM), lambda b: (0, 0, 0)),
        ],
        out_specs=pl.BlockSpec((1, GROUP_HEADS, HEAD_DIM), lambda b: (b, 0, 0)),
        out_shape=jax.ShapeDtypeStruct((Bs, GROUP_HEADS, HEAD_DIM), F32),
        compiler_params=_cparams(("parallel",)),
        name="attn_decode",
    )(q3, *new3, *cviews, bc_all, bn_all)
    return y.reshape(Bs, GROUP_W)


def _merge_body(ya_ref, yb_ref, wa_ref, wb_ref, ga_ref, gb_ref, o_ref):
    pa = jnp.dot(ya_ref[...], wa_ref[...], preferred_element_type=F32)
    pb = jnp.dot(yb_ref[...], wb_ref[...], preferred_element_type=F32)
    o_ref[...] = (_sigmoid(ga_ref[...]) * pa + _sigmoid(gb_ref[...]) * pb).astype(o_ref.dtype)


def branch_merge(ya, yb, wa, wb, proj):
    M = ya.shape[0]
    tm = min(M, 1024)
    tn = 512
    return pl.pallas_call(
        _merge_body,
        grid=(M // tm, D_MODEL // tn),
        in_specs=[
            pl.BlockSpec((tm, GDN_V_W), lambda i, j: (i, 0)),
            pl.BlockSpec((tm, GROUP_W), lambda i, j: (i, 0)),
            pl.BlockSpec((GDN_V_W, tn), lambda i, j: (0, j)),
            pl.BlockSpec((GROUP_W, tn), lambda i, j: (0, j)),
            pl.BlockSpec((tm, tn), lambda i, j: (i, OFF_GA // tn + j)),
            pl.BlockSpec((tm, tn), lambda i, j: (i, OFF_GB // tn + j)),
        ],
        out_specs=pl.BlockSpec((tm, tn), lambda i, j: (i, j)),
        out_shape=jax.ShapeDtypeStruct((M, D_MODEL), BF16),
        compiler_params=_cparams(("parallel", "parallel")),
        name="branch_merge",
    )(ya, yb, wa, wb, proj, proj)


def _extract_top(x, count, tie_safe):
    R = x.shape[0]
    rid = lax.broadcasted_iota(jnp.int32, x.shape, 0).astype(F32)
    vid = lax.broadcasted_iota(jnp.int32, (count, x.shape[1]), 0)
    vals = jnp.zeros((count, x.shape[1]), F32)
    rank = jnp.full(x.shape, float(count), F32)
    for a in range(count):
        m = jnp.max(x, axis=0, keepdims=True)
        hit = x == m
        if tie_safe:
            first = jnp.min(jnp.where(hit, rid, float(R)), axis=0, keepdims=True)
            hit = rid == first
        x = jnp.where(hit, -jnp.inf, x)
        rank = jnp.where(hit, float(a), rank)
        vals = jnp.where(vid == a, m, vals)
    return vals, rank


CAND_HALF = PEER_TOPK // 2
N_CAND = PEER_TOPK + (CAND_HALF - 1) * CAND_HALF + CAND_HALF


def _peer_topk_body(q_ref, sk_ref, cnt_ref, r2_ref, e1_ref, e2_ref, cand_s, s1_s, s2_s):
    K = PEER_TOPK
    KH = CAND_HALF

    def select(h, s1, s2, tie_safe):
        v1, rank1 = _extract_top(s1, K, tie_safe)
        v2, rank2 = _extract_top(s2, K, tie_safe)
        cand_s[0:K, :] = v2 + v1[0:1, :]
        for a in range(1, KH):
            cand_s[K + (a - 1) * KH:K + a * KH, :] = v2[0:KH, :] + v1[a:a + 1, :]
        cand_s[N_CAND - KH:N_CAND, :] = v1[KH:K, :] + v2[0:1, :]
        best, crank = _extract_top(cand_s[...], K, tie_safe)
        picked = jnp.where(crank < float(K), 1.0, 0.0)
        vid = lax.broadcasted_iota(jnp.int32, (KH, picked.shape[1]), 0)
        low = jnp.where(vid == 0, jnp.sum(picked[0:K], axis=0, keepdims=True), 0.0)
        for a in range(1, KH):
            rows = picked[K + (a - 1) * KH:K + a * KH]
            low = jnp.where(vid == a, jnp.sum(rows, axis=0, keepdims=True), low)
        count = jnp.concatenate([low, picked[N_CAND - KH:N_CAND]], axis=0)
        cnt1 = jnp.zeros(rank1.shape, F32)
        for a in range(K):
            cnt1 = jnp.where(rank1 == float(a), count[a:a + 1, :], cnt1)
        cmax = best[0:1, :]
        z = jnp.sum(jnp.exp(best - cmax), axis=0, keepdims=True)
        in1 = jnp.where(rank1 < float(K), 1.0, 0.0)
        in2 = jnp.where(rank2 < float(K), 1.0, 0.0)
        cnt_ref[h] = cnt1
        r2_ref[h] = rank2
        e1_ref[h] = in1 * jnp.exp(s1 - v1[0:1, :])
        e2_ref[h] = in2 * jnp.exp(s2 - v2[0:1, :]) / z * 0.5
        return (jnp.sum(in1, axis=0, keepdims=True) + jnp.sum(in2, axis=0, keepdims=True)
                + jnp.sum(picked, axis=0, keepdims=True) - 3.0 * K)

    def head(h, carry):
        off = pl.multiple_of(h * (2 * N_KEYS), 2 * N_KEYS)
        q1 = q_ref[:, pl.ds(off, N_KEYS)]
        q2 = q_ref[:, pl.ds(pl.multiple_of(off + N_KEYS, N_KEYS), N_KEYS)]
        s1_s[...] = lax.dot_general(sk_ref[0], q1, (((1,), (1,)), ((), ())), precision=HIGHEST,
                                    preferred_element_type=F32)
        s2_s[...] = lax.dot_general(sk_ref[1], q2, (((1,), (1,)), ((), ())), precision=HIGHEST,
                                    preferred_element_type=F32)
        extra = select(h, s1_s[...], s2_s[...], tie_safe=False)

        @pl.when(jnp.max(extra) > 0.0)
        def _():
            select(h, s1_s[...], s2_s[...], tie_safe=True)

        return carry

    lax.fori_loop(0, PEER_HEADS, head, 0)


def peer_topk(query, subkeys):
    M = query.shape[0]
    tm = min(M, 512)
    big = jax.ShapeDtypeStruct((PEER_HEADS, N_KEYS, M), F32)
    bspec = pl.BlockSpec((PEER_HEADS, N_KEYS, tm), lambda i: (0, 0, i))
    return pl.pallas_call(
        _peer_topk_body,
        grid=(M // tm,),
        in_specs=[pl.BlockSpec((tm, PEER_HEADS * 2 * N_KEYS), lambda i: (i, 0)),
                  pl.BlockSpec((2, N_KEYS, N_KEYS), lambda i: (0, 0, 0))],
        out_specs=[bspec, bspec, bspec, bspec],
        out_shape=[big, big, big, big],
        scratch_shapes=[pltpu.VMEM((N_CAND, tm), F32), pltpu.VMEM((N_KEYS, tm), F32),
                        pltpu.VMEM((N_KEYS, tm), F32)],
        compiler_params=_cparams(("parallel",)),
        name="peer_topk",
    )(query, subkeys)


def _gelu_x2(x):
    return x * (1.0 + lax.erf(x * (2.0 ** -0.5)))


GATE_ROWS = 8


def _peer_gate_body(hf_ref, u_ref, cnt_ref, r2_ref, e1_ref, e2_ref, w_ref, g_s):
    j = pl.program_id(1)
    tm = hf_ref.shape[0]
    grp = pl.multiple_of(j * GATE_ROWS, GATE_ROWS)

    for kk in range(GATE_ROWS):
        for tc in range(tm // 128):
            ts = slice(tc * 128, (tc + 1) * 128)
            acc = None
            for h in range(PEER_HEADS):
                cnt_row = cnt_ref[h, pl.ds(grp, GATE_ROWS), ts][kk:kk + 1, :]
                e1_row = e1_ref[h, pl.ds(grp, GATE_ROWS), ts][kk:kk + 1, :]
                term = jnp.where(r2_ref[h, :, ts] < cnt_row, e2_ref[h, :, ts] * e1_row, 0.0)
                acc = term if acc is None else acc + term
            g_s[ts, kk * N_KEYS:(kk + 1) * N_KEYS] = acc.T
    hmat = lax.dot_general(hf_ref[...], u_ref[...], (((1,), (1,)), ((), ())),
                           preferred_element_type=F32)
    w_ref[...] = (g_s[...] * _gelu_x2(hmat)).astype(w_ref.dtype)


def peer_gate_weights(hf, u_tab, cnt, r2, e1, e2, tm=1024):
    M = hf.shape[0]
    tm = min(M, tm)
    tn = GATE_ROWS * N_KEYS
    E = u_tab.shape[0]
    once = pl.Buffered(1)
    sspec = pl.BlockSpec((PEER_HEADS, N_KEYS, tm), lambda i, j: (0, 0, i), pipeline_mode=once)
    return pl.pallas_call(
        _peer_gate_body,
        grid=(M // tm, E // tn),
        in_specs=[
            pl.BlockSpec((tm, D_MODEL), lambda i, j: (i, 0), pipeline_mode=once),
            pl.BlockSpec((tn, D_MODEL), lambda i, j: (j, 0)),
            sspec, sspec, sspec, sspec,
        ],
        out_specs=pl.BlockSpec((tm, tn), lambda i, j: (i, j)),
        out_shape=jax.ShapeDtypeStruct((M, E), BF16),
        scratch_shapes=[pltpu.VMEM((tm, tn), F32)],
        compiler_params=_cparams(("parallel", "arbitrary")),
        name="peer_gate",
    )(hf, u_tab, cnt, r2, e1, e2)


def _mm_kres_body(a_ref, b_ref, r_ref, o_ref):
    k = pl.program_id(2)

    @pl.when(k == 0)
    def _():
        o_ref[...] = r_ref[...]

    o_ref[...] += jnp.dot(a_ref[...], b_ref[...], preferred_element_type=F32)


def matmul_kres(a, b, residual, tm=1024, tn=1024, tk=4096, name="matmul_k"):
    M, K = a.shape
    N = b.shape[1]
    tm = min(tm, M)
    return pl.pallas_call(
        _mm_kres_body,
        grid=(M // tm, N // tn, K // tk),
        in_specs=[pl.BlockSpec((tm, tk), lambda i, j, k: (i, k)),
                  pl.BlockSpec((tk, tn), lambda i, j, k: (k, j)),
                  pl.BlockSpec((tm, tn), lambda i, j, k: (i, j))],
        out_specs=pl.BlockSpec((tm, tn), lambda i, j, k: (i, j)),
        out_shape=jax.ShapeDtypeStruct((M, N), F32),
        compiler_params=_cparams(("parallel", "parallel", "arbitrary")),
        name=name,
    )(a, b, residual)


def peer_ffn(x1, norm_ffn, wq_bf, subkeys, u_bf, v_bf):
    M = x1.shape[0]
    Mp = max(M, 128)
    if Mp != M:
        x1 = jnp.concatenate([x1, jnp.zeros((Mp - M, D_MODEL), F32)], axis=0)
    hf = rmsnorm_bf16(x1, norm_ffn)
    query = matmul(hf, wq_bf, name="peer_query")
    cnt, r2, e1, e2 = peer_topk(query, subkeys)
    w = peer_gate_weights(hf, u_bf, cnt, r2, e1, e2)
    return matmul_kres(w, v_bf, x1, name="peer_out")[:M]


W_BLK = 512
N_AB = 2 * GDN_HEADS
J_AB = OFF_AB // W_BLK


def _prep_w_in_body(a_ref, b_ref, o_ref):
    j = pl.program_id(0)

    @pl.when(j < OFF_AQ // W_BLK)
    def _():
        o_ref[...] = a_ref[...].astype(o_ref.dtype)

    @pl.when(jnp.logical_and(j >= OFF_AQ // W_BLK, j < J_AB))
    def _():
        o_ref[0:W_BLK - N_AB, :] = a_ref[N_AB:, :].astype(o_ref.dtype)
        o_ref[W_BLK - N_AB:, :] = b_ref[...].astype(o_ref.dtype)

    @pl.when(j == J_AB)
    def _():
        o_ref[0:N_AB, :] = b_ref[...].astype(o_ref.dtype)
        o_ref[N_AB:, :] = jnp.zeros((W_BLK - N_AB, o_ref.shape[1]), o_ref.dtype)


def _prep_w_in(w_in_t):
    D = w_in_t.shape[1]
    j0 = OFF_AQ // W_BLK
    sub = W_BLK // N_AB
    return pl.pallas_call(
        _prep_w_in_body,
        grid=(N_PROJ // W_BLK,),
        in_specs=[
            pl.BlockSpec((W_BLK, D), lambda j: (jnp.where(j == J_AB, j0, j), 0)),
            pl.BlockSpec((N_AB, D), lambda j: (jnp.where(j == J_AB, j0 * sub,
                                                         jnp.where(j >= j0, (j + 1) * sub, 0)), 0)),
        ],
        out_specs=pl.BlockSpec((W_BLK, D), lambda j: (j, 0)),
        out_shape=jax.ShapeDtypeStruct((N_PROJ, D), BF16),
        compiler_params=_cparams(("parallel",)),
        name="prep_w_in",
    )(w_in_t, w_in_t)


def _gdn_group_params(a_log, dt_bias):
    pad = jnp.zeros((GDN_HG, 128 - GDN_HB), F32)
    alog = jnp.concatenate([a_log.reshape(GDN_HG, GDN_HB), pad], axis=1).reshape(GDN_HG, 1, 128)
    dtb = jnp.concatenate([dt_bias.reshape(GDN_HG, GDN_HB), pad], axis=1).reshape(GDN_HG, 1, 128)
    return alog, dtb


def _ab_grouped(proj, B, T):
    ab = proj[:, OFF_AB:OFF_AB + 2 * GDN_HEADS].reshape(B, T, 2, GDN_HG, GDN_HB)
    ab = ab.transpose(0, 3, 1, 2, 4).reshape(B, GDN_HG, T, 2 * GDN_HB)
    return jnp.concatenate([ab, jnp.zeros((B, GDN_HG, T, 128 - 2 * GDN_HB), F32)], axis=-1)


def kernel(x_prompt, x_sample, state_ssm, state_conv, cache_kv_g0, cache_kv_g1, cache_kv_g2, rel_bias, norm_mix, w_in, conv_w, a_log, dt_bias, gdn_norm, q_norm, k_norm, w_branch_a, w_branch_b, w_out, norm_ffn, peer_w_query, peer_subkeys, peer_u, peer_v):
    B, T, D = x_prompt.shape
    Bs = x_sample.shape[0]
    assert state_ssm.shape[0] == 1 and x_sample.shape[1] == 1
    caches = (cache_kv_g0[0], cache_kv_g1[0], cache_kv_g2[0])
    w_cat_t = _prep_w_in(jnp.swapaxes(w_in[0], 0, 1))
    wa_bf = w_branch_a[0].astype(BF16)
    wb_bf = w_branch_b[0].astype(BF16)
    wo_bf = w_out[0].astype(BF16)
    wq_bf = peer_w_query[0].astype(BF16)
    u_bf = peer_u[0].astype(BF16)
    v_bf = peer_v[0].astype(BF16)
    cw = conv_w[0]

    xp = x_prompt.reshape(B * T, D)
    xs = x_sample.reshape(Bs, D)

    proj, proj_s = matmul_nt2(rmsnorm_bf16(xp, norm_mix[0]), rmsnorm_bf16(xs, norm_mix[0]),
                              w_cat_t, name="in_proj")
    alog_g, dtb_g = _gdn_group_params(a_log[0], dt_bias[0])
    ya, ssm_p = gdn_prompt(proj.reshape(B, T, N_PROJ), _ab_grouped(proj, B, T), cw, alog_g, dtb_g,
                           gdn_norm[0])
    conv_p = proj.reshape(B, T, N_PROJ)[:, T - (CONV_W - 1):, :CONV_CH]
    qn, kv = qk_norm(proj, q_norm[0], k_norm[0], F32)
    kv3 = kv.reshape(B, T, N_GROUPS * 2 * GROUP_W)
    kvs = [kv3[:, :, gi * 2 * GROUP_W:(gi + 1) * 2 * GROUP_W] for gi in range(N_GROUPS)]
    bias_all = jnp.stack([_prompt_bias_matrix(rel_bias, gi) for gi in range(N_GROUPS)], 0)
    yb = attn_prompt(qn.reshape(B, T, ATTN_W), kv3, bias_all)
    merged = branch_merge(ya.reshape(B * T, GDN_V_W), yb, wa_bf, wb_bf, proj)
    x1 = matmul(merged, wo_bf, residual=xp, name="out_proj")
    p_kv = []
    for gi, (window, _) in enumerate(ATTN_GROUPS):
        keep = min(window, T)
        kvg = kvs[gi][:, T - keep:].reshape(B, keep, 2, GROUP_HEADS, HEAD_DIM)
        p_kv.append(kvg[None])

    ya_s, conv_s, ssm_s = gdn_step(proj_s, state_conv[0], state_ssm[0], cw, a_log[0], dt_bias[0],
                                   gdn_norm[0])
    qn_s, kv_s = qk_norm(proj_s, q_norm[0], k_norm[0], F32)
    kvs_s = [kv_s[:, gi * 2 * GROUP_W:(gi + 1) * 2 * GROUP_W] for gi in range(N_GROUPS)]
    yb_s = attn_decode(qn_s, kvs_s, caches, rel_bias)
    merged_s = branch_merge(ya_s.astype(BF16), yb_s.astype(BF16), wa_bf, wb_bf, proj_s)
    x1_s = matmul(merged_s, wo_bf, residual=xs, name="out_proj_s")
    y_prompt = peer_ffn(x1, norm_ffn[0], wq_bf, peer_subkeys[0], u_bf, v_bf).reshape(B, T, D)
    y_sample = peer_ffn(x1_s, norm_ffn[0], wq_bf, peer_subkeys[0], u_bf, v_bf).reshape(Bs, 1, D)
    s_kv =[kvs_s[gi].reshape(1, Bs, 1, 2, GROUP_HEADS, HEAD_DIM) for gi in range(N_GROUPS)]

    return (y_prompt, y_sample,
            ssm_p[None], conv_p[None], p_kv[0], p_kv[1], p_kv[2],
            ssm_s[None], conv_s[None], s_kv[0], s_kv[1], s_kv[2])
```

```python
import functools

import numpy as np
import jax
import jax.numpy as jnp
from jax import lax
from jax.experimental import pallas as pl
from jax.experimental.pallas import tpu as pltpu

F32 = jnp.float32
BF16 = jnp.bfloat16
HIGHEST = lax.Precision.HIGHEST

D_MODEL = 4096
GDN_HEADS = 16
GDN_DK = 128
GDN_DV = 128
GDN_QK_W = GDN_HEADS * GDN_DK
GDN_V_W = GDN_HEADS * GDN_DV
CONV_W = 4
CONV_CH = 2 * GDN_QK_W + GDN_V_W
GDN_CHUNK = 64
ATTN_GROUPS = ((128, 1), (512, 4), (2048, 16))
N_GROUPS = 3
GROUP_HEADS = 4
HEAD_DIM = 128
GROUP_W = GROUP_HEADS * HEAD_DIM
ATTN_W = N_GROUPS * GROUP_W
Q_BLOCK = 128
N_BUCKETS = 32
MAX_DISTANCE = 2048
PEER_HEADS = 8
N_KEYS = 128
PEER_TOPK = 16
EPS = 1e-6
NEG_INF = -1e30

OFF_QKV = 0
OFF_Z = CONV_CH
OFF_AQ = OFF_Z + GDN_V_W
OFF_AK = OFF_AQ + ATTN_W
OFF_AV = OFF_AK + ATTN_W
OFF_GA = OFF_AV + ATTN_W
OFF_GB = OFF_GA + D_MODEL
OFF_AB = OFF_GB + D_MODEL
N_PROJ = 21504

GDN_HB = 16
GDN_HG = GDN_HEADS // GDN_HB
VMEM_LIMIT = 56 * 1024 * 1024


def _cparams(sem):
    return pltpu.CompilerParams(dimension_semantics=sem, vmem_limit_bytes=VMEM_LIMIT)


def _sigmoid(x):
    return 0.5 * jnp.tanh(0.5 * x) + 0.5


def _silu(x):
    return x * _sigmoid(x)


def _softplus(x):
    return jnp.maximum(x, 0.0) + jnp.log(1.0 + jnp.exp(-jnp.abs(x)))


def _bdot(a, b):
    return jnp.dot(a.astype(BF16), b.astype(BF16), preferred_element_type=F32)


def _bdot_nt(a, b):
    return lax.dot_general(a.astype(BF16), b.astype(BF16), (((1,), (1,)), ((), ())),
                           preferred_element_type=F32)


def _bdot_tn(a, b):
    return lax.dot_general(a.astype(BF16), b.astype(BF16), (((0,), (0,)), ((), ())),
                           preferred_element_type=F32)


def _hdot(a, b):
    return jnp.dot(a, b, precision=HIGHEST, preferred_element_type=F32)


def _split_bf16(a):
    hi = a.astype(BF16)
    lo = (a - hi.astype(F32)).astype(BF16)
    return hi, lo


def _dot3(a_parts, b_parts):
    a_hi, a_lo = a_parts
    b_hi, b_lo = b_parts
    return (jnp.dot(a_hi, b_hi, preferred_element_type=F32)
            + jnp.dot(a_lo, b_hi, preferred_element_type=F32)
            + jnp.dot(a_hi, b_lo, preferred_element_type=F32))


def _dot3_nt(a, b):
    a_hi, a_lo = _split_bf16(a)
    b_hi, b_lo = _split_bf16(b)
    dims = (((1,), (1,)), ((), ()))
    return (lax.dot_general(a_hi, b_hi, dims, preferred_element_type=F32)
            + lax.dot_general(a_lo, b_hi, dims, preferred_element_type=F32)
            + lax.dot_general(a_hi, b_lo, dims, preferred_element_type=F32))


def _rmsnorm_body(x_ref, g_ref, o_ref):
    x = x_ref[...]
    ms = jnp.mean(x * x, axis=-1, keepdims=True)
    o_ref[...] = (x * lax.rsqrt(ms + EPS) * g_ref[...]).astype(o_ref.dtype)


def rmsnorm_bf16(x, gain):
    M, D = x.shape
    tm = min(M, 512)
    return pl.pallas_call(
        _rmsnorm_body,
        grid=(M // tm,),
        in_specs=[pl.BlockSpec((tm, D), lambda i: (i, 0)),
                  pl.BlockSpec((1, D), lambda i: (0, 0))],
        out_specs=pl.BlockSpec((tm, D), lambda i: (i, 0)),
        out_shape=jax.ShapeDtypeStruct((M, D), BF16),
        compiler_params=_cparams(("parallel",)),
        name="rmsnorm",
    )(x, gain.reshape(1, D))


def _mm_body(a_ref, b_ref, o_ref):
    o_ref[...] = jnp.dot(a_ref[...], b_ref[...], preferred_element_type=F32)


def _mm_res_body(a_ref, b_ref, r_ref, o_ref):
    o_ref[...] = r_ref[...] + jnp.dot(a_ref[...], b_ref[...], preferred_element_type=F32)


def matmul(a, b, residual=None, tm=1024, tn=1024, name="matmul"):
    M, K = a.shape
    N = b.shape[1]
    tm = min(tm, M)
    tn = min(tn, N)
    in_specs = [pl.BlockSpec((tm, K), lambda i, j: (i, 0)),
                pl.BlockSpec((K, tn), lambda i, j: (0, j))]
    args = [a, b]
    body = _mm_body
    if residual is not None:
        in_specs.append(pl.BlockSpec((tm, tn), lambda i, j: (i, j)))
        args.append(residual)
        body = _mm_res_body
    return pl.pallas_call(
        body,
        grid=(M // tm, N // tn),
        in_specs=in_specs,
        out_specs=pl.BlockSpec((tm, tn), lambda i, j: (i, j)),
        out_shape=jax.ShapeDtypeStruct((M, N), F32),
        compiler_params=_cparams(("parallel", "parallel")),
        name=name,
    )(*args)


def _mm_nt2_body(a_ref, a2_ref, bt_ref, o_ref, o2_ref):
    dims = (((1,), (1,)), ((), ()))
    o_ref[...] = lax.dot_general(a_ref[...], bt_ref[...], dims, preferred_element_type=F32)

    @pl.when(pl.program_id(1) == 0)
    def _():
        o2_ref[...] = lax.dot_general(a2_ref[...], bt_ref[...], dims, preferred_element_type=F32)


def matmul_nt2(a, a2, bt, tm=1024, tn=1024, name="matmul_nt2"):
    M, K = a.shape
    M2 = a2.shape[0]
    N = bt.shape[0]
    tm = min(tm, M)
    return pl.pallas_call(
        _mm_nt2_body,
        grid=(N // tn, M // tm),
        in_specs=[pl.BlockSpec((tm, K), lambda j, i: (i, 0)),
                  pl.BlockSpec((M2, K), lambda j, i: (0, 0)),
                  pl.BlockSpec((tn, K), lambda j, i: (j, 0))],
        out_specs=[pl.BlockSpec((tm, tn), lambda j, i: (i, j)),
                   pl.BlockSpec((M2, tn), lambda j, i: (0, j))],
        out_shape=[jax.ShapeDtypeStruct((M, N), F32), jax.ShapeDtypeStruct((M2, N), F32)],
        compiler_params=_cparams(("parallel", "arbitrary")),
        name=name,
    )(a, a2, bt)


def _gdn_chunk_body(q_ref, k_ref, v_ref, z_ref, ab_ref, cwq_ref, cwk_ref, cwv_ref,
                    alog_ref, dtb_ref, gn_ref, o_ref, s_out_ref,
                    xq_s, xk_s, xv_s, state_s):
    n = pl.program_id(2)
    C = GDN_CHUNK
    HW = GDN_HB * GDN_DK

    @pl.when(n == 0)
    def _():
        xq_s[0:8, :] = jnp.zeros((8, HW), F32)
        xk_s[0:8, :] = jnp.zeros((8, HW), F32)
        xv_s[0:8, :] = jnp.zeros((8, HW), F32)
        state_s[...] = jnp.zeros(state_s.shape, F32)

    def conv(x_ref, xs, cw_ref):
        xs[8:8 + C, :] = x_ref[0]
        y = cw_ref[CONV_W - 1:CONV_W, :] * xs[8:8 + C, :]
        for w in range(CONV_W - 1):
            y = y + cw_ref[w:w + 1, :] * xs[5 + w:5 + w + C, :]
        xs[0:8, :] = xs[C:C + 8, :]
        return _silu(y)

    q_all = conv(q_ref, xq_s, cwq_ref)
    k_all = conv(k_ref, xk_s, cwk_ref)
    v_all = conv(v_ref, xv_s, cwv_ref)

    ab = ab_ref[0, 0]
    g_all = -jnp.exp(alog_ref[0]) * _softplus(ab + dtb_ref[0])
    beta_all = _sigmoid(ab)
    ri = lax.broadcasted_iota(jnp.int32, (C, C), 0)
    ci = lax.broadcasted_iota(jnp.int32, (C, C), 1)
    tri = ri >= ci
    strict = ri > ci
    rowid = lax.broadcasted_iota(jnp.int32, (C, 128), 0)
    gc_all = g_all
    shift = 1
    while shift < C:
        gc_all = gc_all + jnp.where(rowid >= shift, pltpu.roll(gc_all, shift, axis=0), 0.0)
        shift *= 2
    gc_t = jnp.concatenate([gc_all, jnp.zeros((128 - C, 128), F32)], axis=0).T

    H = range(GDN_HB)
    sls = [slice(j * GDN_DK, (j + 1) * GDN_DK) for j in H]
    qh, kh, kb, gcol, decay, rhs = [], [], [], [], [], []
    for j in H:
        q = q_all[:, sls[j]]
        k = k_all[:, sls[j]]
        q = q * lax.rsqrt(jnp.sum(q * q, axis=-1, keepdims=True) + EPS) * (GDN_DK ** -0.5)
        k = k * lax.rsqrt(jnp.sum(k * k, axis=-1, keepdims=True) + EPS)
        beta = beta_all[:, GDN_HB + j:GDN_HB + j + 1]
        gc = gc_all[:, j:j + 1]
        grow = gc_t[j:j + 1, :C]
        qh.append(q)
        kh.append(k)
        kb.append(k * beta)
        gcol.append(gc)
        decay.append(jnp.where(tri, jnp.exp(jnp.where(tri, gc - grow, 0.0)), 0.0))
        rhs.append(jnp.concatenate([v_all[:, sls[j]] * beta, kb[j] * jnp.exp(gc)], axis=-1))
    kq = [_bdot_nt(jnp.concatenate([kb[j], qh[j]], axis=0), kh[j]) for j in H]
    intra = [jnp.where(tri, kq[j][C:] * decay[j], 0.0) for j in H]
    acc = [-jnp.where(strict, kq[j][:C] * decay[j], 0.0) for j in H]
    parts = [_split_bf16(acc[j]) for j in H]
    pw = [_dot3(parts[j], parts[j]) for j in H]
    for r in range(5):
        pw_parts = [_split_bf16(pw[j]) for j in H]
        acc_parts = [_split_bf16(acc[j]) for j in H]
        acc = [acc[j] + pw[j] + _dot3(pw_parts[j], acc_parts[j]) for j in H]
        if r < 4:
            pw = [_dot3(pw_parts[j], pw_parts[j]) for j in H]
    sol = [rhs[j] + _dot3(_split_bf16(acc[j]), _split_bf16(rhs[j])) for j in H]
    S = [state_s[j] for j in H]
    wq_s = [_bdot(jnp.concatenate([sol[j][:, GDN_DV:], qh[j] * jnp.exp(gcol[j])], axis=0), S[j])
            for j in H]
    v_new = [sol[j][:, :GDN_DV] - wq_s[j][:C] for j in H]
    o = [wq_s[j][C:] + _bdot(intra[j], v_new[j]) for j in H]
    for j in H:
        g_last = gcol[j][C - 1:C, :]
        state_s[j] = (S[j] * jnp.exp(g_last)
                      + _bdot_tn(kh[j] * jnp.exp(g_last - gcol[j]), v_new[j]))
    for j in H:
        ms = jnp.mean(o[j] * o[j], axis=-1, keepdims=True)
        on = o[j] * lax.rsqrt(ms + EPS) * gn_ref[...]
        o_ref[0, :, sls[j]] = (on * _silu(z_ref[0, :, sls[j]])).astype(o_ref.dtype)

    @pl.when(n == pl.num_programs(2) - 1)
    def _():
        s_out_ref[0] = state_s[...]


def gdn_prompt(proj3, ab_g, conv_w, alog_g, dtb_g, gdn_norm):
    B, T, _ = proj3.shape
    C = GDN_CHUNK
    N = T // C
    HW = GDN_HB * GDN_DK
    nq = GDN_QK_W // HW

    def col(base):
        return lambda b, hg, n: (b, n, base + hg)

    def cwcol(base):
        return lambda b, hg, n: (0, base + hg)

    return pl.pallas_call(
        _gdn_chunk_body,
        grid=(B, GDN_HG, N),
        in_specs=[
            pl.BlockSpec((1, C, HW), col(0)),
            pl.BlockSpec((1, C, HW), col(nq)),
            pl.BlockSpec((1, C, HW), col(2 * nq)),
            pl.BlockSpec((1, C, HW), col(OFF_Z // HW)),
            pl.BlockSpec((1, 1, C, 128), lambda b, hg, n: (b, hg, n, 0)),
            pl.BlockSpec((CONV_W, HW), cwcol(0)),
            pl.BlockSpec((CONV_W, HW), cwcol(nq)),
            pl.BlockSpec((CONV_W, HW), cwcol(2 * nq)),
            pl.BlockSpec((1, 1, 128), lambda b, hg, n: (hg, 0, 0)),
            pl.BlockSpec((1, 1, 128), lambda b, hg, n: (hg, 0, 0)),
            pl.BlockSpec((1, GDN_DV), lambda b, hg, n: (0, 0)),
        ],
        out_specs=[
            pl.BlockSpec((1, C, HW), lambda b, hg, n: (b, n, hg)),
            pl.BlockSpec((1, GDN_HB, GDN_DK, GDN_DV), lambda b, hg, n: (b, hg, 0, 0)),
        ],
        out_shape=[
            jax.ShapeDtypeStruct((B, T, GDN_V_W), BF16),
            jax.ShapeDtypeStruct((B, GDN_HEADS, GDN_DK, GDN_DV), F32),
        ],
        scratch_shapes=[
            pltpu.VMEM((C + 8, HW), F32),
            pltpu.VMEM((C + 8, HW), F32),
            pltpu.VMEM((C + 8, HW), F32),
            pltpu.VMEM((GDN_HB, GDN_DK, GDN_DV), F32),
        ],
        compiler_params=_cparams(("parallel", "parallel", "arbitrary")),
        name="gdn_prompt",
    )(proj3, proj3, proj3, proj3, ab_g, conv_w, conv_w, conv_w, alog_g, dtb_g,
      gdn_norm.reshape(1, GDN_DV))


def _gdn_step_body(x_ref, z_ref, ab_ref, cs_ref, s_ref, cw_ref, alog_ref, dtb_ref, gn_ref,
                   o_ref, cs_out_ref, s_out_ref):
    x = x_ref[0]
    y = cw_ref[CONV_W - 1] * x
    for w in range(CONV_W - 1):
        y = y + cw_ref[w] * cs_ref[0, w]
    y = _silu(y)
    cs_out_ref[0, 0] = cs_ref[0, 1]
    cs_out_ref[0, 1] = cs_ref[0, 2]
    cs_out_ref[0, 2] = x
    r_ab = (OFF_AB // 128) % 8
    ab = ab_ref[0, r_ab:r_ab + 1, :]
    g_all = -jnp.exp(alog_ref[...]) * _softplus(ab + dtb_ref[...])
    beta_all = _sigmoid(ab)
    eye = (lax.broadcasted_iota(jnp.int32, (GDN_DK, GDN_DK), 0)
           == lax.broadcasted_iota(jnp.int32, (GDN_DK, GDN_DK), 1))

    def to_col(row):
        return jnp.sum(jnp.where(eye, jnp.broadcast_to(row, (GDN_DK, GDN_DK)), 0.0),
                       axis=1, keepdims=True)

    for h in range(GDN_HEADS):
        q = y[h:h + 1, :]
        k = y[GDN_HEADS + h:GDN_HEADS + h + 1, :]
        v = y[2 * GDN_HEADS + h:2 * GDN_HEADS + h + 1, :]
        q = q * lax.rsqrt(jnp.sum(q * q, axis=-1, keepdims=True) + EPS) * (GDN_DK ** -0.5)
        k = k * lax.rsqrt(jnp.sum(k * k, axis=-1, keepdims=True) + EPS)
        g = g_all[:, h:h + 1]
        beta = beta_all[:, GDN_HEADS + h:GDN_HEADS + h + 1]
        eg = jnp.exp(g)
        S = s_ref[0, h]
        kcol = to_col(k)
        qcol = to_col(q)
        k_s = jnp.sum(S * kcol, axis=0, keepdims=True)
        q_s = jnp.sum(S * qcol, axis=0, keepdims=True)
        v_new = v * beta - (beta * eg) * k_s
        o = eg * q_s + jnp.sum(q * k, axis=-1, keepdims=True) * v_new
        s_out_ref[0, h] = S * eg + kcol * v_new
        ms = jnp.mean(o * o, axis=-1, keepdims=True)
        on = o * lax.rsqrt(ms + EPS) * gn_ref[...]
        o_ref[0, h:h + 1, :] = on * _silu(z_ref[0, h:h + 1, :])


def gdn_step(proj_s, conv_state, ssm_state, conv_w, a_log, dt_bias, gdn_norm):
    Bs = proj_s.shape[0]
    nrow = N_PROJ // 128
    p3 = proj_s.reshape(Bs, nrow, 128)
    cs4 = conv_state.reshape(Bs, CONV_W - 1, CONV_CH // 128, 128)
    cw3 = conv_w.reshape(CONV_W, CONV_CH // 128, 128)
    pad = jnp.zeros((128 - GDN_HEADS,), F32)
    alog = jnp.concatenate([a_log, pad]).reshape(1, 128)
    dtb = jnp.concatenate([dt_bias, pad]).reshape(1, 128)
    nr = CONV_CH // 128
    o, cs_new, s_new = pl.pallas_call(
        _gdn_step_body,
        grid=(Bs,),
        in_specs=[
            pl.BlockSpec((1, nr, 128), lambda b: (b, 0, 0)),
            pl.BlockSpec((1, GDN_HEADS, 128), lambda b: (b, OFF_Z // 128 // GDN_HEADS, 0)),
            pl.BlockSpec((1, 8, 128), lambda b: (b, OFF_AB // 128 // 8, 0)),
            pl.BlockSpec((1, CONV_W - 1, nr, 128), lambda b: (b, 0, 0, 0)),
            pl.BlockSpec((1, GDN_HEADS, GDN_DK, GDN_DV), lambda b: (b, 0, 0, 0)),
            pl.BlockSpec((CONV_W, nr, 128), lambda b: (0, 0, 0)),
            pl.BlockSpec((1, 128), lambda b: (0, 0)),
            pl.BlockSpec((1, 128), lambda b: (0, 0)),
            pl.BlockSpec((1, GDN_DV), lambda b: (0, 0)),
        ],
        out_specs=[
            pl.BlockSpec((1, GDN_HEADS, GDN_DV), lambda b: (b, 0, 0)),
            pl.BlockSpec((1, CONV_W - 1, nr, 128), lambda b: (b, 0, 0, 0)),
            pl.BlockSpec((1, GDN_HEADS, GDN_DK, GDN_DV), lambda b: (b, 0, 0, 0)),
        ],
        out_shape=[
            jax.ShapeDtypeStruct((Bs, GDN_HEADS, GDN_DV), F32),
            jax.ShapeDtypeStruct((Bs, CONV_W - 1, nr, 128), F32),
            jax.ShapeDtypeStruct((Bs, GDN_HEADS, GDN_DK, GDN_DV), F32),
        ],
        compiler_params=_cparams(("parallel",)),
        name="gdn_step",
    )(p3, p3, p3, cs4, ssm_state, cw3, alog, dtb, gdn_norm.reshape(1, GDN_DV))
    return (o.reshape(Bs, GDN_V_W), cs_new.reshape(Bs, CONV_W - 1, CONV_CH), s_new)


def _qknorm_body(q_ref, k_ref, v_ref, qg_ref, kg_ref, qn_ref, kv_ref):
    for h in range(GROUP_HEADS):
        sl = slice(h * HEAD_DIM, (h + 1) * HEAD_DIM)
        q = q_ref[:, sl]
        qn = q * lax.rsqrt(jnp.mean(q * q, axis=-1, keepdims=True) + EPS) * qg_ref[...]
        qn_ref[:, sl] = (qn * (HEAD_DIM ** -0.5)).astype(qn_ref.dtype)
        k = k_ref[:, sl]
        kv_ref[:, sl] = k * lax.rsqrt(jnp.mean(k * k, axis=-1, keepdims=True) + EPS) * kg_ref[...]
    kv_ref[:, GROUP_W:] = v_ref[...]


def qk_norm(proj, q_norm, k_norm, q_dtype):
    M = proj.shape[0]
    tm = min(M, 512)
    blk = GROUP_W
    return pl.pallas_call(
        _qknorm_body,
        grid=(M // tm, N_GROUPS),
        in_specs=[
            pl.BlockSpec((tm, blk), lambda i, g: (i, OFF_AQ // blk + g)),
            pl.BlockSpec((tm, blk), lambda i, g: (i, OFF_AK // blk + g)),
            pl.BlockSpec((tm, blk), lambda i, g: (i, OFF_AV // blk + g)),
            pl.BlockSpec((1, HEAD_DIM), lambda i, g: (0, 0)),
            pl.BlockSpec((1, HEAD_DIM), lambda i, g: (0, 0)),
        ],
        out_specs=[
            pl.BlockSpec((tm, blk), lambda i, g: (i, g)),
            pl.BlockSpec((tm, 2 * blk), lambda i, g: (i, g)),
        ],
        out_shape=[jax.ShapeDtypeStruct((M, ATTN_W), q_dtype),
                   jax.ShapeDtypeStruct((M, N_GROUPS * 2 * GROUP_W), F32)],
        compiler_params=_cparams(("parallel", "parallel")),
        name="qk_norm",
    )(proj, proj, proj, q_norm.reshape(1, HEAD_DIM), k_norm.reshape(1, HEAD_DIM))


def _t5_bucket(dist):
    max_exact = N_BUCKETS // 2
    d = np.maximum(dist, max_exact).astype(np.float32)
    large = max_exact + (np.log(d / max_exact) / np.log(MAX_DISTANCE / max_exact)
                         * (N_BUCKETS - max_exact)).astype(np.int32)
    large = np.minimum(large, N_BUCKETS - 1)
    return np.where(dist < max_exact, dist, large).astype(np.int32)


def _group_bias(rel_bias, gi):
    window, dilation = ATTN_GROUPS[gi]
    n_keys = window // dilation + 1
    buckets = _t5_bucket(np.arange(n_keys) * dilation)
    return rel_bias[buckets][:, gi * GROUP_HEADS:(gi + 1) * GROUP_HEADS].T.astype(F32)


def _attn_prompt_body(T, q0_ref, q1_ref, q2_ref, k0_ref, v0_ref, k1_ref, v1_ref, k2_ref, v2_ref,
                      bias_ref, y_ref, m_s, l_s, o_s):
    QB = Q_BLOCK
    q_refs = (q0_ref, q1_ref, q2_ref)
    k_refs = (k0_ref, k1_ref, k2_ref)
    v_refs = (v0_ref, v1_ref, v2_ref)
    for gi, (window, d) in enumerate(ATTN_GROUPS):
        nq = T // (d * QB)
        for r in range(d):
            for qi in range(nq):
                rows = pl.ds(r + d * QB * qi, QB, stride=d) if d > 1 else pl.ds(QB * qi, QB)
                q = q_refs[gi][0, rows, :].astype(BF16)
                kc = k_refs[gi][0, rows, :]
                vc = v_refs[gi][0, rows, :]
                if qi > 0:
                    prev = (pl.ds(r + d * QB * (qi - 1), QB, stride=d) if d > 1
                            else pl.ds(QB * (qi - 1), QB))
                    kcat = jnp.concatenate([k_refs[gi][0, prev, :], kc], axis=0).astype(BF16)
                    vcat = jnp.concatenate([v_refs[gi][0, prev, :], vc], axis=0).astype(BF16)
                    bias = bias_ref[gi, 0]
                else:
                    kcat = kc.astype(BF16)
                    vcat = vc.astype(BF16)
                    bias = bias_ref[gi, 0, :, QB:]
                s = lax.dot_general(q, kcat, (((1,), (1,)), ((), ())),
                                    preferred_element_type=F32) + bias
                m_blk = jnp.max(s, axis=-1, keepdims=True)
                if gi == 0:
                    m_new = m_blk
                else:
                    m_old = m_s[rows, :][:, 0:1]
                    m_new = jnp.maximum(m_old, m_blk)
                p = jnp.exp(s - m_new)
                l_new = jnp.sum(p, axis=-1, keepdims=True)
                o_new = jnp.dot(p.astype(BF16), vcat, preferred_element_type=F32)
                if gi > 0:
                    alpha = jnp.exp(m_old - m_new)
                    l_new = l_new + alpha * l_s[rows, :][:, 0:1]
                    o_new = o_new + alpha * o_s[rows, :]
                m_s[rows, :] = jnp.broadcast_to(m_new, (QB, HEAD_DIM))
                l_s[rows, :] = jnp.broadcast_to(l_new, (QB, HEAD_DIM))
                o_s[rows, :] = o_new
    y_ref[0] = (o_s[...] / l_s[...]).astype(y_ref.dtype)


def attn_prompt(qn3, kv3, bias_all):
    B, T, _ = qn3.shape
    blk = (1, T, HEAD_DIM)

    def qspec(gi):
        return pl.BlockSpec(blk, lambda b, h: (b, 0, gi * GROUP_HEADS + h))

    def kvspec(gi, is_v):
        return pl.BlockSpec(blk, lambda b, h: (b, 0, (2 * gi + is_v) * GROUP_HEADS + h))

    in_specs = [qspec(0), qspec(1), qspec(2)]
    for gi in range(N_GROUPS):
        in_specs += [kvspec(gi, 0), kvspec(gi, 1)]
    in_specs.append(pl.BlockSpec((N_GROUPS, 1, Q_BLOCK, 2 * Q_BLOCK), lambda b, h: (0, h, 0, 0)))
    y = pl.pallas_call(
        functools.partial(_attn_prompt_body, T),
        grid=(B, GROUP_HEADS),
        in_specs=in_specs,
        out_specs=pl.BlockSpec(blk, lambda b, h: (b, 0, h)),
        out_shape=jax.ShapeDtypeStruct((B, T, GROUP_W), BF16),
        scratch_shapes=[pltpu.VMEM((T, HEAD_DIM), F32), pltpu.VMEM((T, HEAD_DIM), F32),
                        pltpu.VMEM((T, HEAD_DIM), F32)],
        compiler_params=_cparams(("parallel", "parallel")),
        name="attn_prompt",
    )(qn3, qn3, qn3, kv3, kv3, kv3, kv3, kv3, kv3, bias_all)
    return y.reshape(B * T, GROUP_W)


def _prompt_bias_matrix(rel_bias, gi):
    QB = Q_BLOCK
    bias = _group_bias(rel_bias, gi)
    row0 = jnp.concatenate([bias[:, ::-1], jnp.full((GROUP_HEADS, QB), NEG_INF, F32)], axis=1)
    flat = jnp.tile(row0, (1, QB))[:, :QB * 2 * QB]
    return flat.reshape(GROUP_HEADS, QB, 2 * QB)


def _attn_decode_body(q_ref, n0_ref, n1_ref, n2_ref, c0_ref, c1_ref, c2_ref, bc_ref, bn_ref, y_ref):
    new_refs = (n0_ref, n1_ref, n2_ref)
    cache_refs = (c0_ref, c1_ref, c2_ref)
    outs, lds = [], []
    for gi in range(N_GROUPS):
        q = q_ref[0, gi * GROUP_HEADS:(gi + 1) * GROUP_HEADS, :]
        kc = cache_refs[gi][:, 0]
        vc = cache_refs[gi][:, 1]
        kn = new_refs[gi][0, 0:GROUP_HEADS, :]
        vn = new_refs[gi][0, GROUP_HEADS:2 * GROUP_HEADS, :]
        s_c = jnp.sum(kc * q[None], axis=-1, keepdims=True) + bc_ref[gi, :, :, 0:1]
        s_n = jnp.sum(kn * q, axis=-1, keepdims=True) + bn_ref[gi, :, 0:1]
        m = jnp.maximum(jnp.max(s_c, axis=0), s_n)
        p_c = jnp.exp(s_c - m[None])
        p_n = jnp.exp(s_n - m)
        l = jnp.sum(p_c, axis=0) + p_n
        outs.append((jnp.sum(p_c * vc, axis=0) + p_n * vn) / l)
        lds.append(m + jnp.log(l))
    m = jnp.maximum(jnp.maximum(lds[0], lds[1]), lds[2])
    es = [jnp.exp(x - m) for x in lds]
    den = es[0] + es[1] + es[2]
    y_ref[0] = outs[0] * (es[0] / den) + outs[1] * (es[1] / den) + outs[2] * (es[2] / den)


def attn_decode(qn_s, kv_new, caches, rel_bias):
    Bs = qn_s.shape[0]
    q3 = qn_s.reshape(Bs, N_GROUPS * GROUP_HEADS, HEAD_DIM)
    new3 = [x.reshape(Bs, 2 * GROUP_HEADS, HEAD_DIM) for x in kv_new]
    cviews, bcs, bns = [], [], []
    for gi, (window, d) in enumerate(ATTN_GROUPS):
        nk = window // d
        assert caches[gi].shape[1] == window and nk == Q_BLOCK
        cviews.append(caches[gi].reshape(Bs, nk, d, 2, GROUP_HEADS, HEAD_DIM))
        bias = _group_bias(rel_bias, gi)
        bc = bias[:, nk - np.arange(nk)]
        bcs.append(jnp.broadcast_to(bc.T[:, :, None], (nk, GROUP_HEADS, HEAD_DIM)))
        bns.append(jnp.broadcast_to(bias[:, 0:1], (GROUP_HEADS, HEAD_DIM)))
    bc_all = jnp.stack(bcs, 0)
    bn_all = jnp.stack(bns, 0)
    nspec = pl.BlockSpec((1, 2 * GROUP_HEADS, HEAD_DIM), lambda b: (b, 0, 0))
    cspec = pl.BlockSpec((None, Q_BLOCK, None, 2, GROUP_HEADS, HEAD_DIM),
                         lambda b: (b, 0, 0, 0, 0, 0))
    y = pl.pallas_call(
        _attn_decode_body,
        grid=(Bs,),
        in_specs=[
            pl.BlockSpec((1, N_GROUPS * GROUP_HEADS, HEAD_DIM), lambda b: (b, 0, 0)),
            nspec, nspec, nspec, cspec, cspec, cspec,
            pl.BlockSpec((N_GROUPS, Q_BLOCK, GROUP_HEADS, HEAD_DIM), lambda b: (0, 0, 0, 0)),
            pl.BlockSpec((N_GROUPS, GROUP_HEADS, HEAD_DIM), lambda b: (0, 0, 0)),
        ],
        out_specs=pl.BlockSpec((1, GROUP_HEADS, HEAD_DIM), lambda b: (b, 0, 0)),
        out_shape=jax.ShapeDtypeStruct((Bs, GROUP_HEADS, HEAD_DIM), F32),
        compiler_params=_cparams(("parallel",)),
        name="attn_decode",
    )(q3, *new3, *cviews, bc_all, bn_all)
    return y.reshape(Bs, GROUP_W)


def _merge_body(ya_ref, yb_ref, wa_ref, wb_ref, ga_ref, gb_ref, o_ref):
    pa = jnp.dot(ya_ref[...], wa_ref[...], preferred_element_type=F32)
    pb = jnp.dot(yb_ref[...], wb_ref[...], preferred_element_type=F32)
    o_ref[...] = (_sigmoid(ga_ref[...]) * pa + _sigmoid(gb_ref[...]) * pb).astype(o_ref.dtype)


def branch_merge(ya, yb, wa, wb, proj):
    M = ya.shape[0]
    tm = min(M, 1024)
    tn = 512
    return pl.pallas_call(
        _merge_body,
        grid=(M // tm, D_MODEL // tn),
        in_specs=[
            pl.BlockSpec((tm, GDN_V_W), lambda i, j: (i, 0)),
            pl.BlockSpec((tm, GROUP_W), lambda i, j: (i, 0)),
            pl.BlockSpec((GDN_V_W, tn), lambda i, j: (0, j)),
            pl.BlockSpec((GROUP_W, tn), lambda i, j: (0, j)),
            pl.BlockSpec((tm, tn), lambda i, j: (i, OFF_GA // tn + j)),
            pl.BlockSpec((tm, tn), lambda i, j: (i, OFF_GB // tn + j)),
        ],
        out_specs=pl.BlockSpec((tm, tn), lambda i, j: (i, j)),
        out_shape=jax.ShapeDtypeStruct((M, D_MODEL), BF16),
        compiler_params=_cparams(("parallel", "parallel")),
        name="branch_merge",
    )(ya, yb, wa, wb, proj, proj)


def _extract_top(x, count, tie_safe, want_rank=True):
    R = x.shape[0]
    rid = lax.broadcasted_iota(jnp.int32, x.shape, 0).astype(F32)
    vid = lax.broadcasted_iota(jnp.int32, (count, x.shape[1]), 0)
    vals = jnp.zeros((count, x.shape[1]), F32)
    rank = jnp.full(x.shape, float(count), F32)
    for a in range(count):
        m = jnp.max(x, axis=0, keepdims=True)
        hit = x == m
        if tie_safe:
            first = jnp.min(jnp.where(hit, rid, float(R)), axis=0, keepdims=True)
            hit = rid == first
        x = jnp.where(hit, -jnp.inf, x)
        if want_rank:
            rank = jnp.where(hit, float(a), rank)
        vals = jnp.where(vid == a, m, vals)
    return vals, (rank if want_rank else x)


CAND_HALF = PEER_TOPK // 2
N_CAND = PEER_TOPK + (CAND_HALF - 1) * CAND_HALF + CAND_HALF


def _peer_topk_body(q_ref, sk_ref, cnt_ref, r2_ref, e1_ref, e2_ref, cand_s, s1_s, s2_s):
    K = PEER_TOPK
    KH = CAND_HALF

    def select(h, s1, s2, tie_safe):
        v1, rank1 = _extract_top(s1, K, tie_safe)
        v2, rank2 = _extract_top(s2, K, tie_safe)
        cand_s[0:K, :] = v2 + v1[0:1, :]
        for a in range(1, KH):
            cand_s[K + (a - 1) * KH:K + a * KH, :] = v2[0:KH, :] + v1[a:a + 1, :]
        cand_s[N_CAND - KH:N_CAND, :] = v1[KH:K, :] + v2[0:1, :]
        best, cleft = _extract_top(cand_s[...], K, tie_safe, want_rank=False)
        picked = jnp.where(cleft == -jnp.inf, 1.0, 0.0)
        vid = lax.broadcasted_iota(jnp.int32, (KH, picked.shape[1]), 0)
        low = jnp.where(vid == 0, jnp.sum(picked[0:K], axis=0, keepdims=True), 0.0)
        for a in range(1, KH):
            rows = picked[K + (a - 1) * KH:K + a * KH]
            low = jnp.where(vid == a, jnp.sum(rows, axis=0, keepdims=True), low)
        count = jnp.concatenate([low, picked[N_CAND - KH:N_CAND]], axis=0)
        cnt1 = jnp.zeros(rank1.shape, F32)
        for a in range(K):
            cnt1 = jnp.where(rank1 == float(a), count[a:a + 1, :], cnt1)
        cmax = best[0:1, :]
        z = jnp.sum(jnp.exp(best - cmax), axis=0, keepdims=True)
        in1 = jnp.where(rank1 < float(K), 1.0, 0.0)
        in2 = jnp.where(rank2 < float(K), 1.0, 0.0)
        cnt_ref[h] = cnt1
        r2_ref[h] = rank2
        e1_ref[h] = in1 * jnp.exp(s1 - v1[0:1, :])
        e2_ref[h] = in2 * jnp.exp(s2 - v2[0:1, :]) / z * 0.5
        return (jnp.sum(in1, axis=0, keepdims=True) + jnp.sum(in2, axis=0, keepdims=True)
                + jnp.sum(picked, axis=0, keepdims=True) - 3.0 * K)

    def head(h, carry):
        off = pl.multiple_of(h * (2 * N_KEYS), 2 * N_KEYS)
        q1 = q_ref[:, pl.ds(off, N_KEYS)]
        q2 = q_ref[:, pl.ds(pl.multiple_of(off + N_KEYS, N_KEYS), N_KEYS)]
        s1_s[...] = _dot3_nt(sk_ref[0], q1)
        s2_s[...] = _dot3_nt(sk_ref[1], q2)
        extra = select(h, s1_s[...], s2_s[...], tie_safe=False)

        @pl.when(jnp.max(extra) > 0.0)
        def _():
            select(h, s1_s[...], s2_s[...], tie_safe=True)

        return carry

    lax.fori_loop(0, PEER_HEADS, head, 0)


def peer_topk(query, subkeys):
    M = query.shape[0]
    tm = min(M, 512)
    big = jax.ShapeDtypeStruct((PEER_HEADS, N_KEYS, M), F32)
    bspec = pl.BlockSpec((PEER_HEADS, N_KEYS, tm), lambda i: (0, 0, i))
    return pl.pallas_call(
        _peer_topk_body,
        grid=(M // tm,),
        in_specs=[pl.BlockSpec((tm, PEER_HEADS * 2 * N_KEYS), lambda i: (i, 0)),
                  pl.BlockSpec((2, N_KEYS, N_KEYS), lambda i: (0, 0, 0))],
        out_specs=[bspec, bspec, bspec, bspec],
        out_shape=[big, big, big, big],
        scratch_shapes=[pltpu.VMEM((N_CAND, tm), F32), pltpu.VMEM((N_KEYS, tm), F32),
                        pltpu.VMEM((N_KEYS, tm), F32)],
        compiler_params=_cparams(("parallel",)),
        name="peer_topk",
    )(query, subkeys)


def _gelu_x2(x):
    return x * (1.0 + lax.erf(x * (2.0 ** -0.5)))


GATE_ROWS = 8


def _peer_gate_body(hf_ref, u_ref, cnt_ref, r2_ref, e1_ref, e2_ref, w_ref, g_s):
    j = pl.program_id(1)
    tm = hf_ref.shape[0]
    grp = pl.multiple_of(j * GATE_ROWS, GATE_ROWS)

    for kk in range(GATE_ROWS):
        for tc in range(tm // 128):
            ts = slice(tc * 128, (tc + 1) * 128)
            acc = None
            for h in range(PEER_HEADS):
                cnt_row = cnt_ref[h, pl.ds(grp, GATE_ROWS), ts][kk:kk + 1, :]
                e1_row = e1_ref[h, pl.ds(grp, GATE_ROWS), ts][kk:kk + 1, :]
                term = jnp.where(r2_ref[h, :, ts] < cnt_row, e2_ref[h, :, ts] * e1_row, 0.0)
                acc = term if acc is None else acc + term
            g_s[ts, kk * N_KEYS:(kk + 1) * N_KEYS] = acc.T
    hmat = lax.dot_general(hf_ref[...], u_ref[...], (((1,), (1,)), ((), ())),
                           preferred_element_type=F32)
    w_ref[...] = (g_s[...] * _gelu_x2(hmat)).astype(w_ref.dtype)


def peer_gate_weights(hf, u_tab, cnt, r2, e1, e2, tm=1024):
    M = hf.shape[0]
    tm = min(M, tm)
    tn = GATE_ROWS * N_KEYS
    E = u_tab.shape[0]
    once = pl.Buffered(1)
    sspec = pl.BlockSpec((PEER_HEADS, N_KEYS, tm), lambda i, j: (0, 0, i), pipeline_mode=once)
    return pl.pallas_call(
        _peer_gate_body,
        grid=(M // tm, E // tn),
        in_specs=[
            pl.BlockSpec((tm, D_MODEL), lambda i, j: (i, 0), pipeline_mode=once),
            pl.BlockSpec((tn, D_MODEL), lambda i, j: (j, 0)),
            sspec, sspec, sspec, sspec,
        ],
        out_specs=pl.BlockSpec((tm, tn), lambda i, j: (i, j)),
        out_shape=jax.ShapeDtypeStruct((M, E), BF16),
        scratch_shapes=[pltpu.VMEM((tm, tn), F32)],
        compiler_params=_cparams(("parallel", "arbitrary")),
        name="peer_gate",
    )(hf, u_tab, cnt, r2, e1, e2)


def _mm_kres_body(a_ref, b_ref, r_ref, o_ref):
    k = pl.program_id(2)

    @pl.when(k == 0)
    def _():
        o_ref[...] = r_ref[...]

    o_ref[...] += jnp.dot(a_ref[...], b_ref[...], preferred_element_type=F32)


def matmul_kres(a, b, residual, tm=1024, tn=1024, tk=4096, name="matmul_k"):
    M, K = a.shape
    N = b.shape[1]
    tm = min(tm, M)
    return pl.pallas_call(
        _mm_kres_body,
        grid=(M // tm, N // tn, K // tk),
        in_specs=[pl.BlockSpec((tm, tk), lambda i, j, k: (i, k)),
                  pl.BlockSpec((tk, tn), lambda i, j, k: (k, j)),
                  pl.BlockSpec((tm, tn), lambda i, j, k: (i, j))],
        out_specs=pl.BlockSpec((tm, tn), lambda i, j, k: (i, j)),
        out_shape=jax.ShapeDtypeStruct((M, N), F32),
        compiler_params=_cparams(("parallel", "parallel", "arbitrary")),
        name=name,
    )(a, b, residual)


def peer_ffn(x1, norm_ffn, wq_bf, subkeys, u_bf, v_bf):
    M = x1.shape[0]
    Mp = max(M, 128)
    if Mp != M:
        x1 = jnp.concatenate([x1, jnp.zeros((Mp - M, D_MODEL), F32)], axis=0)
    hf = rmsnorm_bf16(x1, norm_ffn)
    query = matmul(hf, wq_bf, name="peer_query")
    cnt, r2, e1, e2 = peer_topk(query, subkeys)
    w = peer_gate_weights(hf, u_bf, cnt, r2, e1, e2)
    return matmul_kres(w, v_bf, x1, name="peer_out")[:M]


W_BLK = 512
N_AB = 2 * GDN_HEADS
J_AB = OFF_AB // W_BLK


def _prep_w_in_body(a_ref, b_ref, o_ref):
    j = pl.program_id(0)

    @pl.when(j < OFF_AQ // W_BLK)
    def _():
        o_ref[...] = a_ref[...].astype(o_ref.dtype)

    @pl.when(jnp.logical_and(j >= OFF_AQ // W_BLK, j < J_AB))
    def _():
        o_ref[0:W_BLK - N_AB, :] = a_ref[N_AB:, :].astype(o_ref.dtype)
        o_ref[W_BLK - N_AB:, :] = b_ref[...].astype(o_ref.dtype)

    @pl.when(j == J_AB)
    def _():
        o_ref[0:N_AB, :] = b_ref[...].astype(o_ref.dtype)
        o_ref[N_AB:, :] = jnp.zeros((W_BLK - N_AB, o_ref.shape[1]), o_ref.dtype)


def _prep_w_in(w_in_t):
    D = w_in_t.shape[1]
    j0 = OFF_AQ // W_BLK
    sub = W_BLK // N_AB
    return pl.pallas_call(
        _prep_w_in_body,
        grid=(N_PROJ // W_BLK,),
        in_specs=[
            pl.BlockSpec((W_BLK, D), lambda j: (jnp.where(j == J_AB, j0, j), 0)),
            pl.BlockSpec((N_AB, D), lambda j: (jnp.where(j == J_AB, j0 * sub,
                                                         jnp.where(j >= j0, (j + 1) * sub, 0)), 0)),
        ],
        out_specs=pl.BlockSpec((W_BLK, D), lambda j: (j, 0)),
        out_shape=jax.ShapeDtypeStruct((N_PROJ, D), BF16),
        compiler_params=_cparams(("parallel",)),
        name="prep_w_in",
    )(w_in_t, w_in_t)


def _gdn_group_params(a_log, dt_bias):
    pad = jnp.zeros((GDN_HG, 128 - GDN_HB), F32)
    alog = jnp.concatenate([a_log.reshape(GDN_HG, GDN_HB), pad], axis=1).reshape(GDN_HG, 1, 128)
    dtb = jnp.concatenate([dt_bias.reshape(GDN_HG, GDN_HB), pad], axis=1).reshape(GDN_HG, 1, 128)
    return alog, dtb


def _ab_grouped(proj, B, T):
    ab = proj[:, OFF_AB:OFF_AB + 2 * GDN_HEADS].reshape(B, T, 2, GDN_HG, GDN_HB)
    ab = ab.transpose(0, 3, 1, 2, 4).reshape(B, GDN_HG, T, 2 * GDN_HB)
    return jnp.concatenate([ab, jnp.zeros((B, GDN_HG, T, 128 - 2 * GDN_HB), F32)], axis=-1)


def kernel(x_prompt, x_sample, state_ssm, state_conv, cache_kv_g0, cache_kv_g1, cache_kv_g2, rel_bias, norm_mix, w_in, conv_w, a_log, dt_bias, gdn_norm, q_norm, k_norm, w_branch_a, w_branch_b, w_out, norm_ffn, peer_w_query, peer_subkeys, peer_u, peer_v):
    B, T, D = x_prompt.shape
    Bs = x_sample.shape[0]
    assert state_ssm.shape[0] == 1 and x_sample.shape[1] == 1
    caches = (cache_kv_g0[0], cache_kv_g1[0], cache_kv_g2[0])
    w_cat_t = _prep_w_in(jnp.swapaxes(w_in[0], 0, 1))
    wa_bf = w_branch_a[0].astype(BF16)
    wb_bf = w_branch_b[0].astype(BF16)
    wo_bf = w_out[0].astype(BF16)
    wq_bf = peer_w_query[0].astype(BF16)
    u_bf = peer_u[0].astype(BF16)
    v_bf = peer_v[0].astype(BF16)
    cw = conv_w[0]

    xp = x_prompt.reshape(B * T, D)
    xs = x_sample.reshape(Bs, D)

    proj, proj_s = matmul_nt2(rmsnorm_bf16(xp, norm_mix[0]), rmsnorm_bf16(xs, norm_mix[0]),
                              w_cat_t, name="in_proj")
    alog_g, dtb_g = _gdn_group_params(a_log[0], dt_bias[0])
    ya, ssm_p = gdn_prompt(proj.reshape(B, T, N_PROJ), _ab_grouped(proj, B, T), cw, alog_g, dtb_g,
                           gdn_norm[0])
    conv_p = proj.reshape(B, T, N_PROJ)[:, T - (CONV_W - 1):, :CONV_CH]
    qn, kv = qk_norm(proj, q_norm[0], k_norm[0], F32)
    kv3 = kv.reshape(B, T, N_GROUPS * 2 * GROUP_W)
    kvs = [kv3[:, :, gi * 2 * GROUP_W:(gi + 1) * 2 * GROUP_W] for gi in range(N_GROUPS)]
    bias_all = jnp.stack([_prompt_bias_matrix(rel_bias, gi) for gi in range(N_GROUPS)], 0)
    yb = attn_prompt(qn.reshape(B, T, ATTN_W), kv3, bias_all)
    merged = branch_merge(ya.reshape(B * T, GDN_V_W), yb, wa_bf, wb_bf, proj)
    x1 = matmul(merged, wo_bf, residual=xp, name="out_proj")
    p_kv = []
    for gi, (window, _) in enumerate(ATTN_GROUPS):
        keep = min(window, T)
        kvg = kvs[gi][:, T - keep:].reshape(B, keep, 2, GROUP_HEADS, HEAD_DIM)
        p_kv.append(kvg[None])

    ya_s, conv_s, ssm_s = gdn_step(proj_s, state_conv[0], state_ssm[0], cw, a_log[0], dt_bias[0],
                                   gdn_norm[0])
    qn_s, kv_s = qk_norm(proj_s, q_norm[0], k_norm[0], F32)
    kvs_s = [kv_s[:, gi * 2 * GROUP_W:(gi + 1) * 2 * GROUP_W] for gi in range(N_GROUPS)]
    yb_s = attn_decode(qn_s, kvs_s, caches, rel_bias)
    merged_s = branch_merge(ya_s.astype(BF16), yb_s.astype(BF16), wa_bf, wb_bf, proj_s)
    x1_s = matmul(merged_s, wo_bf, residual=xs, name="out_proj_s")
    y_prompt = peer_ffn(x1, norm_ffn[0], wq_bf, peer_subkeys[0], u_bf, v_bf).reshape(B, T, D)
    y_sample = peer_ffn(x1_s, norm_ffn[0], wq_bf, peer_subkeys[0], u_bf, v_bf).reshape(Bs, 1, D)
    s_kv =[kvs_s[gi].reshape(1, Bs, 1, 2, GROUP_HEADS, HEAD_DIM) for gi in range(N_GROUPS)]

    return (y_prompt, y_sample,
            ssm_p[None], conv_p[None], p_kv[0], p_kv[1], p_kv[2],
            ssm_s[None], conv_s[None], s_kv[0], s_kv[1], s_kv[2])
```

```python
import functools

import numpy as np
import jax
import jax.numpy as jnp
from jax import lax
from jax.experimental import pallas as pl
from jax.experimental.pallas import tpu as pltpu

F32 = jnp.float32
BF16 = jnp.bfloat16

D_MODEL = 4096
GDN_HEADS = 16
GDN_DK = 128
GDN_DV = 128
GDN_QK_W = GDN_HEADS * GDN_DK
GDN_V_W = GDN_HEADS * GDN_DV
CONV_W = 4
CONV_CH = 2 * GDN_QK_W + GDN_V_W
GDN_CHUNK = 64
ATTN_GROUPS = ((128, 1), (512, 4), (2048, 16))
N_GROUPS = 3
GROUP_HEADS = 4
HEAD_DIM = 128
GROUP_W = GROUP_HEADS * HEAD_DIM
ATTN_W = N_GROUPS * GROUP_W
Q_BLOCK = 128
N_BUCKETS = 32
MAX_DISTANCE = 2048
PEER_HEADS = 8
N_KEYS = 128
PEER_TOPK = 16
EPS = 1e-6
NEG_INF = -1e30

OFF_Z = CONV_CH
OFF_AQ = OFF_Z + GDN_V_W
OFF_AK = OFF_AQ + ATTN_W
OFF_AV = OFF_AK + ATTN_W
OFF_GA = OFF_AV + ATTN_W
OFF_GB = OFF_GA + D_MODEL
OFF_AB = OFF_GB + D_MODEL
N_PROJ = 21504

GDN_HB = 16
GDN_HG = GDN_HEADS // GDN_HB
VMEM_LIMIT = 56 * 1024 * 1024


def _cparams(sem):
    return pltpu.CompilerParams(dimension_semantics=sem, vmem_limit_bytes=VMEM_LIMIT)


def _sigmoid(x):
    return 0.5 * jnp.tanh(0.5 * x) + 0.5


def _silu(x):
    return x * _sigmoid(x)


def _softplus(x):
    return jnp.maximum(x, 0.0) + jnp.log(1.0 + jnp.exp(-jnp.abs(x)))


def _bdot(a, b):
    return jnp.dot(a.astype(BF16), b.astype(BF16), preferred_element_type=F32)


def _bdot_nt(a, b):
    return lax.dot_general(a.astype(BF16), b.astype(BF16), (((1,), (1,)), ((), ())),
                           preferred_element_type=F32)


def _bdot_tn(a, b):
    return lax.dot_general(a.astype(BF16), b.astype(BF16), (((0,), (0,)), ((), ())),
                           preferred_element_type=F32)


def _split_bf16(a):
    hi = a.astype(BF16)
    lo = (a - hi.astype(F32)).astype(BF16)
    return hi, lo


def _dot3(a_parts, b_parts):
    a_hi, a_lo = a_parts
    b_hi, b_lo = b_parts
    return (jnp.dot(a_hi, b_hi, preferred_element_type=F32)
            + jnp.dot(a_lo, b_hi, preferred_element_type=F32)
            + jnp.dot(a_hi, b_lo, preferred_element_type=F32))


def _dot3_nt(a, b):
    a_hi, a_lo = _split_bf16(a)
    b_hi, b_lo = _split_bf16(b)
    dims = (((1,), (1,)), ((), ()))
    return (lax.dot_general(a_hi, b_hi, dims, preferred_element_type=F32)
            + lax.dot_general(a_lo, b_hi, dims, preferred_element_type=F32)
            + lax.dot_general(a_hi, b_lo, dims, preferred_element_type=F32))


def _rmsnorm_body(x_ref, g_ref, o_ref):
    x = x_ref[...]
    ms = jnp.mean(x * x, axis=-1, keepdims=True)
    o_ref[...] = (x * lax.rsqrt(ms + EPS) * g_ref[...]).astype(o_ref.dtype)


def rmsnorm_bf16(x, gain):
    M, D = x.shape
    tm = min(M, 512)
    return pl.pallas_call(
        _rmsnorm_body,
        grid=(M // tm,),
        in_specs=[pl.BlockSpec((tm, D), lambda i: (i, 0)),
                  pl.BlockSpec((1, D), lambda i: (0, 0))],
        out_specs=pl.BlockSpec((tm, D), lambda i: (i, 0)),
        out_shape=jax.ShapeDtypeStruct((M, D), BF16),
        compiler_params=_cparams(("parallel",)),
        name="rmsnorm",
    )(x, gain.reshape(1, D))


def _mm_body(a_ref, b_ref, o_ref):
    o_ref[...] = jnp.dot(a_ref[...], b_ref[...], preferred_element_type=F32)


def _mm_res_body(a_ref, b_ref, r_ref, o_ref):
    o_ref[...] = r_ref[...] + jnp.dot(a_ref[...], b_ref[...], preferred_element_type=F32)


def matmul(a, b, residual=None, tm=1024, tn=1024, name="matmul"):
    M, K = a.shape
    N = b.shape[1]
    tm = min(tm, M)
    tn = min(tn, N)
    in_specs = [pl.BlockSpec((tm, K), lambda i, j: (i, 0)),
                pl.BlockSpec((K, tn), lambda i, j: (0, j))]
    args = [a, b]
    body = _mm_body
    if residual is not None:
        in_specs.append(pl.BlockSpec((tm, tn), lambda i, j: (i, j)))
        args.append(residual)
        body = _mm_res_body
    return pl.pallas_call(
        body,
        grid=(M // tm, N // tn),
        in_specs=in_specs,
        out_specs=pl.BlockSpec((tm, tn), lambda i, j: (i, j)),
        out_shape=jax.ShapeDtypeStruct((M, N), F32),
        compiler_params=_cparams(("parallel", "parallel")),
        name=name,
    )(*args)


def _mm_nt2_body(a_ref, a2_ref, bt_ref, o_ref, o2_ref):
    dims = (((1,), (1,)), ((), ()))
    o_ref[...] = lax.dot_general(a_ref[...], bt_ref[...], dims, preferred_element_type=F32)

    @pl.when(pl.program_id(1) == 0)
    def _():
        o2_ref[...] = lax.dot_general(a2_ref[...], bt_ref[...], dims, preferred_element_type=F32)


def matmul_nt2(a, a2, bt, tm=1024, tn=1024, name="matmul_nt2"):
    M, K = a.shape
    M2 = a2.shape[0]
    N = bt.shape[0]
    tm = min(tm, M)
    return pl.pallas_call(
        _mm_nt2_body,
        grid=(N // tn, M // tm),
        in_specs=[pl.BlockSpec((tm, K), lambda j, i: (i, 0)),
                  pl.BlockSpec((M2, K), lambda j, i: (0, 0)),
                  pl.BlockSpec((tn, K), lambda j, i: (j, 0))],
        out_specs=[pl.BlockSpec((tm, tn), lambda j, i: (i, j)),
                   pl.BlockSpec((M2, tn), lambda j, i: (0, j))],
        out_shape=[jax.ShapeDtypeStruct((M, N), F32), jax.ShapeDtypeStruct((M2, N), F32)],
        compiler_params=_cparams(("parallel", "arbitrary")),
        name=name,
    )(a, a2, bt)


def _gdn_chunk_body(q_ref, k_ref, v_ref, z_ref, ab_ref, cwq_ref, cwk_ref, cwv_ref,
                    alog_ref, dtb_ref, gn_ref, o_ref, s_out_ref,
                    xq_s, xk_s, xv_s, state_s):
    n = pl.program_id(2)
    C = GDN_CHUNK
    HW = GDN_HB * GDN_DK

    @pl.when(n == 0)
    def _():
        xq_s[0:8, :] = jnp.zeros((8, HW), F32)
        xk_s[0:8, :] = jnp.zeros((8, HW), F32)
        xv_s[0:8, :] = jnp.zeros((8, HW), F32)
        state_s[...] = jnp.zeros(state_s.shape, F32)

    def conv(x_ref, xs, cw_ref):
        xs[8:8 + C, :] = x_ref[0]
        y = cw_ref[CONV_W - 1:CONV_W, :] * xs[8:8 + C, :]
        for w in range(CONV_W - 1):
            y = y + cw_ref[w:w + 1, :] * xs[5 + w:5 + w + C, :]
        xs[0:8, :] = xs[C:C + 8, :]
        return _silu(y)

    q_all = conv(q_ref, xq_s, cwq_ref)
    k_all = conv(k_ref, xk_s, cwk_ref)
    v_all = conv(v_ref, xv_s, cwv_ref)

    ab = ab_ref[0, 0]
    g_all = -jnp.exp(alog_ref[0]) * _softplus(ab + dtb_ref[0])
    beta_all = _sigmoid(ab)
    ri = lax.broadcasted_iota(jnp.int32, (C, C), 0)
    ci = lax.broadcasted_iota(jnp.int32, (C, C), 1)
    tri = ri >= ci
    strict = ri > ci
    rowid = lax.broadcasted_iota(jnp.int32, (C, 128), 0)
    gc_all = g_all
    shift = 1
    while shift < C:
        gc_all = gc_all + jnp.where(rowid >= shift, pltpu.roll(gc_all, shift, axis=0), 0.0)
        shift *= 2
    gc_t = jnp.concatenate([gc_all, jnp.zeros((128 - C, 128), F32)], axis=0).T

    H = range(GDN_HB)
    sls = [slice(j * GDN_DK, (j + 1) * GDN_DK) for j in H]
    qh, kh, kb, gcol, decay, rhs = [], [], [], [], [], []
    for j in H:
        q = q_all[:, sls[j]]
        k = k_all[:, sls[j]]
        q = q * lax.rsqrt(jnp.sum(q * q, axis=-1, keepdims=True) + EPS) * (GDN_DK ** -0.5)
        k = k * lax.rsqrt(jnp.sum(k * k, axis=-1, keepdims=True) + EPS)
        beta = beta_all[:, GDN_HB + j:GDN_HB + j + 1]
        gc = gc_all[:, j:j + 1]
        grow = gc_t[j:j + 1, :C]
        qh.append(q)
        kh.append(k)
        kb.append(k * beta)
        gcol.append(gc)
        decay.append(jnp.where(tri, jnp.exp(jnp.where(tri, gc - grow, 0.0)), 0.0))
        rhs.append(jnp.concatenate([v_all[:, sls[j]] * beta, kb[j] * jnp.exp(gc)], axis=-1))
    kq = [_bdot_nt(jnp.concatenate([kb[j], qh[j]], axis=0), kh[j]) for j in H]
    intra = [jnp.where(tri, kq[j][C:] * decay[j], 0.0) for j in H]
    acc = [-jnp.where(strict, kq[j][:C] * decay[j], 0.0) for j in H]
    parts = [_split_bf16(acc[j]) for j in H]
    pw = [_dot3(parts[j], parts[j]) for j in H]
    for r in range(5):
        pw_parts = [_split_bf16(pw[j]) for j in H]
        acc_parts = [_split_bf16(acc[j]) for j in H]
        acc = [acc[j] + pw[j] + _dot3(pw_parts[j], acc_parts[j]) for j in H]
        if r < 4:
            pw = [_dot3(pw_parts[j], pw_parts[j]) for j in H]
    sol = [rhs[j] + _dot3(_split_bf16(acc[j]), _split_bf16(rhs[j])) for j in H]
    S = [state_s[j] for j in H]
    wq_s = [_bdot(jnp.concatenate([sol[j][:, GDN_DV:], qh[j] * jnp.exp(gcol[j])], axis=0), S[j])
            for j in H]
    v_new = [sol[j][:, :GDN_DV] - wq_s[j][:C] for j in H]
    o = [wq_s[j][C:] + _bdot(intra[j], v_new[j]) for j in H]
    for j in H:
        g_last = gcol[j][C - 1:C, :]
        state_s[j] = (S[j] * jnp.exp(g_last)
                      + _bdot_tn(kh[j] * jnp.exp(g_last - gcol[j]), v_new[j]))
    for j in H:
        ms = jnp.mean(o[j] * o[j], axis=-1, keepdims=True)
        on = o[j] * lax.rsqrt(ms + EPS) * gn_ref[...]
        o_ref[0, :, sls[j]] = (on * _silu(z_ref[0, :, sls[j]])).astype(o_ref.dtype)

    @pl.when(n == pl.num_programs(2) - 1)
    def _():
        s_out_ref[0] = state_s[...]


def gdn_prompt(proj3, ab_g, conv_w, alog_g, dtb_g, gdn_norm):
    B, T, _ = proj3.shape
    C = GDN_CHUNK
    N = T // C
    HW = GDN_HB * GDN_DK
    nq = GDN_QK_W // HW

    def col(base):
        return lambda b, hg, n: (b, n, base + hg)

    def cwcol(base):
        return lambda b, hg, n: (0, base + hg)

    return pl.pallas_call(
        _gdn_chunk_body,
        grid=(B, GDN_HG, N),
        in_specs=[
            pl.BlockSpec((1, C, HW), col(0)),
            pl.BlockSpec((1, C, HW), col(nq)),
            pl.BlockSpec((1, C, HW), col(2 * nq)),
            pl.BlockSpec((1, C, HW), col(OFF_Z // HW)),
            pl.BlockSpec((1, 1, C, 128), lambda b, hg, n: (b, hg, n, 0)),
            pl.BlockSpec((CONV_W, HW), cwcol(0)),
            pl.BlockSpec((CONV_W, HW), cwcol(nq)),
            pl.BlockSpec((CONV_W, HW), cwcol(2 * nq)),
            pl.BlockSpec((1, 1, 128), lambda b, hg, n: (hg, 0, 0)),
            pl.BlockSpec((1, 1, 128), lambda b, hg, n: (hg, 0, 0)),
            pl.BlockSpec((1, GDN_DV), lambda b, hg, n: (0, 0)),
        ],
        out_specs=[
            pl.BlockSpec((1, C, HW), lambda b, hg, n: (b, n, hg)),
            pl.BlockSpec((1, GDN_HB, GDN_DK, GDN_DV), lambda b, hg, n: (b, hg, 0, 0)),
        ],
        out_shape=[
            jax.ShapeDtypeStruct((B, T, GDN_V_W), BF16),
            jax.ShapeDtypeStruct((B, GDN_HEADS, GDN_DK, GDN_DV), F32),
        ],
        scratch_shapes=[
            pltpu.VMEM((C + 8, HW), F32),
            pltpu.VMEM((C + 8, HW), F32),
            pltpu.VMEM((C + 8, HW), F32),
            pltpu.VMEM((GDN_HB, GDN_DK, GDN_DV), F32),
        ],
        compiler_params=_cparams(("parallel", "parallel", "arbitrary")),
        name="gdn_prompt",
    )(proj3, proj3, proj3, proj3, ab_g, conv_w, conv_w, conv_w, alog_g, dtb_g,
      gdn_norm.reshape(1, GDN_DV))


def _gdn_step_body(x_ref, z_ref, ab_ref, cs_ref, s_ref, cw_ref, alog_ref, dtb_ref, gn_ref,
                   o_ref, cs_out_ref, s_out_ref):
    x = x_ref[0]
    y = cw_ref[CONV_W - 1] * x
    for w in range(CONV_W - 1):
        y = y + cw_ref[w] * cs_ref[0, w]
    y = _silu(y)
    cs_out_ref[0, 0] = cs_ref[0, 1]
    cs_out_ref[0, 1] = cs_ref[0, 2]
    cs_out_ref[0, 2] = x
    r_ab = (OFF_AB // 128) % 8
    ab = ab_ref[0, r_ab:r_ab + 1, :]
    g_all = -jnp.exp(alog_ref[...]) * _softplus(ab + dtb_ref[...])
    beta_all = _sigmoid(ab)
    eye = (lax.broadcasted_iota(jnp.int32, (GDN_DK, GDN_DK), 0)
           == lax.broadcasted_iota(jnp.int32, (GDN_DK, GDN_DK), 1))

    def to_col(row):
        return jnp.sum(jnp.where(eye, jnp.broadcast_to(row, (GDN_DK, GDN_DK)), 0.0),
                       axis=1, keepdims=True)

    for h in range(GDN_HEADS):
        q = y[h:h + 1, :]
        k = y[GDN_HEADS + h:GDN_HEADS + h + 1, :]
        v = y[2 * GDN_HEADS + h:2 * GDN_HEADS + h + 1, :]
        q = q * lax.rsqrt(jnp.sum(q * q, axis=-1, keepdims=True) + EPS) * (GDN_DK ** -0.5)
        k = k * lax.rsqrt(jnp.sum(k * k, axis=-1, keepdims=True) + EPS)
        g = g_all[:, h:h + 1]
        beta = beta_all[:, GDN_HEADS + h:GDN_HEADS + h + 1]
        eg = jnp.exp(g)
        S = s_ref[0, h]
        kcol = to_col(k)
        qcol = to_col(q)
        k_s = jnp.sum(S * kcol, axis=0, keepdims=True)
        q_s = jnp.sum(S * qcol, axis=0, keepdims=True)
        v_new = v * beta - (beta * eg) * k_s
        o = eg * q_s + jnp.sum(q * k, axis=-1, keepdims=True) * v_new
        s_out_ref[0, h] = S * eg + kcol * v_new
        ms = jnp.mean(o * o, axis=-1, keepdims=True)
        on = o * lax.rsqrt(ms + EPS) * gn_ref[...]
        o_ref[0, h:h + 1, :] = on * _silu(z_ref[0, h:h + 1, :])


def gdn_step(proj_s, conv_state, ssm_state, conv_w, a_log, dt_bias, gdn_norm):
    Bs = proj_s.shape[0]
    nrow = N_PROJ // 128
    p3 = proj_s.reshape(Bs, nrow, 128)
    cs4 = conv_state.reshape(Bs, CONV_W - 1, CONV_CH // 128, 128)
    cw3 = conv_w.reshape(CONV_W, CONV_CH // 128, 128)
    pad = jnp.zeros((128 - GDN_HEADS,), F32)
    alog = jnp.concatenate([a_log, pad]).reshape(1, 128)
    dtb = jnp.concatenate([dt_bias, pad]).reshape(1, 128)
    nr = CONV_CH // 128
    o, cs_new, s_new = pl.pallas_call(
        _gdn_step_body,
        grid=(Bs,),
        in_specs=[
            pl.BlockSpec((1, nr, 128), lambda b: (b, 0, 0)),
            pl.BlockSpec((1, GDN_HEADS, 128), lambda b: (b, OFF_Z // 128 // GDN_HEADS, 0)),
            pl.BlockSpec((1, 8, 128), lambda b: (b, OFF_AB // 128 // 8, 0)),
            pl.BlockSpec((1, CONV_W - 1, nr, 128), lambda b: (b, 0, 0, 0)),
            pl.BlockSpec((1, GDN_HEADS, GDN_DK, GDN_DV), lambda b: (b, 0, 0, 0)),
            pl.BlockSpec((CONV_W, nr, 128), lambda b: (0, 0, 0)),
            pl.BlockSpec((1, 128), lambda b: (0, 0)),
            pl.BlockSpec((1, 128), lambda b: (0, 0)),
            pl.BlockSpec((1, GDN_DV), lambda b: (0, 0)),
        ],
        out_specs=[
            pl.BlockSpec((1, GDN_HEADS, GDN_DV), lambda b: (b, 0, 0)),
            pl.BlockSpec((1, CONV_W - 1, nr, 128), lambda b: (b, 0, 0, 0)),
            pl.BlockSpec((1, GDN_HEADS, GDN_DK, GDN_DV), lambda b: (b, 0, 0, 0)),
        ],
        out_shape=[
            jax.ShapeDtypeStruct((Bs, GDN_HEADS, GDN_DV), F32),
            jax.ShapeDtypeStruct((Bs, CONV_W - 1, nr, 128), F32),
            jax.ShapeDtypeStruct((Bs, GDN_HEADS, GDN_DK, GDN_DV), F32),
        ],
        compiler_params=_cparams(("parallel",)),
        name="gdn_step",
    )(p3, p3, p3, cs4, ssm_state, cw3, alog, dtb, gdn_norm.reshape(1, GDN_DV))
    return (o.reshape(Bs, GDN_V_W), cs_new.reshape(Bs, CONV_W - 1, CONV_CH), s_new)


def _qknorm_body(q_ref, k_ref, v_ref, qg_ref, kg_ref, qn_ref, kv_ref):
    for h in range(GROUP_HEADS):
        sl = slice(h * HEAD_DIM, (h + 1) * HEAD_DIM)
        q = q_ref[:, sl]
        qn = q * lax.rsqrt(jnp.mean(q * q, axis=-1, keepdims=True) + EPS) * qg_ref[...]
        qn_ref[:, sl] = (qn * (HEAD_DIM ** -0.5)).astype(qn_ref.dtype)
        k = k_ref[:, sl]
        kv_ref[:, sl] = k * lax.rsqrt(jnp.mean(k * k, axis=-1, keepdims=True) + EPS) * kg_ref[...]
    kv_ref[:, GROUP_W:] = v_ref[...]


def qk_norm(proj, q_norm, k_norm, q_dtype):
    M = proj.shape[0]
    tm = min(M, 512)
    blk = GROUP_W
    return pl.pallas_call(
        _qknorm_body,
        grid=(M // tm, N_GROUPS),
        in_specs=[
            pl.BlockSpec((tm, blk), lambda i, g: (i, OFF_AQ // blk + g)),
            pl.BlockSpec((tm, blk), lambda i, g: (i, OFF_AK // blk + g)),
            pl.BlockSpec((tm, blk), lambda i, g: (i, OFF_AV // blk + g)),
            pl.BlockSpec((1, HEAD_DIM), lambda i, g: (0, 0)),
            pl.BlockSpec((1, HEAD_DIM), lambda i, g: (0, 0)),
        ],
        out_specs=[
            pl.BlockSpec((tm, blk), lambda i, g: (i, g)),
            pl.BlockSpec((tm, 2 * blk), lambda i, g: (i, g)),
        ],
        out_shape=[jax.ShapeDtypeStruct((M, ATTN_W), q_dtype),
                   jax.ShapeDtypeStruct((M, N_GROUPS * 2 * GROUP_W), F32)],
        compiler_params=_cparams(("parallel", "parallel")),
        name="qk_norm",
    )(proj, proj, proj, q_norm.reshape(1, HEAD_DIM), k_norm.reshape(1, HEAD_DIM))


def _t5_bucket(dist):
    max_exact = N_BUCKETS // 2
    d = np.maximum(dist, max_exact).astype(np.float32)
    large = max_exact + (np.log(d / max_exact) / np.log(MAX_DISTANCE / max_exact)
                         * (N_BUCKETS - max_exact)).astype(np.int32)
    large = np.minimum(large, N_BUCKETS - 1)
    return np.where(dist < max_exact, dist, large).astype(np.int32)


def _group_bias(rel_bias, gi):
    window, dilation = ATTN_GROUPS[gi]
    n_keys = window // dilation + 1
    buckets = _t5_bucket(np.arange(n_keys) * dilation)
    return rel_bias[buckets][:, gi * GROUP_HEADS:(gi + 1) * GROUP_HEADS].T.astype(F32)


def _attn_prompt_body(T, q0_ref, q1_ref, q2_ref, k0_ref, v0_ref, k1_ref, v1_ref, k2_ref, v2_ref,
                      bias_ref, y_ref, m_s, l_s, o_s):
    QB = Q_BLOCK
    q_refs = (q0_ref, q1_ref, q2_ref)
    k_refs = (k0_ref, k1_ref, k2_ref)
    v_refs = (v0_ref, v1_ref, v2_ref)
    for gi, (window, d) in enumerate(ATTN_GROUPS):
        nq = T // (d * QB)
        for r in range(d):
            for qi in range(nq):
                rows = pl.ds(r + d * QB * qi, QB, stride=d) if d > 1 else pl.ds(QB * qi, QB)
                q = q_refs[gi][0, rows, :].astype(BF16)
                kc = k_refs[gi][0, rows, :]
                vc = v_refs[gi][0, rows, :]
                if qi > 0:
                    prev = (pl.ds(r + d * QB * (qi - 1), QB, stride=d) if d > 1
                            else pl.ds(QB * (qi - 1), QB))
                    kcat = jnp.concatenate([k_refs[gi][0, prev, :], kc], axis=0).astype(BF16)
                    vcat = jnp.concatenate([v_refs[gi][0, prev, :], vc], axis=0).astype(BF16)
                    bias = bias_ref[gi, 0]
                else:
                    kcat = kc.astype(BF16)
                    vcat = vc.astype(BF16)
                    bias = bias_ref[gi, 0, :, QB:]
                s = lax.dot_general(q, kcat, (((1,), (1,)), ((), ())),
                                    preferred_element_type=F32) + bias
                m_blk = jnp.max(s, axis=-1, keepdims=True)
                if gi == 0:
                    m_new = m_blk
                else:
                    m_old = m_s[rows, :][:, 0:1]
                    m_new = jnp.maximum(m_old, m_blk)
                p = jnp.exp(s - m_new)
                l_new = jnp.sum(p, axis=-1, keepdims=True)
                o_new = jnp.dot(p.astype(BF16), vcat, preferred_element_type=F32)
                if gi > 0:
                    alpha = jnp.exp(m_old - m_new)
                    l_new = l_new + alpha * l_s[rows, :][:, 0:1]
                    o_new = o_new + alpha * o_s[rows, :]
                m_s[rows, :] = jnp.broadcast_to(m_new, (QB, HEAD_DIM))
                l_s[rows, :] = jnp.broadcast_to(l_new, (QB, HEAD_DIM))
                o_s[rows, :] = o_new
    y_ref[0] = (o_s[...] / l_s[...]).astype(y_ref.dtype)


def attn_prompt(qn3, kv3, bias_all):
    B, T, _ = qn3.shape
    blk = (1, T, HEAD_DIM)

    def qspec(gi):
        return pl.BlockSpec(blk, lambda b, h: (b, 0, gi * GROUP_HEADS + h))

    def kvspec(gi, is_v):
        return pl.BlockSpec(blk, lambda b, h: (b, 0, (2 * gi + is_v) * GROUP_HEADS + h))

    in_specs = [qspec(0), qspec(1), qspec(2)]
    for gi in range(N_GROUPS):
        in_specs += [kvspec(gi, 0), kvspec(gi, 1)]
    in_specs.append(pl.BlockSpec((N_GROUPS, 1, Q_BLOCK, 2 * Q_BLOCK), lambda b, h: (0, h, 0, 0)))
    y = pl.pallas_call(
        functools.partial(_attn_prompt_body, T),
        grid=(B, GROUP_HEADS),
        in_specs=in_specs,
        out_specs=pl.BlockSpec(blk, lambda b, h: (b, 0, h)),
        out_shape=jax.ShapeDtypeStruct((B, T, GROUP_W), BF16),
        scratch_shapes=[pltpu.VMEM((T, HEAD_DIM), F32), pltpu.VMEM((T, HEAD_DIM), F32),
                        pltpu.VMEM((T, HEAD_DIM), F32)],
        compiler_params=_cparams(("parallel", "parallel")),
        name="attn_prompt",
    )(qn3, qn3, qn3, kv3, kv3, kv3, kv3, kv3, kv3, bias_all)
    return y.reshape(B * T, GROUP_W)


def _prompt_bias_matrix(rel_bias, gi):
    QB = Q_BLOCK
    bias = _group_bias(rel_bias, gi)
    row0 = jnp.concatenate([bias[:, ::-1], jnp.full((GROUP_HEADS, QB), NEG_INF, F32)], axis=1)
    flat = jnp.tile(row0, (1, QB))[:, :QB * 2 * QB]
    return flat.reshape(GROUP_HEADS, QB, 2 * QB)


def _attn_decode_body(q_ref, n0_ref, n1_ref, n2_ref, c0_ref, c1_ref, c2_ref, bc_ref, bn_ref, y_ref):
    new_refs = (n0_ref, n1_ref, n2_ref)
    cache_refs = (c0_ref, c1_ref, c2_ref)
    outs, lds = [], []
    for gi in range(N_GROUPS):
        q = q_ref[0, gi * GROUP_HEADS:(gi + 1) * GROUP_HEADS, :]
        kc = cache_refs[gi][:, 0]
        vc = cache_refs[gi][:, 1]
        kn = new_refs[gi][0, 0:GROUP_HEADS, :]
        vn = new_refs[gi][0, GROUP_HEADS:2 * GROUP_HEADS, :]
        s_c = jnp.sum(kc * q[None], axis=-1, keepdims=True) + bc_ref[gi, :, :, 0:1]
        s_n = jnp.sum(kn * q, axis=-1, keepdims=True) + bn_ref[gi, :, 0:1]
        m = jnp.maximum(jnp.max(s_c, axis=0), s_n)
        p_c = jnp.exp(s_c - m[None])
        p_n = jnp.exp(s_n - m)
        l = jnp.sum(p_c, axis=0) + p_n
        outs.append((jnp.sum(p_c * vc, axis=0) + p_n * vn) / l)
        lds.append(m + jnp.log(l))
    m = jnp.maximum(jnp.maximum(lds[0], lds[1]), lds[2])
    es = [jnp.exp(x - m) for x in lds]
    den = es[0] + es[1] + es[2]
    y_ref[0] = outs[0] * (es[0] / den) + outs[1] * (es[1] / den) + outs[2] * (es[2] / den)


def attn_decode(qn_s, kv_new, caches, rel_bias):
    Bs = qn_s.shape[0]
    q3 = qn_s.reshape(Bs, N_GROUPS * GROUP_HEADS, HEAD_DIM)
    new3 = [x.reshape(Bs, 2 * GROUP_HEADS, HEAD_DIM) for x in kv_new]
    cviews, bcs, bns = [], [], []
    for gi, (window, d) in enumerate(ATTN_GROUPS):
        nk = window // d
        assert caches[gi].shape[1] == window and nk == Q_BLOCK
        cviews.append(caches[gi].reshape(Bs, nk, d, 2, GROUP_HEADS, HEAD_DIM))
        bias = _group_bias(rel_bias, gi)
        bc = bias[:, nk - np.arange(nk)]
        bcs.append(jnp.broadcast_to(bc.T[:, :, None], (nk, GROUP_HEADS, HEAD_DIM)))
        bns.append(jnp.broadcast_to(bias[:, 0:1], (GROUP_HEADS, HEAD_DIM)))
    bc_all = jnp.stack(bcs, 0)
    bn_all = jnp.stack(bns, 0)
    nspec = pl.BlockSpec((1, 2 * GROUP_HEADS, HEAD_DIM), lambda b: (b, 0, 0))
    cspec = pl.BlockSpec((None, Q_BLOCK, None, 2, GROUP_HEADS, HEAD_DIM),
                         lambda b: (b, 0, 0, 0, 0, 0))
    y = pl.pallas_call(
        _attn_decode_body,
        grid=(Bs,),
        in_specs=[
            pl.BlockSpec((1, N_GROUPS * GROUP_HEADS, HEAD_DIM), lambda b: (b, 0, 0)),
            nspec, nspec, nspec, cspec, cspec, cspec,
            pl.BlockSpec((N_GROUPS, Q_BLOCK, GROUP_HEADS, HEAD_DIM), lambda b: (0, 0, 0, 0)),
            pl.BlockSpec((N_GROUPS, GROUP_HEADS, HEAD_DIM), lambda b: (0, 0, 0)),
        ],
        out_specs=pl.BlockSpec((1, GROUP_HEADS, HEAD_DIM), lambda b: (b, 0, 0)),
        out_shape=jax.ShapeDtypeStruct((Bs, GROUP_HEADS, HEAD_DIM), F32),
        compiler_params=_cparams(("parallel",)),
        name="attn_decode",
    )(q3, *new3, *cviews, bc_all, bn_all)
    return y.reshape(Bs, GROUP_W)


def _merge_body(ya_ref, yb_ref, wa_ref, wb_ref, ga_ref, gb_ref, o_ref):
    pa = jnp.dot(ya_ref[...], wa_ref[...], preferred_element_type=F32)
    pb = jnp.dot(yb_ref[...], wb_ref[...], preferred_element_type=F32)
    o_ref[...] = (_sigmoid(ga_ref[...]) * pa + _sigmoid(gb_ref[...]) * pb).astype(o_ref.dtype)


def branch_merge(ya, yb, wa, wb, proj):
    M = ya.shape[0]
    tm = min(M, 1024)
    tn = 512
    return pl.pallas_call(
        _merge_body,
        grid=(M // tm, D_MODEL // tn),
        in_specs=[
            pl.BlockSpec((tm, GDN_V_W), lambda i, j: (i, 0)),
            pl.BlockSpec((tm, GROUP_W), lambda i, j: (i, 0)),
            pl.BlockSpec((GDN_V_W, tn), lambda i, j: (0, j)),
            pl.BlockSpec((GROUP_W, tn), lambda i, j: (0, j)),
            pl.BlockSpec((tm, tn), lambda i, j: (i, OFF_GA // tn + j)),
            pl.BlockSpec((tm, tn), lambda i, j: (i, OFF_GB // tn + j)),
        ],
        out_specs=pl.BlockSpec((tm, tn), lambda i, j: (i, j)),
        out_shape=jax.ShapeDtypeStruct((M, D_MODEL), BF16),
        compiler_params=_cparams(("parallel", "parallel")),
        name="branch_merge",
    )(ya, yb, wa, wb, proj, proj)


def _extract_top(x, count, tie_safe, want_rank=True):
    R = x.shape[0]
    rid = lax.broadcasted_iota(jnp.int32, x.shape, 0).astype(F32)
    vid = lax.broadcasted_iota(jnp.int32, (count, x.shape[1]), 0)
    vals = jnp.zeros((count, x.shape[1]), F32)
    rank = jnp.full(x.shape, float(count), F32)
    for a in range(count):
        m = jnp.max(x, axis=0, keepdims=True)
        hit = x == m
        if tie_safe:
            first = jnp.min(jnp.where(hit, rid, float(R)), axis=0, keepdims=True)
            hit = rid == first
        x = jnp.where(hit, -jnp.inf, x)
        if want_rank:
            rank = jnp.where(hit, float(a), rank)
        vals = jnp.where(vid == a, m, vals)
    return vals, (rank if want_rank else x)


CAND_HALF = PEER_TOPK // 2
N_CAND = PEER_TOPK + (CAND_HALF - 1) * CAND_HALF + CAND_HALF


def _peer_topk_body(q_ref, sk_ref, cnt_ref, r2_ref, e1_ref, e2_ref, cand_s, s1_s, s2_s):
    K = PEER_TOPK
    KH = CAND_HALF

    def select(h, s1, s2, tie_safe):
        v1, rank1 = _extract_top(s1, K, tie_safe)
        v2, rank2 = _extract_top(s2, K, tie_safe)
        cand_s[0:K, :] = v2 + v1[0:1, :]
        for a in range(1, KH):
            cand_s[K + (a - 1) * KH:K + a * KH, :] = v2[0:KH, :] + v1[a:a + 1, :]
        cand_s[N_CAND - KH:N_CAND, :] = v1[KH:K, :] + v2[0:1, :]
        best, cleft = _extract_top(cand_s[...], K, tie_safe, want_rank=False)
        picked = jnp.where(cleft == -jnp.inf, 1.0, 0.0)
        vid = lax.broadcasted_iota(jnp.int32, (KH, picked.shape[1]), 0)
        low = jnp.where(vid == 0, jnp.sum(picked[0:K], axis=0, keepdims=True), 0.0)
        for a in range(1, KH):
            rows = picked[K + (a - 1) * KH:K + a * KH]
            low = jnp.where(vid == a, jnp.sum(rows, axis=0, keepdims=True), low)
        count = jnp.concatenate([low, picked[N_CAND - KH:N_CAND]], axis=0)
        cnt1 = jnp.zeros(rank1.shape, F32)
        for a in range(K):
            cnt1 = jnp.where(rank1 == float(a), count[a:a + 1, :], cnt1)
        cmax = best[0:1, :]
        z = jnp.sum(jnp.exp(best - cmax), axis=0, keepdims=True)
        in1 = jnp.where(rank1 < float(K), 1.0, 0.0)
        in2 = jnp.where(rank2 < float(K), 1.0, 0.0)
        cnt_ref[h] = cnt1
        r2_ref[h] = rank2
        e1_ref[h] = in1 * jnp.exp(s1 - v1[0:1, :])
        e2_ref[h] = in2 * jnp.exp(s2 - v2[0:1, :]) / z * 0.5
        return (jnp.sum(in1, axis=0, keepdims=True) + jnp.sum(in2, axis=0, keepdims=True)
                + jnp.sum(picked, axis=0, keepdims=True) - 3.0 * K)

    def head(h, carry):
        off = pl.multiple_of(h * (2 * N_KEYS), 2 * N_KEYS)
        q1 = q_ref[:, pl.ds(off, N_KEYS)]
        q2 = q_ref[:, pl.ds(pl.multiple_of(off + N_KEYS, N_KEYS), N_KEYS)]
        s1_s[...] = _dot3_nt(sk_ref[0], q1)
        s2_s[...] = _dot3_nt(sk_ref[1], q2)
        extra = select(h, s1_s[...], s2_s[...], tie_safe=False)

        @pl.when(jnp.max(extra) > 0.0)
        def _():
            select(h, s1_s[...], s2_s[...], tie_safe=True)

        return carry

    lax.fori_loop(0, PEER_HEADS, head, 0)


def peer_topk(query, subkeys):
    M = query.shape[0]
    tm = min(M, 512)
    big = jax.ShapeDtypeStruct((PEER_HEADS, N_KEYS, M), F32)
    bspec = pl.BlockSpec((PEER_HEADS, N_KEYS, tm), lambda i: (0, 0, i))
    return pl.pallas_call(
        _peer_topk_body,
        grid=(M // tm,),
        in_specs=[pl.BlockSpec((tm, PEER_HEADS * 2 * N_KEYS), lambda i: (i, 0)),
                  pl.BlockSpec((2, N_KEYS, N_KEYS), lambda i: (0, 0, 0))],
        out_specs=[bspec, bspec, bspec, bspec],
        out_shape=[big, big, big, big],
        scratch_shapes=[pltpu.VMEM((N_CAND, tm), F32), pltpu.VMEM((N_KEYS, tm), F32),
                        pltpu.VMEM((N_KEYS, tm), F32)],
        compiler_params=_cparams(("parallel",)),
        name="peer_topk",
    )(query, subkeys)


def _gelu_x2(x):
    return x * (1.0 + lax.erf(x * (2.0 ** -0.5)))


GATE_ROWS = 8


def _peer_gate_body(hf_ref, u_ref, cnt_ref, r2_ref, e1_ref, e2_ref, w_ref, g_s):
    j = pl.program_id(1)
    tm = hf_ref.shape[0]
    grp = pl.multiple_of(j * GATE_ROWS, GATE_ROWS)

    for kk in range(GATE_ROWS):
        for tc in range(tm // 128):
            ts = slice(tc * 128, (tc + 1) * 128)
            acc = None
            for h in range(PEER_HEADS):
                cnt_row = cnt_ref[h, pl.ds(grp, GATE_ROWS), ts][kk:kk + 1, :]
                e1_row = e1_ref[h, pl.ds(grp, GATE_ROWS), ts][kk:kk + 1, :]
                term = jnp.where(r2_ref[h, :, ts] < cnt_row, e2_ref[h, :, ts] * e1_row, 0.0)
                acc = term if acc is None else acc + term
            g_s[ts, kk * N_KEYS:(kk + 1) * N_KEYS] = acc.T
    hmat = lax.dot_general(hf_ref[...], u_ref[...], (((1,), (1,)), ((), ())),
                           preferred_element_type=F32)
    w_ref[...] = (g_s[...] * _gelu_x2(hmat)).astype(w_ref.dtype)


def peer_gate_weights(hf, u_tab, cnt, r2, e1, e2, tm=1024):
    M = hf.shape[0]
    tm = min(M, tm)
    tn = GATE_ROWS * N_KEYS
    E = u_tab.shape[0]
    once = pl.Buffered(1)
    sspec = pl.BlockSpec((PEER_HEADS, N_KEYS, tm), lambda i, j: (0, 0, i), pipeline_mode=once)
    return pl.pallas_call(
        _peer_gate_body,
        grid=(M // tm, E // tn),
        in_specs=[
            pl.BlockSpec((tm, D_MODEL), lambda i, j: (i, 0), pipeline_mode=once),
            pl.BlockSpec((tn, D_MODEL), lambda i, j: (j, 0)),
            sspec, sspec, sspec, sspec,
        ],
        out_specs=pl.BlockSpec((tm, tn), lambda i, j: (i, j)),
        out_shape=jax.ShapeDtypeStruct((M, E), BF16),
        scratch_shapes=[pltpu.VMEM((tm, tn), F32)],
        compiler_params=_cparams(("parallel", "arbitrary")),
        name="peer_gate",
    )(hf, u_tab, cnt, r2, e1, e2)


def _mm_kres_body(a_ref, b_ref, r_ref, o_ref):
    k = pl.program_id(2)

    @pl.when(k == 0)
    def _():
        o_ref[...] = r_ref[...]

    o_ref[...] += jnp.dot(a_ref[...], b_ref[...], preferred_element_type=F32)


def matmul_kres(a, b, residual, tm=1024, tn=1024, tk=4096, name="matmul_k"):
    M, K = a.shape
    N = b.shape[1]
    tm = min(tm, M)
    return pl.pallas_call(
        _mm_kres_body,
        grid=(M // tm, N // tn, K // tk),
        in_specs=[pl.BlockSpec((tm, tk), lambda i, j, k: (i, k)),
                  pl.BlockSpec((tk, tn), lambda i, j, k: (k, j)),
                  pl.BlockSpec((tm, tn), lambda i, j, k: (i, j))],
        out_specs=pl.BlockSpec((tm, tn), lambda i, j, k: (i, j)),
        out_shape=jax.ShapeDtypeStruct((M, N), F32),
        compiler_params=_cparams(("parallel", "parallel", "arbitrary")),
        name=name,
    )(a, b, residual)


def peer_ffn(x1, norm_ffn, wq_bf, subkeys, u_bf, v_bf):
    M = x1.shape[0]
    Mp = max(M, 128)
    if Mp != M:
        x1 = jnp.concatenate([x1, jnp.zeros((Mp - M, D_MODEL), F32)], axis=0)
    hf = rmsnorm_bf16(x1, norm_ffn)
    query = matmul(hf, wq_bf, name="peer_query")
    cnt, r2, e1, e2 = peer_topk(query, subkeys)
    w = peer_gate_weights(hf, u_bf, cnt, r2, e1, e2)
    return matmul_kres(w, v_bf, x1, name="peer_out")[:M]


W_BLK = 512
N_AB = 2 * GDN_HEADS
J_AB = OFF_AB // W_BLK


def _prep_w_in_body(a_ref, b_ref, o_ref):
    j = pl.program_id(0)

    @pl.when(j < OFF_AQ // W_BLK)
    def _():
        o_ref[...] = a_ref[...].astype(o_ref.dtype)

    @pl.when(jnp.logical_and(j >= OFF_AQ // W_BLK, j < J_AB))
    def _():
        o_ref[0:W_BLK - N_AB, :] = a_ref[N_AB:, :].astype(o_ref.dtype)
        o_ref[W_BLK - N_AB:, :] = b_ref[...].astype(o_ref.dtype)

    @pl.when(j == J_AB)
    def _():
        o_ref[0:N_AB, :] = b_ref[...].astype(o_ref.dtype)
        o_ref[N_AB:, :] = jnp.zeros((W_BLK - N_AB, o_ref.shape[1]), o_ref.dtype)


def _prep_w_in(w_in_t):
    D = w_in_t.shape[1]
    j0 = OFF_AQ // W_BLK
    sub = W_BLK // N_AB
    return pl.pallas_call(
        _prep_w_in_body,
        grid=(N_PROJ // W_BLK,),
        in_specs=[
            pl.BlockSpec((W_BLK, D), lambda j: (jnp.where(j == J_AB, j0, j), 0)),
            pl.BlockSpec((N_AB, D), lambda j: (jnp.where(j == J_AB, j0 * sub,
                                                         jnp.where(j >= j0, (j + 1) * sub, 0)), 0)),
        ],
        out_specs=pl.BlockSpec((W_BLK, D), lambda j: (j, 0)),
        out_shape=jax.ShapeDtypeStruct((N_PROJ, D), BF16),
        compiler_params=_cparams(("parallel",)),
        name="prep_w_in",
    )(w_in_t, w_in_t)


def _gdn_group_params(a_log, dt_bias):
    pad = jnp.zeros((GDN_HG, 128 - GDN_HB), F32)
    alog = jnp.concatenate([a_log.reshape(GDN_HG, GDN_HB), pad], axis=1).reshape(GDN_HG, 1, 128)
    dtb = jnp.concatenate([dt_bias.reshape(GDN_HG, GDN_HB), pad], axis=1).reshape(GDN_HG, 1, 128)
    return alog, dtb


def _ab_grouped(proj, B, T):
    ab = proj[:, OFF_AB:OFF_AB + 2 * GDN_HEADS].reshape(B, T, 2, GDN_HG, GDN_HB)
    ab = ab.transpose(0, 3, 1, 2, 4).reshape(B, GDN_HG, T, 2 * GDN_HB)
    return jnp.concatenate([ab, jnp.zeros((B, GDN_HG, T, 128 - 2 * GDN_HB), F32)], axis=-1)


def kernel(x_prompt, x_sample, state_ssm, state_conv, cache_kv_g0, cache_kv_g1, cache_kv_g2, rel_bias, norm_mix, w_in, conv_w, a_log, dt_bias, gdn_norm, q_norm, k_norm, w_branch_a, w_branch_b, w_out, norm_ffn, peer_w_query, peer_subkeys, peer_u, peer_v):
    B, T, D = x_prompt.shape
    Bs = x_sample.shape[0]
    assert state_ssm.shape[0] == 1 and x_sample.shape[1] == 1
    caches = (cache_kv_g0[0], cache_kv_g1[0], cache_kv_g2[0])
    w_cat_t = _prep_w_in(jnp.swapaxes(w_in[0], 0, 1))
    wa_bf = w_branch_a[0].astype(BF16)
    wb_bf = w_branch_b[0].astype(BF16)
    wo_bf = w_out[0].astype(BF16)
    wq_bf = peer_w_query[0].astype(BF16)
    u_bf = peer_u[0].astype(BF16)
    v_bf = peer_v[0].astype(BF16)
    cw = conv_w[0]

    xp = x_prompt.reshape(B * T, D)
    xs = x_sample.reshape(Bs, D)

    proj, proj_s = matmul_nt2(rmsnorm_bf16(xp, norm_mix[0]), rmsnorm_bf16(xs, norm_mix[0]),
                              w_cat_t, name="in_proj")
    alog_g, dtb_g = _gdn_group_params(a_log[0], dt_bias[0])
    ya, ssm_p = gdn_prompt(proj.reshape(B, T, N_PROJ), _ab_grouped(proj, B, T), cw, alog_g, dtb_g,
                           gdn_norm[0])
    conv_p = proj.reshape(B, T, N_PROJ)[:, T - (CONV_W - 1):, :CONV_CH]
    qn, kv = qk_norm(proj, q_norm[0], k_norm[0], F32)
    kv3 = kv.reshape(B, T, N_GROUPS * 2 * GROUP_W)
    kvs = [kv3[:, :, gi * 2 * GROUP_W:(gi + 1) * 2 * GROUP_W] for gi in range(N_GROUPS)]
    bias_all = jnp.stack([_prompt_bias_matrix(rel_bias, gi) for gi in range(N_GROUPS)], 0)
    yb = attn_prompt(qn.reshape(B, T, ATTN_W), kv3, bias_all)
    merged = branch_merge(ya.reshape(B * T, GDN_V_W), yb, wa_bf, wb_bf, proj)
    x1 = matmul(merged, wo_bf, residual=xp, name="out_proj")
    p_kv = []
    for gi, (window, _) in enumerate(ATTN_GROUPS):
        keep = min(window, T)
        kvg = kvs[gi][:, T - keep:].reshape(B, keep, 2, GROUP_HEADS, HEAD_DIM)
        p_kv.append(kvg[None])

    ya_s, conv_s, ssm_s = gdn_step(proj_s, state_conv[0], state_ssm[0], cw, a_log[0], dt_bias[0],
                                   gdn_norm[0])
    qn_s, kv_s = qk_norm(proj_s, q_norm[0], k_norm[0], F32)
    kvs_s = [kv_s[:, gi * 2 * GROUP_W:(gi + 1) * 2 * GROUP_W] for gi in range(N_GROUPS)]
    yb_s = attn_decode(qn_s, kvs_s, caches, rel_bias)
    merged_s = branch_merge(ya_s.astype(BF16), yb_s.astype(BF16), wa_bf, wb_bf, proj_s)
    x1_s = matmul(merged_s, wo_bf, residual=xs, name="out_proj_s")
    y_prompt = peer_ffn(x1, norm_ffn[0], wq_bf, peer_subkeys[0], u_bf, v_bf).reshape(B, T, D)
    y_sample = peer_ffn(x1_s, norm_ffn[0], wq_bf, peer_subkeys[0], u_bf, v_bf).reshape(Bs, 1, D)
    s_kv =[kvs_s[gi].reshape(1, Bs, 1, 2, GROUP_HEADS, HEAD_DIM) for gi in range(N_GROUPS)]

    return (y_prompt, y_sample,
            ssm_p[None], conv_p[None], p_kv[0], p_kv[1], p_kv[2],
            ssm_s[None], conv_s[None], s_kv[0], s_kv[1], s_kv[2])
```

```python
import functools

import numpy as np
import jax
import jax.numpy as jnp
from jax import lax
from jax.experimental import pallas as pl
from jax.experimental.pallas import tpu as pltpu

F32 = jnp.float32
BF16 = jnp.bfloat16

D_MODEL = 4096
GDN_HEADS = 16
GDN_DK = 128
GDN_DV = 128
GDN_QK_W = GDN_HEADS * GDN_DK
GDN_V_W = GDN_HEADS * GDN_DV
CONV_W = 4
CONV_CH = 2 * GDN_QK_W + GDN_V_W
GDN_CHUNK = 64
ATTN_GROUPS = ((128, 1), (512, 4), (2048, 16))
N_GROUPS = 3
GROUP_HEADS = 4
HEAD_DIM = 128
GROUP_W = GROUP_HEADS * HEAD_DIM
ATTN_W = N_GROUPS * GROUP_W
Q_BLOCK = 128
N_BUCKETS = 32
MAX_DISTANCE = 2048
PEER_HEADS = 8
N_KEYS = 128
PEER_TOPK = 16
EPS = 1e-6
NEG_INF = -1e30

OFF_Z = CONV_CH
OFF_AQ = OFF_Z + GDN_V_W
OFF_AK = OFF_AQ + ATTN_W
OFF_AV = OFF_AK + ATTN_W
OFF_GA = OFF_AV + ATTN_W
OFF_GB = OFF_GA + D_MODEL
OFF_AB = OFF_GB + D_MODEL
N_PROJ = 21504

GDN_HB = 16
GDN_HG = GDN_HEADS // GDN_HB
VMEM_LIMIT = 56 * 1024 * 1024


def _cparams(sem):
    return pltpu.CompilerParams(dimension_semantics=sem, vmem_limit_bytes=VMEM_LIMIT)


def _sigmoid(x):
    return 0.5 * jnp.tanh(0.5 * x) + 0.5


def _silu(x):
    return x * _sigmoid(x)


def _softplus(x):
    return jnp.maximum(x, 0.0) + jnp.log(1.0 + jnp.exp(-jnp.abs(x)))


def _bdot(a, b):
    return jnp.dot(a.astype(BF16), b.astype(BF16), preferred_element_type=F32)


def _bdot_nt(a, b):
    return lax.dot_general(a.astype(BF16), b.astype(BF16), (((1,), (1,)), ((), ())),
                           preferred_element_type=F32)


def _bdot_tn(a, b):
    return lax.dot_general(a.astype(BF16), b.astype(BF16), (((0,), (0,)), ((), ())),
                           preferred_element_type=F32)


def _split_bf16(a):
    hi = a.astype(BF16)
    lo = (a - hi.astype(F32)).astype(BF16)
    return hi, lo


def _dot3(a_parts, b_parts):
    a_hi, a_lo = a_parts
    b_hi, b_lo = b_parts
    return (jnp.dot(a_hi, b_hi, preferred_element_type=F32)
            + jnp.dot(a_lo, b_hi, preferred_element_type=F32)
            + jnp.dot(a_hi, b_lo, preferred_element_type=F32))


def _dot3_nt(a, b):
    a_hi, a_lo = _split_bf16(a)
    b_hi, b_lo = _split_bf16(b)
    dims = (((1,), (1,)), ((), ()))
    return (lax.dot_general(a_hi, b_hi, dims, preferred_element_type=F32)
            + lax.dot_general(a_lo, b_hi, dims, preferred_element_type=F32)
            + lax.dot_general(a_hi, b_lo, dims, preferred_element_type=F32))


def _rmsnorm_body(x_ref, g_ref, o_ref):
    x = x_ref[...]
    ms = jnp.mean(x * x, axis=-1, keepdims=True)
    o_ref[...] = (x * lax.rsqrt(ms + EPS) * g_ref[...]).astype(o_ref.dtype)


def rmsnorm_bf16(x, gain):
    M, D = x.shape
    tm = min(M, 512)
    return pl.pallas_call(
        _rmsnorm_body,
        grid=(M // tm,),
        in_specs=[pl.BlockSpec((tm, D), lambda i: (i, 0)),
                  pl.BlockSpec((1, D), lambda i: (0, 0))],
        out_specs=pl.BlockSpec((tm, D), lambda i: (i, 0)),
        out_shape=jax.ShapeDtypeStruct((M, D), BF16),
        compiler_params=_cparams(("parallel",)),
        name="rmsnorm",
    )(x, gain.reshape(1, D))


def _mm_body(a_ref, b_ref, o_ref):
    o_ref[...] = jnp.dot(a_ref[...], b_ref[...], preferred_element_type=F32)


def _mm_res_body(a_ref, b_ref, r_ref, o_ref):
    o_ref[...] = r_ref[...] + jnp.dot(a_ref[...], b_ref[...], preferred_element_type=F32)


def matmul(a, b, residual=None, tm=1024, tn=1024, name="matmul"):
    M, K = a.shape
    N = b.shape[1]
    tm = min(tm, M)
    tn = min(tn, N)
    in_specs = [pl.BlockSpec((tm, K), lambda i, j: (i, 0)),
                pl.BlockSpec((K, tn), lambda i, j: (0, j))]
    args = [a, b]
    body = _mm_body
    if residual is not None:
        in_specs.append(pl.BlockSpec((tm, tn), lambda i, j: (i, j)))
        args.append(residual)
        body = _mm_res_body
    return pl.pallas_call(
        body,
        grid=(M // tm, N // tn),
        in_specs=in_specs,
        out_specs=pl.BlockSpec((tm, tn), lambda i, j: (i, j)),
        out_shape=jax.ShapeDtypeStruct((M, N), F32),
        compiler_params=_cparams(("parallel", "parallel")),
        name=name,
    )(*args)


def _mm_nt2_body(a_ref, a2_ref, bt_ref, o_ref, o2_ref):
    dims = (((1,), (1,)), ((), ()))
    o_ref[...] = lax.dot_general(a_ref[...], bt_ref[...], dims, preferred_element_type=F32)

    @pl.when(pl.program_id(1) == 0)
    def _():
        o2_ref[...] = lax.dot_general(a2_ref[...], bt_ref[...], dims, preferred_element_type=F32)


def matmul_nt2(a, a2, bt, tm=1024, tn=1024, name="matmul_nt2"):
    M, K = a.shape
    M2 = a2.shape[0]
    N = bt.shape[0]
    tm = min(tm, M)
    return pl.pallas_call(
        _mm_nt2_body,
        grid=(N // tn, M // tm),
        in_specs=[pl.BlockSpec((tm, K), lambda j, i: (i, 0)),
                  pl.BlockSpec((M2, K), lambda j, i: (0, 0)),
                  pl.BlockSpec((tn, K), lambda j, i: (j, 0))],
        out_specs=[pl.BlockSpec((tm, tn), lambda j, i: (i, j)),
                   pl.BlockSpec((M2, tn), lambda j, i: (0, j))],
        out_shape=[jax.ShapeDtypeStruct((M, N), F32), jax.ShapeDtypeStruct((M2, N), F32)],
        compiler_params=_cparams(("parallel", "arbitrary")),
        name=name,
    )(a, a2, bt)


def _gdn_chunk_body(q_ref, k_ref, v_ref, z_ref, ab_ref, cwq_ref, cwk_ref, cwv_ref,
                    alog_ref, dtb_ref, gn_ref, o_ref, s_out_ref,
                    xq_s, xk_s, xv_s, state_s):
    n = pl.program_id(2)
    C = GDN_CHUNK
    HW = GDN_HB * GDN_DK

    @pl.when(n == 0)
    def _():
        xq_s[0:8, :] = jnp.zeros((8, HW), F32)
        xk_s[0:8, :] = jnp.zeros((8, HW), F32)
        xv_s[0:8, :] = jnp.zeros((8, HW), F32)
        state_s[...] = jnp.zeros(state_s.shape, F32)

    def conv(x_ref, xs, cw_ref):
        xs[8:8 + C, :] = x_ref[0]
        y = cw_ref[CONV_W - 1:CONV_W, :] * xs[8:8 + C, :]
        for w in range(CONV_W - 1):
            y = y + cw_ref[w:w + 1, :] * xs[5 + w:5 + w + C, :]
        xs[0:8, :] = xs[C:C + 8, :]
        return _silu(y)

    q_all = conv(q_ref, xq_s, cwq_ref)
    k_all = conv(k_ref, xk_s, cwk_ref)
    v_all = conv(v_ref, xv_s, cwv_ref)

    ab = ab_ref[0, 0]
    g_all = -jnp.exp(alog_ref[0]) * _softplus(ab + dtb_ref[0])
    beta_all = _sigmoid(ab)
    ri = lax.broadcasted_iota(jnp.int32, (C, C), 0)
    ci = lax.broadcasted_iota(jnp.int32, (C, C), 1)
    tri = ri >= ci
    strict = ri > ci
    rowid = lax.broadcasted_iota(jnp.int32, (C, 128), 0)
    gc_all = g_all
    shift = 1
    while shift < C:
        gc_all = gc_all + jnp.where(rowid >= shift, pltpu.roll(gc_all, shift, axis=0), 0.0)
        shift *= 2
    gc_t = jnp.concatenate([gc_all, jnp.zeros((128 - C, 128), F32)], axis=0).T

    H = range(GDN_HB)
    sls = [slice(j * GDN_DK, (j + 1) * GDN_DK) for j in H]
    qh, kh, kb, gcol, decay, rhs = [], [], [], [], [], []
    for j in H:
        q = q_all[:, sls[j]]
        k = k_all[:, sls[j]]
        q = q * lax.rsqrt(jnp.sum(q * q, axis=-1, keepdims=True) + EPS) * (GDN_DK ** -0.5)
        k = k * lax.rsqrt(jnp.sum(k * k, axis=-1, keepdims=True) + EPS)
        beta = beta_all[:, GDN_HB + j:GDN_HB + j + 1]
        gc = gc_all[:, j:j + 1]
        grow = gc_t[j:j + 1, :C]
        qh.append(q)
        kh.append(k)
        kb.append(k * beta)
        gcol.append(gc)
        decay.append(jnp.where(tri, jnp.exp(jnp.where(tri, gc - grow, 0.0)), 0.0))
        rhs.append(jnp.concatenate([v_all[:, sls[j]] * beta, kb[j] * jnp.exp(gc)], axis=-1))
    kq = [_bdot_nt(jnp.concatenate([kb[j], qh[j]], axis=0), kh[j]) for j in H]
    intra = [jnp.where(tri, kq[j][C:] * decay[j], 0.0) for j in H]
    acc = [-jnp.where(strict, kq[j][:C] * decay[j], 0.0) for j in H]
    parts = [_split_bf16(acc[j]) for j in H]
    pw = [_dot3(parts[j], parts[j]) for j in H]
    for r in range(5):
        pw_parts = [_split_bf16(pw[j]) for j in H]
        acc_parts = [_split_bf16(acc[j]) for j in H]
        acc = [acc[j] + pw[j] + _dot3(pw_parts[j], acc_parts[j]) for j in H]
        if r < 4:
            pw = [_dot3(pw_parts[j], pw_parts[j]) for j in H]
    sol = [rhs[j] + _dot3(_split_bf16(acc[j]), _split_bf16(rhs[j])) for j in H]
    S = [state_s[j] for j in H]
    wq_s = [_bdot(jnp.concatenate([sol[j][:, GDN_DV:], qh[j] * jnp.exp(gcol[j])], axis=0), S[j])
            for j in H]
    v_new = [sol[j][:, :GDN_DV] - wq_s[j][:C] for j in H]
    o = [wq_s[j][C:] + _bdot(intra[j], v_new[j]) for j in H]
    for j in H:
        g_last = gcol[j][C - 1:C, :]
        state_s[j] = (S[j] * jnp.exp(g_last)
                      + _bdot_tn(kh[j] * jnp.exp(g_last - gcol[j]), v_new[j]))
    for j in H:
        ms = jnp.mean(o[j] * o[j], axis=-1, keepdims=True)
        on = o[j] * lax.rsqrt(ms + EPS) * gn_ref[...]
        o_ref[0, :, sls[j]] = (on * _silu(z_ref[0, :, sls[j]])).astype(o_ref.dtype)

    @pl.when(n == pl.num_programs(2) - 1)
    def _():
        s_out_ref[0] = state_s[...]


def gdn_prompt(proj3, ab_g, conv_w, alog_g, dtb_g, gdn_norm):
    B, T, _ = proj3.shape
    C = GDN_CHUNK
    N = T // C
    HW = GDN_HB * GDN_DK
    nq = GDN_QK_W // HW

    def col(base):
        return lambda b, hg, n: (b, n, base + hg)

    def cwcol(base):
        return lambda b, hg, n: (0, base + hg)

    return pl.pallas_call(
        _gdn_chunk_body,
        grid=(B, GDN_HG, N),
        in_specs=[
            pl.BlockSpec((1, C, HW), col(0)),
            pl.BlockSpec((1, C, HW), col(nq)),
            pl.BlockSpec((1, C, HW), col(2 * nq)),
            pl.BlockSpec((1, C, HW), col(OFF_Z // HW)),
            pl.BlockSpec((1, 1, C, 128), lambda b, hg, n: (b, hg, n, 0)),
            pl.BlockSpec((CONV_W, HW), cwcol(0)),
            pl.BlockSpec((CONV_W, HW), cwcol(nq)),
            pl.BlockSpec((CONV_W, HW), cwcol(2 * nq)),
            pl.BlockSpec((1, 1, 128), lambda b, hg, n: (hg, 0, 0)),
            pl.BlockSpec((1, 1, 128), lambda b, hg, n: (hg, 0, 0)),
            pl.BlockSpec((1, GDN_DV), lambda b, hg, n: (0, 0)),
        ],
        out_specs=[
            pl.BlockSpec((1, C, HW), lambda b, hg, n: (b, n, hg)),
            pl.BlockSpec((1, GDN_HB, GDN_DK, GDN_DV), lambda b, hg, n: (b, hg, 0, 0)),
        ],
        out_shape=[
            jax.ShapeDtypeStruct((B, T, GDN_V_W), BF16),
            jax.ShapeDtypeStruct((B, GDN_HEADS, GDN_DK, GDN_DV), F32),
        ],
        scratch_shapes=[
            pltpu.VMEM((C + 8, HW), F32),
            pltpu.VMEM((C + 8, HW), F32),
            pltpu.VMEM((C + 8, HW), F32),
            pltpu.VMEM((GDN_HB, GDN_DK, GDN_DV), F32),
        ],
        compiler_params=_cparams(("parallel", "parallel", "arbitrary")),
        name="gdn_prompt",
    )(proj3, proj3, proj3, proj3, ab_g, conv_w, conv_w, conv_w, alog_g, dtb_g,
      gdn_norm.reshape(1, GDN_DV))


def _gdn_step_body(x_ref, z_ref, ab_ref, cs_ref, s_ref, cw_ref, alog_ref, dtb_ref, gn_ref,
                   o_ref, cs_out_ref, s_out_ref):
    x = x_ref[0]
    y = cw_ref[CONV_W - 1] * x
    for w in range(CONV_W - 1):
        y = y + cw_ref[w] * cs_ref[0, w]
    y = _silu(y)
    cs_out_ref[0, 0] = cs_ref[0, 1]
    cs_out_ref[0, 1] = cs_ref[0, 2]
    cs_out_ref[0, 2] = x
    r_ab = (OFF_AB // 128) % 8
    ab = ab_ref[0, r_ab:r_ab + 1, :]
    g_all = -jnp.exp(alog_ref[...]) * _softplus(ab + dtb_ref[...])
    beta_all = _sigmoid(ab)
    eye = (lax.broadcasted_iota(jnp.int32, (GDN_DK, GDN_DK), 0)
           == lax.broadcasted_iota(jnp.int32, (GDN_DK, GDN_DK), 1))

    def to_col(row):
        return jnp.sum(jnp.where(eye, jnp.broadcast_to(row, (GDN_DK, GDN_DK)), 0.0),
                       axis=1, keepdims=True)

    for h in range(GDN_HEADS):
        q = y[h:h + 1, :]
        k = y[GDN_HEADS + h:GDN_HEADS + h + 1, :]
        v = y[2 * GDN_HEADS + h:2 * GDN_HEADS + h + 1, :]
        q = q * lax.rsqrt(jnp.sum(q * q, axis=-1, keepdims=True) + EPS) * (GDN_DK ** -0.5)
        k = k * lax.rsqrt(jnp.sum(k * k, axis=-1, keepdims=True) + EPS)
        g = g_all[:, h:h + 1]
        beta = beta_all[:, GDN_HEADS + h:GDN_HEADS + h + 1]
        eg = jnp.exp(g)
        S = s_ref[0, h]
        kcol = to_col(k)
        qcol = to_col(q)
        k_s = jnp.sum(S * kcol, axis=0, keepdims=True)
        q_s = jnp.sum(S * qcol, axis=0, keepdims=True)
        v_new = v * beta - (beta * eg) * k_s
        o = eg * q_s + jnp.sum(q * k, axis=-1, keepdims=True) * v_new
        s_out_ref[0, h] = S * eg + kcol * v_new
        ms = jnp.mean(o * o, axis=-1, keepdims=True)
        on = o * lax.rsqrt(ms + EPS) * gn_ref[...]
        o_ref[0, h:h + 1, :] = on * _silu(z_ref[0, h:h + 1, :])


def gdn_step(proj_s, conv_state, ssm_state, conv_w, a_log, dt_bias, gdn_norm):
    Bs = proj_s.shape[0]
    nrow = N_PROJ // 128
    p3 = proj_s.reshape(Bs, nrow, 128)
    cs4 = conv_state.reshape(Bs, CONV_W - 1, CONV_CH // 128, 128)
    cw3 = conv_w.reshape(CONV_W, CONV_CH // 128, 128)
    pad = jnp.zeros((128 - GDN_HEADS,), F32)
    alog = jnp.concatenate([a_log, pad]).reshape(1, 128)
    dtb = jnp.concatenate([dt_bias, pad]).reshape(1, 128)
    nr = CONV_CH // 128
    o, cs_new, s_new = pl.pallas_call(
        _gdn_step_body,
        grid=(Bs,),
        in_specs=[
            pl.BlockSpec((1, nr, 128), lambda b: (b, 0, 0)),
            pl.BlockSpec((1, GDN_HEADS, 128), lambda b: (b, OFF_Z // 128 // GDN_HEADS, 0)),
            pl.BlockSpec((1, 8, 128), lambda b: (b, OFF_AB // 128 // 8, 0)),
            pl.BlockSpec((1, CONV_W - 1, nr, 128), lambda b: (b, 0, 0, 0)),
            pl.BlockSpec((1, GDN_HEADS, GDN_DK, GDN_DV), lambda b: (b, 0, 0, 0)),
            pl.BlockSpec((CONV_W, nr, 128), lambda b: (0, 0, 0)),
            pl.BlockSpec((1, 128), lambda b: (0, 0)),
            pl.BlockSpec((1, 128), lambda b: (0, 0)),
            pl.BlockSpec((1, GDN_DV), lambda b: (0, 0)),
        ],
        out_specs=[
            pl.BlockSpec((1, GDN_HEADS, GDN_DV), lambda b: (b, 0, 0)),
            pl.BlockSpec((1, CONV_W - 1, nr, 128), lambda b: (b, 0, 0, 0)),
            pl.BlockSpec((1, GDN_HEADS, GDN_DK, GDN_DV), lambda b: (b, 0, 0, 0)),
        ],
        out_shape=[
            jax.ShapeDtypeStruct((Bs, GDN_HEADS, GDN_DV), F32),
            jax.ShapeDtypeStruct((Bs, CONV_W - 1, nr, 128), F32),
            jax.ShapeDtypeStruct((Bs, GDN_HEADS, GDN_DK, GDN_DV), F32),
        ],
        compiler_params=_cparams(("parallel",)),
        name="gdn_step",
    )(p3, p3, p3, cs4, ssm_state, cw3, alog, dtb, gdn_norm.reshape(1, GDN_DV))
    return (o.reshape(Bs, GDN_V_W), cs_new.reshape(Bs, CONV_W - 1, CONV_CH), s_new)


def _qknorm_body(q_ref, k_ref, v_ref, qg_ref, kg_ref, qn_ref, kv_ref):
    for h in range(GROUP_HEADS):
        sl = slice(h * HEAD_DIM, (h + 1) * HEAD_DIM)
        q = q_ref[:, sl]
        qn = q * lax.rsqrt(jnp.mean(q * q, axis=-1, keepdims=True) + EPS) * qg_ref[...]
        qn_ref[:, sl] = (qn * (HEAD_DIM ** -0.5)).astype(qn_ref.dtype)
        k = k_ref[:, sl]
        kv_ref[:, sl] = k * lax.rsqrt(jnp.mean(k * k, axis=-1, keepdims=True) + EPS) * kg_ref[...]
    kv_ref[:, GROUP_W:] = v_ref[...]


def qk_norm(proj, q_norm, k_norm, q_dtype):
    M = proj.shape[0]
    tm = min(M, 512)
    blk = GROUP_W
    return pl.pallas_call(
        _qknorm_body,
        grid=(M // tm, N_GROUPS),
        in_specs=[
            pl.BlockSpec((tm, blk), lambda i, g: (i, OFF_AQ // blk + g)),
            pl.BlockSpec((tm, blk), lambda i, g: (i, OFF_AK // blk + g)),
            pl.BlockSpec((tm, blk), lambda i, g: (i, OFF_AV // blk + g)),
            pl.BlockSpec((1, HEAD_DIM), lambda i, g: (0, 0)),
            pl.BlockSpec((1, HEAD_DIM), lambda i, g: (0, 0)),
        ],
        out_specs=[
            pl.BlockSpec((tm, blk), lambda i, g: (i, g)),
            pl.BlockSpec((tm, 2 * blk), lambda i, g: (i, g)),
        ],
        out_shape=[jax.ShapeDtypeStruct((M, ATTN_W), q_dtype),
                   jax.ShapeDtypeStruct((M, N_GROUPS * 2 * GROUP_W), F32)],
        compiler_params=_cparams(("parallel", "parallel")),
        name="qk_norm",
    )(proj, proj, proj, q_norm.reshape(1, HEAD_DIM), k_norm.reshape(1, HEAD_DIM))


def _t5_bucket(dist):
    max_exact = N_BUCKETS // 2
    d = np.maximum(dist, max_exact).astype(np.float32)
    large = max_exact + (np.log(d / max_exact) / np.log(MAX_DISTANCE / max_exact)
                         * (N_BUCKETS - max_exact)).astype(np.int32)
    large = np.minimum(large, N_BUCKETS - 1)
    return np.where(dist < max_exact, dist, large).astype(np.int32)


def _group_bias(rel_bias, gi):
    window, dilation = ATTN_GROUPS[gi]
    n_keys = window // dilation + 1
    buckets = _t5_bucket(np.arange(n_keys) * dilation)
    return rel_bias[buckets][:, gi * GROUP_HEADS:(gi + 1) * GROUP_HEADS].T.astype(F32)


def _attn_prompt_body(T, q0_ref, q1_ref, q2_ref, k0_ref, v0_ref, k1_ref, v1_ref, k2_ref, v2_ref,
                      bias_ref, y_ref, m_s, l_s, o_s):
    QB = Q_BLOCK
    q_refs = (q0_ref, q1_ref, q2_ref)
    k_refs = (k0_ref, k1_ref, k2_ref)
    v_refs = (v0_ref, v1_ref, v2_ref)
    for gi, (window, d) in enumerate(ATTN_GROUPS):
        nq = T // (d * QB)
        for r in range(d):
            for qi in range(nq):
                rows = pl.ds(r + d * QB * qi, QB, stride=d) if d > 1 else pl.ds(QB * qi, QB)
                q = q_refs[gi][0, rows, :].astype(BF16)
                kc = k_refs[gi][0, rows, :]
                vc = v_refs[gi][0, rows, :]
                if qi > 0:
                    prev = (pl.ds(r + d * QB * (qi - 1), QB, stride=d) if d > 1
                            else pl.ds(QB * (qi - 1), QB))
                    kcat = jnp.concatenate([k_refs[gi][0, prev, :], kc], axis=0).astype(BF16)
                    vcat = jnp.concatenate([v_refs[gi][0, prev, :], vc], axis=0).astype(BF16)
                    bias = bias_ref[gi, 0]
                else:
                    kcat = kc.astype(BF16)
                    vcat = vc.astype(BF16)
                    bias = bias_ref[gi, 0, :, QB:]
                s = lax.dot_general(q, kcat, (((1,), (1,)), ((), ())),
                                    preferred_element_type=F32) + bias
                m_blk = jnp.max(s, axis=-1, keepdims=True)
                if gi == 0:
                    m_new = m_blk
                else:
                    m_old = m_s[rows, :][:, 0:1]
                    m_new = jnp.maximum(m_old, m_blk)
                p = jnp.exp(s - m_new)
                l_new = jnp.sum(p, axis=-1, keepdims=True)
                o_new = jnp.dot(p.astype(BF16), vcat, preferred_element_type=F32)
                if gi > 0:
                    alpha = jnp.exp(m_old - m_new)
                    l_new = l_new + alpha * l_s[rows, :][:, 0:1]
                    o_new = o_new + alpha * o_s[rows, :]
                m_s[rows, :] = jnp.broadcast_to(m_new, (QB, HEAD_DIM))
                l_s[rows, :] = jnp.broadcast_to(l_new, (QB, HEAD_DIM))
                o_s[rows, :] = o_new
    y_ref[0] = (o_s[...] / l_s[...]).astype(y_ref.dtype)


def attn_prompt(qn3, kv3, bias_all):
    B, T, _ = qn3.shape
    blk = (1, T, HEAD_DIM)

    def qspec(gi):
        return pl.BlockSpec(blk, lambda b, h: (b, 0, gi * GROUP_HEADS + h))

    def kvspec(gi, is_v):
        return pl.BlockSpec(blk, lambda b, h: (b, 0, (2 * gi + is_v) * GROUP_HEADS + h))

    in_specs = [qspec(0), qspec(1), qspec(2)]
    for gi in range(N_GROUPS):
        in_specs += [kvspec(gi, 0), kvspec(gi, 1)]
    in_specs.append(pl.BlockSpec((N_GROUPS, 1, Q_BLOCK, 2 * Q_BLOCK), lambda b, h: (0, h, 0, 0)))
    y = pl.pallas_call(
        functools.partial(_attn_prompt_body, T),
        grid=(B, GROUP_HEADS),
        in_specs=in_specs,
        out_specs=pl.BlockSpec(blk, lambda b, h: (b, 0, h)),
        out_shape=jax.ShapeDtypeStruct((B, T, GROUP_W), BF16),
        scratch_shapes=[pltpu.VMEM((T, HEAD_DIM), F32), pltpu.VMEM((T, HEAD_DIM), F32),
                        pltpu.VMEM((T, HEAD_DIM), F32)],
        compiler_params=_cparams(("parallel", "parallel")),
        name="attn_prompt",
    )(qn3, qn3, qn3, kv3, kv3, kv3, kv3, kv3, kv3, bias_all)
    return y.reshape(B * T, GROUP_W)


def _prompt_bias_matrix(rel_bias, gi):
    QB = Q_BLOCK
    bias = _group_bias(rel_bias, gi)
    row0 = jnp.concatenate([bias[:, ::-1], jnp.full((GROUP_HEADS, QB), NEG_INF, F32)], axis=1)
    flat = jnp.tile(row0, (1, QB))[:, :QB * 2 * QB]
    return flat.reshape(GROUP_HEADS, QB, 2 * QB)


def _attn_decode_body(q_ref, n0_ref, n1_ref, n2_ref, c0_ref, c1_ref, c2_ref, bc_ref, bn_ref, y_ref):
    new_refs = (n0_ref, n1_ref, n2_ref)
    cache_refs = (c0_ref, c1_ref, c2_ref)
    outs, lds = [], []
    for gi in range(N_GROUPS):
        q = q_ref[0, gi * GROUP_HEADS:(gi + 1) * GROUP_HEADS, :]
        kc = cache_refs[gi][:, 0]
        vc = cache_refs[gi][:, 1]
        kn = new_refs[gi][0, 0:GROUP_HEADS, :]
        vn = new_refs[gi][0, GROUP_HEADS:2 * GROUP_HEADS, :]
        s_c = jnp.sum(kc * q[None], axis=-1, keepdims=True) + bc_ref[gi, :, :, 0:1]
        s_n = jnp.sum(kn * q, axis=-1, keepdims=True) + bn_ref[gi, :, 0:1]
        m = jnp.maximum(jnp.max(s_c, axis=0), s_n)
        p_c = jnp.exp(s_c - m[None])
        p_n = jnp.exp(s_n - m)
        l = jnp.sum(p_c, axis=0) + p_n
        outs.append((jnp.sum(p_c * vc, axis=0) + p_n * vn) / l)
        lds.append(m + jnp.log(l))
    m = jnp.maximum(jnp.maximum(lds[0], lds[1]), lds[2])
    es = [jnp.exp(x - m) for x in lds]
    den = es[0] + es[1] + es[2]
    y_ref[0] = outs[0] * (es[0] / den) + outs[1] * (es[1] / den) + outs[2] * (es[2] / den)


def attn_decode(qn_s, kv_new, caches, rel_bias):
    Bs = qn_s.shape[0]
    q3 = qn_s.reshape(Bs, N_GROUPS * GROUP_HEADS, HEAD_DIM)
    new3 = [x.reshape(Bs, 2 * GROUP_HEADS, HEAD_DIM) for x in kv_new]
    cviews, bcs, bns = [], [], []
    for gi, (window, d) in enumerate(ATTN_GROUPS):
        nk = window // d
        assert caches[gi].shape[1] == window and nk == Q_BLOCK
        cviews.append(caches[gi].reshape(Bs, nk, d, 2, GROUP_HEADS, HEAD_DIM))
        bias = _group_bias(rel_bias, gi)
        bc = bias[:, nk - np.arange(nk)]
        bcs.append(jnp.broadcast_to(bc.T[:, :, None], (nk, GROUP_HEADS, HEAD_DIM)))
        bns.append(jnp.broadcast_to(bias[:, 0:1], (GROUP_HEADS, HEAD_DIM)))
    bc_all = jnp.stack(bcs, 0)
    bn_all = jnp.stack(bns, 0)
    nspec = pl.BlockSpec((1, 2 * GROUP_HEADS, HEAD_DIM), lambda b: (b, 0, 0))
    cspec = pl.BlockSpec((None, Q_BLOCK, None, 2, GROUP_HEADS, HEAD_DIM),
                         lambda b: (b, 0, 0, 0, 0, 0))
    y = pl.pallas_call(
        _attn_decode_body,
        grid=(Bs,),
        in_specs=[
            pl.BlockSpec((1, N_GROUPS * GROUP_HEADS, HEAD_DIM), lambda b: (b, 0, 0)),
            nspec, nspec, nspec, cspec, cspec, cspec,
            pl.BlockSpec((N_GROUPS, Q_BLOCK, GROUP_HEADS, HEAD_DIM), lambda b: (0, 0, 0, 0)),
            pl.BlockSpec((N_GROUPS, GROUP_HEADS, HEAD_DIM), lambda b: (0, 0, 0)),
        ],
        out_specs=pl.BlockSpec((1, GROUP_HEADS, HEAD_DIM), lambda b: (b, 0, 0)),
        out_shape=jax.ShapeDtypeStruct((Bs, GROUP_HEADS, HEAD_DIM), F32),
        compiler_params=_cparams(("parallel",)),
        name="attn_decode",
    )(q3, *new3, *cviews, bc_all, bn_all)
    return y.reshape(Bs, GROUP_W)


def _merge_body(ya_ref, yb_ref, wa_ref, wb_ref, ga_ref, gb_ref, o_ref):
    pa = jnp.dot(ya_ref[...], wa_ref[...], preferred_element_type=F32)
    pb = jnp.dot(yb_ref[...], wb_ref[...], preferred_element_type=F32)
    o_ref[...] = (_sigmoid(ga_ref[...]) * pa + _sigmoid(gb_ref[...]) * pb).astype(o_ref.dtype)


def branch_merge(ya, yb, wa, wb, proj):
    M = ya.shape[0]
    tm = min(M, 1024)
    tn = 512
    return pl.pallas_call(
        _merge_body,
        grid=(M // tm, D_MODEL // tn),
        in_specs=[
            pl.BlockSpec((tm, GDN_V_W), lambda i, j: (i, 0)),
            pl.BlockSpec((tm, GROUP_W), lambda i, j: (i, 0)),
            pl.BlockSpec((GDN_V_W, tn), lambda i, j: (0, j)),
            pl.BlockSpec((GROUP_W, tn), lambda i, j: (0, j)),
            pl.BlockSpec((tm, tn), lambda i, j: (i, OFF_GA // tn + j)),
            pl.BlockSpec((tm, tn), lambda i, j: (i, OFF_GB // tn + j)),
        ],
        out_specs=pl.BlockSpec((tm, tn), lambda i, j: (i, j)),
        out_shape=jax.ShapeDtypeStruct((M, D_MODEL), BF16),
        compiler_params=_cparams(("parallel", "parallel")),
        name="branch_merge",
    )(ya, yb, wa, wb, proj, proj)


def _extract_top(x, count, tie_safe, want_rank=True):
    R = x.shape[0]
    rid = lax.broadcasted_iota(jnp.int32, x.shape, 0).astype(F32)
    vid = lax.broadcasted_iota(jnp.int32, (count, x.shape[1]), 0)
    vals = jnp.zeros((count, x.shape[1]), F32)
    rank = jnp.full(x.shape, float(count), F32)
    for a in range(count):
        m = jnp.max(x, axis=0, keepdims=True)
        hit = x == m
        if tie_safe:
            first = jnp.min(jnp.where(hit, rid, float(R)), axis=0, keepdims=True)
            hit = rid == first
        x = jnp.where(hit, -jnp.inf, x)
        if want_rank:
            rank = jnp.where(hit, float(a), rank)
        vals = jnp.where(vid == a, m, vals)
    return vals, (rank if want_rank else x)


CAND_HALF = PEER_TOPK // 2
N_CAND = PEER_TOPK + (CAND_HALF - 1) * CAND_HALF + CAND_HALF


def _peer_topk_body(q_ref, sk_ref, cnt_ref, r2_ref, e1_ref, e2_ref, cand_s, s1_s, s2_s):
    K = PEER_TOPK
    KH = CAND_HALF

    def select(h, s1, s2, tie_safe):
        v1, rank1 = _extract_top(s1, K, tie_safe)
        v2, rank2 = _extract_top(s2, K, tie_safe)
        cand_s[0:K, :] = v2 + v1[0:1, :]
        for a in range(1, KH):
            cand_s[K + (a - 1) * KH:K + a * KH, :] = v2[0:KH, :] + v1[a:a + 1, :]
        cand_s[N_CAND - KH:N_CAND, :] = v1[KH:K, :] + v2[0:1, :]
        best, cleft = _extract_top(cand_s[...], K, tie_safe, want_rank=False)
        picked = jnp.where(cleft == -jnp.inf, 1.0, 0.0)
        vid = lax.broadcasted_iota(jnp.int32, (KH, picked.shape[1]), 0)
        low = jnp.where(vid == 0, jnp.sum(picked[0:K], axis=0, keepdims=True), 0.0)
        for a in range(1, KH):
            rows = picked[K + (a - 1) * KH:K + a * KH]
            low = jnp.where(vid == a, jnp.sum(rows, axis=0, keepdims=True), low)
        count = jnp.concatenate([low, picked[N_CAND - KH:N_CAND]], axis=0)
        cnt1 = jnp.zeros(rank1.shape, F32)
        for a in range(K):
            cnt1 = jnp.where(rank1 == float(a), count[a:a + 1, :], cnt1)
        cmax = best[0:1, :]
        z = jnp.sum(jnp.exp(best - cmax), axis=0, keepdims=True)
        in1 = jnp.where(rank1 < float(K), 1.0, 0.0)
        in2 = jnp.where(rank2 < float(K), 1.0, 0.0)
        cnt_ref[h] = cnt1
        r2_ref[h] = rank2
        e1_ref[h] = in1 * jnp.exp(s1 - v1[0:1, :])
        e2_ref[h] = in2 * jnp.exp(s2 - v2[0:1, :]) / z * (2.0 ** -0.5)
        return (jnp.sum(in1, axis=0, keepdims=True) + jnp.sum(in2, axis=0, keepdims=True)
                + jnp.sum(picked, axis=0, keepdims=True) - 3.0 * K)

    def head(h, carry):
        off = pl.multiple_of(h * (2 * N_KEYS), 2 * N_KEYS)
        q1 = q_ref[:, pl.ds(off, N_KEYS)]
        q2 = q_ref[:, pl.ds(pl.multiple_of(off + N_KEYS, N_KEYS), N_KEYS)]
        s1_s[...] = _dot3_nt(sk_ref[0], q1)
        s2_s[...] = _dot3_nt(sk_ref[1], q2)
        extra = select(h, s1_s[...], s2_s[...], tie_safe=False)

        @pl.when(jnp.max(extra) > 0.0)
        def _():
            select(h, s1_s[...], s2_s[...], tie_safe=True)

        return carry

    lax.fori_loop(0, PEER_HEADS, head, 0)


def peer_topk(query, subkeys):
    M = query.shape[0]
    tm = min(M, 512)
    big = jax.ShapeDtypeStruct((PEER_HEADS, N_KEYS, M), F32)
    bspec = pl.BlockSpec((PEER_HEADS, N_KEYS, tm), lambda i: (0, 0, i))
    return pl.pallas_call(
        _peer_topk_body,
        grid=(M // tm,),
        in_specs=[pl.BlockSpec((tm, PEER_HEADS * 2 * N_KEYS), lambda i: (i, 0)),
                  pl.BlockSpec((2, N_KEYS, N_KEYS), lambda i: (0, 0, 0))],
        out_specs=[bspec, bspec, bspec, bspec],
        out_shape=[big, big, big, big],
        scratch_shapes=[pltpu.VMEM((N_CAND, tm), F32), pltpu.VMEM((N_KEYS, tm), F32),
                        pltpu.VMEM((N_KEYS, tm), F32)],
        compiler_params=_cparams(("parallel",)),
        name="peer_topk",
    )(query, subkeys)


def _gelu_x2(x):
    return x * (1.0 + lax.erf(x))


GATE_ROWS = 8


def _peer_gate_body(hf_ref, u_ref, cnt_ref, r2_ref, e1_ref, e2_ref, w_ref, g_s):
    j = pl.program_id(1)
    tm = hf_ref.shape[0]
    grp = pl.multiple_of(j * GATE_ROWS, GATE_ROWS)

    for kk in range(GATE_ROWS):
        for tc in range(tm // 128):
            ts = slice(tc * 128, (tc + 1) * 128)
            acc = None
            for h in range(PEER_HEADS):
                cnt_row = cnt_ref[h, pl.ds(grp, GATE_ROWS), ts][kk:kk + 1, :]
                e1_row = e1_ref[h, pl.ds(grp, GATE_ROWS), ts][kk:kk + 1, :]
                term = jnp.where(r2_ref[h, :, ts] < cnt_row, e2_ref[h, :, ts] * e1_row, 0.0)
                acc = term if acc is None else acc + term
            g_s[ts, kk * N_KEYS:(kk + 1) * N_KEYS] = acc.T
    hmat = lax.dot_general(hf_ref[...], u_ref[...], (((1,), (1,)), ((), ())),
                           preferred_element_type=F32)
    w_ref[...] = (g_s[...] * _gelu_x2(hmat)).astype(w_ref.dtype)


def peer_gate_weights(hf, u_tab, cnt, r2, e1, e2, tm=1024):
    M = hf.shape[0]
    tm = min(M, tm)
    tn = GATE_ROWS * N_KEYS
    E = u_tab.shape[0]
    once = pl.Buffered(1)
    sspec = pl.BlockSpec((PEER_HEADS, N_KEYS, tm), lambda i, j: (0, 0, i), pipeline_mode=once)
    return pl.pallas_call(
        _peer_gate_body,
        grid=(M // tm, E // tn),
        in_specs=[
            pl.BlockSpec((tm, D_MODEL), lambda i, j: (i, 0), pipeline_mode=once),
            pl.BlockSpec((tn, D_MODEL), lambda i, j: (j, 0)),
            sspec, sspec, sspec, sspec,
        ],
        out_specs=pl.BlockSpec((tm, tn), lambda i, j: (i, j)),
        out_shape=jax.ShapeDtypeStruct((M, E), BF16),
        scratch_shapes=[pltpu.VMEM((tm, tn), F32)],
        compiler_params=_cparams(("parallel", "arbitrary")),
        name="peer_gate",
    )(hf, u_tab, cnt, r2, e1, e2)


def _mm_kres_body(a_ref, b_ref, r_ref, o_ref):
    k = pl.program_id(2)

    @pl.when(k == 0)
    def _():
        o_ref[...] = r_ref[...]

    o_ref[...] += jnp.dot(a_ref[...], b_ref[...], preferred_element_type=F32)


def matmul_kres(a, b, residual, tm=1024, tn=1024, tk=4096, name="matmul_k"):
    M, K = a.shape
    N = b.shape[1]
    tm = min(tm, M)
    return pl.pallas_call(
        _mm_kres_body,
        grid=(M // tm, N // tn, K // tk),
        in_specs=[pl.BlockSpec((tm, tk), lambda i, j, k: (i, k)),
                  pl.BlockSpec((tk, tn), lambda i, j, k: (k, j)),
                  pl.BlockSpec((tm, tn), lambda i, j, k: (i, j))],
        out_specs=pl.BlockSpec((tm, tn), lambda i, j, k: (i, j)),
        out_shape=jax.ShapeDtypeStruct((M, N), F32),
        compiler_params=_cparams(("parallel", "parallel", "arbitrary")),
        name=name,
    )(a, b, residual)


def peer_ffn(x1, norm_ffn, wq_bf, subkeys, u_bf, v_bf):
    M = x1.shape[0]
    Mp = max(M, 128)
    if Mp != M:
        x1 = jnp.concatenate([x1, jnp.zeros((Mp - M, D_MODEL), F32)], axis=0)
    hf = rmsnorm_bf16(x1, norm_ffn)
    query = matmul(hf, wq_bf, name="peer_query")
    cnt, r2, e1, e2 = peer_topk(query, subkeys)
    w = peer_gate_weights(hf, u_bf, cnt, r2, e1, e2)
    return matmul_kres(w, v_bf, x1, name="peer_out")[:M]


W_BLK = 512
N_AB = 2 * GDN_HEADS
J_AB = OFF_AB // W_BLK


def _prep_w_in_body(a_ref, b_ref, o_ref):
    j = pl.program_id(0)

    @pl.when(j < OFF_AQ // W_BLK)
    def _():
        o_ref[...] = a_ref[...].astype(o_ref.dtype)

    @pl.when(jnp.logical_and(j >= OFF_AQ // W_BLK, j < J_AB))
    def _():
        o_ref[0:W_BLK - N_AB, :] = a_ref[N_AB:, :].astype(o_ref.dtype)
        o_ref[W_BLK - N_AB:, :] = b_ref[...].astype(o_ref.dtype)

    @pl.when(j == J_AB)
    def _():
        o_ref[0:N_AB, :] = b_ref[...].astype(o_ref.dtype)
        o_ref[N_AB:, :] = jnp.zeros((W_BLK - N_AB, o_ref.shape[1]), o_ref.dtype)


def _prep_w_in(w_in_t):
    D = w_in_t.shape[1]
    j0 = OFF_AQ // W_BLK
    sub = W_BLK // N_AB
    return pl.pallas_call(
        _prep_w_in_body,
        grid=(N_PROJ // W_BLK,),
        in_specs=[
            pl.BlockSpec((W_BLK, D), lambda j: (jnp.where(j == J_AB, j0, j), 0)),
            pl.BlockSpec((N_AB, D), lambda j: (jnp.where(j == J_AB, j0 * sub,
                                                         jnp.where(j >= j0, (j + 1) * sub, 0)), 0)),
        ],
        out_specs=pl.BlockSpec((W_BLK, D), lambda j: (j, 0)),
        out_shape=jax.ShapeDtypeStruct((N_PROJ, D), BF16),
        compiler_params=_cparams(("parallel",)),
        name="prep_w_in",
    )(w_in_t, w_in_t)


def _gdn_group_params(a_log, dt_bias):
    pad = jnp.zeros((GDN_HG, 128 - GDN_HB), F32)
    alog = jnp.concatenate([a_log.reshape(GDN_HG, GDN_HB), pad], axis=1).reshape(GDN_HG, 1, 128)
    dtb = jnp.concatenate([dt_bias.reshape(GDN_HG, GDN_HB), pad], axis=1).reshape(GDN_HG, 1, 128)
    return alog, dtb


def _ab_grouped(proj, B, T):
    ab = proj[:, OFF_AB:OFF_AB + 2 * GDN_HEADS].reshape(B, T, 2, GDN_HG, GDN_HB)
    ab = ab.transpose(0, 3, 1, 2, 4).reshape(B, GDN_HG, T, 2 * GDN_HB)
    return jnp.concatenate([ab, jnp.zeros((B, GDN_HG, T, 128 - 2 * GDN_HB), F32)], axis=-1)


def kernel(x_prompt, x_sample, state_ssm, state_conv, cache_kv_g0, cache_kv_g1, cache_kv_g2, rel_bias, norm_mix, w_in, conv_w, a_log, dt_bias, gdn_norm, q_norm, k_norm, w_branch_a, w_branch_b, w_out, norm_ffn, peer_w_query, peer_subkeys, peer_u, peer_v):
    B, T, D = x_prompt.shape
    Bs = x_sample.shape[0]
    assert state_ssm.shape[0] == 1 and x_sample.shape[1] == 1
    caches = (cache_kv_g0[0], cache_kv_g1[0], cache_kv_g2[0])
    w_cat_t = _prep_w_in(jnp.swapaxes(w_in[0], 0, 1))
    wa_bf = w_branch_a[0].astype(BF16)
    wb_bf = w_branch_b[0].astype(BF16)
    wo_bf = w_out[0].astype(BF16)
    wq_bf = peer_w_query[0].astype(BF16)
    u_bf = (peer_u[0] * (2.0 ** -0.5)).astype(BF16)
    v_bf = peer_v[0].astype(BF16)
    cw = conv_w[0]

    xp = x_prompt.reshape(B * T, D)
    xs = x_sample.reshape(Bs, D)

    proj, proj_s = matmul_nt2(rmsnorm_bf16(xp, norm_mix[0]), rmsnorm_bf16(xs, norm_mix[0]),
                              w_cat_t, name="in_proj")
    alog_g, dtb_g = _gdn_group_params(a_log[0], dt_bias[0])
    ya, ssm_p = gdn_prompt(proj.reshape(B, T, N_PROJ), _ab_grouped(proj, B, T), cw, alog_g, dtb_g,
                           gdn_norm[0])
    conv_p = proj.reshape(B, T, N_PROJ)[:, T - (CONV_W - 1):, :CONV_CH]
    qn, kv = qk_norm(proj, q_norm[0], k_norm[0], F32)
    kv3 = kv.reshape(B, T, N_GROUPS * 2 * GROUP_W)
    kvs = [kv3[:, :, gi * 2 * GROUP_W:(gi + 1) * 2 * GROUP_W] for gi in range(N_GROUPS)]
    bias_all = jnp.stack([_prompt_bias_matrix(rel_bias, gi) for gi in range(N_GROUPS)], 0)
    yb = attn_prompt(qn.reshape(B, T, ATTN_W), kv3, bias_all)
    merged = branch_merge(ya.reshape(B * T, GDN_V_W), yb, wa_bf, wb_bf, proj)
    x1 = matmul(merged, wo_bf, residual=xp, name="out_proj")
    p_kv = []
    for gi, (window, _) in enumerate(ATTN_GROUPS):
        keep = min(window, T)
        kvg = kvs[gi][:, T - keep:].reshape(B, keep, 2, GROUP_HEADS, HEAD_DIM)
        p_kv.append(kvg[None])

    ya_s, conv_s, ssm_s = gdn_step(proj_s, state_conv[0], state_ssm[0], cw, a_log[0], dt_bias[0],
                                   gdn_norm[0])
    qn_s, kv_s = qk_norm(proj_s, q_norm[0], k_norm[0], F32)
    kvs_s = [kv_s[:, gi * 2 * GROUP_W:(gi + 1) * 2 * GROUP_W] for gi in range(N_GROUPS)]
    yb_s = attn_decode(qn_s, kvs_s, caches, rel_bias)
    merged_s = branch_merge(ya_s.astype(BF16), yb_s.astype(BF16), wa_bf, wb_bf, proj_s)
    x1_s = matmul(merged_s, wo_bf, residual=xs, name="out_proj_s")
    y_prompt = peer_ffn(x1, norm_ffn[0], wq_bf, peer_subkeys[0], u_bf, v_bf).reshape(B, T, D)
    y_sample = peer_ffn(x1_s, norm_ffn[0], wq_bf, peer_subkeys[0], u_bf, v_bf).reshape(Bs, 1, D)
    s_kv =[kvs_s[gi].reshape(1, Bs, 1, 2, GROUP_HEADS, HEAD_DIM) for gi in range(N_GROUPS)]

    return (y_prompt, y_sample,
            ssm_p[None], conv_p[None], p_kv[0], p_kv[1], p_kv[2],
            ssm_s[None], conv_s[None], s_kv[0], s_kv[1], s_kv[2])
```
